```python
import math, functools
import jax, jax.numpy as jnp
from jax import lax
import numpy as np

D_MODEL = 2048
BATCH = 2
SEQ = 8192
DEPTH = 1
DEC_BATCH = 16
DEC_SEQ = 16
PAST_LEN = 1024

CHUNK = 64
D_FF = 4096
D_CONV = 1024
CONV_W = 3
N_HEADS = 16
N_KV = 4
HEAD_DIM = 64
GROUP = N_HEADS // N_KV
WINDOW = 128
WIN_ROWS = WINDOW
N_BAND = WINDOW // CHUNK + 1
N_BUCKETS = 32
MAX_DIST = 128
D_PLE = 256
EPS = 1e-6
NEG = -1e30

COL_SIZES = (D_CONV, D_CONV, D_CONV, N_HEADS * HEAD_DIM, N_KV * HEAD_DIM, N_KV * HEAD_DIM, D_MODEL, D_MODEL)
W_IN_COLS = sum(COL_SIZES)
SPLIT_IDX = tuple(int(s) for s in np.cumsum(COL_SIZES)[:-1])

kernel_name = "hybrid_streaming_conv_swa_step"


def rms_norm(x, g):
    xf = x.astype(jnp.float32)
    y = xf * lax.rsqrt(jnp.mean(xf * xf, axis=-1, keepdims=True) + EPS)
    return (y * g.astype(jnp.float32)).astype(x.dtype)


def swiglu(x, wg, wu, wd):
    return (jax.nn.silu(x @ wg) * (x @ wu)) @ wd


def t5_bucket(rel):
    nb = N_BUCKETS // 2
    max_exact = nb // 2
    ret = jnp.where(rel > 0, nb, 0)
    n = jnp.abs(rel)
    nf = jnp.maximum(n, 1).astype(jnp.float32)
    large = max_exact + (jnp.log(nf / max_exact) / math.log(MAX_DIST / max_exact) * (nb - max_exact)).astype(jnp.int32)
    large = jnp.minimum(large, nb - 1)
    return ret + jnp.where(n < max_exact, n, large)


def rel_bias(table, n_q, n_k):
    i = jnp.arange(n_q)[:, None]
    j = jnp.arange(n_k)[None, :]
    b = table[t5_bucket(j - WIN_ROWS - i)]
    return b.transpose(2, 0, 1).reshape(N_KV, GROUP, n_q, n_k).astype(jnp.float32)


def sink_softmax(logits, sink):
    s = sink.astype(jnp.float32).reshape(N_KV, GROUP, 1, 1)
    m = jnp.maximum(jnp.max(logits, axis=-1, keepdims=True), s)
    e = jnp.exp(logits - m)
    return e / (jnp.sum(e, axis=-1, keepdims=True) + jnp.exp(s - m))


def short_conv(u, prev, w):
    T = u.shape[1]
    full = jnp.concatenate([prev.astype(u.dtype), u], axis=1)
    out = w[0] * full[:, 0:T]
    for t in range(1, CONV_W):
        out = out + w[t] * full[:, t:t + T]
    return out, full[:, -(CONV_W - 1):]


def swa_prompt(q, k, v, bias, sink):
    B, T = q.shape[:2]
    nc = T // CHUNK
    scale = HEAD_DIM ** -0.5
    qb = q.reshape(B, nc, CHUNK, N_KV, GROUP, HEAD_DIM)
    pad = ((0, 0), (WINDOW, 0), (0, 0), (0, 0))
    kp = jnp.pad(k, pad).reshape(B, nc + N_BAND - 1, CHUNK, N_KV, HEAD_DIM)
    vp = jnp.pad(v, pad).reshape(B, nc + N_BAND - 1, CHUNK, N_KV, HEAD_DIM)
    kb = jnp.concatenate([kp[:, i:i + nc] for i in range(N_BAND)], axis=2)
    vb = jnp.concatenate([vp[:, i:i + nc] for i in range(N_BAND)], axis=2)
    logits = jnp.einsum('bcqkgd,bcskd->bckgqs', qb, kb).astype(jnp.float32) * scale + bias
    key_pos = jnp.arange(nc)[:, None] * CHUNK + jnp.arange(N_BAND * CHUNK)[None, :] - WINDOW
    logits = jnp.where((key_pos >= 0)[None, :, None, None, None, :], logits, NEG)
    p = sink_softmax(logits, sink).astype(v.dtype)
    o = jnp.einsum('bckgqs,bcskd->bcqkgd', p, vb).reshape(B, T, N_HEADS * HEAD_DIM)
    return o, k[:, -WIN_ROWS:], v[:, -WIN_ROWS:]


def swa_sample(q, k, v, k_cache, v_cache, bias, sink):
    B, S = q.shape[:2]
    scale = HEAD_DIM ** -0.5
    qh = q.reshape(B, S, N_KV, GROUP, HEAD_DIM)
    ka = jnp.concatenate([k_cache.astype(k.dtype), k], axis=1)
    va = jnp.concatenate([v_cache.astype(v.dtype), v], axis=1)
    logits = jnp.einsum('bqkgd,bskd->bkgqs', qh, ka).astype(jnp.float32) * scale + bias
    p = sink_softmax(logits, sink).astype(v.dtype)
    o = jnp.einsum('bkgqs,bskd->bqkgd', p, va).reshape(B, S, N_HEADS * HEAD_DIM)
    return o, k, v


def layer(x, pe, conv_prev, attend, w):
    (f1_norm, f1_wg, f1_wu, f1_wd, mix_norm, w_in, conv_w, q_norm, k_norm,
     w_conv_out, w_attn_o, w_out, f2_norm, f2_wg, f2_wu, f2_wd, ple_norm, w_ple, w_ple_gate) = w
    B, T = x.shape[:2]
    h = x + 0.5 * swiglu(rms_norm(x, f1_norm), f1_wg, f1_wu, f1_wd)
    n = rms_norm(h, mix_norm)
    cb, cc, cv, q, k, v, gc, ga = jnp.split(n @ w_in, SPLIT_IDX, axis=-1)
    cu, conv_state = short_conv(cc * cv, conv_prev, conv_w)
    y_conv = (cb * cu) @ w_conv_out
    q = rms_norm(q.reshape(B, T, N_HEADS, HEAD_DIM), q_norm)
    k = rms_norm(k.reshape(B, T, N_KV, HEAD_DIM), k_norm)
    v = v.reshape(B, T, N_KV, HEAD_DIM)
    o, k_state, v_state = attend(q, k, v)
    y_attn = o @ w_attn_o
    h = h + (jax.nn.sigmoid(gc) * y_conv + jax.nn.sigmoid(ga) * y_attn) @ w_out
    h = h + 0.5 * swiglu(rms_norm(h, f2_norm), f2_wg, f2_wu, f2_wd)
    h = h + (pe @ w_ple) * jax.nn.sigmoid(rms_norm(h, ple_norm) @ w_ple_gate)
    return h, conv_state, k_state, v_state


def setup_inputs(seed: int = 0) -> dict:
    key = jax.random.key(seed)
    ks = iter(jax.random.split(key, 40))
    f32 = jnp.float32

    def nrm(shape, scale):
        return jax.random.normal(next(ks), shape, f32) * scale

    def gain(shape):
        return 1.0 + nrm(shape, 0.02)

    L, D = DEPTH, D_MODEL
    return {
        "x_prompt": nrm((BATCH, SEQ, D), 1.0),
        "x_sample": nrm((DEC_BATCH, DEC_SEQ, D), 1.0),
        "p_prompt": nrm((DEPTH, BATCH, SEQ, D_PLE), 1.0),
        "p_sample": nrm((DEPTH, DEC_BATCH, DEC_SEQ, D_PLE), 1.0),
        "state_conv": nrm((DEPTH, DEC_BATCH, CONV_W - 1, D_CONV), 0.5),
        "cache_k": nrm((DEPTH, DEC_BATCH, WIN_ROWS, N_KV, HEAD_DIM), 1.0),
        "cache_v": nrm((DEPTH, DEC_BATCH, WIN_ROWS, N_KV, HEAD_DIM), 1.0),
        "rel_table": nrm((N_BUCKETS, N_HEADS), 0.1),
        "ffn1_norm": gain((L, D)),
        "ffn1_wg": nrm((L, D, D_FF), D ** -0.5),
        "ffn1_wu": nrm((L, D, D_FF), D ** -0.5),
        "ffn1_wd": nrm((L, D_FF, D), D_FF ** -0.5),
        "mix_norm": gain((L, D)),
        "w_in": nrm((L, D, W_IN_COLS), D ** -0.5),
        "conv_w": nrm((L, CONV_W, D_CONV), CONV_W ** -0.5),
        "q_norm": gain((L, HEAD_DIM)),
        "k_norm": gain((L, HEAD_DIM)),
        "attn_sink": nrm((L, N_HEADS), 0.5),
        "w_conv_out": nrm((L, D_CONV, D), D_CONV ** -0.5),
        "w_attn_o": nrm((L, N_HEADS * HEAD_DIM, D), (N_HEADS * HEAD_DIM) ** -0.5),
        "w_out": nrm((L, D, D), D ** -0.5),
        "ffn2_norm": gain((L, D)),
        "ffn2_wg": nrm((L, D, D_FF), D ** -0.5),
        "ffn2_wu": nrm((L, D, D_FF), D ** -0.5),
        "ffn2_wd": nrm((L, D_FF, D), D_FF ** -0.5),
        "ple_norm": gain((L, D)),
        "w_ple": nrm((L, D_PLE, D), D_PLE ** -0.5),
        "w_ple_gate": nrm((L, D, D), D ** -0.5),
    }


def reference(x_prompt, x_sample, p_prompt, p_sample, state_conv, cache_k, cache_v, rel_table,
              ffn1_norm, ffn1_wg, ffn1_wu, ffn1_wd, mix_norm, w_in, conv_w, q_norm, k_norm, attn_sink,
              w_conv_out, w_attn_o, w_out, ffn2_norm, ffn2_wg, ffn2_wu, ffn2_wd, ple_norm, w_ple, w_ple_gate):
    S = x_sample.shape[1]
    bias_p = rel_bias(rel_table, CHUNK, WIN_ROWS + CHUNK)
    bias_s = rel_bias(rel_table, S, WIN_ROWS + S)
    yp, ys = x_prompt, x_sample
    conv_p, k_p, v_p, conv_s, k_s, v_s = [], [], [], [], [], []
    for l in range(DEPTH):
        w = (ffn1_norm[l], ffn1_wg[l], ffn1_wu[l], ffn1_wd[l], mix_norm[l], w_in[l], conv_w[l],
             q_norm[l], k_norm[l], w_conv_out[l], w_attn_o[l], w_out[l],
             ffn2_norm[l], ffn2_wg[l], ffn2_wu[l], ffn2_wd[l], ple_norm[l], w_ple[l], w_ple_gate[l])
        zero_prev = jnp.zeros((yp.shape[0], CONV_W - 1, D_CONV), yp.dtype)
        attend_p = functools.partial(swa_prompt, bias=bias_p, sink=attn_sink[l])
        yp, cs, kk, vv = layer(yp, p_prompt[l], zero_prev, attend_p, w)
        conv_p.append(cs); k_p.append(kk); v_p.append(vv)
        attend_s = functools.partial(swa_sample, k_cache=cache_k[l], v_cache=cache_v[l], bias=bias_s, sink=attn_sink[l])
        ys, cs, kk, vv = layer(ys, p_sample[l], state_conv[l], attend_s, w)
        conv_s.append(cs); k_s.append(kk); v_s.append(vv)
    conv_prompt = jnp.stack(conv_p)
    k_prompt = jnp.stack(k_p)
    v_prompt = jnp.stack(v_p)
    conv_sample = jnp.stack(conv_s)
    k_sample = jnp.stack(k_s)
    v_sample = jnp.stack(v_s)
    return (yp, ys, conv_prompt, k_prompt, v_prompt, conv_sample, k_sample, v_sample)
```

```python
import functools
import math

import jax
import jax.numpy as jnp
from jax import lax
from jax.experimental import pallas as pl
from jax.experimental.pallas import tpu as pltpu

_BF = jnp.bfloat16
_F32 = jnp.float32

CHUNK = 64
WINDOW = 128
N_HEADS = 16
N_KV = 4
GROUP = N_HEADS // N_KV
HEAD_DIM = 64
N_BUCKETS = 32
MAX_DIST = 128
EPS = 1e-6
NEG = -1e30
QK_SCALE = HEAD_DIM ** -0.5

V7X_VMEM_BYTES = 64 * 1024 * 1024
VMEM_INTERNAL_BYTES = 12 * 1024 * 1024


def _vmem_limit(block_bytes, scratch_bytes=0):
    need = 2 * sum(block_bytes) + scratch_bytes + VMEM_INTERNAL_BYTES
    return int(min(need, V7X_VMEM_BYTES - 4 * 1024 * 1024))


def _nbytes(shape, dtype):
    return math.prod(shape) * jnp.dtype(dtype).itemsize


def _dot(a, b):
    return jnp.dot(a, b, preferred_element_type=_F32)


def _rms(x, g):
    ms = jnp.mean(x * x, axis=-1, keepdims=True)
    return x * lax.rsqrt(ms + EPS) * g


def _ffn_kernel(*refs, emit_norm):
    if emit_norm:
        x_ref, g_ref, wg_ref, wu_ref, wd_ref, g2_ref, out_ref, n_ref, xn_ref = refs
    else:
        x_ref, g_ref, wg_ref, wu_ref, wd_ref, out_ref, xn_ref = refs
    j = pl.program_id(1)
    last = pl.num_programs(1) - 1

    @pl.when(j == 0)
    def _():
        xn_ref[...] = _rms(x_ref[...], g_ref[...]).astype(_BF)

    xn = xn_ref[...]
    gate = _dot(xn, wg_ref[...])
    up = _dot(xn, wu_ref[...])
    hid = (jax.nn.silu(gate) * up).astype(_BF)
    part = _dot(hid, wd_ref[...])

    @pl.when(j == 0)
    def _():
        out_ref[...] = part

    @pl.when(j > 0)
    def _():
        out_ref[...] += part

    @pl.when(j == last)
    def _():
        h = x_ref[...] + 0.5 * out_ref[...]
        out_ref[...] = h
        if emit_norm:
            n_ref[...] = _rms(h, g2_ref[...]).astype(_BF)


def _ffn(x, g, wg, wu, wd, g2, *, tm, tf, name):
    M, D = x.shape
    F = wg.shape[1]
    emit_norm = g2 is not None
    row = lambda i, j: (i, 0)
    in_specs = [
        pl.BlockSpec((tm, D), row),
        pl.BlockSpec((1, D), lambda i, j: (0, 0)),
        pl.BlockSpec((D, tf), lambda i, j: (0, j)),
        pl.BlockSpec((D, tf), lambda i, j: (0, j)),
        pl.BlockSpec((tf, D), lambda i, j: (j, 0)),
    ]
    args = [x, g, wg, wu, wd]
    out_shape = [jax.ShapeDtypeStruct((M, D), _F32)]
    out_specs = [pl.BlockSpec((tm, D), row)]
    blocks = [_nbytes((tm, D), _F32) * 2, 3 * _nbytes((D, tf), _BF)]
    if emit_norm:
        in_specs.append(pl.BlockSpec((1, D), lambda i, j: (0, 0)))
        args.append(g2)
        out_shape.append(jax.ShapeDtypeStruct((M, D), _BF))
        out_specs.append(pl.BlockSpec((tm, D), row))
        blocks.append(_nbytes((tm, D), _BF))
    res = pl.pallas_call(
        functools.partial(_ffn_kernel, emit_norm=emit_norm),
        grid=(M // tm, F // tf),
        in_specs=in_specs,
        out_specs=out_specs,
        out_shape=out_shape,
        scratch_shapes=[pltpu.VMEM((tm, D), _BF)],
        compiler_params=pltpu.CompilerParams(
            dimension_semantics=("arbitrary", "arbitrary"),
            vmem_limit_bytes=_vmem_limit(blocks, _nbytes((tm, D), _BF)),
        ),
        name=name,
    )(*args)
    return res if emit_norm else res[0]


def _conv_kernel(*refs, carry_rows, seg, tiles_per_seg, d_conv, cw):
    if carry_rows:
        n_ref, w_ref, cwt_ref, ucb_ref, ulast_ref, carry_ref = refs
    else:
        n_ref, w_ref, cwt_ref, e1_ref, e2_ref, ucb_ref, u_ref = refs
    tm = n_ref.shape[0]
    n = n_ref[...]
    if carry_rows:
        @pl.when(pl.program_id(0) % tiles_per_seg == 0)
        def _():
            carry_ref[...] = jnp.zeros_like(carry_ref)

    for c in range(d_conv // cw):
        lo = c * cw
        cb = _dot(n, w_ref[:, lo:lo + cw])
        cc = _dot(n, w_ref[:, d_conv + lo:d_conv + lo + cw])
        cv = _dot(n, w_ref[:, 2 * d_conv + lo:2 * d_conv + lo + cw])
        u = cc * cv
        r1 = pltpu.roll(u, 1, 0)
        r2 = pltpu.roll(u, 2, 0)
        if carry_rows:
            prev = carry_ref[:, lo:lo + cw]
            rows8 = lax.broadcasted_iota(jnp.int32, (8, cw), 0)
            h1 = jnp.where(rows8 < 1, pltpu.roll(prev, 1, 0), r1[:8])
            h2 = jnp.where(rows8 < 2, pltpu.roll(prev, 2, 0), r2[:8])
            r1 = jnp.concatenate([h1, r1[8:]], axis=0)
            r2 = jnp.concatenate([h2, r2[8:]], axis=0)
            carry_ref[:, lo:lo + cw] = u[tm - 8:]
            ulast_ref[0, :, lo:lo + cw] = u[tm - 8:]
        else:
            pos = lax.broadcasted_iota(jnp.int32, (tm, cw), 0) % seg
            r1 = jnp.where(pos < 1, e1_ref[:, lo:lo + cw], r1)
            r2 = jnp.where(pos < 2, e2_ref[:, lo:lo + cw], r2)
            u_ref[:, lo:lo + cw] = u
        w0 = cwt_ref[0:1, lo:lo + cw]
        w1 = cwt_ref[1:2, lo:lo + cw]
        w2 = cwt_ref[2:3, lo:lo + cw]
        conv = w0 * r2 + w1 * r1 + w2 * u
        ucb_ref[:, lo:lo + cw] = (cb * conv).astype(_BF)


def _conv_prompt(n, w_in, conv_w, *, batch, tm, cw, name):
    M, D = n.shape
    d_conv = conv_w.shape[1]
    tiles_per_seg = (M // batch) // tm
    blocks = [_nbytes((tm, D), _BF), _nbytes((D, 3 * d_conv), _BF), _nbytes((tm, d_conv), _BF)]
    return pl.pallas_call(
        functools.partial(_conv_kernel, carry_rows=True, seg=None, tiles_per_seg=tiles_per_seg,
                          d_conv=d_conv, cw=cw),
        grid=(M // tm,),
        in_specs=[
            pl.BlockSpec((tm, D), lambda i: (i, 0)),
            pl.BlockSpec((D, 3 * d_conv), lambda i: (0, 0)),
            pl.BlockSpec((3, d_conv), lambda i: (0, 0)),
        ],
        out_specs=[
            pl.BlockSpec((tm, d_conv), lambda i: (i, 0)),
            pl.BlockSpec((1, 8, d_conv), lambda i: (i // tiles_per_seg, 0, 0)),
        ],
        out_shape=[
            jax.ShapeDtypeStruct((M, d_conv), _BF),
            jax.ShapeDtypeStruct((batch, 8, d_conv), _F32),
        ],
        scratch_shapes=[pltpu.VMEM((8, d_conv), _F32)],
        compiler_params=pltpu.CompilerParams(
            dimension_semantics=("arbitrary",),
            vmem_limit_bytes=_vmem_limit(blocks),
        ),
        name=name,
    )(n, w_in, conv_w)


def _conv_sample(n, w_in, conv_w, e1, e2, *, seg, cw, name):
    M, D = n.shape
    d_conv = conv_w.shape[1]
    blocks = [_nbytes((M, D), _BF), _nbytes((D, 3 * d_conv), _BF), 4 * _nbytes((M, d_conv), _F32)]
    full = lambda i: (0, 0)
    return pl.pallas_call(
        functools.partial(_conv_kernel, carry_rows=False, seg=seg, tiles_per_seg=None,
                          d_conv=d_conv, cw=cw),
        grid=(1,),
        in_specs=[
            pl.BlockSpec((M, D), full),
            pl.BlockSpec((D, 3 * d_conv), full),
            pl.BlockSpec((3, d_conv), full),
            pl.BlockSpec((M, d_conv), full),
            pl.BlockSpec((M, d_conv), full),
        ],
        out_specs=[pl.BlockSpec((M, d_conv), full), pl.BlockSpec((M, d_conv), full)],
        out_shape=[
            jax.ShapeDtypeStruct((M, d_conv), _BF),
            jax.ShapeDtypeStruct((M, d_conv), _F32),
        ],
        compiler_params=pltpu.CompilerParams(
            dimension_semantics=("arbitrary",),
            vmem_limit_bytes=_vmem_limit(blocks),
        ),
        name=name,
    )(n, w_in, conv_w, e1, e2)


def _qkv_kernel(n_ref, w_ref, gq_ref, gk_ref, ones_ref, q_ref, k_ref, v_ref, *, cw):
    n = n_ref[...]
    ones = ones_ref[...]
    dq = q_ref.shape[1]
    dk = k_ref.shape[1]

    def head_norm(x, g):
        sq = x * x
        hi = sq.astype(_BF)
        lo = (sq - hi.astype(_F32)).astype(_BF)
        ss = _dot(hi, ones) + _dot(lo, ones)
        return x * lax.rsqrt(ss * (1.0 / HEAD_DIM) + EPS) * g

    for c in range(dq // cw):
        x = _dot(n, w_ref[:, c * cw:(c + 1) * cw])
        q_ref[:, c * cw:(c + 1) * cw] = (head_norm(x, gq_ref[...]) * QK_SCALE).astype(_BF)
    for c in range(dk // cw):
        x = _dot(n, w_ref[:, dq + c * cw:dq + (c + 1) * cw])
        k_ref[:, c * cw:(c + 1) * cw] = head_norm(x, gk_ref[...])
    v_ref[...] = _dot(n, w_ref[:, dq + dk:dq + 2 * dk])


def _qkv(n, w_in, gq, gk, ones, *, col_block, tm, name):
    M, D = n.shape
    cw = ones.shape[0]
    dq, dk = N_HEADS * HEAD_DIM, N_KV * HEAD_DIM
    wcols = dq + 2 * dk
    blocks = [_nbytes((tm, D), _BF), _nbytes((D, wcols), _BF), _nbytes((tm, dq), _BF), 2 * _nbytes((tm, dk), _F32)]
    const = lambda i: (0, 0)
    row = lambda i: (i, 0)
    return pl.pallas_call(
        functools.partial(_qkv_kernel, cw=cw),
        grid=(M // tm,),
        in_specs=[
            pl.BlockSpec((tm, D), row),
            pl.BlockSpec((D, wcols), lambda i: (0, col_block)),
            pl.BlockSpec((1, cw), const),
            pl.BlockSpec((1, cw), const),
            pl.BlockSpec((cw, cw), const),
        ],
        out_specs=[pl.BlockSpec((tm, dq), row), pl.BlockSpec((tm, dk), row), pl.BlockSpec((tm, dk), row)],
        out_shape=[
            jax.ShapeDtypeStruct((M, dq), _BF),
            jax.ShapeDtypeStruct((M, dk), _F32),
            jax.ShapeDtypeStruct((M, dk), _F32),
        ],
        compiler_params=pltpu.CompilerParams(
            dimension_semantics=("arbitrary",),
            vmem_limit_bytes=_vmem_limit(blocks),
        ),
        name=name,
    )(n, w_in, gq, gk, ones)


def _bias_kernel(tab_ref, bkt_ref, out_ref):
    bkt = bkt_ref[...]
    for h in range(out_ref.shape[0]):
        acc = jnp.zeros(bkt.shape, _F32)
        for b in range(tab_ref.shape[0]):
            acc = jnp.where(bkt == b, tab_ref[b, h], acc)
        out_ref[h] = acc


def _rel_bias(table, bucket, name):
    nq, nk = bucket.shape
    return pl.pallas_call(
        _bias_kernel,
        in_specs=[pl.BlockSpec(memory_space=pltpu.SMEM), pl.BlockSpec((nq, nk), lambda: (0, 0))],
        out_specs=pl.BlockSpec((table.shape[1], nq, nk), lambda: (0, 0, 0)),
        out_shape=jax.ShapeDtypeStruct((table.shape[1], nq, nk), _F32),
        name=name,
    )(table, bucket)


def _t5_bucket(rel):
    nb = N_BUCKETS // 2
    max_exact = nb // 2
    ret = jnp.where(rel > 0, nb, 0)
    n = jnp.abs(rel)
    nf = jnp.maximum(n, 1).astype(jnp.float32)
    large = max_exact + (jnp.log(nf / max_exact) / math.log(MAX_DIST / max_exact) * (nb - max_exact)).astype(jnp.int32)
    large = jnp.minimum(large, nb - 1)
    return ret + jnp.where(n < max_exact, n, large)


def _bucket_index(n_q, n_k):
    i = jnp.arange(n_q)[:, None]
    j = jnp.arange(n_k)[None, :]
    return _t5_bucket(j - WINDOW - i).astype(jnp.int32)


def _attn_group(qg, kk, vv, bias, sink, valid):
    lg = lax.dot_general(qg, kk, (((1,), (1,)), ((), ())), preferred_element_type=_F32) + bias
    if valid is not None:
        lg = jnp.where(valid, lg, NEG)
    m = jnp.maximum(jnp.max(lg, axis=-1, keepdims=True), sink)
    e = jnp.exp(lg - m)
    den = jnp.sum(e, axis=-1, keepdims=True) + jnp.exp(sink - m)
    p = (e * (1.0 / den)).astype(_BF)
    return _dot(p, vv)


def _attn_rows(qc, kw, vw, bias_ref, sink_ref, valid):
    nq = qc.shape[0]
    outs = []
    for kv in range(N_KV):
        qg = jnp.concatenate(
            [qc[:, (kv * GROUP + g) * HEAD_DIM:(kv * GROUP + g + 1) * HEAD_DIM] for g in range(GROUP)], axis=0)
        kk = kw[:, kv * HEAD_DIM:(kv + 1) * HEAD_DIM]
        vv = vw[:, kv * HEAD_DIM:(kv + 1) * HEAD_DIM]
        og = _attn_group(qg, kk, vv, bias_ref[kv], sink_ref[kv], valid)
        outs.extend(og[g * nq:(g + 1) * nq] for g in range(GROUP))
    return jnp.concatenate(outs, axis=1)


def _attn_prompt_kernel(q_ref, kp_ref, kc_ref, vp_ref, vc_ref, bias_ref, sink_ref, o_ref, kwin, vwin,
                        *, tiles_per_batch):
    ta = q_ref.shape[0]
    nk = WINDOW + CHUNK
    kwin[0:WINDOW, :] = kp_ref[...].astype(_BF)
    kwin[WINDOW:, :] = kc_ref[...].astype(_BF)
    vwin[0:WINDOW, :] = vp_ref[...].astype(_BF)
    vwin[WINDOW:, :] = vc_ref[...].astype(_BF)
    base = (pl.program_id(0) % tiles_per_batch) * ta - WINDOW
    col = lax.broadcasted_iota(jnp.int32, (1, nk), 1)

    def chunk(c, carry):
        r = pl.multiple_of(c * CHUNK, CHUNK)
        valid = (col + (base + r)) >= 0
        o = _attn_rows(q_ref[pl.ds(r, CHUNK), :], kwin[pl.ds(r, nk), :], vwin[pl.ds(r, nk), :],
                       bias_ref, sink_ref, valid)
        o_ref[pl.ds(r, CHUNK), :] = o.astype(_BF)
        return carry

    lax.fori_loop(0, ta // CHUNK, chunk, 0)


def _attn_prompt(q, k, v, bias, sink, *, batch, ta, name):
    M, dq = q.shape
    dk = k.shape[1]
    tiles_per_batch = (M // batch) // ta
    wb = ta // WINDOW
    nk = WINDOW + CHUNK

    def prev(i):
        return (jnp.where(i % tiles_per_batch == 0, i * wb, i * wb - 1), 0)

    row = lambda i: (i, 0)
    blocks = [_nbytes((ta, dq), _BF) * 2, 2 * _nbytes((WINDOW + ta, dk), _F32),
              _nbytes(bias.shape, _F32), _nbytes((N_KV, GROUP * CHUNK, 128), _F32)]
    return pl.pallas_call(
        functools.partial(_attn_prompt_kernel, tiles_per_batch=tiles_per_batch),
        grid=(M // ta,),
        in_specs=[
            pl.BlockSpec((ta, dq), row),
            pl.BlockSpec((WINDOW, dk), prev),
            pl.BlockSpec((ta, dk), row),
            pl.BlockSpec((WINDOW, dk), prev),
            pl.BlockSpec((ta, dk), row),
            pl.BlockSpec((N_KV, GROUP * CHUNK, nk), lambda i: (0, 0, 0)),
            pl.BlockSpec((N_KV, GROUP * CHUNK, 1), lambda i: (0, 0, 0)),
        ],
        out_specs=pl.BlockSpec((ta, dq), row),
        out_shape=jax.ShapeDtypeStruct((M, dq), _BF),
        scratch_shapes=[pltpu.VMEM((WINDOW + ta, dk), _BF), pltpu.VMEM((WINDOW + ta, dk), _BF)],
        compiler_params=pltpu.CompilerParams(
            dimension_semantics=("arbitrary",),
            vmem_limit_bytes=_vmem_limit(blocks, 2 * _nbytes((WINDOW + ta, dk), _BF)),
        ),
        name=name,
    )(q, k, k, v, v, bias, sink)


def _attn_sample_kernel(q_ref, ck_ref, cv_ref, k_ref, v_ref, bias_ref, sink_ref, o_ref):
    kw = jnp.concatenate([ck_ref[0].astype(_BF), k_ref[...].astype(_BF)], axis=0)
    vw = jnp.concatenate([cv_ref[0].astype(_BF), v_ref[...].astype(_BF)], axis=0)
    o_ref[...] = _attn_rows(q_ref[...], kw, vw, bias_ref, sink_ref, None).astype(_BF)


def _attn_sample(q, k, v, cache_k, cache_v, bias, sink, *, name):
    B, W, dk = cache_k.shape
    M, dq = q.shape
    S = M // B
    row = lambda b: (b, 0)
    return pl.pallas_call(
        _attn_sample_kernel,
        grid=(B,),
        in_specs=[
            pl.BlockSpec((S, dq), row),
            pl.BlockSpec((1, W, dk), lambda b: (b, 0, 0)),
            pl.BlockSpec((1, W, dk), lambda b: (b, 0, 0)),
            pl.BlockSpec((S, dk), row),
            pl.BlockSpec((S, dk), row),
            pl.BlockSpec(bias.shape, lambda b: (0, 0, 0)),
            pl.BlockSpec(sink.shape, lambda b: (0, 0, 0)),
        ],
        out_specs=pl.BlockSpec((S, dq), row),
        out_shape=jax.ShapeDtypeStruct((M, dq), _BF),
        compiler_params=pltpu.CompilerParams(dimension_semantics=("arbitrary",)),
        name=name,
    )(q, cache_k, cache_v, k, v, bias, sink)


def _merge_kernel(h_ref, n_ref, ucb_ref, o_ref, wco_ref, wao_ref, wgc_ref, wga_ref, wout_ref, out_ref):
    j = pl.program_id(1)
    last = pl.num_programs(1) - 1
    n = n_ref[...]
    y_conv = _dot(ucb_ref[...], wco_ref[...])
    y_attn = _dot(o_ref[...], wao_ref[...])
    g_conv = jax.nn.sigmoid(_dot(n, wgc_ref[...]))
    g_attn = jax.nn.sigmoid(_dot(n, wga_ref[...]))
    mix = (g_conv * y_conv + g_attn * y_attn).astype(_BF)
    part = _dot(mix, wout_ref[...])

    @pl.when(j == 0)
    def _():
        out_ref[...] = part

    @pl.when(j > 0)
    def _():
        out_ref[...] += part

    @pl.when(j == last)
    def _():
        out_ref[...] = h_ref[...] + out_ref[...]


def _merge(h, n, ucb, o, w_conv_out, w_attn_o, w_in, w_out, *, gc_col, ga_col, tm, tn, name):
    M, D = h.shape
    dc = ucb.shape[1]
    da = o.shape[1]
    row = lambda i, j: (i, 0)
    col = lambda i, j: (0, j)
    blocks = [2 * _nbytes((tm, D), _F32), _nbytes((tm, D), _BF), _nbytes((tm, dc), _BF), _nbytes((tm, da), _BF),
              _nbytes((dc + da + 3 * D, tn), _BF)]
    return pl.pallas_call(
        _merge_kernel,
        grid=(M // tm, D // tn),
        in_specs=[
            pl.BlockSpec((tm, D), row),
            pl.BlockSpec((tm, D), row),
            pl.BlockSpec((tm, dc), row),
            pl.BlockSpec((tm, da), row),
            pl.BlockSpec((dc, tn), col),
            pl.BlockSpec((da, tn), col),
            pl.BlockSpec((D, tn), lambda i, j: (0, gc_col // tn + j)),
            pl.BlockSpec((D, tn), lambda i, j: (0, ga_col // tn + j)),
            pl.BlockSpec((tn, D), lambda i, j: (j, 0)),
        ],
        out_specs=pl.BlockSpec((tm, D), row),
        out_shape=jax.ShapeDtypeStruct((M, D), _F32),
        compiler_params=pltpu.CompilerParams(
            dimension_semantics=("arbitrary", "arbitrary"),
            vmem_limit_bytes=_vmem_limit(blocks),
        ),
        name=name,
    )(h, n, ucb, o, w_conv_out, w_attn_o, w_in, w_in, w_out)


def _ple_kernel(h_ref, pe_ref, g_ref, wp_ref, wg_ref, out_ref):
    h = h_ref[...]
    gate = jax.nn.sigmoid(_dot(_rms(h, g_ref[...]).astype(_BF), wg_ref[...]))
    emb = _dot(pe_ref[...].astype(_BF), wp_ref[...])
    out_ref[...] = h + emb * gate


def _ple(h, pe, g, w_ple, w_gate, *, tm, name):
    M, D = h.shape
    dp = pe.shape[1]
    row = lambda i: (i, 0)
    const = lambda i: (0, 0)
    blocks = [2 * _nbytes((tm, D), _F32), _nbytes((tm, dp), _F32), _nbytes((dp + D, D), _BF)]
    return pl.pallas_call(
        _ple_kernel,
        grid=(M // tm,),
        in_specs=[
            pl.BlockSpec((tm, D), row),
            pl.BlockSpec((tm, dp), row),
            pl.BlockSpec((1, D), const),
            pl.BlockSpec((dp, D), const),
            pl.BlockSpec((D, D), const),
        ],
        out_specs=pl.BlockSpec((tm, D), row),
        out_shape=jax.ShapeDtypeStruct((M, D), _F32),
        compiler_params=pltpu.CompilerParams(
            dimension_semantics=("arbitrary",),
            vmem_limit_bytes=_vmem_limit(blocks),
        ),
        name=name,
    )(h, pe, g, w_ple, w_gate)


def _row_tile(m, want):
    return want if m % want == 0 else m


def kernel(x_prompt, x_sample, p_prompt, p_sample, state_conv, cache_k, cache_v, rel_table, ffn1_norm, ffn1_wg, ffn1_wu, ffn1_wd, mix_norm, w_in, conv_w, q_norm, k_norm, attn_sink, w_conv_out, w_attn_o, w_out, ffn2_norm, ffn2_wg, ffn2_wu, ffn2_wd, ple_norm, w_ple, w_ple_gate):
    B, T, D = x_prompt.shape
    Bs, S, _ = x_sample.shape
    depth = ffn1_wg.shape[0]
    d_conv = conv_w.shape[2]
    dq, dk = N_HEADS * HEAD_DIM, N_KV * HEAD_DIM
    qkv_col = 3 * d_conv
    gc_col = qkv_col + dq + 2 * dk
    ga_col = gc_col + D
    Mp, Ms = B * T, Bs * S
    tm_p = _row_tile(Mp, 512)
    tm_s = _row_tile(Ms, 512)
    tf = 512
    cw = 256

    bias_p = _rel_bias(rel_table, _bucket_index(CHUNK, WINDOW + CHUNK), "rel_bias_prompt")
    bias_p = bias_p.reshape(N_KV, GROUP * CHUNK, WINDOW + CHUNK)
    bias_s = _rel_bias(rel_table, _bucket_index(S, WINDOW + S), "rel_bias_sample")
    bias_s = bias_s.reshape(N_KV, GROUP * S, WINDOW + S)
    lane = jnp.arange(cw) // HEAD_DIM
    ones = (lane[:, None] == lane[None, :]).astype(_BF)

    yp = x_prompt.reshape(Mp, D)
    ys = x_sample.reshape(Ms, D)
    conv_p, k_p, v_p, conv_s, k_s, v_s = [], [], [], [], [], []
    for l in range(depth):
        bf = lambda w: w[l].astype(_BF)
        vec = lambda g: g[l].reshape(1, -1)
        f1 = (vec(ffn1_norm), bf(ffn1_wg), bf(ffn1_wu), bf(ffn1_wd))
        f2 = (vec(ffn2_norm), bf(ffn2_wg), bf(ffn2_wu), bf(ffn2_wd))
        w_in_l, w_co, w_ao, w_o = bf(w_in), bf(w_conv_out), bf(w_attn_o), bf(w_out)
        w_p, w_pg = bf(w_ple), bf(w_ple_gate)
        gq = jnp.tile(q_norm[l], cw // HEAD_DIM).reshape(1, cw)
        gk = jnp.tile(k_norm[l], cw // HEAD_DIM).reshape(1, cw)
        sink = attn_sink[l].reshape(N_KV, GROUP, 1, 1)
        sink_p = jnp.broadcast_to(sink, (N_KV, GROUP, CHUNK, 1)).reshape(N_KV, GROUP * CHUNK, 1)
        sink_s = jnp.broadcast_to(sink, (N_KV, GROUP, S, 1)).reshape(N_KV, GROUP * S, 1)

        h1, n = _ffn(yp, *f1, vec(mix_norm), tm=tm_p, tf=tf, name="ffn1_prompt")
        ucb, ulast = _conv_prompt(n, w_in_l, conv_w[l], batch=B, tm=tm_p, cw=cw, name="conv_prompt")
        q, k, v = _qkv(n, w_in_l, gq, gk, ones, col_block=qkv_col // (dq + 2 * dk), tm=tm_p, name="qkv_prompt")
        o = _attn_prompt(q, k, v, bias_p, sink_p, batch=B, ta=tm_p, name="attn_prompt")
        h2 = _merge(h1, n, ucb, o, w_co, w_ao, w_in_l, w_o, gc_col=gc_col, ga_col=ga_col,
                    tm=tm_p, tn=tf, name="merge_prompt")
        h3 = _ffn(h2, *f2, None, tm=tm_p, tf=tf, name="ffn2_prompt")
        yp = _ple(h3, p_prompt[l].reshape(Mp, -1), vec(ple_norm), w_p, w_pg, tm=tm_p, name="ple_prompt")
        conv_p.append(ulast[:, 6:8])
        k_p.append(k.reshape(B, T, N_KV, HEAD_DIM)[:, T - WINDOW:])
        v_p.append(v.reshape(B, T, N_KV, HEAD_DIM)[:, T - WINDOW:])

        st = state_conv[l]
        zeros = jnp.zeros((Bs, S - 2, d_conv), _F32)
        e2 = jnp.concatenate([st, zeros], axis=1).reshape(Ms, d_conv)
        e1 = jnp.concatenate([st[:, 1:], zeros, zeros[:, :1]], axis=1).reshape(Ms, d_conv)
        h1, n = _ffn(ys, *f1, vec(mix_norm), tm=tm_s, tf=tf, name="ffn1_sample")
        ucb, u = _conv_sample(n, w_in_l, conv_w[l], e1, e2, seg=S, cw=cw, name="conv_sample")
        q, k, v = _qkv(n, w_in_l, gq, gk, ones, col_block=qkv_col // (dq + 2 * dk), tm=tm_s, name="qkv_sample")
        o = _attn_sample(q, k, v, cache_k[l].reshape(Bs, WINDOW, dk), cache_v[l].reshape(Bs, WINDOW, dk),
                         bias_s, sink_s, name="attn_sample")
        h2 = _merge(h1, n, ucb, o, w_co, w_ao, w_in_l, w_o, gc_col=gc_col, ga_col=ga_col,
                    tm=tm_s, tn=tf, name="merge_sample")
        h3 = _ffn(h2, *f2, None, tm=tm_s, tf=tf, name="ffn2_sample")
        ys = _ple(h3, p_sample[l].reshape(Ms, -1), vec(ple_norm), w_p, w_pg, tm=tm_s, name="ple_sample")
        conv_s.append(u.reshape(Bs, S, d_conv)[:, S - 2:])
        k_s.append(k.reshape(Bs, S, N_KV, HEAD_DIM))
        v_s.append(v.reshape(Bs, S, N_KV, HEAD_DIM))

    return (yp.reshape(B, T, D), ys.reshape(Bs, S, D), jnp.stack(conv_p), jnp.stack(k_p), jnp.stack(v_p),
            jnp.stack(conv_s), jnp.stack(k_s), jnp.stack(v_s))
```

```python
import functools
import math

import jax
import jax.numpy as jnp
from jax import lax
from jax.experimental import pallas as pl
from jax.experimental.pallas import tpu as pltpu

_BF = jnp.bfloat16
_F32 = jnp.float32

CHUNK = 64
WINDOW = 128
N_HEADS = 16
N_KV = 4
GROUP = N_HEADS // N_KV
HEAD_DIM = 64
N_BUCKETS = 32
MAX_DIST = 128
EPS = 1e-6
NEG = -1e30
QK_SCALE = HEAD_DIM ** -0.5

V7X_VMEM_BYTES = 64 * 1024 * 1024
VMEM_INTERNAL_BYTES = 12 * 1024 * 1024


def _vmem_limit(block_bytes, scratch_bytes=0):
    need = 2 * sum(block_bytes) + scratch_bytes + VMEM_INTERNAL_BYTES
    return int(min(need, V7X_VMEM_BYTES - 4 * 1024 * 1024))


def _nbytes(shape, dtype):
    return math.prod(shape) * jnp.dtype(dtype).itemsize


def _dot(a, b):
    return jnp.dot(a, b, preferred_element_type=_F32)


def _rms(x, g):
    ms = jnp.mean(x * x, axis=-1, keepdims=True)
    return x * lax.rsqrt(ms + EPS) * g


def _ffn_kernel(*refs, emit_norm):
    if emit_norm:
        x_ref, g_ref, wg_ref, wu_ref, wd_ref, g2_ref, out_ref, n_ref, xn_ref = refs
    else:
        x_ref, g_ref, wg_ref, wu_ref, wd_ref, out_ref, xn_ref = refs
    j = pl.program_id(1)
    last = pl.num_programs(1) - 1

    @pl.when(j == 0)
    def _():
        xn_ref[...] = _rms(x_ref[...], g_ref[...]).astype(_BF)

    xn = xn_ref[...]
    gate = _dot(xn, wg_ref[...])
    up = _dot(xn, wu_ref[...])
    hid = (jax.nn.silu(gate) * up).astype(_BF)
    part = _dot(hid, wd_ref[...])

    @pl.when(j == 0)
    def _():
        out_ref[...] = part

    @pl.when(j > 0)
    def _():
        out_ref[...] += part

    @pl.when(j == last)
    def _():
        h = x_ref[...] + 0.5 * out_ref[...]
        out_ref[...] = h
        if emit_norm:
            n_ref[...] = _rms(h, g2_ref[...]).astype(_BF)


def _ffn(x, g, wg, wu, wd, g2, *, tm, tf, name):
    M, D = x.shape
    F = wg.shape[1]
    emit_norm = g2 is not None
    row = lambda i, j: (i, 0)
    in_specs = [
        pl.BlockSpec((tm, D), row),
        pl.BlockSpec((1, D), lambda i, j: (0, 0)),
        pl.BlockSpec((D, tf), lambda i, j: (0, j)),
        pl.BlockSpec((D, tf), lambda i, j: (0, j)),
        pl.BlockSpec((tf, D), lambda i, j: (j, 0)),
    ]
    args = [x, g, wg, wu, wd]
    out_shape = [jax.ShapeDtypeStruct((M, D), _F32)]
    out_specs = [pl.BlockSpec((tm, D), row)]
    blocks = [_nbytes((tm, D), _F32) * 2, 3 * _nbytes((D, tf), _BF)]
    if emit_norm:
        in_specs.append(pl.BlockSpec((1, D), lambda i, j: (0, 0)))
        args.append(g2)
        out_shape.append(jax.ShapeDtypeStruct((M, D), _BF))
        out_specs.append(pl.BlockSpec((tm, D), row))
        blocks.append(_nbytes((tm, D), _BF))
    res = pl.pallas_call(
        functools.partial(_ffn_kernel, emit_norm=emit_norm),
        grid=(M // tm, F // tf),
        in_specs=in_specs,
        out_specs=out_specs,
        out_shape=out_shape,
        scratch_shapes=[pltpu.VMEM((tm, D), _BF)],
        compiler_params=pltpu.CompilerParams(
            dimension_semantics=("arbitrary", "arbitrary"),
            vmem_limit_bytes=_vmem_limit(blocks, _nbytes((tm, D), _BF)),
        ),
        name=name,
    )(*args)
    return res if emit_norm else res[0]


def _conv_kernel(*refs, carry_rows, seg, tiles_per_seg, d_conv, cw):
    if carry_rows:
        n_ref, w_ref, cwt_ref, ucb_ref, ulast_ref, carry_ref = refs
    else:
        n_ref, w_ref, cwt_ref, e1_ref, e2_ref, ucb_ref, u_ref = refs
    tm = n_ref.shape[0]
    n = n_ref[...]
    if carry_rows:
        @pl.when(pl.program_id(0) % tiles_per_seg == 0)
        def _():
            carry_ref[...] = jnp.zeros_like(carry_ref)

    for c in range(d_conv // cw):
        lo = c * cw
        cb = _dot(n, w_ref[:, lo:lo + cw])
        cc = _dot(n, w_ref[:, d_conv + lo:d_conv + lo + cw])
        cv = _dot(n, w_ref[:, 2 * d_conv + lo:2 * d_conv + lo + cw])
        u = cc * cv
        r1 = pltpu.roll(u, 1, 0)
        r2 = pltpu.roll(u, 2, 0)
        if carry_rows:
            prev = carry_ref[:, lo:lo + cw]
            rows8 = lax.broadcasted_iota(jnp.int32, (8, cw), 0)
            h1 = jnp.where(rows8 < 1, pltpu.roll(prev, 1, 0), r1[:8])
            h2 = jnp.where(rows8 < 2, pltpu.roll(prev, 2, 0), r2[:8])
            r1 = jnp.concatenate([h1, r1[8:]], axis=0)
            r2 = jnp.concatenate([h2, r2[8:]], axis=0)
            carry_ref[:, lo:lo + cw] = u[tm - 8:]
            ulast_ref[0, :, lo:lo + cw] = u[tm - 8:]
        else:
            pos = lax.broadcasted_iota(jnp.int32, (tm, cw), 0) % seg
            r1 = jnp.where(pos < 1, e1_ref[:, lo:lo + cw], r1)
            r2 = jnp.where(pos < 2, e2_ref[:, lo:lo + cw], r2)
            u_ref[:, lo:lo + cw] = u
        w0 = cwt_ref[0:1, lo:lo + cw]
        w1 = cwt_ref[1:2, lo:lo + cw]
        w2 = cwt_ref[2:3, lo:lo + cw]
        conv = w0 * r2 + w1 * r1 + w2 * u
        ucb_ref[:, lo:lo + cw] = (cb * conv).astype(_BF)


def _conv_prompt(n, w_in, conv_w, *, batch, tm, cw, name):
    M, D = n.shape
    d_conv = conv_w.shape[1]
    tiles_per_seg = (M // batch) // tm
    blocks = [_nbytes((tm, D), _BF), _nbytes((D, 3 * d_conv), _BF), _nbytes((tm, d_conv), _BF)]
    return pl.pallas_call(
        functools.partial(_conv_kernel, carry_rows=True, seg=None, tiles_per_seg=tiles_per_seg,
                          d_conv=d_conv, cw=cw),
        grid=(M // tm,),
        in_specs=[
            pl.BlockSpec((tm, D), lambda i: (i, 0)),
            pl.BlockSpec((D, 3 * d_conv), lambda i: (0, 0)),
            pl.BlockSpec((3, d_conv), lambda i: (0, 0)),
        ],
        out_specs=[
            pl.BlockSpec((tm, d_conv), lambda i: (i, 0)),
            pl.BlockSpec((1, 8, d_conv), lambda i: (i // tiles_per_seg, 0, 0)),
        ],
        out_shape=[
            jax.ShapeDtypeStruct((M, d_conv), _BF),
            jax.ShapeDtypeStruct((batch, 8, d_conv), _F32),
        ],
        scratch_shapes=[pltpu.VMEM((8, d_conv), _F32)],
        compiler_params=pltpu.CompilerParams(
            dimension_semantics=("arbitrary",),
            vmem_limit_bytes=_vmem_limit(blocks),
        ),
        name=name,
    )(n, w_in, conv_w)


def _conv_sample(n, w_in, conv_w, e1, e2, *, seg, cw, name):
    M, D = n.shape
    d_conv = conv_w.shape[1]
    blocks = [_nbytes((M, D), _BF), _nbytes((D, 3 * d_conv), _BF), 4 * _nbytes((M, d_conv), _F32)]
    full = lambda i: (0, 0)
    return pl.pallas_call(
        functools.partial(_conv_kernel, carry_rows=False, seg=seg, tiles_per_seg=None,
                          d_conv=d_conv, cw=cw),
        grid=(1,),
        in_specs=[
            pl.BlockSpec((M, D), full),
            pl.BlockSpec((D, 3 * d_conv), full),
            pl.BlockSpec((3, d_conv), full),
            pl.BlockSpec((M, d_conv), full),
            pl.BlockSpec((M, d_conv), full),
        ],
        out_specs=[pl.BlockSpec((M, d_conv), full), pl.BlockSpec((M, d_conv), full)],
        out_shape=[
            jax.ShapeDtypeStruct((M, d_conv), _BF),
            jax.ShapeDtypeStruct((M, d_conv), _F32),
        ],
        compiler_params=pltpu.CompilerParams(
            dimension_semantics=("arbitrary",),
            vmem_limit_bytes=_vmem_limit(blocks),
        ),
        name=name,
    )(n, w_in, conv_w, e1, e2)


def _qkv_kernel(n_ref, w_ref, gq_ref, gk_ref, ones_ref, q_ref, k_ref, v_ref, *, cw):
    n = n_ref[...]
    ones = ones_ref[...]
    dq = q_ref.shape[1]
    dk = k_ref.shape[1]

    def head_norm(x, g):
        sq = x * x
        hi = sq.astype(_BF)
        lo = (sq - hi.astype(_F32)).astype(_BF)
        ss = _dot(hi, ones) + _dot(lo, ones)
        return x * lax.rsqrt(ss * (1.0 / HEAD_DIM) + EPS) * g

    for c in range(dq // cw):
        x = _dot(n, w_ref[:, c * cw:(c + 1) * cw])
        q_ref[:, c * cw:(c + 1) * cw] = (head_norm(x, gq_ref[...]) * QK_SCALE).astype(_BF)
    for c in range(dk // cw):
        x = _dot(n, w_ref[:, dq + c * cw:dq + (c + 1) * cw])
        k_ref[:, c * cw:(c + 1) * cw] = head_norm(x, gk_ref[...])
    v_ref[...] = _dot(n, w_ref[:, dq + dk:dq + 2 * dk])


def _qkv(n, w_in, gq, gk, ones, *, col_block, tm, name):
    M, D = n.shape
    cw = ones.shape[0]
    dq, dk = N_HEADS * HEAD_DIM, N_KV * HEAD_DIM
    wcols = dq + 2 * dk
    blocks = [_nbytes((tm, D), _BF), _nbytes((D, wcols), _BF), _nbytes((tm, dq), _BF), 2 * _nbytes((tm, dk), _F32)]
    const = lambda i: (0, 0)
    row = lambda i: (i, 0)
    return pl.pallas_call(
        functools.partial(_qkv_kernel, cw=cw),
        grid=(M // tm,),
        in_specs=[
            pl.BlockSpec((tm, D), row),
            pl.BlockSpec((D, wcols), lambda i: (0, col_block)),
            pl.BlockSpec((1, cw), const),
            pl.BlockSpec((1, cw), const),
            pl.BlockSpec((cw, cw), const),
        ],
        out_specs=[pl.BlockSpec((tm, dq), row), pl.BlockSpec((tm, dk), row), pl.BlockSpec((tm, dk), row)],
        out_shape=[
            jax.ShapeDtypeStruct((M, dq), _BF),
            jax.ShapeDtypeStruct((M, dk), _F32),
            jax.ShapeDtypeStruct((M, dk), _F32),
        ],
        compiler_params=pltpu.CompilerParams(
            dimension_semantics=("arbitrary",),
            vmem_limit_bytes=_vmem_limit(blocks),
        ),
        name=name,
    )(n, w_in, gq, gk, ones)


def _bias_kernel(tab_ref, bkt_ref, out_ref):
    bkt = bkt_ref[...]
    for h in range(out_ref.shape[0]):
        acc = jnp.zeros(bkt.shape, _F32)
        for b in range(tab_ref.shape[0]):
            acc = jnp.where(bkt == b, tab_ref[b, h], acc)
        out_ref[h] = acc


def _rel_bias(table, bucket, name):
    nq, nk = bucket.shape
    return pl.pallas_call(
        _bias_kernel,
        in_specs=[pl.BlockSpec(memory_space=pltpu.SMEM), pl.BlockSpec((nq, nk), lambda: (0, 0))],
        out_specs=pl.BlockSpec((table.shape[1], nq, nk), lambda: (0, 0, 0)),
        out_shape=jax.ShapeDtypeStruct((table.shape[1], nq, nk), _F32),
        name=name,
    )(table, bucket)


def _t5_bucket(rel):
    nb = N_BUCKETS // 2
    max_exact = nb // 2
    ret = jnp.where(rel > 0, nb, 0)
    n = jnp.abs(rel)
    nf = jnp.maximum(n, 1).astype(jnp.float32)
    large = max_exact + (jnp.log(nf / max_exact) / math.log(MAX_DIST / max_exact) * (nb - max_exact)).astype(jnp.int32)
    large = jnp.minimum(large, nb - 1)
    return ret + jnp.where(n < max_exact, n, large)


def _bucket_index(n_q, n_k):
    i = jnp.arange(n_q)[:, None]
    j = jnp.arange(n_k)[None, :]
    return _t5_bucket(j - WINDOW - i).astype(jnp.int32)


def _attn_group(qg, kk, vv, bias, sink, valid):
    lg = lax.dot_general(qg, kk, (((1,), (1,)), ((), ())), preferred_element_type=_F32) + bias
    if valid is not None:
        lg = jnp.where(valid, lg, NEG)
    m = jnp.maximum(jnp.max(lg, axis=-1, keepdims=True), sink)
    e = jnp.exp(lg - m)
    den = jnp.sum(e, axis=-1, keepdims=True) + jnp.exp(sink - m)
    p = (e * (1.0 / den)).astype(_BF)
    return _dot(p, vv)


def _attn_rows(qc, kw, vw, bias_ref, sink_ref, valid):
    nq = qc.shape[0]
    outs = []
    for kv in range(N_KV):
        qg = jnp.concatenate(
            [qc[:, (kv * GROUP + g) * HEAD_DIM:(kv * GROUP + g + 1) * HEAD_DIM] for g in range(GROUP)], axis=0)
        kk = kw[:, kv * HEAD_DIM:(kv + 1) * HEAD_DIM]
        vv = vw[:, kv * HEAD_DIM:(kv + 1) * HEAD_DIM]
        og = _attn_group(qg, kk, vv, bias_ref[kv], sink_ref[kv], valid)
        outs.extend(og[g * nq:(g + 1) * nq] for g in range(GROUP))
    return jnp.concatenate(outs, axis=1)


def _attn_chunk_keys_major(qc, kw, vw, bias_t, sink_t, first_valid_key):
    nq = qc.shape[0]
    gq = GROUP * nq
    nt = (((1,), (1,)), ((), ()))
    tn = (((0,), (0,)), ((), ()))
    logits = []
    for kv in range(N_KV):
        qg = jnp.concatenate(
            [qc[:, (kv * GROUP + g) * HEAD_DIM:(kv * GROUP + g + 1) * HEAD_DIM] for g in range(GROUP)], axis=0)
        kk = kw[:, kv * HEAD_DIM:(kv + 1) * HEAD_DIM]
        logits.append(lax.dot_general(kk, qg, nt, preferred_element_type=_F32))
    lg = jnp.concatenate(logits, axis=1) + bias_t
    key = lax.broadcasted_iota(jnp.int32, lg.shape, 0)
    lg = jnp.where(key >= first_valid_key, lg, NEG)
    m = jnp.maximum(jnp.max(lg, axis=0, keepdims=True), sink_t)
    e = jnp.exp(lg - m)
    den = jnp.sum(e, axis=0, keepdims=True) + jnp.exp(sink_t - m)
    p = (e * (1.0 / den)).astype(_BF)
    outs = []
    for kv in range(N_KV):
        vv = vw[:, kv * HEAD_DIM:(kv + 1) * HEAD_DIM]
        outs.append(lax.dot_general(vv, p[:, kv * gq:(kv + 1) * gq], tn, preferred_element_type=_F32))
    return jnp.concatenate(outs, axis=0).T


def _attn_prompt_kernel(q_ref, kp_ref, kc_ref, vp_ref, vc_ref, bias_ref, sink_ref, o_ref, kwin, vwin,
                        *, tiles_per_batch):
    ta = q_ref.shape[0]
    nk = WINDOW + CHUNK
    dkv = N_KV * HEAD_DIM
    kwin[0:WINDOW, :] = kp_ref[...].astype(_BF)
    kwin[WINDOW:, :] = kc_ref[...].astype(_BF)
    vwin[0:WINDOW, :] = vp_ref[...].astype(_BF)
    vwin[WINDOW:, :] = vc_ref[...].astype(_BF)
    base = (pl.program_id(0) % tiles_per_batch) * ta - WINDOW

    def chunk(c, carry):
        r = pl.multiple_of(c * CHUNK, CHUNK)
        o = _attn_chunk_keys_major(q_ref[pl.ds(r, CHUNK), :], kwin[pl.ds(r, nk), :], vwin[pl.ds(r, nk), :],
                                   bias_ref[...], sink_ref[...], -(base + r))
        for g in range(GROUP):
            o_ref[pl.ds(r, CHUNK), g * dkv:(g + 1) * dkv] = o[g * CHUNK:(g + 1) * CHUNK].astype(_BF)
        return carry

    lax.fori_loop(0, ta // CHUNK, chunk, 0)


def _attn_prompt(q, k, v, bias, sink, *, batch, ta, name):
    M, dq = q.shape
    dk = k.shape[1]
    tiles_per_batch = (M // batch) // ta
    wb = ta // WINDOW
    nk = WINDOW + CHUNK

    def prev(i):
        return (jnp.where(i % tiles_per_batch == 0, i * wb, i * wb - 1), 0)

    row = lambda i: (i, 0)
    blocks = [_nbytes((ta, dq), _BF) * 2, 2 * _nbytes((WINDOW + ta, dk), _F32),
              _nbytes(bias.shape, _F32), _nbytes((8, N_HEADS * CHUNK), _F32)]
    return pl.pallas_call(
        functools.partial(_attn_prompt_kernel, tiles_per_batch=tiles_per_batch),
        grid=(M // ta,),
        in_specs=[
            pl.BlockSpec((ta, dq), row),
            pl.BlockSpec((WINDOW, dk), prev),
            pl.BlockSpec((ta, dk), row),
            pl.BlockSpec((WINDOW, dk), prev),
            pl.BlockSpec((ta, dk), row),
            pl.BlockSpec((nk, N_HEADS * CHUNK), lambda i: (0, 0)),
            pl.BlockSpec((1, N_HEADS * CHUNK), lambda i: (0, 0)),
        ],
        out_specs=pl.BlockSpec((ta, dq), row),
        out_shape=jax.ShapeDtypeStruct((M, dq), _BF),
        scratch_shapes=[pltpu.VMEM((WINDOW + ta, dk), _BF), pltpu.VMEM((WINDOW + ta, dk), _BF)],
        compiler_params=pltpu.CompilerParams(
            dimension_semantics=("arbitrary",),
            vmem_limit_bytes=_vmem_limit(blocks, 2 * _nbytes((WINDOW + ta, dk), _BF)),
        ),
        name=name,
    )(q, k, k, v, v, bias, sink)


def _attn_sample_kernel(q_ref, ck_ref, cv_ref, k_ref, v_ref, bias_ref, sink_ref, o_ref):
    kw = jnp.concatenate([ck_ref[0].astype(_BF), k_ref[...].astype(_BF)], axis=0)
    vw = jnp.concatenate([cv_ref[0].astype(_BF), v_ref[...].astype(_BF)], axis=0)
    o_ref[...] = _attn_rows(q_ref[...], kw, vw, bias_ref, sink_ref, None).astype(_BF)


def _attn_sample(q, k, v, cache_k, cache_v, bias, sink, *, name):
    B, W, dk = cache_k.shape
    M, dq = q.shape
    S = M // B
    row = lambda b: (b, 0)
    return pl.pallas_call(
        _attn_sample_kernel,
        grid=(B,),
        in_specs=[
            pl.BlockSpec((S, dq), row),
            pl.BlockSpec((1, W, dk), lambda b: (b, 0, 0)),
            pl.BlockSpec((1, W, dk), lambda b: (b, 0, 0)),
            pl.BlockSpec((S, dk), row),
            pl.BlockSpec((S, dk), row),
            pl.BlockSpec(bias.shape, lambda b: (0, 0, 0)),
            pl.BlockSpec(sink.shape, lambda b: (0, 0, 0)),
        ],
        out_specs=pl.BlockSpec((S, dq), row),
        out_shape=jax.ShapeDtypeStruct((M, dq), _BF),
        compiler_params=pltpu.CompilerParams(dimension_semantics=("arbitrary",)),
        name=name,
    )(q, cache_k, cache_v, k, v, bias, sink)


def _merge_kernel(h_ref, n_ref, ucb_ref, o_ref, wco_ref, wao_ref, wgc_ref, wga_ref, wout_ref, out_ref):
    j = pl.program_id(1)
    last = pl.num_programs(1) - 1
    n = n_ref[...]
    y_conv = _dot(ucb_ref[...], wco_ref[...])
    y_attn = _dot(o_ref[...], wao_ref[...])
    g_conv = jax.nn.sigmoid(_dot(n, wgc_ref[...]))
    g_attn = jax.nn.sigmoid(_dot(n, wga_ref[...]))
    mix = (g_conv * y_conv + g_attn * y_attn).astype(_BF)
    part = _dot(mix, wout_ref[...])

    @pl.when(j == 0)
    def _():
        out_ref[...] = part

    @pl.when(j > 0)
    def _():
        out_ref[...] += part

    @pl.when(j == last)
    def _():
        out_ref[...] = h_ref[...] + out_ref[...]


def _merge(h, n, ucb, o, w_conv_out, w_attn_o, w_in, w_out, *, gc_col, ga_col, tm, tn, name):
    M, D = h.shape
    dc = ucb.shape[1]
    da = o.shape[1]
    row = lambda i, j: (i, 0)
    col = lambda i, j: (0, j)
    blocks = [2 * _nbytes((tm, D), _F32), _nbytes((tm, D), _BF), _nbytes((tm, dc), _BF), _nbytes((tm, da), _BF),
              _nbytes((dc + da + 3 * D, tn), _BF)]
    return pl.pallas_call(
        _merge_kernel,
        grid=(M // tm, D // tn),
        in_specs=[
            pl.BlockSpec((tm, D), row),
            pl.BlockSpec((tm, D), row),
            pl.BlockSpec((tm, dc), row),
            pl.BlockSpec((tm, da), row),
            pl.BlockSpec((dc, tn), col),
            pl.BlockSpec((da, tn), col),
            pl.BlockSpec((D, tn), lambda i, j: (0, gc_col // tn + j)),
            pl.BlockSpec((D, tn), lambda i, j: (0, ga_col // tn + j)),
            pl.BlockSpec((tn, D), lambda i, j: (j, 0)),
        ],
        out_specs=pl.BlockSpec((tm, D), row),
        out_shape=jax.ShapeDtypeStruct((M, D), _F32),
        compiler_params=pltpu.CompilerParams(
            dimension_semantics=("arbitrary", "arbitrary"),
            vmem_limit_bytes=_vmem_limit(blocks),
        ),
        name=name,
    )(h, n, ucb, o, w_conv_out, w_attn_o, w_in, w_in, w_out)


def _ple_kernel(h_ref, pe_ref, g_ref, wp_ref, wg_ref, out_ref):
    h = h_ref[...]
    gate = jax.nn.sigmoid(_dot(_rms(h, g_ref[...]).astype(_BF), wg_ref[...]))
    emb = _dot(pe_ref[...].astype(_BF), wp_ref[...])
    out_ref[...] = h + emb * gate


def _ple(h, pe, g, w_ple, w_gate, *, tm, name):
    M, D = h.shape
    dp = pe.shape[1]
    row = lambda i: (i, 0)
    const = lambda i: (0, 0)
    blocks = [2 * _nbytes((tm, D), _F32), _nbytes((tm, dp), _F32), _nbytes((dp + D, D), _BF)]
    return pl.pallas_call(
        _ple_kernel,
        grid=(M // tm,),
        in_specs=[
            pl.BlockSpec((tm, D), row),
            pl.BlockSpec((tm, dp), row),
            pl.BlockSpec((1, D), const),
            pl.BlockSpec((dp, D), const),
            pl.BlockSpec((D, D), const),
        ],
        out_specs=pl.BlockSpec((tm, D), row),
        out_shape=jax.ShapeDtypeStruct((M, D), _F32),
        compiler_params=pltpu.CompilerParams(
            dimension_semantics=("arbitrary",),
            vmem_limit_bytes=_vmem_limit(blocks),
        ),
        name=name,
    )(h, pe, g, w_ple, w_gate)


def _row_tile(m, want):
    return want if m % want == 0 else m


def kernel(x_prompt, x_sample, p_prompt, p_sample, state_conv, cache_k, cache_v, rel_table, ffn1_norm, ffn1_wg, ffn1_wu, ffn1_wd, mix_norm, w_in, conv_w, q_norm, k_norm, attn_sink, w_conv_out, w_attn_o, w_out, ffn2_norm, ffn2_wg, ffn2_wu, ffn2_wd, ple_norm, w_ple, w_ple_gate):
    B, T, D = x_prompt.shape
    Bs, S, _ = x_sample.shape
    depth = ffn1_wg.shape[0]
    d_conv = conv_w.shape[2]
    dq, dk = N_HEADS * HEAD_DIM, N_KV * HEAD_DIM
    qkv_col = 3 * d_conv
    gc_col = qkv_col + dq + 2 * dk
    ga_col = gc_col + D
    Mp, Ms = B * T, Bs * S
    tm_p = _row_tile(Mp, 512)
    tm_s = _row_tile(Ms, 512)
    tf = 512
    cw = 256

    nk_p = WINDOW + CHUNK
    bias_p = _rel_bias(rel_table, _bucket_index(CHUNK, nk_p), "rel_bias_prompt")
    bias_p = bias_p.transpose(2, 0, 1).reshape(nk_p, N_HEADS * CHUNK)
    bias_s = _rel_bias(rel_table, _bucket_index(S, WINDOW + S), "rel_bias_sample")
    bias_s = bias_s.reshape(N_KV, GROUP * S, WINDOW + S)
    lane = jnp.arange(cw) // HEAD_DIM
    ones = (lane[:, None] == lane[None, :]).astype(_BF)

    yp = x_prompt.reshape(Mp, D)
    ys = x_sample.reshape(Ms, D)
    conv_p, k_p, v_p, conv_s, k_s, v_s = [], [], [], [], [], []
    for l in range(depth):
        bf = lambda w: w[l].astype(_BF)
        vec = lambda g: g[l].reshape(1, -1)
        f1 = (vec(ffn1_norm), bf(ffn1_wg), bf(ffn1_wu), bf(ffn1_wd))
        f2 = (vec(ffn2_norm), bf(ffn2_wg), bf(ffn2_wu), bf(ffn2_wd))
        w_in_l, w_co, w_ao, w_o = bf(w_in), bf(w_conv_out), bf(w_attn_o), bf(w_out)
        w_p, w_pg = bf(w_ple), bf(w_ple_gate)
        gq = jnp.tile(q_norm[l], cw // HEAD_DIM).reshape(1, cw)
        gk = jnp.tile(k_norm[l], cw // HEAD_DIM).reshape(1, cw)
        sink = attn_sink[l].reshape(N_KV, GROUP, 1, 1)
        sink_p = jnp.repeat(attn_sink[l], CHUNK).reshape(1, N_HEADS * CHUNK)
        w_ao_p = w_ao.reshape(N_KV, GROUP, HEAD_DIM, D).transpose(1, 0, 2, 3).reshape(dq, D)
        sink_s = jnp.broadcast_to(sink, (N_KV, GROUP, S, 1)).reshape(N_KV, GROUP * S, 1)

        h1, n = _ffn(yp, *f1, vec(mix_norm), tm=tm_p, tf=tf, name="ffn1_prompt")
        ucb, ulast = _conv_prompt(n, w_in_l, conv_w[l], batch=B, tm=tm_p, cw=cw, name="conv_prompt")
        q, k, v = _qkv(n, w_in_l, gq, gk, ones, col_block=qkv_col // (dq + 2 * dk), tm=tm_p, name="qkv_prompt")
        o = _attn_prompt(q, k, v, bias_p, sink_p, batch=B, ta=tm_p, name="attn_prompt")
        h2 = _merge(h1, n, ucb, o, w_co, w_ao_p, w_in_l, w_o, gc_col=gc_col, ga_col=ga_col,
                    tm=tm_p, tn=tf, name="merge_prompt")
        h3 = _ffn(h2, *f2, None, tm=tm_p, tf=tf, name="ffn2_prompt")
        yp = _ple(h3, p_prompt[l].reshape(Mp, -1), vec(ple_norm), w_p, w_pg, tm=tm_p, name="ple_prompt")
        conv_p.append(ulast[:, 6:8])
        k_p.append(k.reshape(B, T, dk)[:, T - WINDOW:].reshape(B, WINDOW, N_KV, HEAD_DIM))
        v_p.append(v.reshape(B, T, dk)[:, T - WINDOW:].reshape(B, WINDOW, N_KV, HEAD_DIM))

        st = state_conv[l]
        zeros = jnp.zeros((Bs, S - 2, d_conv), _F32)
        e2 = jnp.concatenate([st, zeros], axis=1).reshape(Ms, d_conv)
        e1 = jnp.concatenate([st[:, 1:], zeros, zeros[:, :1]], axis=1).reshape(Ms, d_conv)
        h1, n = _ffn(ys, *f1, vec(mix_norm), tm=tm_s, tf=tf, name="ffn1_sample")
        ucb, u = _conv_sample(n, w_in_l, conv_w[l], e1, e2, seg=S, cw=cw, name="conv_sample")
        q, k, v = _qkv(n, w_in_l, gq, gk, ones, col_block=qkv_col // (dq + 2 * dk), tm=tm_s, name="qkv_sample")
        o = _attn_sample(q, k, v, cache_k[l].reshape(Bs, WINDOW, dk), cache_v[l].reshape(Bs, WINDOW, dk),
                         bias_s, sink_s, name="attn_sample")
        h2 = _merge(h1, n, ucb, o, w_co, w_ao, w_in_l, w_o, gc_col=gc_col, ga_col=ga_col,
                    tm=tm_s, tn=tf, name="merge_sample")
        h3 = _ffn(h2, *f2, None, tm=tm_s, tf=tf, name="ffn2_sample")
        ys = _ple(h3, p_sample[l].reshape(Ms, -1), vec(ple_norm), w_p, w_pg, tm=tm_s, name="ple_sample")
        conv_s.append(u.reshape(Bs, S, d_conv)[:, S - 2:])
        k_s.append(k.reshape(Bs, S, N_KV, HEAD_DIM))
        v_s.append(v.reshape(Bs, S, N_KV, HEAD_DIM))

    return (yp.reshape(B, T, D), ys.reshape(Bs, S, D), jnp.stack(conv_p), jnp.stack(k_p), jnp.stack(v_p),
            jnp.stack(conv_s), jnp.stack(k_s), jnp.stack(v_s))
```

```python
import functools
import math

import jax
import jax.numpy as jnp
from jax import lax
from jax.experimental import pallas as pl
from jax.experimental.pallas import tpu as pltpu

_BF = jnp.bfloat16
_F32 = jnp.float32

CHUNK = 64
WINDOW = 128
N_HEADS = 16
N_KV = 4
GROUP = N_HEADS // N_KV
HEAD_DIM = 64
N_BUCKETS = 32
MAX_DIST = 128
EPS = 1e-6
NEG = -1e30
QK_SCALE = HEAD_DIM ** -0.5

V7X_VMEM_BYTES = 64 * 1024 * 1024
VMEM_INTERNAL_BYTES = 12 * 1024 * 1024


def _vmem_limit(block_bytes, scratch_bytes=0):
    need = 2 * sum(block_bytes) + scratch_bytes + VMEM_INTERNAL_BYTES
    return int(min(need, V7X_VMEM_BYTES - 4 * 1024 * 1024))


def _nbytes(shape, dtype):
    return math.prod(shape) * jnp.dtype(dtype).itemsize


def _dot(a, b):
    return jnp.dot(a, b, preferred_element_type=_F32)


def _rms(x, g):
    ms = jnp.mean(x * x, axis=-1, keepdims=True)
    return x * lax.rsqrt(ms + EPS) * g


def _ffn_kernel(*refs, emit_norm):
    if emit_norm:
        x_ref, g_ref, wg_ref, wu_ref, wd_ref, g2_ref, out_ref, n_ref, xn_ref = refs
    else:
        x_ref, g_ref, wg_ref, wu_ref, wd_ref, out_ref, xn_ref = refs
    j = pl.program_id(1)
    last = pl.num_programs(1) - 1

    @pl.when(j == 0)
    def _():
        xn_ref[...] = _rms(x_ref[...], g_ref[...]).astype(_BF)

    xn = xn_ref[...]
    gate = _dot(xn, wg_ref[...])
    up = _dot(xn, wu_ref[...])
    hid = (jax.nn.silu(gate) * up).astype(_BF)
    out_ref[...] = jnp.where(j == 0, 0.0, out_ref[...]) + _dot(hid, wd_ref[...])

    @pl.when(j == last)
    def _():
        h = x_ref[...] + 0.5 * out_ref[...]
        out_ref[...] = h
        if emit_norm:
            n_ref[...] = _rms(h, g2_ref[...]).astype(_BF)


def _ffn(x, g, wg, wu, wd, g2, *, tm, tf, name):
    M, D = x.shape
    F = wg.shape[1]
    emit_norm = g2 is not None
    row = lambda i, j: (i, 0)
    in_specs = [
        pl.BlockSpec((tm, D), row),
        pl.BlockSpec((1, D), lambda i, j: (0, 0)),
        pl.BlockSpec((D, tf), lambda i, j: (0, j)),
        pl.BlockSpec((D, tf), lambda i, j: (0, j)),
        pl.BlockSpec((tf, D), lambda i, j: (j, 0)),
    ]
    args = [x, g, wg, wu, wd]
    out_shape = [jax.ShapeDtypeStruct((M, D), _F32)]
    out_specs = [pl.BlockSpec((tm, D), row)]
    blocks = [_nbytes((tm, D), _F32) * 2, 3 * _nbytes((D, tf), _BF)]
    if emit_norm:
        in_specs.append(pl.BlockSpec((1, D), lambda i, j: (0, 0)))
        args.append(g2)
        out_shape.append(jax.ShapeDtypeStruct((M, D), _BF))
        out_specs.append(pl.BlockSpec((tm, D), row))
        blocks.append(_nbytes((tm, D), _BF))
    res = pl.pallas_call(
        functools.partial(_ffn_kernel, emit_norm=emit_norm),
        grid=(M // tm, F // tf),
        in_specs=in_specs,
        out_specs=out_specs,
        out_shape=out_shape,
        scratch_shapes=[pltpu.VMEM((tm, D), _BF)],
        compiler_params=pltpu.CompilerParams(
            dimension_semantics=("arbitrary", "arbitrary"),
            vmem_limit_bytes=_vmem_limit(blocks, _nbytes((tm, D), _BF)),
        ),
        name=name,
    )(*args)
    return res if emit_norm else res[0]


def _conv_kernel(*refs, carry_rows, seg, tiles_per_seg, d_conv, cw):
    if carry_rows:
        n_ref, w_ref, cwt_ref, ucb_ref, ulast_ref, carry_ref = refs
    else:
        n_ref, w_ref, cwt_ref, e1_ref, e2_ref, ucb_ref, u_ref = refs
    tm = n_ref.shape[0]
    n = n_ref[...]
    if carry_rows:
        @pl.when(pl.program_id(0) % tiles_per_seg == 0)
        def _():
            carry_ref[...] = jnp.zeros_like(carry_ref)

    for c in range(d_conv // cw):
        lo = c * cw
        cb = _dot(n, w_ref[:, lo:lo + cw])
        cc = _dot(n, w_ref[:, d_conv + lo:d_conv + lo + cw])
        cv = _dot(n, w_ref[:, 2 * d_conv + lo:2 * d_conv + lo + cw])
        u = cc * cv
        r1 = pltpu.roll(u, 1, 0)
        r2 = pltpu.roll(u, 2, 0)
        if carry_rows:
            prev = carry_ref[:, lo:lo + cw]
            rows8 = lax.broadcasted_iota(jnp.int32, (8, cw), 0)
            h1 = jnp.where(rows8 < 1, pltpu.roll(prev, 1, 0), r1[:8])
            h2 = jnp.where(rows8 < 2, pltpu.roll(prev, 2, 0), r2[:8])
            r1 = jnp.concatenate([h1, r1[8:]], axis=0)
            r2 = jnp.concatenate([h2, r2[8:]], axis=0)
            carry_ref[:, lo:lo + cw] = u[tm - 8:]
            ulast_ref[0, :, lo:lo + cw] = u[tm - 8:]
        else:
            pos = lax.broadcasted_iota(jnp.int32, (tm, cw), 0) % seg
            r1 = jnp.where(pos < 1, e1_ref[:, lo:lo + cw], r1)
            r2 = jnp.where(pos < 2, e2_ref[:, lo:lo + cw], r2)
            u_ref[:, lo:lo + cw] = u
        w0 = cwt_ref[0:1, lo:lo + cw]
        w1 = cwt_ref[1:2, lo:lo + cw]
        w2 = cwt_ref[2:3, lo:lo + cw]
        conv = w0 * r2 + w1 * r1 + w2 * u
        ucb_ref[:, lo:lo + cw] = (cb * conv).astype(_BF)


def _conv_prompt(n, w_in, conv_w, *, batch, tm, cw, name):
    M, D = n.shape
    d_conv = conv_w.shape[1]
    tiles_per_seg = (M // batch) // tm
    blocks = [_nbytes((tm, D), _BF), _nbytes((D, 3 * d_conv), _BF), _nbytes((tm, d_conv), _BF)]
    return pl.pallas_call(
        functools.partial(_conv_kernel, carry_rows=True, seg=None, tiles_per_seg=tiles_per_seg,
                          d_conv=d_conv, cw=cw),
        grid=(M // tm,),
        in_specs=[
            pl.BlockSpec((tm, D), lambda i: (i, 0)),
            pl.BlockSpec((D, 3 * d_conv), lambda i: (0, 0)),
            pl.BlockSpec((3, d_conv), lambda i: (0, 0)),
        ],
        out_specs=[
            pl.BlockSpec((tm, d_conv), lambda i: (i, 0)),
            pl.BlockSpec((1, 8, d_conv), lambda i: (i // tiles_per_seg, 0, 0)),
        ],
        out_shape=[
            jax.ShapeDtypeStruct((M, d_conv), _BF),
            jax.ShapeDtypeStruct((batch, 8, d_conv), _F32),
        ],
        scratch_shapes=[pltpu.VMEM((8, d_conv), _F32)],
        compiler_params=pltpu.CompilerParams(
            dimension_semantics=("arbitrary",),
            vmem_limit_bytes=_vmem_limit(blocks),
        ),
        name=name,
    )(n, w_in, conv_w)


def _conv_sample(n, w_in, conv_w, e1, e2, *, seg, cw, name):
    M, D = n.shape
    d_conv = conv_w.shape[1]
    blocks = [_nbytes((M, D), _BF), _nbytes((D, 3 * d_conv), _BF), 4 * _nbytes((M, d_conv), _F32)]
    full = lambda i: (0, 0)
    return pl.pallas_call(
        functools.partial(_conv_kernel, carry_rows=False, seg=seg, tiles_per_seg=None,
                          d_conv=d_conv, cw=cw),
        grid=(1,),
        in_specs=[
            pl.BlockSpec((M, D), full),
            pl.BlockSpec((D, 3 * d_conv), full),
            pl.BlockSpec((3, d_conv), full),
            pl.BlockSpec((M, d_conv), full),
            pl.BlockSpec((M, d_conv), full),
        ],
        out_specs=[pl.BlockSpec((M, d_conv), full), pl.BlockSpec((M, d_conv), full)],
        out_shape=[
            jax.ShapeDtypeStruct((M, d_conv), _BF),
            jax.ShapeDtypeStruct((M, d_conv), _F32),
        ],
        compiler_params=pltpu.CompilerParams(
            dimension_semantics=("arbitrary",),
            vmem_limit_bytes=_vmem_limit(blocks),
        ),
        name=name,
    )(n, w_in, conv_w, e1, e2)


def _qkv_kernel(n_ref, w_ref, gq_ref, gk_ref, ones_ref, q_ref, k_ref, v_ref, *, cw):
    ones = ones_ref[...]
    dq = q_ref.shape[1]
    dk = k_ref.shape[1]
    p = _dot(n_ref[...], w_ref[...])
    x = p[:, :dq + dk]
    sq = x * x
    hi = sq.astype(_BF)
    lo = (sq - hi.astype(_F32)).astype(_BF)
    ss = jnp.concatenate(
        [_dot(hi[:, c * cw:(c + 1) * cw], ones) + _dot(lo[:, c * cw:(c + 1) * cw], ones)
         for c in range((dq + dk) // cw)], axis=1)
    xr = x * lax.rsqrt(ss * (1.0 / HEAD_DIM) + EPS)
    q_ref[...] = (xr[:, :dq] * gq_ref[...] * QK_SCALE).astype(_BF)
    k_ref[...] = xr[:, dq:] * gk_ref[...]
    v_ref[...] = p[:, dq + dk:]


def _qkv(n, w_in, gq, gk, ones, *, col_block, tm, name):
    M, D = n.shape
    cw = ones.shape[0]
    dq, dk = N_HEADS * HEAD_DIM, N_KV * HEAD_DIM
    wcols = dq + 2 * dk
    blocks = [_nbytes((tm, D), _BF), _nbytes((D, wcols), _BF), _nbytes((tm, dq), _BF), 2 * _nbytes((tm, dk), _F32)]
    const = lambda i: (0, 0)
    row = lambda i: (i, 0)
    return pl.pallas_call(
        functools.partial(_qkv_kernel, cw=cw),
        grid=(M // tm,),
        in_specs=[
            pl.BlockSpec((tm, D), row),
            pl.BlockSpec((D, wcols), lambda i: (0, col_block)),
            pl.BlockSpec((1, dq), const),
            pl.BlockSpec((1, dk), const),
            pl.BlockSpec((cw, cw), const),
        ],
        out_specs=[pl.BlockSpec((tm, dq), row), pl.BlockSpec((tm, dk), row), pl.BlockSpec((tm, dk), row)],
        out_shape=[
            jax.ShapeDtypeStruct((M, dq), _BF),
            jax.ShapeDtypeStruct((M, dk), _F32),
            jax.ShapeDtypeStruct((M, dk), _F32),
        ],
        compiler_params=pltpu.CompilerParams(
            dimension_semantics=("arbitrary",),
            vmem_limit_bytes=_vmem_limit(blocks),
        ),
        name=name,
    )(n, w_in, gq, gk, ones)


def _bias_kernel(tab_ref, bkt_ref, out_ref):
    bkt = bkt_ref[...]
    for h in range(out_ref.shape[0]):
        acc = jnp.zeros(bkt.shape, _F32)
        for b in range(tab_ref.shape[0]):
            acc = jnp.where(bkt == b, tab_ref[b, h], acc)
        out_ref[h] = acc


def _rel_bias(table, bucket, name):
    nq, nk = bucket.shape
    return pl.pallas_call(
        _bias_kernel,
        in_specs=[pl.BlockSpec(memory_space=pltpu.SMEM), pl.BlockSpec((nq, nk), lambda: (0, 0))],
        out_specs=pl.BlockSpec((table.shape[1], nq, nk), lambda: (0, 0, 0)),
        out_shape=jax.ShapeDtypeStruct((table.shape[1], nq, nk), _F32),
        name=name,
    )(table, bucket)


def _t5_bucket(rel):
    nb = N_BUCKETS // 2
    max_exact = nb // 2
    ret = jnp.where(rel > 0, nb, 0)
    n = jnp.abs(rel)
    nf = jnp.maximum(n, 1).astype(jnp.float32)
    large = max_exact + (jnp.log(nf / max_exact) / math.log(MAX_DIST / max_exact) * (nb - max_exact)).astype(jnp.int32)
    large = jnp.minimum(large, nb - 1)
    return ret + jnp.where(n < max_exact, n, large)


def _bucket_index(n_q, n_k):
    i = jnp.arange(n_q)[:, None]
    j = jnp.arange(n_k)[None, :]
    return _t5_bucket(j - WINDOW - i).astype(jnp.int32)


def _attn_group(qg, kk, vv, bias, sink, valid):
    lg = lax.dot_general(qg, kk, (((1,), (1,)), ((), ())), preferred_element_type=_F32) + bias
    if valid is not None:
        lg = jnp.where(valid, lg, NEG)
    m = jnp.maximum(jnp.max(lg, axis=-1, keepdims=True), sink)
    e = jnp.exp(lg - m)
    den = jnp.sum(e, axis=-1, keepdims=True) + jnp.exp(sink - m)
    p = (e * (1.0 / den)).astype(_BF)
    return _dot(p, vv)


def _attn_rows(qc, kw, vw, bias_ref, sink_ref, valid):
    nq = qc.shape[0]
    outs = []
    for kv in range(N_KV):
        qg = jnp.concatenate(
            [qc[:, (kv * GROUP + g) * HEAD_DIM:(kv * GROUP + g + 1) * HEAD_DIM] for g in range(GROUP)], axis=0)
        kk = kw[:, kv * HEAD_DIM:(kv + 1) * HEAD_DIM]
        vv = vw[:, kv * HEAD_DIM:(kv + 1) * HEAD_DIM]
        og = _attn_group(qg, kk, vv, bias_ref[kv], sink_ref[kv], valid)
        outs.extend(og[g * nq:(g + 1) * nq] for g in range(GROUP))
    return jnp.concatenate(outs, axis=1)


def _attn_chunk_keys_major(qc, kw, vw, bias_t, sink_t, first_valid_key):
    nq = qc.shape[0]
    gq = GROUP * nq
    nt = (((1,), (1,)), ((), ()))
    tn = (((0,), (0,)), ((), ()))
    logits = []
    for kv in range(N_KV):
        qg = jnp.concatenate(
            [qc[:, (kv * GROUP + g) * HEAD_DIM:(kv * GROUP + g + 1) * HEAD_DIM] for g in range(GROUP)], axis=0)
        kk = kw[:, kv * HEAD_DIM:(kv + 1) * HEAD_DIM]
        logits.append(lax.dot_general(kk, qg, nt, preferred_element_type=_F32))
    lg = jnp.concatenate(logits, axis=1) + bias_t
    key = lax.broadcasted_iota(jnp.int32, lg.shape, 0)
    lg = jnp.where(key >= first_valid_key, lg, NEG)
    m = jnp.maximum(jnp.max(lg, axis=0, keepdims=True), sink_t)
    e = jnp.exp(lg - m)
    den = jnp.sum(e, axis=0, keepdims=True) + jnp.exp(sink_t - m)
    p = (e * (1.0 / den)).astype(_BF)
    outs = []
    for kv in range(N_KV):
        vv = vw[:, kv * HEAD_DIM:(kv + 1) * HEAD_DIM]
        outs.append(lax.dot_general(vv, p[:, kv * gq:(kv + 1) * gq], tn, preferred_element_type=_F32))
    return jnp.concatenate(outs, axis=0).T


def _attn_prompt_kernel(q_ref, kp_ref, kc_ref, vp_ref, vc_ref, bias_ref, sink_ref, o_ref, kwin, vwin,
                        *, tiles_per_batch):
    ta = q_ref.shape[0]
    nk = WINDOW + CHUNK
    dkv = N_KV * HEAD_DIM
    kwin[0:WINDOW, :] = kp_ref[...].astype(_BF)
    kwin[WINDOW:, :] = kc_ref[...].astype(_BF)
    vwin[0:WINDOW, :] = vp_ref[...].astype(_BF)
    vwin[WINDOW:, :] = vc_ref[...].astype(_BF)
    base = (pl.program_id(0) % tiles_per_batch) * ta - WINDOW

    def chunk(c, carry):
        r = pl.multiple_of(c * CHUNK, CHUNK)
        o = _attn_chunk_keys_major(q_ref[pl.ds(r, CHUNK), :], kwin[pl.ds(r, nk), :], vwin[pl.ds(r, nk), :],
                                   bias_ref[...], sink_ref[...], -(base + r))
        for g in range(GROUP):
            o_ref[pl.ds(r, CHUNK), g * dkv:(g + 1) * dkv] = o[g * CHUNK:(g + 1) * CHUNK].astype(_BF)
        return carry

    lax.fori_loop(0, ta // CHUNK, chunk, 0)


def _attn_prompt(q, k, v, bias, sink, *, batch, ta, name):
    M, dq = q.shape
    dk = k.shape[1]
    tiles_per_batch = (M // batch) // ta
    wb = ta // WINDOW
    nk = WINDOW + CHUNK

    def prev(i):
        return (jnp.where(i % tiles_per_batch == 0, i * wb, i * wb - 1), 0)

    row = lambda i: (i, 0)
    blocks = [_nbytes((ta, dq), _BF) * 2, 2 * _nbytes((WINDOW + ta, dk), _F32),
              _nbytes(bias.shape, _F32), _nbytes((8, N_HEADS * CHUNK), _F32)]
    return pl.pallas_call(
        functools.partial(_attn_prompt_kernel, tiles_per_batch=tiles_per_batch),
        grid=(M // ta,),
        in_specs=[
            pl.BlockSpec((ta, dq), row),
            pl.BlockSpec((WINDOW, dk), prev),
            pl.BlockSpec((ta, dk), row),
            pl.BlockSpec((WINDOW, dk), prev),
            pl.BlockSpec((ta, dk), row),
            pl.BlockSpec((nk, N_HEADS * CHUNK), lambda i: (0, 0)),
            pl.BlockSpec((1, N_HEADS * CHUNK), lambda i: (0, 0)),
        ],
        out_specs=pl.BlockSpec((ta, dq), row),
        out_shape=jax.ShapeDtypeStruct((M, dq), _BF),
        scratch_shapes=[pltpu.VMEM((WINDOW + ta, dk), _BF), pltpu.VMEM((WINDOW + ta, dk), _BF)],
        compiler_params=pltpu.CompilerParams(
            dimension_semantics=("arbitrary",),
            vmem_limit_bytes=_vmem_limit(blocks, 2 * _nbytes((WINDOW + ta, dk), _BF)),
        ),
        name=name,
    )(q, k, k, v, v, bias, sink)


def _attn_sample_kernel(q_ref, ck_ref, cv_ref, k_ref, v_ref, bias_ref, sink_ref, o_ref):
    kw = jnp.concatenate([ck_ref[0].astype(_BF), k_ref[...].astype(_BF)], axis=0)
    vw = jnp.concatenate([cv_ref[0].astype(_BF), v_ref[...].astype(_BF)], axis=0)
    o_ref[...] = _attn_rows(q_ref[...], kw, vw, bias_ref, sink_ref, None).astype(_BF)


def _attn_sample(q, k, v, cache_k, cache_v, bias, sink, *, name):
    B, W, dk = cache_k.shape
    M, dq = q.shape
    S = M // B
    row = lambda b: (b, 0)
    return pl.pallas_call(
        _attn_sample_kernel,
        grid=(B,),
        in_specs=[
            pl.BlockSpec((S, dq), row),
            pl.BlockSpec((1, W, dk), lambda b: (b, 0, 0)),
            pl.BlockSpec((1, W, dk), lambda b: (b, 0, 0)),
            pl.BlockSpec((S, dk), row),
            pl.BlockSpec((S, dk), row),
            pl.BlockSpec(bias.shape, lambda b: (0, 0, 0)),
            pl.BlockSpec(sink.shape, lambda b: (0, 0, 0)),
        ],
        out_specs=pl.BlockSpec((S, dq), row),
        out_shape=jax.ShapeDtypeStruct((M, dq), _BF),
        compiler_params=pltpu.CompilerParams(dimension_semantics=("arbitrary",)),
        name=name,
    )(q, cache_k, cache_v, k, v, bias, sink)


def _merge_kernel(h_ref, n_ref, ucb_ref, o_ref, wco_ref, wao_ref, wgc_ref, wga_ref, wout_ref, out_ref):
    j = pl.program_id(1)
    last = pl.num_programs(1) - 1
    n = n_ref[...]
    y_conv = _dot(ucb_ref[...], wco_ref[...])
    y_attn = _dot(o_ref[...], wao_ref[...])
    g_conv = jax.nn.sigmoid(_dot(n, wgc_ref[...]))
    g_attn = jax.nn.sigmoid(_dot(n, wga_ref[...]))
    mix = (g_conv * y_conv + g_attn * y_attn).astype(_BF)
    out_ref[...] = jnp.where(j == 0, 0.0, out_ref[...]) + _dot(mix, wout_ref[...])

    @pl.when(j == last)
    def _():
        out_ref[...] = h_ref[...] + out_ref[...]


def _merge(h, n, ucb, o, w_conv_out, w_attn_o, w_in, w_out, *, gc_col, ga_col, tm, tn, name):
    M, D = h.shape
    dc = ucb.shape[1]
    da = o.shape[1]
    row = lambda i, j: (i, 0)
    col = lambda i, j: (0, j)
    blocks = [2 * _nbytes((tm, D), _F32), _nbytes((tm, D), _BF), _nbytes((tm, dc), _BF), _nbytes((tm, da), _BF),
              _nbytes((dc + da + 3 * D, tn), _BF)]
    return pl.pallas_call(
        _merge_kernel,
        grid=(M // tm, D // tn),
        in_specs=[
            pl.BlockSpec((tm, D), row),
            pl.BlockSpec((tm, D), row),
            pl.BlockSpec((tm, dc), row),
            pl.BlockSpec((tm, da), row),
            pl.BlockSpec((dc, tn), col),
            pl.BlockSpec((da, tn), col),
            pl.BlockSpec((D, tn), lambda i, j: (0, gc_col // tn + j)),
            pl.BlockSpec((D, tn), lambda i, j: (0, ga_col // tn + j)),
            pl.BlockSpec((tn, D), lambda i, j: (j, 0)),
        ],
        out_specs=pl.BlockSpec((tm, D), row),
        out_shape=jax.ShapeDtypeStruct((M, D), _F32),
        compiler_params=pltpu.CompilerParams(
            dimension_semantics=("arbitrary", "arbitrary"),
            vmem_limit_bytes=_vmem_limit(blocks),
        ),
        name=name,
    )(h, n, ucb, o, w_conv_out, w_attn_o, w_in, w_in, w_out)


def _ple_kernel(h_ref, pe_ref, g_ref, wp_ref, wg_ref, out_ref):
    h = h_ref[...]
    gate = jax.nn.sigmoid(_dot(_rms(h, g_ref[...]).astype(_BF), wg_ref[...]))
    emb = _dot(pe_ref[...].astype(_BF), wp_ref[...])
    out_ref[...] = h + emb * gate


def _ple(h, pe, g, w_ple, w_gate, *, tm, name):
    M, D = h.shape
    dp = pe.shape[1]
    row = lambda i: (i, 0)
    const = lambda i: (0, 0)
    blocks = [2 * _nbytes((tm, D), _F32), _nbytes((tm, dp), _F32), _nbytes((dp + D, D), _BF)]
    return pl.pallas_call(
        _ple_kernel,
        grid=(M // tm,),
        in_specs=[
            pl.BlockSpec((tm, D), row),
            pl.BlockSpec((tm, dp), row),
            pl.BlockSpec((1, D), const),
            pl.BlockSpec((dp, D), const),
            pl.BlockSpec((D, D), const),
        ],
        out_specs=pl.BlockSpec((tm, D), row),
        out_shape=jax.ShapeDtypeStruct((M, D), _F32),
        compiler_params=pltpu.CompilerParams(
            dimension_semantics=("arbitrary",),
            vmem_limit_bytes=_vmem_limit(blocks),
        ),
        name=name,
    )(h, pe, g, w_ple, w_gate)


def _row_tile(m, want):
    return want if m % want == 0 else m


def kernel(x_prompt, x_sample, p_prompt, p_sample, state_conv, cache_k, cache_v, rel_table, ffn1_norm, ffn1_wg, ffn1_wu, ffn1_wd, mix_norm, w_in, conv_w, q_norm, k_norm, attn_sink, w_conv_out, w_attn_o, w_out, ffn2_norm, ffn2_wg, ffn2_wu, ffn2_wd, ple_norm, w_ple, w_ple_gate):
    B, T, D = x_prompt.shape
    Bs, S, _ = x_sample.shape
    depth = ffn1_wg.shape[0]
    d_conv = conv_w.shape[2]
    dq, dk = N_HEADS * HEAD_DIM, N_KV * HEAD_DIM
    qkv_col = 3 * d_conv
    gc_col = qkv_col + dq + 2 * dk
    ga_col = gc_col + D
    Mp, Ms = B * T, Bs * S
    tm_p = _row_tile(Mp, 512)
    tm_s = _row_tile(Ms, 512)
    tf = 512
    cw = 256

    nk_p = WINDOW + CHUNK
    bias_p = _rel_bias(rel_table, _bucket_index(CHUNK, nk_p), "rel_bias_prompt")
    bias_p = bias_p.transpose(2, 0, 1).reshape(nk_p, N_HEADS * CHUNK)
    bias_s = _rel_bias(rel_table, _bucket_index(S, WINDOW + S), "rel_bias_sample")
    bias_s = bias_s.reshape(N_KV, GROUP * S, WINDOW + S)
    lane = jnp.arange(cw) // HEAD_DIM
    ones = (lane[:, None] == lane[None, :]).astype(_BF)

    yp = x_prompt.reshape(Mp, D)
    ys = x_sample.reshape(Ms, D)
    conv_p, k_p, v_p, conv_s, k_s, v_s = [], [], [], [], [], []
    for l in range(depth):
        bf = lambda w: w[l].astype(_BF)
        vec = lambda g: g[l].reshape(1, -1)
        f1 = (vec(ffn1_norm), bf(ffn1_wg), bf(ffn1_wu), bf(ffn1_wd))
        f2 = (vec(ffn2_norm), bf(ffn2_wg), bf(ffn2_wu), bf(ffn2_wd))
        w_in_l, w_co, w_ao, w_o = bf(w_in), bf(w_conv_out), bf(w_attn_o), bf(w_out)
        w_p, w_pg = bf(w_ple), bf(w_ple_gate)
        gq = jnp.tile(q_norm[l], N_HEADS).reshape(1, dq)
        gk = jnp.tile(k_norm[l], N_KV).reshape(1, dk)
        sink = attn_sink[l].reshape(N_KV, GROUP, 1, 1)
        sink_p = jnp.repeat(attn_sink[l], CHUNK).reshape(1, N_HEADS * CHUNK)
        w_ao_p = w_ao.reshape(N_KV, GROUP, HEAD_DIM, D).transpose(1, 0, 2, 3).reshape(dq, D)
        sink_s = jnp.broadcast_to(sink, (N_KV, GROUP, S, 1)).reshape(N_KV, GROUP * S, 1)

        h1, n = _ffn(yp, *f1, vec(mix_norm), tm=tm_p, tf=tf, name="ffn1_prompt")
        ucb, ulast = _conv_prompt(n, w_in_l, conv_w[l], batch=B, tm=tm_p, cw=cw, name="conv_prompt")
        q, k, v = _qkv(n, w_in_l, gq, gk, ones, col_block=qkv_col // (dq + 2 * dk), tm=tm_p, name="qkv_prompt")
        o = _attn_prompt(q, k, v, bias_p, sink_p, batch=B, ta=tm_p, name="attn_prompt")
        h2 = _merge(h1, n, ucb, o, w_co, w_ao_p, w_in_l, w_o, gc_col=gc_col, ga_col=ga_col,
                    tm=tm_p, tn=tf, name="merge_prompt")
        h3 = _ffn(h2, *f2, None, tm=tm_p, tf=tf, name="ffn2_prompt")
        yp = _ple(h3, p_prompt[l].reshape(Mp, -1), vec(ple_norm), w_p, w_pg, tm=tm_p, name="ple_prompt")
        conv_p.append(ulast[:, 6:8])
        k_p.append(k.reshape(B, T, dk)[:, T - WINDOW:].reshape(B, WINDOW, N_KV, HEAD_DIM))
        v_p.append(v.reshape(B, T, dk)[:, T - WINDOW:].reshape(B, WINDOW, N_KV, HEAD_DIM))

        st = state_conv[l]
        zeros = jnp.zeros((Bs, S - 2, d_conv), _F32)
        e2 = jnp.concatenate([st, zeros], axis=1).reshape(Ms, d_conv)
        e1 = jnp.concatenate([st[:, 1:], zeros, zeros[:, :1]], axis=1).reshape(Ms, d_conv)
        h1, n = _ffn(ys, *f1, vec(mix_norm), tm=tm_s, tf=tf, name="ffn1_sample")
        ucb, u = _conv_sample(n, w_in_l, conv_w[l], e1, e2, seg=S, cw=cw, name="conv_sample")
        q, k, v = _qkv(n, w_in_l, gq, gk, ones, col_block=qkv_col // (dq + 2 * dk), tm=tm_s, name="qkv_sample")
        o = _attn_sample(q, k, v, cache_k[l].reshape(Bs, WINDOW, dk), cache_v[l].reshape(Bs, WINDOW, dk),
                         bias_s, sink_s, name="attn_sample")
        h2 = _merge(h1, n, ucb, o, w_co, w_ao, w_in_l, w_o, gc_col=gc_col, ga_col=ga_col,
                    tm=tm_s, tn=tf, name="merge_sample")
        h3 = _ffn(h2, *f2, None, tm=tm_s, tf=tf, name="ffn2_sample")
        ys = _ple(h3, p_sample[l].reshape(Ms, -1), vec(ple_norm), w_p, w_pg, tm=tm_s, name="ple_sample")
        conv_s.append(u.reshape(Bs, S, d_conv)[:, S - 2:])
        k_s.append(k.reshape(Bs, S, N_KV, HEAD_DIM))
        v_s.append(v.reshape(Bs, S, N_KV, HEAD_DIM))

    return (yp.reshape(B, T, D), ys.reshape(Bs, S, D), jnp.stack(conv_p), jnp.stack(k_p), jnp.stack(v_p),
            jnp.stack(conv_s), jnp.stack(k_s), jnp.stack(v_s))
```

```python
import functools
import math

import jax
import jax.numpy as jnp
from jax import lax
from jax.experimental import pallas as pl
from jax.experimental.pallas import tpu as pltpu

_BF = jnp.bfloat16
_F32 = jnp.float32

CHUNK = 64
WINDOW = 128
N_HEADS = 16
N_KV = 4
GROUP = N_HEADS // N_KV
HEAD_DIM = 64
N_BUCKETS = 32
MAX_DIST = 128
EPS = 1e-6
NEG = -1e30
QK_SCALE = HEAD_DIM ** -0.5

V7X_VMEM_BYTES = 64 * 1024 * 1024
VMEM_INTERNAL_BYTES = 12 * 1024 * 1024


def _vmem_limit(block_bytes, scratch_bytes=0):
    need = 2 * sum(block_bytes) + scratch_bytes + VMEM_INTERNAL_BYTES
    return int(min(need, V7X_VMEM_BYTES - 4 * 1024 * 1024))


def _nbytes(shape, dtype):
    return math.prod(shape) * jnp.dtype(dtype).itemsize


def _dot(a, b):
    return jnp.dot(a, b, preferred_element_type=_F32)


def _rms(x, g):
    ms = jnp.mean(x * x, axis=-1, keepdims=True)
    return x * lax.rsqrt(ms + EPS) * g


BF16_TILE_ROWS = 16


def _cast_specs(arrays, grid):
    steps = math.prod(grid)

    def flat(*ids):
        s = 0
        for k, g in zip(ids, grid):
            s = s * g + k
        return s

    in_specs, out_specs, out_shape, nbytes = [], [], [], 0
    for a in arrays:
        rows, cols = a.shape
        rb = max(BF16_TILE_ROWS, rows // steps)
        assert rows % rb == 0 and steps % (rows // rb) == 0, (a.shape, steps)
        rep = steps // (rows // rb)
        idx = lambda *ids, rep=rep: (flat(*ids) // rep, 0)
        in_specs.append(pl.BlockSpec((rb, cols), idx))
        out_specs.append(pl.BlockSpec((rb, cols), idx))
        out_shape.append(jax.ShapeDtypeStruct((rows, cols), _BF))
        nbytes += _nbytes((rb, cols), _F32) + _nbytes((rb, cols), _BF)
    return in_specs, out_specs, out_shape, nbytes


def _with_casts(body, n_in, n_out, n_cast):
    def kernel(*refs):
        outs_at = n_in + n_cast
        scratch_at = outs_at + n_out + n_cast
        for src, dst in zip(refs[n_in:outs_at], refs[outs_at + n_out:scratch_at]):
            dst[...] = src[...].astype(_BF)
        body(*refs[:n_in], *refs[outs_at:outs_at + n_out], *refs[scratch_at:])
    return kernel


def _ffn_kernel(*refs, emit_norm):
    if emit_norm:
        x_ref, g_ref, wg_ref, wu_ref, wd_ref, g2_ref, out_ref, n_ref, xn_ref = refs
    else:
        x_ref, g_ref, wg_ref, wu_ref, wd_ref, out_ref, xn_ref = refs
    j = pl.program_id(1)
    last = pl.num_programs(1) - 1

    @pl.when(j == 0)
    def _():
        xn_ref[...] = _rms(x_ref[...], g_ref[...]).astype(_BF)

    xn = xn_ref[...]
    gate = _dot(xn, wg_ref[...])
    up = _dot(xn, wu_ref[...])
    hid = (jax.nn.silu(gate) * up).astype(_BF)
    out_ref[...] = jnp.where(j == 0, 0.0, out_ref[...]) + _dot(hid, wd_ref[...])

    @pl.when(j == last)
    def _():
        h = x_ref[...] + 0.5 * out_ref[...]
        out_ref[...] = h
        if emit_norm:
            n_ref[...] = _rms(h, g2_ref[...]).astype(_BF)


def _ffn(x, g, wg, wu, wd, g2, *, tm, tf, name, casts=()):
    M, D = x.shape
    F = wg.shape[1]
    emit_norm = g2 is not None
    grid = (M // tm, F // tf)
    row = lambda i, j: (i, 0)
    in_specs = [
        pl.BlockSpec((tm, D), row),
        pl.BlockSpec((1, D), lambda i, j: (0, 0)),
        pl.BlockSpec((D, tf), lambda i, j: (0, j)),
        pl.BlockSpec((D, tf), lambda i, j: (0, j)),
        pl.BlockSpec((tf, D), lambda i, j: (j, 0)),
    ]
    args = [x, g, wg, wu, wd]
    out_shape = [jax.ShapeDtypeStruct((M, D), _F32)]
    out_specs = [pl.BlockSpec((tm, D), row)]
    blocks = [_nbytes((tm, D), _F32) * 2, 3 * _nbytes((D, tf), _BF)]
    if emit_norm:
        in_specs.append(pl.BlockSpec((1, D), lambda i, j: (0, 0)))
        args.append(g2)
        out_shape.append(jax.ShapeDtypeStruct((M, D), _BF))
        out_specs.append(pl.BlockSpec((tm, D), row))
        blocks.append(_nbytes((tm, D), _BF))
    c_in, c_out, c_shape, c_bytes = _cast_specs(casts, grid)
    blocks.append(c_bytes)
    res = pl.pallas_call(
        _with_casts(functools.partial(_ffn_kernel, emit_norm=emit_norm), len(in_specs), len(out_specs), len(casts)),
        grid=grid,
        in_specs=in_specs + c_in,
        out_specs=out_specs + c_out,
        out_shape=out_shape + c_shape,
        scratch_shapes=[pltpu.VMEM((tm, D), _BF)],
        compiler_params=pltpu.CompilerParams(
            dimension_semantics=("arbitrary", "arbitrary"),
            vmem_limit_bytes=_vmem_limit(blocks, _nbytes((tm, D), _BF)),
        ),
        name=name,
    )(*args, *casts)
    return res


def _conv_kernel(*refs, carry_rows, seg, tiles_per_seg, d_conv, cw):
    if carry_rows:
        n_ref, w_ref, cwt_ref, ucb_ref, ulast_ref, carry_ref = refs
    else:
        n_ref, w_ref, cwt_ref, e1_ref, e2_ref, ucb_ref, u_ref = refs
    tm = n_ref.shape[0]
    n = n_ref[...]
    if carry_rows:
        @pl.when(pl.program_id(0) % tiles_per_seg == 0)
        def _():
            carry_ref[...] = jnp.zeros_like(carry_ref)

    for c in range(d_conv // cw):
        lo = c * cw
        cb = _dot(n, w_ref[:, lo:lo + cw])
        cc = _dot(n, w_ref[:, d_conv + lo:d_conv + lo + cw])
        cv = _dot(n, w_ref[:, 2 * d_conv + lo:2 * d_conv + lo + cw])
        u = cc * cv
        r1 = pltpu.roll(u, 1, 0)
        r2 = pltpu.roll(u, 2, 0)
        if carry_rows:
            prev = carry_ref[:, lo:lo + cw]
            rows8 = lax.broadcasted_iota(jnp.int32, (8, cw), 0)
            h1 = jnp.where(rows8 < 1, pltpu.roll(prev, 1, 0), r1[:8])
            h2 = jnp.where(rows8 < 2, pltpu.roll(prev, 2, 0), r2[:8])
            r1 = jnp.concatenate([h1, r1[8:]], axis=0)
            r2 = jnp.concatenate([h2, r2[8:]], axis=0)
            carry_ref[:, lo:lo + cw] = u[tm - 8:]
            ulast_ref[0, :, lo:lo + cw] = u[tm - 8:]
        else:
            pos = lax.broadcasted_iota(jnp.int32, (tm, cw), 0) % seg
            r1 = jnp.where(pos < 1, e1_ref[:, lo:lo + cw], r1)
            r2 = jnp.where(pos < 2, e2_ref[:, lo:lo + cw], r2)
            u_ref[:, lo:lo + cw] = u
        w0 = cwt_ref[0:1, lo:lo + cw]
        w1 = cwt_ref[1:2, lo:lo + cw]
        w2 = cwt_ref[2:3, lo:lo + cw]
        conv = w0 * r2 + w1 * r1 + w2 * u
        ucb_ref[:, lo:lo + cw] = (cb * conv).astype(_BF)


def _conv_prompt(n, w_in, conv_w, *, batch, tm, cw, name, casts=()):
    M, D = n.shape
    d_conv = conv_w.shape[1]
    tiles_per_seg = (M // batch) // tm
    grid = (M // tm,)
    c_in, c_out, c_shape, c_bytes = _cast_specs(casts, grid)
    blocks = [_nbytes((tm, D), _BF), _nbytes((D, 3 * d_conv), _BF), _nbytes((tm, d_conv), _BF), c_bytes]
    body = functools.partial(_conv_kernel, carry_rows=True, seg=None, tiles_per_seg=tiles_per_seg,
                             d_conv=d_conv, cw=cw)
    return pl.pallas_call(
        _with_casts(body, 3, 2, len(casts)),
        grid=grid,
        in_specs=[
            pl.BlockSpec((tm, D), lambda i: (i, 0)),
            pl.BlockSpec((D, 3 * d_conv), lambda i: (0, 0)),
            pl.BlockSpec((3, d_conv), lambda i: (0, 0)),
        ] + c_in,
        out_specs=[
            pl.BlockSpec((tm, d_conv), lambda i: (i, 0)),
            pl.BlockSpec((1, 8, d_conv), lambda i: (i // tiles_per_seg, 0, 0)),
        ] + c_out,
        out_shape=[
            jax.ShapeDtypeStruct((M, d_conv), _BF),
            jax.ShapeDtypeStruct((batch, 8, d_conv), _F32),
        ] + c_shape,
        scratch_shapes=[pltpu.VMEM((8, d_conv), _F32)],
        compiler_params=pltpu.CompilerParams(
            dimension_semantics=("arbitrary",),
            vmem_limit_bytes=_vmem_limit(blocks),
        ),
        name=name,
    )(n, w_in, conv_w, *casts)


def _conv_sample(n, w_in, conv_w, e1, e2, *, seg, cw, name):
    M, D = n.shape
    d_conv = conv_w.shape[1]
    blocks = [_nbytes((M, D), _BF), _nbytes((D, 3 * d_conv), _BF), 4 * _nbytes((M, d_conv), _F32)]
    full = lambda i: (0, 0)
    return pl.pallas_call(
        functools.partial(_conv_kernel, carry_rows=False, seg=seg, tiles_per_seg=None,
                          d_conv=d_conv, cw=cw),
        grid=(1,),
        in_specs=[
            pl.BlockSpec((M, D), full),
            pl.BlockSpec((D, 3 * d_conv), full),
            pl.BlockSpec((3, d_conv), full),
            pl.BlockSpec((M, d_conv), full),
            pl.BlockSpec((M, d_conv), full),
        ],
        out_specs=[pl.BlockSpec((M, d_conv), full), pl.BlockSpec((M, d_conv), full)],
        out_shape=[
            jax.ShapeDtypeStruct((M, d_conv), _BF),
            jax.ShapeDtypeStruct((M, d_conv), _F32),
        ],
        compiler_params=pltpu.CompilerParams(
            dimension_semantics=("arbitrary",),
            vmem_limit_bytes=_vmem_limit(blocks),
        ),
        name=name,
    )(n, w_in, conv_w, e1, e2)


def _qkv_kernel(n_ref, w_ref, gq_ref, gk_ref, ones_ref, q_ref, k_ref, v_ref, *, cw):
    ones = ones_ref[...]
    dq = q_ref.shape[1]
    dk = k_ref.shape[1]
    p = _dot(n_ref[...], w_ref[...])
    x = p[:, :dq + dk]
    sq = x * x
    hi = sq.astype(_BF)
    lo = (sq - hi.astype(_F32)).astype(_BF)
    ss = jnp.concatenate(
        [_dot(hi[:, c * cw:(c + 1) * cw], ones) + _dot(lo[:, c * cw:(c + 1) * cw], ones)
         for c in range((dq + dk) // cw)], axis=1)
    xr = x * lax.rsqrt(ss * (1.0 / HEAD_DIM) + EPS)
    q_ref[...] = (xr[:, :dq] * gq_ref[...] * QK_SCALE).astype(_BF)
    k_ref[...] = xr[:, dq:] * gk_ref[...]
    v_ref[...] = p[:, dq + dk:]


def _qkv(n, w_in, gq, gk, ones, *, col_block, tm, name, casts=()):
    M, D = n.shape
    cw = ones.shape[0]
    dq, dk = N_HEADS * HEAD_DIM, N_KV * HEAD_DIM
    wcols = dq + 2 * dk
    grid = (M // tm,)
    c_in, c_out, c_shape, c_bytes = _cast_specs(casts, grid)
    blocks = [_nbytes((tm, D), _BF), _nbytes((D, wcols), _BF), _nbytes((tm, dq), _BF), 2 * _nbytes((tm, dk), _F32),
              c_bytes]
    const = lambda i: (0, 0)
    row = lambda i: (i, 0)
    return pl.pallas_call(
        _with_casts(functools.partial(_qkv_kernel, cw=cw), 5, 3, len(casts)),
        grid=grid,
        in_specs=[
            pl.BlockSpec((tm, D), row),
            pl.BlockSpec((D, wcols), lambda i: (0, col_block)),
            pl.BlockSpec((1, dq), const),
            pl.BlockSpec((1, dk), const),
            pl.BlockSpec((cw, cw), const),
        ] + c_in,
        out_specs=[pl.BlockSpec((tm, dq), row), pl.BlockSpec((tm, dk), row), pl.BlockSpec((tm, dk), row)] + c_out,
        out_shape=[
            jax.ShapeDtypeStruct((M, dq), _BF),
            jax.ShapeDtypeStruct((M, dk), _F32),
            jax.ShapeDtypeStruct((M, dk), _F32),
        ] + c_shape,
        compiler_params=pltpu.CompilerParams(
            dimension_semantics=("arbitrary",),
            vmem_limit_bytes=_vmem_limit(blocks),
        ),
        name=name,
    )(n, w_in, gq, gk, ones, *casts)


def _bias_kernel(tab_ref, bkt_ref, out_ref):
    bkt = bkt_ref[...]
    for h in range(out_ref.shape[0]):
        acc = jnp.zeros(bkt.shape, _F32)
        for b in range(tab_ref.shape[0]):
            acc = jnp.where(bkt == b, tab_ref[b, h], acc)
        out_ref[h] = acc


def _rel_bias(table, bucket, name):
    nq, nk = bucket.shape
    return pl.pallas_call(
        _bias_kernel,
        in_specs=[pl.BlockSpec(memory_space=pltpu.SMEM), pl.BlockSpec((nq, nk), lambda: (0, 0))],
        out_specs=pl.BlockSpec((table.shape[1], nq, nk), lambda: (0, 0, 0)),
        out_shape=jax.ShapeDtypeStruct((table.shape[1], nq, nk), _F32),
        name=name,
    )(table, bucket)


def _t5_bucket(rel):
    nb = N_BUCKETS // 2
    max_exact = nb // 2
    ret = jnp.where(rel > 0, nb, 0)
    n = jnp.abs(rel)
    nf = jnp.maximum(n, 1).astype(jnp.float32)
    large = max_exact + (jnp.log(nf / max_exact) / math.log(MAX_DIST / max_exact) * (nb - max_exact)).astype(jnp.int32)
    large = jnp.minimum(large, nb - 1)
    return ret + jnp.where(n < max_exact, n, large)


def _bucket_index(n_q, n_k):
    i = jnp.arange(n_q)[:, None]
    j = jnp.arange(n_k)[None, :]
    return _t5_bucket(j - WINDOW - i).astype(jnp.int32)


def _attn_group(qg, kk, vv, bias, sink, valid):
    lg = lax.dot_general(qg, kk, (((1,), (1,)), ((), ())), preferred_element_type=_F32) + bias
    if valid is not None:
        lg = jnp.where(valid, lg, NEG)
    m = jnp.maximum(jnp.max(lg, axis=-1, keepdims=True), sink)
    e = jnp.exp(lg - m)
    den = jnp.sum(e, axis=-1, keepdims=True) + jnp.exp(sink - m)
    p = (e * (1.0 / den)).astype(_BF)
    return _dot(p, vv)


def _attn_rows(qc, kw, vw, bias_ref, sink_ref, valid):
    nq = qc.shape[0]
    outs = []
    for kv in range(N_KV):
        qg = jnp.concatenate(
            [qc[:, (kv * GROUP + g) * HEAD_DIM:(kv * GROUP + g + 1) * HEAD_DIM] for g in range(GROUP)], axis=0)
        kk = kw[:, kv * HEAD_DIM:(kv + 1) * HEAD_DIM]
        vv = vw[:, kv * HEAD_DIM:(kv + 1) * HEAD_DIM]
        og = _attn_group(qg, kk, vv, bias_ref[kv], sink_ref[kv], valid)
        outs.extend(og[g * nq:(g + 1) * nq] for g in range(GROUP))
    return jnp.concatenate(outs, axis=1)


def _attn_chunk_keys_major(qc, kw, vw, bias_t, sink_t, first_valid_key):
    nq = qc.shape[0]
    gq = GROUP * nq
    nt = (((1,), (1,)), ((), ()))
    tn = (((0,), (0,)), ((), ()))
    logits = []
    for kv in range(N_KV):
        qg = jnp.concatenate(
            [qc[:, (kv * GROUP + g) * HEAD_DIM:(kv * GROUP + g + 1) * HEAD_DIM] for g in range(GROUP)], axis=0)
        kk = kw[:, kv * HEAD_DIM:(kv + 1) * HEAD_DIM]
        logits.append(lax.dot_general(kk, qg, nt, preferred_element_type=_F32))
    lg = jnp.concatenate(logits, axis=1) + bias_t
    key = lax.broadcasted_iota(jnp.int32, lg.shape, 0)
    lg = jnp.where(key >= first_valid_key, lg, NEG)
    m = jnp.maximum(jnp.max(lg, axis=0, keepdims=True), sink_t)
    e = jnp.exp(lg - m)
    den = jnp.sum(e, axis=0, keepdims=True) + jnp.exp(sink_t - m)
    p = (e * (1.0 / den)).astype(_BF)
    outs = []
    for kv in range(N_KV):
        vv = vw[:, kv * HEAD_DIM:(kv + 1) * HEAD_DIM]
        outs.append(lax.dot_general(vv, p[:, kv * gq:(kv + 1) * gq], tn, preferred_element_type=_F32))
    return jnp.concatenate(outs, axis=0).T


def _attn_prompt_kernel(q_ref, kp_ref, kc_ref, vp_ref, vc_ref, bias_ref, sink_ref, o_ref, kwin, vwin,
                        *, tiles_per_batch):
    ta = q_ref.shape[0]
    nk = WINDOW + CHUNK
    dkv = N_KV * HEAD_DIM
    kwin[0:WINDOW, :] = kp_ref[...].astype(_BF)
    kwin[WINDOW:, :] = kc_ref[...].astype(_BF)
    vwin[0:WINDOW, :] = vp_ref[...].astype(_BF)
    vwin[WINDOW:, :] = vc_ref[...].astype(_BF)
    base = (pl.program_id(0) % tiles_per_batch) * ta - WINDOW

    def chunk(c, carry):
        r = pl.multiple_of(c * CHUNK, CHUNK)
        o = _attn_chunk_keys_major(q_ref[pl.ds(r, CHUNK), :], kwin[pl.ds(r, nk), :], vwin[pl.ds(r, nk), :],
                                   bias_ref[...], sink_ref[...], -(base + r))
        for g in range(GROUP):
            o_ref[pl.ds(r, CHUNK), g * dkv:(g + 1) * dkv] = o[g * CHUNK:(g + 1) * CHUNK].astype(_BF)
        return carry

    lax.fori_loop(0, ta // CHUNK, chunk, 0, unroll=2)


def _attn_prompt(q, k, v, bias, sink, *, batch, ta, name, casts=()):
    M, dq = q.shape
    dk = k.shape[1]
    tiles_per_batch = (M // batch) // ta
    wb = ta // WINDOW
    nk = WINDOW + CHUNK

    def prev(i):
        return (jnp.where(i % tiles_per_batch == 0, i * wb, i * wb - 1), 0)

    row = lambda i: (i, 0)
    grid = (M // ta,)
    c_in, c_out, c_shape, c_bytes = _cast_specs(casts, grid)
    blocks = [_nbytes((ta, dq), _BF) * 2, 2 * _nbytes((WINDOW + ta, dk), _F32),
              _nbytes(bias.shape, _F32), _nbytes((8, N_HEADS * CHUNK), _F32), c_bytes]
    return pl.pallas_call(
        _with_casts(functools.partial(_attn_prompt_kernel, tiles_per_batch=tiles_per_batch), 7, 1, len(casts)),
        grid=grid,
        in_specs=[
            pl.BlockSpec((ta, dq), row),
            pl.BlockSpec((WINDOW, dk), prev),
            pl.BlockSpec((ta, dk), row),
            pl.BlockSpec((WINDOW, dk), prev),
            pl.BlockSpec((ta, dk), row),
            pl.BlockSpec((nk, N_HEADS * CHUNK), lambda i: (0, 0)),
            pl.BlockSpec((1, N_HEADS * CHUNK), lambda i: (0, 0)),
        ] + c_in,
        out_specs=[pl.BlockSpec((ta, dq), row)] + c_out,
        out_shape=[jax.ShapeDtypeStruct((M, dq), _BF)] + c_shape,
        scratch_shapes=[pltpu.VMEM((WINDOW + ta, dk), _BF), pltpu.VMEM((WINDOW + ta, dk), _BF)],
        compiler_params=pltpu.CompilerParams(
            dimension_semantics=("arbitrary",),
            vmem_limit_bytes=_vmem_limit(blocks, 2 * _nbytes((WINDOW + ta, dk), _BF)),
        ),
        name=name,
    )(q, k, k, v, v, bias, sink, *casts)


def _attn_sample_kernel(q_ref, ck_ref, cv_ref, k_ref, v_ref, bias_ref, sink_ref, o_ref):
    kw = jnp.concatenate([ck_ref[0].astype(_BF), k_ref[...].astype(_BF)], axis=0)
    vw = jnp.concatenate([cv_ref[0].astype(_BF), v_ref[...].astype(_BF)], axis=0)
    o_ref[...] = _attn_rows(q_ref[...], kw, vw, bias_ref, sink_ref, None).astype(_BF)


def _attn_sample(q, k, v, cache_k, cache_v, bias, sink, *, name):
    B, W, dk = cache_k.shape
    M, dq = q.shape
    S = M // B
    row = lambda b: (b, 0)
    return pl.pallas_call(
        _attn_sample_kernel,
        grid=(B,),
        in_specs=[
            pl.BlockSpec((S, dq), row),
            pl.BlockSpec((1, W, dk), lambda b: (b, 0, 0)),
            pl.BlockSpec((1, W, dk), lambda b: (b, 0, 0)),
            pl.BlockSpec((S, dk), row),
            pl.BlockSpec((S, dk), row),
            pl.BlockSpec(bias.shape, lambda b: (0, 0, 0)),
            pl.BlockSpec(sink.shape, lambda b: (0, 0, 0)),
        ],
        out_specs=pl.BlockSpec((S, dq), row),
        out_shape=jax.ShapeDtypeStruct((M, dq), _BF),
        compiler_params=pltpu.CompilerParams(dimension_semantics=("arbitrary",)),
        name=name,
    )(q, cache_k, cache_v, k, v, bias, sink)


def _merge_kernel(h_ref, n_ref, ucb_ref, o_ref, wco_ref, wao_ref, wgc_ref, wga_ref, wout_ref, out_ref):
    j = pl.program_id(1)
    last = pl.num_programs(1) - 1
    n = n_ref[...]
    y_conv = _dot(ucb_ref[...], wco_ref[...])
    y_attn = _dot(o_ref[...], wao_ref[...])
    g_conv = jax.nn.sigmoid(_dot(n, wgc_ref[...]))
    g_attn = jax.nn.sigmoid(_dot(n, wga_ref[...]))
    mix = (g_conv * y_conv + g_attn * y_attn).astype(_BF)
    out_ref[...] = jnp.where(j == 0, 0.0, out_ref[...]) + _dot(mix, wout_ref[...])

    @pl.when(j == last)
    def _():
        out_ref[...] = h_ref[...] + out_ref[...]


def _merge(h, n, ucb, o, w_conv_out, w_attn_o, w_in, w_out, *, gc_col, ga_col, tm, tn, name, casts=()):
    M, D = h.shape
    grid = (M // tm, D // tn)
    dc = ucb.shape[1]
    da = o.shape[1]
    row = lambda i, j: (i, 0)
    col = lambda i, j: (0, j)
    blocks = [2 * _nbytes((tm, D), _F32), _nbytes((tm, D), _BF), _nbytes((tm, dc), _BF), _nbytes((tm, da), _BF),
              _nbytes((dc + da + 3 * D, tn), _BF)]
    c_in, c_out, c_shape, c_bytes = _cast_specs(casts, grid)
    blocks.append(c_bytes)
    return pl.pallas_call(
        _with_casts(_merge_kernel, 9, 1, len(casts)),
        grid=grid,
        in_specs=[
            pl.BlockSpec((tm, D), row),
            pl.BlockSpec((tm, D), row),
            pl.BlockSpec((tm, dc), row),
            pl.BlockSpec((tm, da), row),
            pl.BlockSpec((dc, tn), col),
            pl.BlockSpec((da, tn), col),
            pl.BlockSpec((D, tn), lambda i, j: (0, gc_col // tn + j)),
            pl.BlockSpec((D, tn), lambda i, j: (0, ga_col // tn + j)),
            pl.BlockSpec((tn, D), lambda i, j: (j, 0)),
        ] + c_in,
        out_specs=[pl.BlockSpec((tm, D), row)] + c_out,
        out_shape=[jax.ShapeDtypeStruct((M, D), _F32)] + c_shape,
        compiler_params=pltpu.CompilerParams(
            dimension_semantics=("arbitrary", "arbitrary"),
            vmem_limit_bytes=_vmem_limit(blocks),
        ),
        name=name,
    )(h, n, ucb, o, w_conv_out, w_attn_o, w_in, w_in, w_out, *casts)


def _ple_kernel(h_ref, pe_ref, g_ref, wp_ref, wg_ref, out_ref):
    h = h_ref[...]
    gate = jax.nn.sigmoid(_dot(_rms(h, g_ref[...]).astype(_BF), wg_ref[...]))
    emb = _dot(pe_ref[...].astype(_BF), wp_ref[...])
    out_ref[...] = h + emb * gate


def _ple(h, pe, g, w_ple, w_gate, *, tm, name):
    M, D = h.shape
    dp = pe.shape[1]
    row = lambda i: (i, 0)
    const = lambda i: (0, 0)
    blocks = [2 * _nbytes((tm, D), _F32), _nbytes((tm, dp), _F32), _nbytes((dp + D, D), _BF)]
    return pl.pallas_call(
        _ple_kernel,
        grid=(M // tm,),
        in_specs=[
            pl.BlockSpec((tm, D), row),
            pl.BlockSpec((tm, dp), row),
            pl.BlockSpec((1, D), const),
            pl.BlockSpec((dp, D), const),
            pl.BlockSpec((D, D), const),
        ],
        out_specs=pl.BlockSpec((tm, D), row),
        out_shape=jax.ShapeDtypeStruct((M, D), _F32),
        compiler_params=pltpu.CompilerParams(
            dimension_semantics=("arbitrary",),
            vmem_limit_bytes=_vmem_limit(blocks),
        ),
        name=name,
    )(h, pe, g, w_ple, w_gate)


def _row_tile(m, want):
    return want if m % want == 0 else m


def kernel(x_prompt, x_sample, p_prompt, p_sample, state_conv, cache_k, cache_v, rel_table, ffn1_norm, ffn1_wg, ffn1_wu, ffn1_wd, mix_norm, w_in, conv_w, q_norm, k_norm, attn_sink, w_conv_out, w_attn_o, w_out, ffn2_norm, ffn2_wg, ffn2_wu, ffn2_wd, ple_norm, w_ple, w_ple_gate):
    B, T, D = x_prompt.shape
    Bs, S, _ = x_sample.shape
    depth = ffn1_wg.shape[0]
    d_conv = conv_w.shape[2]
    dq, dk = N_HEADS * HEAD_DIM, N_KV * HEAD_DIM
    qkv_col = 3 * d_conv
    gc_col = qkv_col + dq + 2 * dk
    ga_col = gc_col + D
    Mp, Ms = B * T, Bs * S
    tm_p = _row_tile(Mp, 512)
    tm_s = _row_tile(Ms, 512)
    tf = 512
    cw = 256

    nk_p = WINDOW + CHUNK
    bias_p = _rel_bias(rel_table, _bucket_index(CHUNK, nk_p), "rel_bias_prompt")
    bias_p = bias_p.transpose(2, 0, 1).reshape(nk_p, N_HEADS * CHUNK)
    bias_s = _rel_bias(rel_table, _bucket_index(S, WINDOW + S), "rel_bias_sample")
    bias_s = bias_s.reshape(N_KV, GROUP * S, WINDOW + S)
    lane = jnp.arange(cw) // HEAD_DIM
    ones = (lane[:, None] == lane[None, :]).astype(_BF)

    yp = x_prompt.reshape(Mp, D)
    ys = x_sample.reshape(Ms, D)
    conv_p, k_p, v_p, conv_s, k_s, v_s = [], [], [], [], [], []
    for l in range(depth):
        bf = lambda w: w[l].astype(_BF)
        vec = lambda g: g[l].reshape(1, -1)
        f1 = (vec(ffn1_norm), bf(ffn1_wg), bf(ffn1_wu), bf(ffn1_wd))
        gq = jnp.tile(q_norm[l], N_HEADS).reshape(1, dq)
        gk = jnp.tile(k_norm[l], N_KV).reshape(1, dk)
        sink = attn_sink[l].reshape(N_KV, GROUP, 1, 1)
        sink_p = jnp.repeat(attn_sink[l], CHUNK).reshape(1, N_HEADS * CHUNK)
        sink_s = jnp.broadcast_to(sink, (N_KV, GROUP, S, 1)).reshape(N_KV, GROUP * S, 1)

        h1, n, w_in_l = _ffn(yp, *f1, vec(mix_norm), tm=tm_p, tf=tf, name="ffn1_prompt", casts=(w_in[l],))
        ucb, ulast, f2_wg, f2_wu = _conv_prompt(n, w_in_l, conv_w[l], batch=B, tm=tm_p, cw=cw, name="conv_prompt",
                                                casts=(ffn2_wg[l], ffn2_wu[l]))
        q, k, v, w_co, w_ao, w_o = _qkv(n, w_in_l, gq, gk, ones, col_block=qkv_col // (dq + 2 * dk), tm=tm_p,
                                        name="qkv_prompt", casts=(w_conv_out[l], w_attn_o[l], w_out[l]))
        o, f2_wd, w_p, w_pg = _attn_prompt(q, k, v, bias_p, sink_p, batch=B, ta=tm_p, name="attn_prompt",
                                           casts=(ffn2_wd[l], w_ple[l], w_ple_gate[l]))
        w_ao_p = w_ao.reshape(N_KV, GROUP, HEAD_DIM, D).transpose(1, 0, 2, 3).reshape(dq, D)
        h2, = _merge(h1, n, ucb, o, w_co, w_ao_p, w_in_l, w_o, gc_col=gc_col, ga_col=ga_col, tm=tm_p, tn=tf,
                     name="merge_prompt")
        f2 = (vec(ffn2_norm), f2_wg, f2_wu, f2_wd)
        h3, = _ffn(h2, *f2, None, tm=tm_p, tf=tf, name="ffn2_prompt")
        yp = _ple(h3, p_prompt[l].reshape(Mp, -1), vec(ple_norm), w_p, w_pg, tm=tm_p, name="ple_prompt")
        conv_p.append(ulast[:, 6:8])
        k_p.append(k.reshape(B, T, dk)[:, T - WINDOW:].reshape(B, WINDOW, N_KV, HEAD_DIM))
        v_p.append(v.reshape(B, T, dk)[:, T - WINDOW:].reshape(B, WINDOW, N_KV, HEAD_DIM))

        st = state_conv[l]
        zeros = jnp.zeros((Bs, S - 2, d_conv), _F32)
        e2 = jnp.concatenate([st, zeros], axis=1).reshape(Ms, d_conv)
        e1 = jnp.concatenate([st[:, 1:], zeros, zeros[:, :1]], axis=1).reshape(Ms, d_conv)
        h1, n = _ffn(ys, *f1, vec(mix_norm), tm=tm_s, tf=tf, name="ffn1_sample")
        ucb, u = _conv_sample(n, w_in_l, conv_w[l], e1, e2, seg=S, cw=cw, name="conv_sample")
        q, k, v = _qkv(n, w_in_l, gq, gk, ones, col_block=qkv_col // (dq + 2 * dk), tm=tm_s, name="qkv_sample")
        o = _attn_sample(q, k, v, cache_k[l].reshape(Bs, WINDOW, dk), cache_v[l].reshape(Bs, WINDOW, dk),
                         bias_s, sink_s, name="attn_sample")
        h2, = _merge(h1, n, ucb, o, w_co, w_ao, w_in_l, w_o, gc_col=gc_col, ga_col=ga_col,
                     tm=tm_s, tn=tf, name="merge_sample")
        h3, = _ffn(h2, *f2, None, tm=tm_s, tf=tf, name="ffn2_sample")
        ys = _ple(h3, p_sample[l].reshape(Ms, -1), vec(ple_norm), w_p, w_pg, tm=tm_s, name="ple_sample")
        conv_s.append(u.reshape(Bs, S, d_conv)[:, S - 2:])
        k_s.append(k.reshape(Bs, S, N_KV, HEAD_DIM))
        v_s.append(v.reshape(Bs, S, N_KV, HEAD_DIM))

    return (yp.reshape(B, T, D), ys.reshape(Bs, S, D), jnp.stack(conv_p), jnp.stack(k_p), jnp.stack(v_p),
            jnp.stack(conv_s), jnp.stack(k_s), jnp.stack(v_s))
```

```python
import functools
import math

import jax
import jax.numpy as jnp
from jax import lax
from jax.experimental import pallas as pl
from jax.experimental.pallas import tpu as pltpu

_BF = jnp.bfloat16
_F32 = jnp.float32

CHUNK = 64
WINDOW = 128
N_HEADS = 16
N_KV = 4
GROUP = N_HEADS // N_KV
HEAD_DIM = 64
N_BUCKETS = 32
MAX_DIST = 128
EPS = 1e-6
NEG = -1e30
QK_SCALE = HEAD_DIM ** -0.5

V7X_VMEM_BYTES = 64 * 1024 * 1024
VMEM_INTERNAL_BYTES = 12 * 1024 * 1024


def _vmem_limit(block_bytes, scratch_bytes=0):
    need = 2 * sum(block_bytes) + scratch_bytes + VMEM_INTERNAL_BYTES
    return int(min(need, V7X_VMEM_BYTES - 4 * 1024 * 1024))


def _nbytes(shape, dtype):
    return math.prod(shape) * jnp.dtype(dtype).itemsize


def _dot(a, b):
    return jnp.dot(a, b, preferred_element_type=_F32)


def _rms(x, g):
    ms = jnp.mean(x * x, axis=-1, keepdims=True)
    return x * lax.rsqrt(ms + EPS) * g


BF16_TILE_ROWS = 16


def _cast_specs(arrays, grid):
    steps = math.prod(grid)

    def flat(*ids):
        s = 0
        for k, g in zip(ids, grid):
            s = s * g + k
        return s

    in_specs, out_specs, out_shape, nbytes = [], [], [], 0
    for a in arrays:
        rows, cols = a.shape
        rb = max(BF16_TILE_ROWS, rows // steps)
        assert rows % rb == 0 and steps % (rows // rb) == 0, (a.shape, steps)
        rep = steps // (rows // rb)
        idx = lambda *ids, rep=rep: (flat(*ids) // rep, 0)
        in_specs.append(pl.BlockSpec((rb, cols), idx))
        out_specs.append(pl.BlockSpec((rb, cols), idx))
        out_shape.append(jax.ShapeDtypeStruct((rows, cols), _BF))
        nbytes += _nbytes((rb, cols), _F32) + _nbytes((rb, cols), _BF)
    return in_specs, out_specs, out_shape, nbytes


def _with_casts(body, n_in, n_out, n_cast):
    def kernel(*refs):
        outs_at = n_in + n_cast
        scratch_at = outs_at + n_out + n_cast
        for src, dst in zip(refs[n_in:outs_at], refs[outs_at + n_out:scratch_at]):
            dst[...] = src[...].astype(_BF)
        body(*refs[:n_in], *refs[outs_at:outs_at + n_out], *refs[scratch_at:])
    return kernel


def _ffn_kernel(*refs, emit_norm):
    if emit_norm:
        x_ref, g_ref, wg_ref, wu_ref, wd_ref, g2_ref, out_ref, n_ref, xn_ref = refs
    else:
        x_ref, g_ref, wg_ref, wu_ref, wd_ref, out_ref, xn_ref = refs
    j = pl.program_id(1)
    last = pl.num_programs(1) - 1

    @pl.when(j == 0)
    def _():
        xn_ref[...] = _rms(x_ref[...], g_ref[...]).astype(_BF)

    xn = xn_ref[...]
    gate = _dot(xn, wg_ref[...])
    up = _dot(xn, wu_ref[...])
    hid = (jax.nn.silu(gate) * up).astype(_BF)
    out_ref[...] = jnp.where(j == 0, 0.0, out_ref[...]) + _dot(hid, wd_ref[...])

    @pl.when(j == last)
    def _():
        h = x_ref[...] + 0.5 * out_ref[...]
        out_ref[...] = h
        if emit_norm:
            n_ref[...] = _rms(h, g2_ref[...]).astype(_BF)


def _ffn(x, g, wg, wu, wd, g2, *, tm, tf, name, casts=()):
    M, D = x.shape
    F = wg.shape[1]
    emit_norm = g2 is not None
    grid = (M // tm, F // tf)
    row = lambda i, j: (i, 0)
    in_specs = [
        pl.BlockSpec((tm, D), row),
        pl.BlockSpec((1, D), lambda i, j: (0, 0)),
        pl.BlockSpec((D, tf), lambda i, j: (0, j)),
        pl.BlockSpec((D, tf), lambda i, j: (0, j)),
        pl.BlockSpec((tf, D), lambda i, j: (j, 0)),
    ]
    args = [x, g, wg, wu, wd]
    out_shape = [jax.ShapeDtypeStruct((M, D), _F32)]
    out_specs = [pl.BlockSpec((tm, D), row)]
    blocks = [_nbytes((tm, D), _F32) * 2, 3 * _nbytes((D, tf), _BF)]
    if emit_norm:
        in_specs.append(pl.BlockSpec((1, D), lambda i, j: (0, 0)))
        args.append(g2)
        out_shape.append(jax.ShapeDtypeStruct((M, D), _BF))
        out_specs.append(pl.BlockSpec((tm, D), row))
        blocks.append(_nbytes((tm, D), _BF))
    c_in, c_out, c_shape, c_bytes = _cast_specs(casts, grid)
    blocks.append(c_bytes)
    res = pl.pallas_call(
        _with_casts(functools.partial(_ffn_kernel, emit_norm=emit_norm), len(in_specs), len(out_specs), len(casts)),
        grid=grid,
        in_specs=in_specs + c_in,
        out_specs=out_specs + c_out,
        out_shape=out_shape + c_shape,
        scratch_shapes=[pltpu.VMEM((tm, D), _BF)],
        compiler_params=pltpu.CompilerParams(
            dimension_semantics=("arbitrary", "arbitrary"),
            vmem_limit_bytes=_vmem_limit(blocks, _nbytes((tm, D), _BF)),
        ),
        name=name,
    )(*args, *casts)
    return res


def _conv_kernel(*refs, carry_rows, seg, tiles_per_seg, d_conv, cw):
    if carry_rows:
        n_ref, w_ref, cwt_ref, ucb_ref, ulast_ref, carry_ref = refs
    else:
        n_ref, w_ref, cwt_ref, e1_ref, e2_ref, ucb_ref, u_ref = refs
    tm = n_ref.shape[0]
    n = n_ref[...]
    if carry_rows:
        @pl.when(pl.program_id(0) % tiles_per_seg == 0)
        def _():
            carry_ref[...] = jnp.zeros_like(carry_ref)

    for c in range(d_conv // cw):
        lo = c * cw
        cb = _dot(n, w_ref[:, lo:lo + cw])
        cc = _dot(n, w_ref[:, d_conv + lo:d_conv + lo + cw])
        cv = _dot(n, w_ref[:, 2 * d_conv + lo:2 * d_conv + lo + cw])
        u = cc * cv
        r1 = pltpu.roll(u, 1, 0)
        r2 = pltpu.roll(u, 2, 0)
        if carry_rows:
            prev = carry_ref[:, lo:lo + cw]
            rows8 = lax.broadcasted_iota(jnp.int32, (8, cw), 0)
            h1 = jnp.where(rows8 < 1, pltpu.roll(prev, 1, 0), r1[:8])
            h2 = jnp.where(rows8 < 2, pltpu.roll(prev, 2, 0), r2[:8])
            r1 = jnp.concatenate([h1, r1[8:]], axis=0)
            r2 = jnp.concatenate([h2, r2[8:]], axis=0)
            carry_ref[:, lo:lo + cw] = u[tm - 8:]
            ulast_ref[0, :, lo:lo + cw] = u[tm - 8:]
        else:
            pos = lax.broadcasted_iota(jnp.int32, (tm, cw), 0) % seg
            r1 = jnp.where(pos < 1, e1_ref[:, lo:lo + cw], r1)
            r2 = jnp.where(pos < 2, e2_ref[:, lo:lo + cw], r2)
            u_ref[:, lo:lo + cw] = u
        w0 = cwt_ref[0:1, lo:lo + cw]
        w1 = cwt_ref[1:2, lo:lo + cw]
        w2 = cwt_ref[2:3, lo:lo + cw]
        conv = w0 * r2 + w1 * r1 + w2 * u
        ucb_ref[:, lo:lo + cw] = (cb * conv).astype(_BF)


def _conv_prompt(n, w_in, conv_w, *, batch, tm, cw, name, casts=()):
    M, D = n.shape
    d_conv = conv_w.shape[1]
    tiles_per_seg = (M // batch) // tm
    grid = (M // tm,)
    c_in, c_out, c_shape, c_bytes = _cast_specs(casts, grid)
    blocks = [_nbytes((tm, D), _BF), _nbytes((D, 3 * d_conv), _BF), _nbytes((tm, d_conv), _BF), c_bytes]
    body = functools.partial(_conv_kernel, carry_rows=True, seg=None, tiles_per_seg=tiles_per_seg,
                             d_conv=d_conv, cw=cw)
    return pl.pallas_call(
        _with_casts(body, 3, 2, len(casts)),
        grid=grid,
        in_specs=[
            pl.BlockSpec((tm, D), lambda i: (i, 0)),
            pl.BlockSpec((D, 3 * d_conv), lambda i: (0, 0)),
            pl.BlockSpec((3, d_conv), lambda i: (0, 0)),
        ] + c_in,
        out_specs=[
            pl.BlockSpec((tm, d_conv), lambda i: (i, 0)),
            pl.BlockSpec((1, 8, d_conv), lambda i: (i // tiles_per_seg, 0, 0)),
        ] + c_out,
        out_shape=[
            jax.ShapeDtypeStruct((M, d_conv), _BF),
            jax.ShapeDtypeStruct((batch, 8, d_conv), _F32),
        ] + c_shape,
        scratch_shapes=[pltpu.VMEM((8, d_conv), _F32)],
        compiler_params=pltpu.CompilerParams(
            dimension_semantics=("arbitrary",),
            vmem_limit_bytes=_vmem_limit(blocks),
        ),
        name=name,
    )(n, w_in, conv_w, *casts)


def _conv_sample(n, w_in, conv_w, e1, e2, *, seg, cw, name):
    M, D = n.shape
    d_conv = conv_w.shape[1]
    blocks = [_nbytes((M, D), _BF), _nbytes((D, 3 * d_conv), _BF), 4 * _nbytes((M, d_conv), _F32)]
    full = lambda i: (0, 0)
    return pl.pallas_call(
        functools.partial(_conv_kernel, carry_rows=False, seg=seg, tiles_per_seg=None,
                          d_conv=d_conv, cw=cw),
        grid=(1,),
        in_specs=[
            pl.BlockSpec((M, D), full),
            pl.BlockSpec((D, 3 * d_conv), full),
            pl.BlockSpec((3, d_conv), full),
            pl.BlockSpec((M, d_conv), full),
            pl.BlockSpec((M, d_conv), full),
        ],
        out_specs=[pl.BlockSpec((M, d_conv), full), pl.BlockSpec((M, d_conv), full)],
        out_shape=[
            jax.ShapeDtypeStruct((M, d_conv), _BF),
            jax.ShapeDtypeStruct((M, d_conv), _F32),
        ],
        compiler_params=pltpu.CompilerParams(
            dimension_semantics=("arbitrary",),
            vmem_limit_bytes=_vmem_limit(blocks),
        ),
        name=name,
    )(n, w_in, conv_w, e1, e2)


def _qkv_kernel(n_ref, w_ref, gq_ref, gk_ref, ones_ref, q_ref, k_ref, v_ref, *, cw):
    ones = ones_ref[...]
    dq = q_ref.shape[1]
    dk = k_ref.shape[1]
    p = _dot(n_ref[...], w_ref[...])
    x = p[:, :dq + dk]
    sq = x * x
    hi = sq.astype(_BF)
    lo = (sq - hi.astype(_F32)).astype(_BF)
    ss = jnp.concatenate(
        [_dot(hi[:, c * cw:(c + 1) * cw], ones) + _dot(lo[:, c * cw:(c + 1) * cw], ones)
         for c in range((dq + dk) // cw)], axis=1)
    xr = x * lax.rsqrt(ss * (1.0 / HEAD_DIM) + EPS)
    q_ref[...] = (xr[:, :dq] * gq_ref[...] * QK_SCALE).astype(_BF)
    k_ref[...] = xr[:, dq:] * gk_ref[...]
    v_ref[...] = p[:, dq + dk:]


def _qkv(n, w_in, gq, gk, ones, *, col_block, tm, name, casts=()):
    M, D = n.shape
    cw = ones.shape[0]
    dq, dk = N_HEADS * HEAD_DIM, N_KV * HEAD_DIM
    wcols = dq + 2 * dk
    grid = (M // tm,)
    c_in, c_out, c_shape, c_bytes = _cast_specs(casts, grid)
    blocks = [_nbytes((tm, D), _BF), _nbytes((D, wcols), _BF), _nbytes((tm, dq), _BF), 2 * _nbytes((tm, dk), _F32),
              c_bytes]
    const = lambda i: (0, 0)
    row = lambda i: (i, 0)
    return pl.pallas_call(
        _with_casts(functools.partial(_qkv_kernel, cw=cw), 5, 3, len(casts)),
        grid=grid,
        in_specs=[
            pl.BlockSpec((tm, D), row),
            pl.BlockSpec((D, wcols), lambda i: (0, col_block)),
            pl.BlockSpec((1, dq), const),
            pl.BlockSpec((1, dk), const),
            pl.BlockSpec((cw, cw), const),
        ] + c_in,
        out_specs=[pl.BlockSpec((tm, dq), row), pl.BlockSpec((tm, dk), row), pl.BlockSpec((tm, dk), row)] + c_out,
        out_shape=[
            jax.ShapeDtypeStruct((M, dq), _BF),
            jax.ShapeDtypeStruct((M, dk), _F32),
            jax.ShapeDtypeStruct((M, dk), _F32),
        ] + c_shape,
        compiler_params=pltpu.CompilerParams(
            dimension_semantics=("arbitrary",),
            vmem_limit_bytes=_vmem_limit(blocks),
        ),
        name=name,
    )(n, w_in, gq, gk, ones, *casts)


def _bias_kernel(tab_ref, bkt_ref, out_ref):
    bkt = bkt_ref[...]
    for h in range(out_ref.shape[0]):
        acc = jnp.zeros(bkt.shape, _F32)
        for b in range(tab_ref.shape[0]):
            acc = jnp.where(bkt == b, tab_ref[b, h], acc)
        out_ref[h] = acc


def _rel_bias(table, bucket, name):
    nq, nk = bucket.shape
    return pl.pallas_call(
        _bias_kernel,
        in_specs=[pl.BlockSpec(memory_space=pltpu.SMEM), pl.BlockSpec((nq, nk), lambda: (0, 0))],
        out_specs=pl.BlockSpec((table.shape[1], nq, nk), lambda: (0, 0, 0)),
        out_shape=jax.ShapeDtypeStruct((table.shape[1], nq, nk), _F32),
        name=name,
    )(table, bucket)


def _t5_bucket(rel):
    nb = N_BUCKETS // 2
    max_exact = nb // 2
    ret = jnp.where(rel > 0, nb, 0)
    n = jnp.abs(rel)
    nf = jnp.maximum(n, 1).astype(jnp.float32)
    large = max_exact + (jnp.log(nf / max_exact) / math.log(MAX_DIST / max_exact) * (nb - max_exact)).astype(jnp.int32)
    large = jnp.minimum(large, nb - 1)
    return ret + jnp.where(n < max_exact, n, large)


def _bucket_index(n_q, n_k):
    i = jnp.arange(n_q)[:, None]
    j = jnp.arange(n_k)[None, :]
    return _t5_bucket(j - WINDOW - i).astype(jnp.int32)


def _attn_group(qg, kk, vv, bias, sink, valid):
    lg = lax.dot_general(qg, kk, (((1,), (1,)), ((), ())), preferred_element_type=_F32) + bias
    if valid is not None:
        lg = jnp.where(valid, lg, NEG)
    m = jnp.maximum(jnp.max(lg, axis=-1, keepdims=True), sink)
    e = jnp.exp(lg - m)
    den = jnp.sum(e, axis=-1, keepdims=True) + jnp.exp(sink - m)
    p = (e * (1.0 / den)).astype(_BF)
    return _dot(p, vv)


def _attn_rows(qc, kw, vw, bias_ref, sink_ref, valid):
    nq = qc.shape[0]
    outs = []
    for kv in range(N_KV):
        qg = jnp.concatenate(
            [qc[:, (kv * GROUP + g) * HEAD_DIM:(kv * GROUP + g + 1) * HEAD_DIM] for g in range(GROUP)], axis=0)
        kk = kw[:, kv * HEAD_DIM:(kv + 1) * HEAD_DIM]
        vv = vw[:, kv * HEAD_DIM:(kv + 1) * HEAD_DIM]
        og = _attn_group(qg, kk, vv, bias_ref[kv], sink_ref[kv], valid)
        outs.extend(og[g * nq:(g + 1) * nq] for g in range(GROUP))
    return jnp.concatenate(outs, axis=1)


def _attn_chunk_keys_major(qc, kw, vw, bias_t, sink_t, first_valid_key):
    nq = qc.shape[0]
    gq = GROUP * nq
    nt = (((1,), (1,)), ((), ()))
    tn = (((0,), (0,)), ((), ()))
    logits = []
    for kv in range(N_KV):
        qg = jnp.concatenate(
            [qc[:, (kv * GROUP + g) * HEAD_DIM:(kv * GROUP + g + 1) * HEAD_DIM] for g in range(GROUP)], axis=0)
        kk = kw[:, kv * HEAD_DIM:(kv + 1) * HEAD_DIM]
        logits.append(lax.dot_general(kk, qg, nt, preferred_element_type=_F32))
    lg = jnp.concatenate(logits, axis=1) + bias_t
    key = lax.broadcasted_iota(jnp.int32, lg.shape, 0)
    lg = jnp.where(key >= first_valid_key, lg, NEG)
    m = jnp.maximum(jnp.max(lg, axis=0, keepdims=True), sink_t)
    e = jnp.exp(lg - m)
    den = jnp.sum(e, axis=0, keepdims=True) + jnp.exp(sink_t - m)
    p = (e * (1.0 / den)).astype(_BF)
    outs = []
    for kv in range(N_KV):
        vv = vw[:, kv * HEAD_DIM:(kv + 1) * HEAD_DIM]
        outs.append(lax.dot_general(vv, p[:, kv * gq:(kv + 1) * gq], tn, preferred_element_type=_F32))
    return jnp.concatenate(outs, axis=0).T


def _attn_prompt_kernel(q_ref, kp_ref, kc_ref, vp_ref, vc_ref, bias_ref, sink_ref, o_ref, kwin, vwin,
                        *, tiles_per_batch):
    ta = q_ref.shape[0]
    nk = WINDOW + CHUNK
    dkv = N_KV * HEAD_DIM
    kwin[0:WINDOW, :] = kp_ref[...].astype(_BF)
    kwin[WINDOW:, :] = kc_ref[...].astype(_BF)
    vwin[0:WINDOW, :] = vp_ref[...].astype(_BF)
    vwin[WINDOW:, :] = vc_ref[...].astype(_BF)
    base = (pl.program_id(0) % tiles_per_batch) * ta - WINDOW

    def chunk(c, carry):
        r = pl.multiple_of(c * CHUNK, CHUNK)
        o = _attn_chunk_keys_major(q_ref[pl.ds(r, CHUNK), :], kwin[pl.ds(r, nk), :], vwin[pl.ds(r, nk), :],
                                   bias_ref[...], sink_ref[...], -(base + r))
        for g in range(GROUP):
            o_ref[pl.ds(r, CHUNK), g * dkv:(g + 1) * dkv] = o[g * CHUNK:(g + 1) * CHUNK].astype(_BF)
        return carry

    lax.fori_loop(0, ta // CHUNK, chunk, 0, unroll=4)


def _attn_prompt(q, k, v, bias, sink, *, batch, ta, name, casts=()):
    M, dq = q.shape
    dk = k.shape[1]
    tiles_per_batch = (M // batch) // ta
    wb = ta // WINDOW
    nk = WINDOW + CHUNK

    def prev(i):
        return (jnp.where(i % tiles_per_batch == 0, i * wb, i * wb - 1), 0)

    row = lambda i: (i, 0)
    grid = (M // ta,)
    c_in, c_out, c_shape, c_bytes = _cast_specs(casts, grid)
    blocks = [_nbytes((ta, dq), _BF) * 2, 2 * _nbytes((WINDOW + ta, dk), _F32),
              _nbytes(bias.shape, _F32), _nbytes((8, N_HEADS * CHUNK), _F32), c_bytes]
    return pl.pallas_call(
        _with_casts(functools.partial(_attn_prompt_kernel, tiles_per_batch=tiles_per_batch), 7, 1, len(casts)),
        grid=grid,
        in_specs=[
            pl.BlockSpec((ta, dq), row),
            pl.BlockSpec((WINDOW, dk), prev),
            pl.BlockSpec((ta, dk), row),
            pl.BlockSpec((WINDOW, dk), prev),
            pl.BlockSpec((ta, dk), row),
            pl.BlockSpec((nk, N_HEADS * CHUNK), lambda i: (0, 0)),
            pl.BlockSpec((1, N_HEADS * CHUNK), lambda i: (0, 0)),
        ] + c_in,
        out_specs=[pl.BlockSpec((ta, dq), row)] + c_out,
        out_shape=[jax.ShapeDtypeStruct((M, dq), _BF)] + c_shape,
        scratch_shapes=[pltpu.VMEM((WINDOW + ta, dk), _BF), pltpu.VMEM((WINDOW + ta, dk), _BF)],
        compiler_params=pltpu.CompilerParams(
            dimension_semantics=("arbitrary",),
            vmem_limit_bytes=_vmem_limit(blocks, 2 * _nbytes((WINDOW + ta, dk), _BF)),
        ),
        name=name,
    )(q, k, k, v, v, bias, sink, *casts)


def _attn_sample_kernel(q_ref, ck_ref, cv_ref, k_ref, v_ref, bias_ref, sink_ref, o_ref):
    kw = jnp.concatenate([ck_ref[0].astype(_BF), k_ref[...].astype(_BF)], axis=0)
    vw = jnp.concatenate([cv_ref[0].astype(_BF), v_ref[...].astype(_BF)], axis=0)
    o_ref[...] = _attn_rows(q_ref[...], kw, vw, bias_ref, sink_ref, None).astype(_BF)


def _attn_sample(q, k, v, cache_k, cache_v, bias, sink, *, name):
    B, W, dk = cache_k.shape
    M, dq = q.shape
    S = M // B
    row = lambda b: (b, 0)
    return pl.pallas_call(
        _attn_sample_kernel,
        grid=(B,),
        in_specs=[
            pl.BlockSpec((S, dq), row),
            pl.BlockSpec((1, W, dk), lambda b: (b, 0, 0)),
            pl.BlockSpec((1, W, dk), lambda b: (b, 0, 0)),
            pl.BlockSpec((S, dk), row),
            pl.BlockSpec((S, dk), row),
            pl.BlockSpec(bias.shape, lambda b: (0, 0, 0)),
            pl.BlockSpec(sink.shape, lambda b: (0, 0, 0)),
        ],
        out_specs=pl.BlockSpec((S, dq), row),
        out_shape=jax.ShapeDtypeStruct((M, dq), _BF),
        compiler_params=pltpu.CompilerParams(dimension_semantics=("arbitrary",)),
        name=name,
    )(q, cache_k, cache_v, k, v, bias, sink)


def _merge_kernel(h_ref, n_ref, ucb_ref, o_ref, wco_ref, wao_ref, wgc_ref, wga_ref, wout_ref, out_ref):
    j = pl.program_id(1)
    last = pl.num_programs(1) - 1
    n = n_ref[...]
    y_conv = _dot(ucb_ref[...], wco_ref[...])
    y_attn = _dot(o_ref[...], wao_ref[...])
    g_conv = jax.nn.sigmoid(_dot(n, wgc_ref[...]))
    g_attn = jax.nn.sigmoid(_dot(n, wga_ref[...]))
    mix = (g_conv * y_conv + g_attn * y_attn).astype(_BF)
    out_ref[...] = jnp.where(j == 0, 0.0, out_ref[...]) + _dot(mix, wout_ref[...])

    @pl.when(j == last)
    def _():
        out_ref[...] = h_ref[...] + out_ref[...]


def _merge(h, n, ucb, o, w_conv_out, w_attn_o, w_in, w_out, *, gc_col, ga_col, tm, tn, name, casts=()):
    M, D = h.shape
    grid = (M // tm, D // tn)
    dc = ucb.shape[1]
    da = o.shape[1]
    row = lambda i, j: (i, 0)
    col = lambda i, j: (0, j)
    blocks = [2 * _nbytes((tm, D), _F32), _nbytes((tm, D), _BF), _nbytes((tm, dc), _BF), _nbytes((tm, da), _BF),
              _nbytes((dc + da + 3 * D, tn), _BF)]
    c_in, c_out, c_shape, c_bytes = _cast_specs(casts, grid)
    blocks.append(c_bytes)
    return pl.pallas_call(
        _with_casts(_merge_kernel, 9, 1, len(casts)),
        grid=grid,
        in_specs=[
            pl.BlockSpec((tm, D), row),
            pl.BlockSpec((tm, D), row),
            pl.BlockSpec((tm, dc), row),
            pl.BlockSpec((tm, da), row),
            pl.BlockSpec((dc, tn), col),
            pl.BlockSpec((da, tn), col),
            pl.BlockSpec((D, tn), lambda i, j: (0, gc_col // tn + j)),
            pl.BlockSpec((D, tn), lambda i, j: (0, ga_col // tn + j)),
            pl.BlockSpec((tn, D), lambda i, j: (j, 0)),
        ] + c_in,
        out_specs=[pl.BlockSpec((tm, D), row)] + c_out,
        out_shape=[jax.ShapeDtypeStruct((M, D), _F32)] + c_shape,
        compiler_params=pltpu.CompilerParams(
            dimension_semantics=("arbitrary", "arbitrary"),
            vmem_limit_bytes=_vmem_limit(blocks),
        ),
        name=name,
    )(h, n, ucb, o, w_conv_out, w_attn_o, w_in, w_in, w_out, *casts)


def _ple_kernel(h_ref, pe_ref, g_ref, wp_ref, wg_ref, out_ref):
    h = h_ref[...]
    gate = jax.nn.sigmoid(_dot(_rms(h, g_ref[...]).astype(_BF), wg_ref[...]))
    emb = _dot(pe_ref[...].astype(_BF), wp_ref[...])
    out_ref[...] = h + emb * gate


def _ple(h, pe, g, w_ple, w_gate, *, tm, name):
    M, D = h.shape
    dp = pe.shape[1]
    row = lambda i: (i, 0)
    const = lambda i: (0, 0)
    blocks = [2 * _nbytes((tm, D), _F32), _nbytes((tm, dp), _F32), _nbytes((dp + D, D), _BF)]
    return pl.pallas_call(
        _ple_kernel,
        grid=(M // tm,),
        in_specs=[
            pl.BlockSpec((tm, D), row),
            pl.BlockSpec((tm, dp), row),
            pl.BlockSpec((1, D), const),
            pl.BlockSpec((dp, D), const),
            pl.BlockSpec((D, D), const),
        ],
        out_specs=pl.BlockSpec((tm, D), row),
        out_shape=jax.ShapeDtypeStruct((M, D), _F32),
        compiler_params=pltpu.CompilerParams(
            dimension_semantics=("arbitrary",),
            vmem_limit_bytes=_vmem_limit(blocks),
        ),
        name=name,
    )(h, pe, g, w_ple, w_gate)


def _row_tile(m, want):
    return want if m % want == 0 else m


def kernel(x_prompt, x_sample, p_prompt, p_sample, state_conv, cache_k, cache_v, rel_table, ffn1_norm, ffn1_wg, ffn1_wu, ffn1_wd, mix_norm, w_in, conv_w, q_norm, k_norm, attn_sink, w_conv_out, w_attn_o, w_out, ffn2_norm, ffn2_wg, ffn2_wu, ffn2_wd, ple_norm, w_ple, w_ple_gate):
    B, T, D = x_prompt.shape
    Bs, S, _ = x_sample.shape
    depth = ffn1_wg.shape[0]
    d_conv = conv_w.shape[2]
    dq, dk = N_HEADS * HEAD_DIM, N_KV * HEAD_DIM
    qkv_col = 3 * d_conv
    gc_col = qkv_col + dq + 2 * dk
    ga_col = gc_col + D
    Mp, Ms = B * T, Bs * S
    tm_p = _row_tile(Mp, 512)
    tm_s = _row_tile(Ms, 512)
    tf = 1024
    tn = 512
    cw = 256

    nk_p = WINDOW + CHUNK
    bias_p = _rel_bias(rel_table, _bucket_index(CHUNK, nk_p), "rel_bias_prompt")
    bias_p = bias_p.transpose(2, 0, 1).reshape(nk_p, N_HEADS * CHUNK)
    bias_s = _rel_bias(rel_table, _bucket_index(S, WINDOW + S), "rel_bias_sample")
    bias_s = bias_s.reshape(N_KV, GROUP * S, WINDOW + S)
    lane = jnp.arange(cw) // HEAD_DIM
    ones = (lane[:, None] == lane[None, :]).astype(_BF)

    yp = x_prompt.reshape(Mp, D)
    ys = x_sample.reshape(Ms, D)
    conv_p, k_p, v_p, conv_s, k_s, v_s = [], [], [], [], [], []
    for l in range(depth):
        bf = lambda w: w[l].astype(_BF)
        vec = lambda g: g[l].reshape(1, -1)
        f1 = (vec(ffn1_norm), bf(ffn1_wg), bf(ffn1_wu), bf(ffn1_wd))
        gq = jnp.tile(q_norm[l], N_HEADS).reshape(1, dq)
        gk = jnp.tile(k_norm[l], N_KV).reshape(1, dk)
        sink = attn_sink[l].reshape(N_KV, GROUP, 1, 1)
        sink_p = jnp.repeat(attn_sink[l], CHUNK).reshape(1, N_HEADS * CHUNK)
        sink_s = jnp.broadcast_to(sink, (N_KV, GROUP, S, 1)).reshape(N_KV, GROUP * S, 1)

        h1, n, w_in_l = _ffn(yp, *f1, vec(mix_norm), tm=tm_p, tf=tf, name="ffn1_prompt", casts=(w_in[l],))
        ucb, ulast, f2_wg, f2_wu = _conv_prompt(n, w_in_l, conv_w[l], batch=B, tm=tm_p, cw=cw, name="conv_prompt",
                                                casts=(ffn2_wg[l], ffn2_wu[l]))
        q, k, v, w_co, w_ao, w_o = _qkv(n, w_in_l, gq, gk, ones, col_block=qkv_col // (dq + 2 * dk), tm=tm_p,
                                        name="qkv_prompt", casts=(w_conv_out[l], w_attn_o[l], w_out[l]))
        o, f2_wd, w_p, w_pg = _attn_prompt(q, k, v, bias_p, sink_p, batch=B, ta=tm_p, name="attn_prompt",
                                           casts=(ffn2_wd[l], w_ple[l], w_ple_gate[l]))
        w_ao_p = w_ao.reshape(N_KV, GROUP, HEAD_DIM, D).transpose(1, 0, 2, 3).reshape(dq, D)
        h2, = _merge(h1, n, ucb, o, w_co, w_ao_p, w_in_l, w_o, gc_col=gc_col, ga_col=ga_col, tm=tm_p, tn=tn,
                     name="merge_prompt")
        f2 = (vec(ffn2_norm), f2_wg, f2_wu, f2_wd)
        h3, = _ffn(h2, *f2, None, tm=tm_p, tf=tf, name="ffn2_prompt")
        yp = _ple(h3, p_prompt[l].reshape(Mp, -1), vec(ple_norm), w_p, w_pg, tm=tm_p, name="ple_prompt")
        conv_p.append(ulast[:, 6:8])
        k_p.append(k.reshape(B, T, dk)[:, T - WINDOW:].reshape(B, WINDOW, N_KV, HEAD_DIM))
        v_p.append(v.reshape(B, T, dk)[:, T - WINDOW:].reshape(B, WINDOW, N_KV, HEAD_DIM))

        st = state_conv[l]
        zeros = jnp.zeros((Bs, S - 2, d_conv), _F32)
        e2 = jnp.concatenate([st, zeros], axis=1).reshape(Ms, d_conv)
        e1 = jnp.concatenate([st[:, 1:], zeros, zeros[:, :1]], axis=1).reshape(Ms, d_conv)
        h1, n = _ffn(ys, *f1, vec(mix_norm), tm=tm_s, tf=tf, name="ffn1_sample")
        ucb, u = _conv_sample(n, w_in_l, conv_w[l], e1, e2, seg=S, cw=cw, name="conv_sample")
        q, k, v = _qkv(n, w_in_l, gq, gk, ones, col_block=qkv_col // (dq + 2 * dk), tm=tm_s, name="qkv_sample")
        o = _attn_sample(q, k, v, cache_k[l].reshape(Bs, WINDOW, dk), cache_v[l].reshape(Bs, WINDOW, dk),
                         bias_s, sink_s, name="attn_sample")
        h2, = _merge(h1, n, ucb, o, w_co, w_ao, w_in_l, w_o, gc_col=gc_col, ga_col=ga_col,
                     tm=tm_s, tn=tn, name="merge_sample")
        h3, = _ffn(h2, *f2, None, tm=tm_s, tf=tf, name="ffn2_sample")
        ys = _ple(h3, p_sample[l].reshape(Ms, -1), vec(ple_norm), w_p, w_pg, tm=tm_s, name="ple_sample")
        conv_s.append(u.reshape(Bs, S, d_conv)[:, S - 2:])
        k_s.append(k.reshape(Bs, S, N_KV, HEAD_DIM))
        v_s.append(v.reshape(Bs, S, N_KV, HEAD_DIM))

    return (yp.reshape(B, T, D), ys.reshape(Bs, S, D), jnp.stack(conv_p), jnp.stack(k_p), jnp.stack(v_p),
            jnp.stack(conv_s), jnp.stack(k_s), jnp.stack(v_s))
```

```python
import functools
import math

import jax
import jax.numpy as jnp
from jax import lax
from jax.experimental import pallas as pl
from jax.experimental.pallas import tpu as pltpu

_BF = jnp.bfloat16
_F32 = jnp.float32

CHUNK = 64
WINDOW = 128
N_HEADS = 16
N_KV = 4
GROUP = N_HEADS // N_KV
HEAD_DIM = 64
N_BUCKETS = 32
MAX_DIST = 128
EPS = 1e-6
NEG = -1e30
QK_SCALE = HEAD_DIM ** -0.5

V7X_VMEM_BYTES = 64 * 1024 * 1024
VMEM_INTERNAL_BYTES = 12 * 1024 * 1024


def _vmem_limit(block_bytes, scratch_bytes=0):
    need = 2 * sum(block_bytes) + scratch_bytes + VMEM_INTERNAL_BYTES
    return int(min(need, V7X_VMEM_BYTES - 4 * 1024 * 1024))


def _nbytes(shape, dtype):
    return math.prod(shape) * jnp.dtype(dtype).itemsize


def _dot(a, b):
    return jnp.dot(a, b, preferred_element_type=_F32)


def _rms(x, g):
    ms = jnp.mean(x * x, axis=-1, keepdims=True)
    return x * lax.rsqrt(ms + EPS) * g


BF16_TILE_ROWS = 16


def _cast_specs(arrays, grid):
    steps = math.prod(grid)

    def flat(*ids):
        s = 0
        for k, g in zip(ids, grid):
            s = s * g + k
        return s

    in_specs, out_specs, out_shape, nbytes = [], [], [], 0
    for a in arrays:
        rows, cols = a.shape
        rb = max(BF16_TILE_ROWS, rows // steps)
        assert rows % rb == 0 and steps % (rows // rb) == 0, (a.shape, steps)
        rep = steps // (rows // rb)
        idx = lambda *ids, rep=rep: (flat(*ids) // rep, 0)
        in_specs.append(pl.BlockSpec((rb, cols), idx))
        out_specs.append(pl.BlockSpec((rb, cols), idx))
        out_shape.append(jax.ShapeDtypeStruct((rows, cols), _BF))
        nbytes += _nbytes((rb, cols), _F32) + _nbytes((rb, cols), _BF)
    return in_specs, out_specs, out_shape, nbytes


def _with_casts(body, n_in, n_out, n_cast):
    def kernel(*refs):
        outs_at = n_in + n_cast
        scratch_at = outs_at + n_out + n_cast
        for src, dst in zip(refs[n_in:outs_at], refs[outs_at + n_out:scratch_at]):
            dst[...] = src[...].astype(_BF)
        body(*refs[:n_in], *refs[outs_at:outs_at + n_out], *refs[scratch_at:])
    return kernel


def _ffn_kernel(*refs, emit_norm):
    if emit_norm:
        x_ref, g_ref, wg_ref, wu_ref, wd_ref, g2_ref, out_ref, n_ref, xn_ref = refs
    else:
        x_ref, g_ref, wg_ref, wu_ref, wd_ref, out_ref, xn_ref = refs
    j = pl.program_id(1)
    last = pl.num_programs(1) - 1

    @pl.when(j == 0)
    def _():
        xn_ref[...] = _rms(x_ref[...], g_ref[...]).astype(_BF)

    xn = xn_ref[...]
    gate = _dot(xn, wg_ref[...])
    up = _dot(xn, wu_ref[...])
    hid = (jax.nn.silu(gate) * up).astype(_BF)
    out_ref[...] = jnp.where(j == 0, 0.0, out_ref[...]) + _dot(hid, wd_ref[...])

    @pl.when(j == last)
    def _():
        h = x_ref[...] + 0.5 * out_ref[...]
        out_ref[...] = h
        if emit_norm:
            n_ref[...] = _rms(h, g2_ref[...]).astype(_BF)


def _ffn(x, g, wg, wu, wd, g2, *, tm, tf, name, casts=()):
    M, D = x.shape
    F = wg.shape[1]
    emit_norm = g2 is not None
    grid = (M // tm, F // tf)
    row = lambda i, j: (i, 0)
    in_specs = [
        pl.BlockSpec((tm, D), row),
        pl.BlockSpec((1, D), lambda i, j: (0, 0)),
        pl.BlockSpec((D, tf), lambda i, j: (0, j)),
        pl.BlockSpec((D, tf), lambda i, j: (0, j)),
        pl.BlockSpec((tf, D), lambda i, j: (j, 0)),
    ]
    args = [x, g, wg, wu, wd]
    out_shape = [jax.ShapeDtypeStruct((M, D), _F32)]
    out_specs = [pl.BlockSpec((tm, D), row)]
    blocks = [_nbytes((tm, D), _F32) * 2, 3 * _nbytes((D, tf), _BF)]
    if emit_norm:
        in_specs.append(pl.BlockSpec((1, D), lambda i, j: (0, 0)))
        args.append(g2)
        out_shape.append(jax.ShapeDtypeStruct((M, D), _BF))
        out_specs.append(pl.BlockSpec((tm, D), row))
        blocks.append(_nbytes((tm, D), _BF))
    c_in, c_out, c_shape, c_bytes = _cast_specs(casts, grid)
    blocks.append(c_bytes)
    res = pl.pallas_call(
        _with_casts(functools.partial(_ffn_kernel, emit_norm=emit_norm), len(in_specs), len(out_specs), len(casts)),
        grid=grid,
        in_specs=in_specs + c_in,
        out_specs=out_specs + c_out,
        out_shape=out_shape + c_shape,
        scratch_shapes=[pltpu.VMEM((tm, D), _BF)],
        compiler_params=pltpu.CompilerParams(
            dimension_semantics=("arbitrary", "arbitrary"),
            vmem_limit_bytes=_vmem_limit(blocks, _nbytes((tm, D), _BF)),
        ),
        name=name,
    )(*args, *casts)
    return res


def _conv_kernel(*refs, carry_rows, seg, tiles_per_seg, d_conv, cw):
    if carry_rows:
        n_ref, w_ref, cwt_ref, ucb_ref, ulast_ref, carry_ref = refs
    else:
        n_ref, w_ref, cwt_ref, e1_ref, e2_ref, ucb_ref, u_ref = refs
    tm = n_ref.shape[0]
    n = n_ref[...]
    if carry_rows:
        @pl.when(pl.program_id(0) % tiles_per_seg == 0)
        def _():
            carry_ref[...] = jnp.zeros_like(carry_ref)

    for c in range(d_conv // cw):
        lo = c * cw
        cb = _dot(n, w_ref[:, lo:lo + cw])
        cc = _dot(n, w_ref[:, d_conv + lo:d_conv + lo + cw])
        cv = _dot(n, w_ref[:, 2 * d_conv + lo:2 * d_conv + lo + cw])
        u = cc * cv
        r1 = pltpu.roll(u, 1, 0)
        r2 = pltpu.roll(u, 2, 0)
        if carry_rows:
            prev = carry_ref[:, lo:lo + cw]
            rows8 = lax.broadcasted_iota(jnp.int32, (8, cw), 0)
            h1 = jnp.where(rows8 < 1, pltpu.roll(prev, 1, 0), r1[:8])
            h2 = jnp.where(rows8 < 2, pltpu.roll(prev, 2, 0), r2[:8])
            r1 = jnp.concatenate([h1, r1[8:]], axis=0)
            r2 = jnp.concatenate([h2, r2[8:]], axis=0)
            carry_ref[:, lo:lo + cw] = u[tm - 8:]
            ulast_ref[0, :, lo:lo + cw] = u[tm - 8:]
        else:
            pos = lax.broadcasted_iota(jnp.int32, (tm, cw), 0) % seg
            r1 = jnp.where(pos < 1, e1_ref[:, lo:lo + cw], r1)
            r2 = jnp.where(pos < 2, e2_ref[:, lo:lo + cw], r2)
            u_ref[:, lo:lo + cw] = u
        w0 = cwt_ref[0:1, lo:lo + cw]
        w1 = cwt_ref[1:2, lo:lo + cw]
        w2 = cwt_ref[2:3, lo:lo + cw]
        conv = w0 * r2 + w1 * r1 + w2 * u
        ucb_ref[:, lo:lo + cw] = (cb * conv).astype(_BF)


def _qkv_kernel(n_ref, w_ref, gq_ref, gk_ref, ones_ref, q_ref, k_ref, v_ref, *, cw, col0):
    ones = ones_ref[...]
    dq = q_ref.shape[1]
    dk = k_ref.shape[1]
    p = _dot(n_ref[...], w_ref[:, col0:col0 + dq + 2 * dk])
    x = p[:, :dq + dk]
    sq = x * x
    hi = sq.astype(_BF)
    lo = (sq - hi.astype(_F32)).astype(_BF)
    ss = jnp.concatenate(
        [_dot(hi[:, c * cw:(c + 1) * cw], ones) + _dot(lo[:, c * cw:(c + 1) * cw], ones)
         for c in range((dq + dk) // cw)], axis=1)
    xr = x * lax.rsqrt(ss * (1.0 / HEAD_DIM) + EPS)
    q_ref[...] = (xr[:, :dq] * gq_ref[...] * QK_SCALE).astype(_BF)
    k_ref[...] = xr[:, dq:] * gk_ref[...]
    v_ref[...] = p[:, dq + dk:]


def _inproj_kernel(*refs, carry_rows, seg, tiles_per_seg, d_conv, cw):
    if carry_rows:
        (n_ref, w_ref, cwt_ref, gq_ref, gk_ref, ones_ref,
         ucb_ref, ulast_ref, q_ref, k_ref, v_ref, carry_ref) = refs
        conv_refs = (n_ref, w_ref, cwt_ref, ucb_ref, ulast_ref, carry_ref)
    else:
        (n_ref, w_ref, cwt_ref, gq_ref, gk_ref, ones_ref, e1_ref, e2_ref,
         ucb_ref, u_ref, q_ref, k_ref, v_ref) = refs
        conv_refs = (n_ref, w_ref, cwt_ref, e1_ref, e2_ref, ucb_ref, u_ref)
    _conv_kernel(*conv_refs, carry_rows=carry_rows, seg=seg, tiles_per_seg=tiles_per_seg, d_conv=d_conv, cw=cw)
    _qkv_kernel(n_ref, w_ref, gq_ref, gk_ref, ones_ref, q_ref, k_ref, v_ref, cw=cw, col0=3 * d_conv)


def _inproj(n, w_in, conv_w, gq, gk, ones, *, tm, name, batch=None, seg=None, edges=(), casts=()):
    M, D = n.shape
    d_conv = conv_w.shape[1]
    cw = ones.shape[0]
    dq, dk = N_HEADS * HEAD_DIM, N_KV * HEAD_DIM
    wcols = 3 * d_conv + dq + 2 * dk
    carry_rows = batch is not None
    grid = (M // tm,)
    tiles_per_seg = (M // batch) // tm if carry_rows else None
    c_in, c_out, c_shape, c_bytes = _cast_specs(casts, grid)
    const = lambda i: (0, 0)
    row = lambda i: (i, 0)
    in_specs = [
        pl.BlockSpec((tm, D), row),
        pl.BlockSpec((D, wcols), const, pipeline_mode=pl.Buffered(1)),
        pl.BlockSpec((3, d_conv), const),
        pl.BlockSpec((1, dq), const),
        pl.BlockSpec((1, dk), const),
        pl.BlockSpec((cw, cw), const),
    ] + [pl.BlockSpec((tm, d_conv), row) for _ in edges]
    if carry_rows:
        u_spec = pl.BlockSpec((1, 8, d_conv), lambda i: (i // tiles_per_seg, 0, 0))
        u_shape = jax.ShapeDtypeStruct((batch, 8, d_conv), _F32)
        scratch = [pltpu.VMEM((8, d_conv), _F32)]
    else:
        u_spec = pl.BlockSpec((tm, d_conv), row)
        u_shape = jax.ShapeDtypeStruct((M, d_conv), _F32)
        scratch = []
    out_specs = [pl.BlockSpec((tm, d_conv), row), u_spec,
                 pl.BlockSpec((tm, dq), row), pl.BlockSpec((tm, dk), row), pl.BlockSpec((tm, dk), row)]
    out_shape = [jax.ShapeDtypeStruct((M, d_conv), _BF), u_shape, jax.ShapeDtypeStruct((M, dq), _BF),
                 jax.ShapeDtypeStruct((M, dk), _F32), jax.ShapeDtypeStruct((M, dk), _F32)]
    blocks = [_nbytes((tm, D), _BF), _nbytes((D, wcols), _BF) // 2, (1 + len(edges)) * _nbytes((tm, d_conv), _F32),
              _nbytes((tm, d_conv + dq), _BF), 2 * _nbytes((tm, dk), _F32), c_bytes]
    body = functools.partial(_inproj_kernel, carry_rows=carry_rows, seg=seg, tiles_per_seg=tiles_per_seg,
                             d_conv=d_conv, cw=cw)
    return pl.pallas_call(
        _with_casts(body, len(in_specs), len(out_specs), len(casts)),
        grid=grid,
        in_specs=in_specs + c_in,
        out_specs=out_specs + c_out,
        out_shape=out_shape + c_shape,
        scratch_shapes=scratch,
        compiler_params=pltpu.CompilerParams(
            dimension_semantics=("arbitrary",),
            vmem_limit_bytes=_vmem_limit(blocks),
        ),
        name=name,
    )(n, w_in, conv_w, gq, gk, ones, *edges, *casts)


def _bias_kernel(tab_ref, bkt_ref, out_ref):
    bkt = bkt_ref[...]
    for h in range(out_ref.shape[0]):
        acc = jnp.zeros(bkt.shape, _F32)
        for b in range(tab_ref.shape[0]):
            acc = jnp.where(bkt == b, tab_ref[b, h], acc)
        out_ref[h] = acc


def _rel_bias(table, bucket, name):
    nq, nk = bucket.shape
    return pl.pallas_call(
        _bias_kernel,
        in_specs=[pl.BlockSpec(memory_space=pltpu.SMEM), pl.BlockSpec((nq, nk), lambda: (0, 0))],
        out_specs=pl.BlockSpec((table.shape[1], nq, nk), lambda: (0, 0, 0)),
        out_shape=jax.ShapeDtypeStruct((table.shape[1], nq, nk), _F32),
        name=name,
    )(table, bucket)


def _t5_bucket(rel):
    nb = N_BUCKETS // 2
    max_exact = nb // 2
    ret = jnp.where(rel > 0, nb, 0)
    n = jnp.abs(rel)
    nf = jnp.maximum(n, 1).astype(jnp.float32)
    large = max_exact + (jnp.log(nf / max_exact) / math.log(MAX_DIST / max_exact) * (nb - max_exact)).astype(jnp.int32)
    large = jnp.minimum(large, nb - 1)
    return ret + jnp.where(n < max_exact, n, large)


def _bucket_index(n_q, n_k):
    i = jnp.arange(n_q)[:, None]
    j = jnp.arange(n_k)[None, :]
    return _t5_bucket(j - WINDOW - i).astype(jnp.int32)


_NT = (((1,), (1,)), ((), ()))
_TN = (((0,), (0,)), ((), ()))


def _sink_softmax_keys_major(lg, sink_t):
    m = jnp.maximum(jnp.max(lg, axis=0, keepdims=True), sink_t)
    e = jnp.exp(lg - m)
    den = jnp.sum(e, axis=0, keepdims=True) + jnp.exp(sink_t - m)
    return (e * (1.0 / den)).astype(_BF)


def _attn_chunk_keys_major(qc, kw, vw, bias_t, sink_t, first_valid_key):
    nq = qc.shape[0]
    gq = GROUP * nq
    logits = []
    for kv in range(N_KV):
        qg = jnp.concatenate(
            [qc[:, (kv * GROUP + g) * HEAD_DIM:(kv * GROUP + g + 1) * HEAD_DIM] for g in range(GROUP)], axis=0)
        kk = kw[:, kv * HEAD_DIM:(kv + 1) * HEAD_DIM]
        logits.append(lax.dot_general(kk, qg, _NT, preferred_element_type=_F32))
    lg = jnp.concatenate(logits, axis=1) + bias_t
    key = lax.broadcasted_iota(jnp.int32, lg.shape, 0)
    lg = jnp.where(key >= first_valid_key, lg, NEG)
    p = _sink_softmax_keys_major(lg, sink_t)
    outs = []
    for kv in range(N_KV):
        vv = vw[:, kv * HEAD_DIM:(kv + 1) * HEAD_DIM]
        outs.append(lax.dot_general(vv, p[:, kv * gq:(kv + 1) * gq], _TN, preferred_element_type=_F32))
    return jnp.concatenate(outs, axis=0).T


def _attn_prompt_kernel(q_ref, kp_ref, kc_ref, vp_ref, vc_ref, bias_ref, sink_ref, o_ref, kwin, vwin,
                        *, tiles_per_batch):
    ta = q_ref.shape[0]
    nk = WINDOW + CHUNK
    dkv = N_KV * HEAD_DIM
    kwin[0:WINDOW, :] = kp_ref[...].astype(_BF)
    kwin[WINDOW:, :] = kc_ref[...].astype(_BF)
    vwin[0:WINDOW, :] = vp_ref[...].astype(_BF)
    vwin[WINDOW:, :] = vc_ref[...].astype(_BF)
    base = (pl.program_id(0) % tiles_per_batch) * ta - WINDOW

    def chunk(c, carry):
        r = pl.multiple_of(c * CHUNK, CHUNK)
        o = _attn_chunk_keys_major(q_ref[pl.ds(r, CHUNK), :], kwin[pl.ds(r, nk), :], vwin[pl.ds(r, nk), :],
                                   bias_ref[...], sink_ref[...], -(base + r))
        for g in range(GROUP):
            o_ref[pl.ds(r, CHUNK), g * dkv:(g + 1) * dkv] = o[g * CHUNK:(g + 1) * CHUNK].astype(_BF)
        return carry

    lax.fori_loop(0, ta // CHUNK, chunk, 0, unroll=4)


def _attn_prompt(q, k, v, bias, sink, *, batch, ta, name, casts=()):
    M, dq = q.shape
    dk = k.shape[1]
    tiles_per_batch = (M // batch) // ta
    wb = ta // WINDOW
    nk = WINDOW + CHUNK

    def prev(i):
        return (jnp.where(i % tiles_per_batch == 0, i * wb, i * wb - 1), 0)

    row = lambda i: (i, 0)
    grid = (M // ta,)
    c_in, c_out, c_shape, c_bytes = _cast_specs(casts, grid)
    blocks = [_nbytes((ta, dq), _BF) * 2, 2 * _nbytes((WINDOW + ta, dk), _F32),
              _nbytes(bias.shape, _F32), _nbytes((8, N_HEADS * CHUNK), _F32), c_bytes]
    return pl.pallas_call(
        _with_casts(functools.partial(_attn_prompt_kernel, tiles_per_batch=tiles_per_batch), 7, 1, len(casts)),
        grid=grid,
        in_specs=[
            pl.BlockSpec((ta, dq), row),
            pl.BlockSpec((WINDOW, dk), prev),
            pl.BlockSpec((ta, dk), row),
            pl.BlockSpec((WINDOW, dk), prev),
            pl.BlockSpec((ta, dk), row),
            pl.BlockSpec((nk, N_HEADS * CHUNK), lambda i: (0, 0)),
            pl.BlockSpec((1, N_HEADS * CHUNK), lambda i: (0, 0)),
        ] + c_in,
        out_specs=[pl.BlockSpec((ta, dq), row)] + c_out,
        out_shape=[jax.ShapeDtypeStruct((M, dq), _BF)] + c_shape,
        scratch_shapes=[pltpu.VMEM((WINDOW + ta, dk), _BF), pltpu.VMEM((WINDOW + ta, dk), _BF)],
        compiler_params=pltpu.CompilerParams(
            dimension_semantics=("arbitrary",),
            vmem_limit_bytes=_vmem_limit(blocks, 2 * _nbytes((WINDOW + ta, dk), _BF)),
        ),
        name=name,
    )(q, k, k, v, v, bias, sink, *casts)


LANES = 128


def _attn_sample_kernel(q_ref, ck_ref, cv_ref, k_ref, v_ref, bias_ref, sink_ref, o_ref, *, nb, s):
    dkv = N_KV * HEAD_DIM
    gs = GROUP * s
    assert LANES == 2 * HEAD_DIM and dkv == 2 * LANES
    for b in range(nb):
        rows = slice(b * s, (b + 1) * s)
        kw = jnp.concatenate([ck_ref[b].astype(_BF), k_ref[rows, :].astype(_BF)], axis=0)
        vw = jnp.concatenate([cv_ref[b].astype(_BF), v_ref[rows, :].astype(_BF)], axis=0)
        qf = q_ref[rows, :].astype(_F32)
        blocks = []
        for kv in range(N_KV):
            for g in range(GROUP):
                h = kv * GROUP + g
                pair = qf[:, (h // 2) * LANES:(h // 2 + 1) * LANES]
                if h % 2 != kv % 2:
                    pair = pltpu.roll(pair, HEAD_DIM, 1)
                half = lax.broadcasted_iota(jnp.int32, pair.shape, 1) // HEAD_DIM
                piece = jnp.where(half == kv % 2, pair, 0.0)
                zero = jnp.zeros_like(piece)
                blocks.append(jnp.concatenate([piece, zero] if kv < 2 else [zero, piece], axis=1))
        q_bd = jnp.concatenate(blocks, axis=0).astype(_BF)
        lg = lax.dot_general(kw, q_bd, _NT, preferred_element_type=_F32) + bias_ref[...]
        p = _sink_softmax_keys_major(lg, sink_ref[...])
        z = lax.dot_general(p, vw, _TN, preferred_element_type=_F32)
        row_kv = lax.broadcasted_iota(jnp.int32, z.shape, 0) // gs
        col_kv = lax.broadcasted_iota(jnp.int32, z.shape, 1) // HEAD_DIM
        z = jnp.where(row_kv == col_kv, z, 0.0)
        r = z[0:gs]
        for kv in range(1, N_KV):
            r = r + z[kv * gs:(kv + 1) * gs]
        for g in range(GROUP):
            o_ref[rows, g * dkv:(g + 1) * dkv] = r[g * s:(g + 1) * s].astype(_BF)


def _attn_sample(q, k, v, cache_k, cache_v, bias, sink, *, nb, name):
    B, W, dk = cache_k.shape
    M, dq = q.shape
    S = M // B
    row = lambda b: (b, 0)
    return pl.pallas_call(
        functools.partial(_attn_sample_kernel, nb=nb, s=S),
        grid=(B // nb,),
        in_specs=[
            pl.BlockSpec((nb * S, dq), row),
            pl.BlockSpec((nb, W, dk), lambda b: (b, 0, 0)),
            pl.BlockSpec((nb, W, dk), lambda b: (b, 0, 0)),
            pl.BlockSpec((nb * S, dk), row),
            pl.BlockSpec((nb * S, dk), row),
            pl.BlockSpec(bias.shape, lambda b: (0, 0)),
            pl.BlockSpec(sink.shape, lambda b: (0, 0)),
        ],
        out_specs=pl.BlockSpec((nb * S, dq), row),
        out_shape=jax.ShapeDtypeStruct((M, dq), _BF),
        compiler_params=pltpu.CompilerParams(dimension_semantics=("arbitrary",)),
        name=name,
    )(q, cache_k, cache_v, k, v, bias, sink)


def _merge_kernel(h_ref, n_ref, ucb_ref, o_ref, wco_ref, wao_ref, wgc_ref, wga_ref, wout_ref, out_ref):
    j = pl.program_id(1)
    last = pl.num_programs(1) - 1
    n = n_ref[...]
    y_conv = _dot(ucb_ref[...], wco_ref[...])
    y_attn = _dot(o_ref[...], wao_ref[...])
    g_conv = jax.nn.sigmoid(_dot(n, wgc_ref[...]))
    g_attn = jax.nn.sigmoid(_dot(n, wga_ref[...]))
    mix = (g_conv * y_conv + g_attn * y_attn).astype(_BF)
    out_ref[...] = jnp.where(j == 0, 0.0, out_ref[...]) + _dot(mix, wout_ref[...])

    @pl.when(j == last)
    def _():
        out_ref[...] = h_ref[...] + out_ref[...]


def _merge(h, n, ucb, o, w_conv_out, w_attn_o, w_in, w_out, *, gc_col, ga_col, tm, tn, name, casts=()):
    M, D = h.shape
    grid = (M // tm, D // tn)
    dc = ucb.shape[1]
    da = o.shape[1]
    row = lambda i, j: (i, 0)
    col = lambda i, j: (0, j)
    blocks = [2 * _nbytes((tm, D), _F32), _nbytes((tm, D), _BF), _nbytes((tm, dc), _BF), _nbytes((tm, da), _BF),
              _nbytes((dc + da + 3 * D, tn), _BF)]
    c_in, c_out, c_shape, c_bytes = _cast_specs(casts, grid)
    blocks.append(c_bytes)
    return pl.pallas_call(
        _with_casts(_merge_kernel, 9, 1, len(casts)),
        grid=grid,
        in_specs=[
            pl.BlockSpec((tm, D), row),
            pl.BlockSpec((tm, D), row),
            pl.BlockSpec((tm, dc), row),
            pl.BlockSpec((tm, da), row),
            pl.BlockSpec((dc, tn), col),
            pl.BlockSpec((da, tn), col),
            pl.BlockSpec((D, tn), lambda i, j: (0, gc_col // tn + j)),
            pl.BlockSpec((D, tn), lambda i, j: (0, ga_col // tn + j)),
            pl.BlockSpec((tn, D), lambda i, j: (j, 0)),
        ] + c_in,
        out_specs=[pl.BlockSpec((tm, D), row)] + c_out,
        out_shape=[jax.ShapeDtypeStruct((M, D), _F32)] + c_shape,
        compiler_params=pltpu.CompilerParams(
            dimension_semantics=("arbitrary", "arbitrary"),
            vmem_limit_bytes=_vmem_limit(blocks),
        ),
        name=name,
    )(h, n, ucb, o, w_conv_out, w_attn_o, w_in, w_in, w_out, *casts)


def _ple_kernel(h_ref, pe_ref, g_ref, wp_ref, wg_ref, out_ref):
    h = h_ref[...]
    gate = jax.nn.sigmoid(_dot(_rms(h, g_ref[...]).astype(_BF), wg_ref[...]))
    emb = _dot(pe_ref[...].astype(_BF), wp_ref[...])
    out_ref[...] = h + emb * gate


def _ple(h, pe, g, w_ple, w_gate, *, tm, name):
    M, D = h.shape
    dp = pe.shape[1]
    row = lambda i: (i, 0)
    const = lambda i: (0, 0)
    blocks = [2 * _nbytes((tm, D), _F32), _nbytes((tm, dp), _F32), _nbytes((dp + D, D), _BF)]
    return pl.pallas_call(
        _ple_kernel,
        grid=(M // tm,),
        in_specs=[
            pl.BlockSpec((tm, D), row),
            pl.BlockSpec((tm, dp), row),
            pl.BlockSpec((1, D), const),
            pl.BlockSpec((dp, D), const),
            pl.BlockSpec((D, D), const),
        ],
        out_specs=pl.BlockSpec((tm, D), row),
        out_shape=jax.ShapeDtypeStruct((M, D), _F32),
        compiler_params=pltpu.CompilerParams(
            dimension_semantics=("arbitrary",),
            vmem_limit_bytes=_vmem_limit(blocks),
        ),
        name=name,
    )(h, pe, g, w_ple, w_gate)


def _row_tile(m, want):
    return want if m % want == 0 else m


def kernel(x_prompt, x_sample, p_prompt, p_sample, state_conv, cache_k, cache_v, rel_table, ffn1_norm, ffn1_wg, ffn1_wu, ffn1_wd, mix_norm, w_in, conv_w, q_norm, k_norm, attn_sink, w_conv_out, w_attn_o, w_out, ffn2_norm, ffn2_wg, ffn2_wu, ffn2_wd, ple_norm, w_ple, w_ple_gate):
    B, T, D = x_prompt.shape
    Bs, S, _ = x_sample.shape
    depth = ffn1_wg.shape[0]
    d_conv = conv_w.shape[2]
    dq, dk = N_HEADS * HEAD_DIM, N_KV * HEAD_DIM
    qkv_col = 3 * d_conv
    gc_col = qkv_col + dq + 2 * dk
    ga_col = gc_col + D
    Mp, Ms = B * T, Bs * S
    tm_p = _row_tile(Mp, 512)
    tm_s = _row_tile(Ms, 512)
    tf = 1024
    tn = 512
    cw = 256

    nk_p, nk_s = WINDOW + CHUNK, WINDOW + S
    bias_p = _rel_bias(rel_table, _bucket_index(CHUNK, nk_p), "rel_bias_prompt")
    bias_p = bias_p.transpose(2, 0, 1).reshape(nk_p, N_HEADS * CHUNK)
    bias_s = _rel_bias(rel_table, _bucket_index(S, nk_s), "rel_bias_sample")
    bias_s = bias_s.transpose(2, 0, 1).reshape(nk_s, N_HEADS * S)
    lane = jnp.arange(cw) // HEAD_DIM
    ones = (lane[:, None] == lane[None, :]).astype(_BF)

    yp = x_prompt.reshape(Mp, D)
    ys = x_sample.reshape(Ms, D)
    conv_p, k_p, v_p, conv_s, k_s, v_s = [], [], [], [], [], []
    for l in range(depth):
        bf = lambda w: w[l].astype(_BF)
        vec = lambda g: g[l].reshape(1, -1)
        f1 = (vec(ffn1_norm), bf(ffn1_wg), bf(ffn1_wu), bf(ffn1_wd))
        gq = jnp.tile(q_norm[l], N_HEADS).reshape(1, dq)
        gk = jnp.tile(k_norm[l], N_KV).reshape(1, dk)
        sink_p = jnp.repeat(attn_sink[l], CHUNK).reshape(1, N_HEADS * CHUNK)
        sink_s = jnp.repeat(attn_sink[l], S).reshape(1, N_HEADS * S)

        h1, n, w_in_l = _ffn(yp, *f1, vec(mix_norm), tm=tm_p, tf=tf, name="ffn1_prompt", casts=(w_in[l],))
        ucb, ulast, q, k, v, f2_wg, f2_wu, w_co, w_ao, w_o = _inproj(
            n, w_in_l, conv_w[l], gq, gk, ones, tm=tm_p, batch=B, name="inproj_prompt",
            casts=(ffn2_wg[l], ffn2_wu[l], w_conv_out[l], w_attn_o[l], w_out[l]))
        o, f2_wd, w_p, w_pg = _attn_prompt(q, k, v, bias_p, sink_p, batch=B, ta=2 * tm_p, name="attn_prompt",
                                           casts=(ffn2_wd[l], w_ple[l], w_ple_gate[l]))
        w_ao_p = w_ao.reshape(N_KV, GROUP, HEAD_DIM, D).transpose(1, 0, 2, 3).reshape(dq, D)
        h2, = _merge(h1, n, ucb, o, w_co, w_ao_p, w_in_l, w_o, gc_col=gc_col, ga_col=ga_col, tm=tm_p, tn=tn,
                     name="merge_prompt")
        f2 = (vec(ffn2_norm), f2_wg, f2_wu, f2_wd)
        h3, = _ffn(h2, *f2, None, tm=tm_p, tf=tf, name="ffn2_prompt")
        yp = _ple(h3, p_prompt[l].reshape(Mp, -1), vec(ple_norm), w_p, w_pg, tm=tm_p, name="ple_prompt")
        conv_p.append(ulast[:, 6:8])
        k_p.append(k.reshape(B, T, dk)[:, T - WINDOW:].reshape(B, WINDOW, N_KV, HEAD_DIM))
        v_p.append(v.reshape(B, T, dk)[:, T - WINDOW:].reshape(B, WINDOW, N_KV, HEAD_DIM))

        st = state_conv[l]
        zeros = jnp.zeros((Bs, S - 2, d_conv), _F32)
        e2 = jnp.concatenate([st, zeros], axis=1).reshape(Ms, d_conv)
        e1 = jnp.concatenate([st[:, 1:], zeros, zeros[:, :1]], axis=1).reshape(Ms, d_conv)
        h1, n = _ffn(ys, *f1, vec(mix_norm), tm=tm_s, tf=tf, name="ffn1_sample")
        ucb, u, q, k, v = _inproj(n, w_in_l, conv_w[l], gq, gk, ones, tm=tm_s, seg=S, edges=(e1, e2),
                                  name="inproj_sample")
        o = _attn_sample(q, k, v, cache_k[l].reshape(Bs, WINDOW, dk), cache_v[l].reshape(Bs, WINDOW, dk),
                         bias_s, sink_s, nb=4, name="attn_sample")
        h2, = _merge(h1, n, ucb, o, w_co, w_ao_p, w_in_l, w_o, gc_col=gc_col, ga_col=ga_col,
                     tm=tm_s, tn=tn, name="merge_sample")
        h3, = _ffn(h2, *f2, None, tm=tm_s, tf=tf, name="ffn2_sample")
        ys = _ple(h3, p_sample[l].reshape(Ms, -1), vec(ple_norm), w_p, w_pg, tm=tm_s, name="ple_sample")
        conv_s.append(u.reshape(Bs, S, d_conv)[:, S - 2:])
        k_s.append(k.reshape(Bs, S, N_KV, HEAD_DIM))
        v_s.append(v.reshape(Bs, S, N_KV, HEAD_DIM))

    return (yp.reshape(B, T, D), ys.reshape(Bs, S, D), jnp.stack(conv_p), jnp.stack(k_p), jnp.stack(v_p),
            jnp.stack(conv_s), jnp.stack(k_s), jnp.stack(v_s))
```

```python
import functools
import math

import jax
import jax.numpy as jnp
from jax import lax
from jax.experimental import pallas as pl
from jax.experimental.pallas import tpu as pltpu

_BF = jnp.bfloat16
_F32 = jnp.float32

CHUNK = 64
WINDOW = 128
N_HEADS = 16
N_KV = 4
GROUP = N_HEADS // N_KV
HEAD_DIM = 64
N_BUCKETS = 32
MAX_DIST = 128
EPS = 1e-6
LOG2E = math.log2(math.e)
QK_SCALE = HEAD_DIM ** -0.5 * LOG2E

V7X_VMEM_BYTES = 64 * 1024 * 1024
VMEM_INTERNAL_BYTES = 12 * 1024 * 1024


def _vmem_limit(block_bytes, scratch_bytes=0):
    need = 2 * sum(block_bytes) + scratch_bytes + VMEM_INTERNAL_BYTES
    return int(min(need, V7X_VMEM_BYTES - 4 * 1024 * 1024))


def _nbytes(shape, dtype):
    return math.prod(shape) * jnp.dtype(dtype).itemsize


def _dot(a, b):
    return jnp.dot(a, b, preferred_element_type=_F32)


def _rms(x, g):
    ms = jnp.mean(x * x, axis=-1, keepdims=True)
    return x * lax.rsqrt(ms + EPS) * g


BF16_TILE_ROWS = 16


def _cast_specs(arrays, grid):
    steps = math.prod(grid)

    def flat(*ids):
        s = 0
        for k, g in zip(ids, grid):
            s = s * g + k
        return s

    in_specs, out_specs, out_shape, nbytes = [], [], [], 0
    for a in arrays:
        rows, cols = a.shape
        rb = max(BF16_TILE_ROWS, rows // steps)
        assert rows % rb == 0 and steps % (rows // rb) == 0, (a.shape, steps)
        rep = steps // (rows // rb)
        idx = lambda *ids, rep=rep: (flat(*ids) // rep, 0)
        in_specs.append(pl.BlockSpec((rb, cols), idx))
        out_specs.append(pl.BlockSpec((rb, cols), idx))
        out_shape.append(jax.ShapeDtypeStruct((rows, cols), _BF))
        nbytes += _nbytes((rb, cols), _F32) + _nbytes((rb, cols), _BF)
    return in_specs, out_specs, out_shape, nbytes


def _with_casts(body, n_in, n_out, n_cast):
    def kernel(*refs):
        outs_at = n_in + n_cast
        scratch_at = outs_at + n_out + n_cast
        for src, dst in zip(refs[n_in:outs_at], refs[outs_at + n_out:scratch_at]):
            dst[...] = src[...].astype(_BF)
        body(*refs[:n_in], *refs[outs_at:outs_at + n_out], *refs[scratch_at:])
    return kernel


def _ffn_kernel(*refs, emit_norm):
    if emit_norm:
        x_ref, g_ref, wg_ref, wu_ref, wd_ref, g2_ref, out_ref, n_ref, xn_ref = refs
    else:
        x_ref, g_ref, wg_ref, wu_ref, wd_ref, out_ref, xn_ref = refs
    j = pl.program_id(1)
    last = pl.num_programs(1) - 1

    @pl.when(j == 0)
    def _():
        xn_ref[...] = _rms(x_ref[...], g_ref[...]).astype(_BF)

    xn = xn_ref[...]
    gate = _dot(xn, wg_ref[...])
    up = _dot(xn, wu_ref[...])
    hid = (jax.nn.silu(gate) * up).astype(_BF)
    out_ref[...] = jnp.where(j == 0, 0.0, out_ref[...]) + _dot(hid, wd_ref[...])

    @pl.when(j == last)
    def _():
        h = x_ref[...] + 0.5 * out_ref[...]
        out_ref[...] = h
        if emit_norm:
            n_ref[...] = _rms(h, g2_ref[...]).astype(_BF)


def _ffn(x, g, wg, wu, wd, g2, *, tm, tf, name, casts=()):
    M, D = x.shape
    F = wg.shape[1]
    emit_norm = g2 is not None
    grid = (M // tm, F // tf)
    row = lambda i, j: (i, 0)
    in_specs = [
        pl.BlockSpec((tm, D), row),
        pl.BlockSpec((1, D), lambda i, j: (0, 0)),
        pl.BlockSpec((D, tf), lambda i, j: (0, j)),
        pl.BlockSpec((D, tf), lambda i, j: (0, j)),
        pl.BlockSpec((tf, D), lambda i, j: (j, 0)),
    ]
    args = [x, g, wg, wu, wd]
    out_shape = [jax.ShapeDtypeStruct((M, D), _F32)]
    out_specs = [pl.BlockSpec((tm, D), row)]
    blocks = [_nbytes((tm, D), _F32) * 2, 3 * _nbytes((D, tf), _BF)]
    if emit_norm:
        in_specs.append(pl.BlockSpec((1, D), lambda i, j: (0, 0)))
        args.append(g2)
        out_shape.append(jax.ShapeDtypeStruct((M, D), _BF))
        out_specs.append(pl.BlockSpec((tm, D), row))
        blocks.append(_nbytes((tm, D), _BF))
    c_in, c_out, c_shape, c_bytes = _cast_specs(casts, grid)
    blocks.append(c_bytes)
    res = pl.pallas_call(
        _with_casts(functools.partial(_ffn_kernel, emit_norm=emit_norm), len(in_specs), len(out_specs), len(casts)),
        grid=grid,
        in_specs=in_specs + c_in,
        out_specs=out_specs + c_out,
        out_shape=out_shape + c_shape,
        scratch_shapes=[pltpu.VMEM((tm, D), _BF)],
        compiler_params=pltpu.CompilerParams(
            dimension_semantics=("arbitrary", "arbitrary"),
            vmem_limit_bytes=_vmem_limit(blocks, _nbytes((tm, D), _BF)),
        ),
        name=name,
    )(*args, *casts)
    return res


def _conv_kernel(*refs, carry_rows, seg, tiles_per_seg, d_conv, cw):
    if carry_rows:
        n_ref, w_ref, cwt_ref, ucb_ref, ulast_ref, carry_ref = refs
    else:
        n_ref, w_ref, cwt_ref, e1_ref, e2_ref, ucb_ref, u_ref = refs
    tm = n_ref.shape[0]
    n = n_ref[...]
    if carry_rows:
        @pl.when(pl.program_id(0) % tiles_per_seg == 0)
        def _():
            carry_ref[...] = jnp.zeros_like(carry_ref)

    for c in range(d_conv // cw):
        lo = c * cw
        cb = _dot(n, w_ref[:, lo:lo + cw])
        cc = _dot(n, w_ref[:, d_conv + lo:d_conv + lo + cw])
        cv = _dot(n, w_ref[:, 2 * d_conv + lo:2 * d_conv + lo + cw])
        u = cc * cv
        r1 = pltpu.roll(u, 1, 0)
        r2 = pltpu.roll(u, 2, 0)
        if carry_rows:
            prev = carry_ref[:, lo:lo + cw]
            rows8 = lax.broadcasted_iota(jnp.int32, (8, cw), 0)
            h1 = jnp.where(rows8 < 1, pltpu.roll(prev, 1, 0), r1[:8])
            h2 = jnp.where(rows8 < 2, pltpu.roll(prev, 2, 0), r2[:8])
            r1 = jnp.concatenate([h1, r1[8:]], axis=0)
            r2 = jnp.concatenate([h2, r2[8:]], axis=0)
            carry_ref[:, lo:lo + cw] = u[tm - 8:]
            ulast_ref[0, :, lo:lo + cw] = u[tm - 8:]
        else:
            pos = lax.broadcasted_iota(jnp.int32, (tm, cw), 0) % seg
            r1 = jnp.where(pos < 1, e1_ref[:, lo:lo + cw], r1)
            r2 = jnp.where(pos < 2, e2_ref[:, lo:lo + cw], r2)
            u_ref[:, lo:lo + cw] = u
        w0 = cwt_ref[0:1, lo:lo + cw]
        w1 = cwt_ref[1:2, lo:lo + cw]
        w2 = cwt_ref[2:3, lo:lo + cw]
        conv = w0 * r2 + w1 * r1 + w2 * u
        ucb_ref[:, lo:lo + cw] = (cb * conv).astype(_BF)


def _qkv_kernel(n_ref, w_ref, gq_ref, gk_ref, ones_ref, q_ref, k_ref, v_ref, *, cw, col0):
    ones = ones_ref[...]
    dq = q_ref.shape[1]
    dk = k_ref.shape[1]
    p = _dot(n_ref[...], w_ref[:, col0:col0 + dq + 2 * dk])
    x = p[:, :dq + dk]
    sq = x * x
    hi = sq.astype(_BF)
    lo = (sq - hi.astype(_F32)).astype(_BF)
    ss = jnp.concatenate(
        [_dot(hi[:, c * cw:(c + 1) * cw], ones) + _dot(lo[:, c * cw:(c + 1) * cw], ones)
         for c in range((dq + dk) // cw)], axis=1)
    xr = x * lax.rsqrt(ss * (1.0 / HEAD_DIM) + EPS)
    q_ref[...] = (xr[:, :dq] * gq_ref[...] * QK_SCALE).astype(_BF)
    k_ref[...] = xr[:, dq:] * gk_ref[...]
    v_ref[...] = p[:, dq + dk:]


def _inproj_kernel(*refs, carry_rows, seg, tiles_per_seg, d_conv, cw):
    if carry_rows:
        (n_ref, w_ref, cwt_ref, gq_ref, gk_ref, ones_ref,
         ucb_ref, ulast_ref, q_ref, k_ref, v_ref, carry_ref) = refs
        conv_refs = (n_ref, w_ref, cwt_ref, ucb_ref, ulast_ref, carry_ref)
    else:
        (n_ref, w_ref, cwt_ref, gq_ref, gk_ref, ones_ref, e1_ref, e2_ref,
         ucb_ref, u_ref, q_ref, k_ref, v_ref) = refs
        conv_refs = (n_ref, w_ref, cwt_ref, e1_ref, e2_ref, ucb_ref, u_ref)
    _conv_kernel(*conv_refs, carry_rows=carry_rows, seg=seg, tiles_per_seg=tiles_per_seg, d_conv=d_conv, cw=cw)
    _qkv_kernel(n_ref, w_ref, gq_ref, gk_ref, ones_ref, q_ref, k_ref, v_ref, cw=cw, col0=3 * d_conv)


def _inproj(n, w_in, conv_w, gq, gk, ones, *, tm, name, batch=None, seg=None, edges=(), casts=()):
    M, D = n.shape
    d_conv = conv_w.shape[1]
    cw = ones.shape[0]
    dq, dk = N_HEADS * HEAD_DIM, N_KV * HEAD_DIM
    wcols = 3 * d_conv + dq + 2 * dk
    carry_rows = batch is not None
    grid = (M // tm,)
    tiles_per_seg = (M // batch) // tm if carry_rows else None
    c_in, c_out, c_shape, c_bytes = _cast_specs(casts, grid)
    const = lambda i: (0, 0)
    row = lambda i: (i, 0)
    in_specs = [
        pl.BlockSpec((tm, D), row),
        pl.BlockSpec((D, wcols), const, pipeline_mode=pl.Buffered(1)),
        pl.BlockSpec((3, d_conv), const),
        pl.BlockSpec((1, dq), const),
        pl.BlockSpec((1, dk), const),
        pl.BlockSpec((cw, cw), const),
    ] + [pl.BlockSpec((tm, d_conv), row) for _ in edges]
    if carry_rows:
        u_spec = pl.BlockSpec((1, 8, d_conv), lambda i: (i // tiles_per_seg, 0, 0))
        u_shape = jax.ShapeDtypeStruct((batch, 8, d_conv), _F32)
        scratch = [pltpu.VMEM((8, d_conv), _F32)]
    else:
        u_spec = pl.BlockSpec((tm, d_conv), row)
        u_shape = jax.ShapeDtypeStruct((M, d_conv), _F32)
        scratch = []
    out_specs = [pl.BlockSpec((tm, d_conv), row), u_spec,
                 pl.BlockSpec((tm, dq), row), pl.BlockSpec((tm, dk), row), pl.BlockSpec((tm, dk), row)]
    out_shape = [jax.ShapeDtypeStruct((M, d_conv), _BF), u_shape, jax.ShapeDtypeStruct((M, dq), _BF),
                 jax.ShapeDtypeStruct((M, dk), _F32), jax.ShapeDtypeStruct((M, dk), _F32)]
    blocks = [_nbytes((tm, D), _BF), _nbytes((D, wcols), _BF) // 2, (1 + len(edges)) * _nbytes((tm, d_conv), _F32),
              _nbytes((tm, d_conv + dq), _BF), 2 * _nbytes((tm, dk), _F32), c_bytes]
    body = functools.partial(_inproj_kernel, carry_rows=carry_rows, seg=seg, tiles_per_seg=tiles_per_seg,
                             d_conv=d_conv, cw=cw)
    return pl.pallas_call(
        _with_casts(body, len(in_specs), len(out_specs), len(casts)),
        grid=grid,
        in_specs=in_specs + c_in,
        out_specs=out_specs + c_out,
        out_shape=out_shape + c_shape,
        scratch_shapes=scratch,
        compiler_params=pltpu.CompilerParams(
            dimension_semantics=("arbitrary",),
            vmem_limit_bytes=_vmem_limit(blocks),
        ),
        name=name,
    )(n, w_in, conv_w, gq, gk, ones, *edges, *casts)


def _bias_kernel(tab_ref, bkt_ref, out_ref, *, mask_step):
    bkt = bkt_ref[...]
    key = lax.broadcasted_iota(jnp.int32, bkt.shape, 1)
    for h in range(out_ref.shape[1]):
        acc = jnp.zeros(bkt.shape, _F32)
        for b in range(tab_ref.shape[0]):
            acc = jnp.where(bkt == b, tab_ref[b, h] * LOG2E, acc)
        for v in range(out_ref.shape[0]):
            out_ref[v, h] = jnp.where(key < v * mask_step, -jnp.inf, acc)


def _rel_bias(table, bucket, n_variants, mask_step, name):
    nq, nk = bucket.shape
    nh = table.shape[1]
    out = pl.pallas_call(
        functools.partial(_bias_kernel, mask_step=mask_step),
        in_specs=[pl.BlockSpec(memory_space=pltpu.SMEM), pl.BlockSpec((nq, nk), lambda: (0, 0))],
        out_specs=pl.BlockSpec((n_variants, nh, nq, nk), lambda: (0, 0, 0, 0)),
        out_shape=jax.ShapeDtypeStruct((n_variants, nh, nq, nk), _F32),
        name=name,
    )(table, bucket)
    return out.transpose(0, 3, 1, 2).reshape(n_variants, nk, nh * nq)


def _t5_bucket(rel):
    nb = N_BUCKETS // 2
    max_exact = nb // 2
    ret = jnp.where(rel > 0, nb, 0)
    n = jnp.abs(rel)
    nf = jnp.maximum(n, 1).astype(jnp.float32)
    large = max_exact + (jnp.log(nf / max_exact) / math.log(MAX_DIST / max_exact) * (nb - max_exact)).astype(jnp.int32)
    large = jnp.minimum(large, nb - 1)
    return ret + jnp.where(n < max_exact, n, large)


def _bucket_index(n_q, n_k):
    i = jnp.arange(n_q)[:, None]
    j = jnp.arange(n_k)[None, :]
    return _t5_bucket(j - WINDOW - i).astype(jnp.int32)


_NT = (((1,), (1,)), ((), ()))
_TN = (((0,), (0,)), ((), ()))


def _sink_softmax_keys_major(lg, sink_t):
    sink2 = sink_t * LOG2E
    m = jnp.maximum(jnp.max(lg, axis=0, keepdims=True), sink2)
    e = jnp.exp2(lg - m)
    den = jnp.sum(e, axis=0, keepdims=True) + jnp.exp2(sink2 - m)
    return e, 1.0 / den


def _attn_chunk_keys_major(qc, kw, vw, bias_t, sink_t):
    nq = qc.shape[0]
    gq = GROUP * nq
    logits = []
    for kv in range(N_KV):
        qg = jnp.concatenate(
            [qc[:, (kv * GROUP + g) * HEAD_DIM:(kv * GROUP + g + 1) * HEAD_DIM] for g in range(GROUP)], axis=0)
        kk = kw[:, kv * HEAD_DIM:(kv + 1) * HEAD_DIM]
        logits.append(lax.dot_general(kk, qg, _NT, preferred_element_type=_F32))
    e, inv = _sink_softmax_keys_major(jnp.concatenate(logits, axis=1) + bias_t, sink_t)
    e = e.astype(_BF)
    outs = []
    for kv in range(N_KV):
        vv = vw[:, kv * HEAD_DIM:(kv + 1) * HEAD_DIM]
        cols = slice(kv * gq, (kv + 1) * gq)
        outs.append(lax.dot_general(vv, e[:, cols], _TN, preferred_element_type=_F32) * inv[:, cols])
    return jnp.concatenate(outs, axis=0).T


def _attn_prompt_kernel(q_ref, kp_ref, kc_ref, vp_ref, vc_ref, bias_ref, sink_ref, o_ref, kwin, vwin,
                        *, tiles_per_batch):
    ta = q_ref.shape[0]
    nk = WINDOW + CHUNK
    dkv = N_KV * HEAD_DIM
    kwin[0:WINDOW, :] = kp_ref[...].astype(_BF)
    kwin[WINDOW:, :] = kc_ref[...].astype(_BF)
    vwin[0:WINDOW, :] = vp_ref[...].astype(_BF)
    vwin[WINDOW:, :] = vc_ref[...].astype(_BF)
    tile_pos = (pl.program_id(0) % tiles_per_batch) * ta

    def chunk(c, carry):
        r = pl.multiple_of(c * CHUNK, CHUNK)
        n_before = jnp.maximum(0, (WINDOW - (tile_pos + r)) // CHUNK)
        o = _attn_chunk_keys_major(q_ref[pl.ds(r, CHUNK), :], kwin[pl.ds(r, nk), :], vwin[pl.ds(r, nk), :],
                                   bias_ref[n_before], sink_ref[...])
        for g in range(GROUP):
            o_ref[pl.ds(r, CHUNK), g * dkv:(g + 1) * dkv] = o[g * CHUNK:(g + 1) * CHUNK].astype(_BF)
        return carry

    lax.fori_loop(0, ta // CHUNK, chunk, 0, unroll=8)


def _attn_prompt(q, k, v, bias, sink, *, batch, ta, name, casts=()):
    M, dq = q.shape
    dk = k.shape[1]
    tiles_per_batch = (M // batch) // ta
    wb = ta // WINDOW
    nk = WINDOW + CHUNK

    def prev(i):
        return (jnp.where(i % tiles_per_batch == 0, i * wb, i * wb - 1), 0)

    row = lambda i: (i, 0)
    grid = (M // ta,)
    c_in, c_out, c_shape, c_bytes = _cast_specs(casts, grid)
    blocks = [_nbytes((ta, dq), _BF) * 2, 2 * _nbytes((WINDOW + ta, dk), _F32),
              _nbytes(bias.shape, _F32), _nbytes((8, N_HEADS * CHUNK), _F32), c_bytes]
    return pl.pallas_call(
        _with_casts(functools.partial(_attn_prompt_kernel, tiles_per_batch=tiles_per_batch), 7, 1, len(casts)),
        grid=grid,
        in_specs=[
            pl.BlockSpec((ta, dq), row),
            pl.BlockSpec((WINDOW, dk), prev),
            pl.BlockSpec((ta, dk), row),
            pl.BlockSpec((WINDOW, dk), prev),
            pl.BlockSpec((ta, dk), row),
            pl.BlockSpec(bias.shape, lambda i: (0, 0, 0)),
            pl.BlockSpec((1, N_HEADS * CHUNK), lambda i: (0, 0)),
        ] + c_in,
        out_specs=[pl.BlockSpec((ta, dq), row)] + c_out,
        out_shape=[jax.ShapeDtypeStruct((M, dq), _BF)] + c_shape,
        scratch_shapes=[pltpu.VMEM((WINDOW + ta, dk), _BF), pltpu.VMEM((WINDOW + ta, dk), _BF)],
        compiler_params=pltpu.CompilerParams(
            dimension_semantics=("arbitrary",),
            vmem_limit_bytes=_vmem_limit(blocks, 2 * _nbytes((WINDOW + ta, dk), _BF)),
        ),
        name=name,
    )(q, k, k, v, v, bias, sink, *casts)


LANES = 128


def _attn_sample_kernel(q_ref, ck_ref, cv_ref, k_ref, v_ref, bias_ref, sink_ref, o_ref, *, nb, s):
    dkv = N_KV * HEAD_DIM
    gs = GROUP * s
    assert LANES == 2 * HEAD_DIM and dkv == 2 * LANES
    for b in range(nb):
        rows = slice(b * s, (b + 1) * s)
        kw = jnp.concatenate([ck_ref[b].astype(_BF), k_ref[rows, :].astype(_BF)], axis=0)
        vw = jnp.concatenate([cv_ref[b].astype(_BF), v_ref[rows, :].astype(_BF)], axis=0)
        qf = q_ref[rows, :].astype(_F32)
        blocks = []
        for kv in range(N_KV):
            for g in range(GROUP):
                h = kv * GROUP + g
                pair = qf[:, (h // 2) * LANES:(h // 2 + 1) * LANES]
                if h % 2 != kv % 2:
                    pair = pltpu.roll(pair, HEAD_DIM, 1)
                half = lax.broadcasted_iota(jnp.int32, pair.shape, 1) // HEAD_DIM
                piece = jnp.where(half == kv % 2, pair, 0.0)
                zero = jnp.zeros_like(piece)
                blocks.append(jnp.concatenate([piece, zero] if kv < 2 else [zero, piece], axis=1))
        q_bd = jnp.concatenate(blocks, axis=0).astype(_BF)
        lg = lax.dot_general(kw, q_bd, _NT, preferred_element_type=_F32) + bias_ref[0]
        e, inv = _sink_softmax_keys_major(lg, sink_ref[...])
        z = lax.dot_general((e * inv).astype(_BF), vw, _TN, preferred_element_type=_F32)
        row_kv = lax.broadcasted_iota(jnp.int32, z.shape, 0) // gs
        col_kv = lax.broadcasted_iota(jnp.int32, z.shape, 1) // HEAD_DIM
        z = jnp.where(row_kv == col_kv, z, 0.0)
        r = z[0:gs]
        for kv in range(1, N_KV):
            r = r + z[kv * gs:(kv + 1) * gs]
        for g in range(GROUP):
            o_ref[rows, g * dkv:(g + 1) * dkv] = r[g * s:(g + 1) * s].astype(_BF)


def _attn_sample(q, k, v, cache_k, cache_v, bias, sink, *, nb, name):
    B, W, dk = cache_k.shape
    M, dq = q.shape
    S = M // B
    row = lambda b: (b, 0)
    return pl.pallas_call(
        functools.partial(_attn_sample_kernel, nb=nb, s=S),
        grid=(B // nb,),
        in_specs=[
            pl.BlockSpec((nb * S, dq), row),
            pl.BlockSpec((nb, W, dk), lambda b: (b, 0, 0)),
            pl.BlockSpec((nb, W, dk), lambda b: (b, 0, 0)),
            pl.BlockSpec((nb * S, dk), row),
            pl.BlockSpec((nb * S, dk), row),
            pl.BlockSpec(bias.shape, lambda b: (0, 0, 0)),
            pl.BlockSpec(sink.shape, lambda b: (0, 0)),
        ],
        out_specs=pl.BlockSpec((nb * S, dq), row),
        out_shape=jax.ShapeDtypeStruct((M, dq), _BF),
        compiler_params=pltpu.CompilerParams(dimension_semantics=("arbitrary",)),
        name=name,
    )(q, cache_k, cache_v, k, v, bias, sink)


def _merge_kernel(h_ref, n_ref, ucb_ref, o_ref, wco_ref, wao_ref, wgc_ref, wga_ref, wout_ref, out_ref):
    j = pl.program_id(1)
    last = pl.num_programs(1) - 1
    n = n_ref[...]
    y_conv = _dot(ucb_ref[...], wco_ref[...])
    y_attn = _dot(o_ref[...], wao_ref[...])
    g_conv = jax.nn.sigmoid(_dot(n, wgc_ref[...]))
    g_attn = jax.nn.sigmoid(_dot(n, wga_ref[...]))
    mix = (g_conv * y_conv + g_attn * y_attn).astype(_BF)
    out_ref[...] = jnp.where(j == 0, 0.0, out_ref[...]) + _dot(mix, wout_ref[...])

    @pl.when(j == last)
    def _():
        out_ref[...] = h_ref[...] + out_ref[...]


def _merge(h, n, ucb, o, w_conv_out, w_attn_o, w_in, w_out, *, gc_col, ga_col, tm, tn, name, casts=()):
    M, D = h.shape
    grid = (M // tm, D // tn)
    dc = ucb.shape[1]
    da = o.shape[1]
    row = lambda i, j: (i, 0)
    col = lambda i, j: (0, j)
    blocks = [2 * _nbytes((tm, D), _F32), _nbytes((tm, D), _BF), _nbytes((tm, dc), _BF), _nbytes((tm, da), _BF),
              _nbytes((dc + da + 3 * D, tn), _BF)]
    c_in, c_out, c_shape, c_bytes = _cast_specs(casts, grid)
    blocks.append(c_bytes)
    return pl.pallas_call(
        _with_casts(_merge_kernel, 9, 1, len(casts)),
        grid=grid,
        in_specs=[
            pl.BlockSpec((tm, D), row),
            pl.BlockSpec((tm, D), row),
            pl.BlockSpec((tm, dc), row),
            pl.BlockSpec((tm, da), row),
            pl.BlockSpec((dc, tn), col),
            pl.BlockSpec((da, tn), col),
            pl.BlockSpec((D, tn), lambda i, j: (0, gc_col // tn + j)),
            pl.BlockSpec((D, tn), lambda i, j: (0, ga_col // tn + j)),
            pl.BlockSpec((tn, D), lambda i, j: (j, 0)),
        ] + c_in,
        out_specs=[pl.BlockSpec((tm, D), row)] + c_out,
        out_shape=[jax.ShapeDtypeStruct((M, D), _F32)] + c_shape,
        compiler_params=pltpu.CompilerParams(
            dimension_semantics=("arbitrary", "arbitrary"),
            vmem_limit_bytes=_vmem_limit(blocks),
        ),
        name=name,
    )(h, n, ucb, o, w_conv_out, w_attn_o, w_in, w_in, w_out, *casts)


def _ple_kernel(h_ref, pe_ref, g_ref, wp_ref, wg_ref, out_ref):
    h = h_ref[...]
    gate = jax.nn.sigmoid(_dot(_rms(h, g_ref[...]).astype(_BF), wg_ref[...]))
    emb = _dot(pe_ref[...].astype(_BF), wp_ref[...])
    out_ref[...] = h + emb * gate


def _ple(h, pe, g, w_ple, w_gate, *, tm, name):
    M, D = h.shape
    dp = pe.shape[1]
    row = lambda i: (i, 0)
    const = lambda i: (0, 0)
    blocks = [2 * _nbytes((tm, D), _F32), _nbytes((tm, dp), _F32), _nbytes((dp + D, D), _BF)]
    return pl.pallas_call(
        _ple_kernel,
        grid=(M // tm,),
        in_specs=[
            pl.BlockSpec((tm, D), row),
            pl.BlockSpec((tm, dp), row),
            pl.BlockSpec((1, D), const),
            pl.BlockSpec((dp, D), const),
            pl.BlockSpec((D, D), const),
        ],
        out_specs=pl.BlockSpec((tm, D), row),
        out_shape=jax.ShapeDtypeStruct((M, D), _F32),
        compiler_params=pltpu.CompilerParams(
            dimension_semantics=("arbitrary",),
            vmem_limit_bytes=_vmem_limit(blocks),
        ),
        name=name,
    )(h, pe, g, w_ple, w_gate)


def _row_tile(m, want):
    return want if m % want == 0 else m


def kernel(x_prompt, x_sample, p_prompt, p_sample, state_conv, cache_k, cache_v, rel_table, ffn1_norm, ffn1_wg, ffn1_wu, ffn1_wd, mix_norm, w_in, conv_w, q_norm, k_norm, attn_sink, w_conv_out, w_attn_o, w_out, ffn2_norm, ffn2_wg, ffn2_wu, ffn2_wd, ple_norm, w_ple, w_ple_gate):
    B, T, D = x_prompt.shape
    Bs, S, _ = x_sample.shape
    depth = ffn1_wg.shape[0]
    d_conv = conv_w.shape[2]
    dq, dk = N_HEADS * HEAD_DIM, N_KV * HEAD_DIM
    qkv_col = 3 * d_conv
    gc_col = qkv_col + dq + 2 * dk
    ga_col = gc_col + D
    Mp, Ms = B * T, Bs * S
    tm_p = _row_tile(Mp, 512)
    tm_s = _row_tile(Ms, 512)
    tf = 1024
    tn = 512
    cw = 256

    nk_p, nk_s = WINDOW + CHUNK, WINDOW + S
    bias_p = _rel_bias(rel_table, _bucket_index(CHUNK, nk_p), WINDOW // CHUNK + 1, CHUNK, "rel_bias_prompt")
    bias_s = _rel_bias(rel_table, _bucket_index(S, nk_s), 1, 0, "rel_bias_sample")
    lane = jnp.arange(cw) // HEAD_DIM
    ones = (lane[:, None] == lane[None, :]).astype(_BF)

    yp = x_prompt.reshape(Mp, D)
    ys = x_sample.reshape(Ms, D)
    conv_p, k_p, v_p, conv_s, k_s, v_s = [], [], [], [], [], []
    for l in range(depth):
        bf = lambda w: w[l].astype(_BF)
        vec = lambda g: g[l].reshape(1, -1)
        f1 = (vec(ffn1_norm), bf(ffn1_wg), bf(ffn1_wu), bf(ffn1_wd))
        gq = jnp.tile(q_norm[l], N_HEADS).reshape(1, dq)
        gk = jnp.tile(k_norm[l], N_KV).reshape(1, dk)
        sink_p = jnp.repeat(attn_sink[l], CHUNK).reshape(1, N_HEADS * CHUNK)
        sink_s = jnp.repeat(attn_sink[l], S).reshape(1, N_HEADS * S)

        h1, n, w_in_l = _ffn(yp, *f1, vec(mix_norm), tm=tm_p, tf=tf, name="ffn1_prompt", casts=(w_in[l],))
        ucb, ulast, q, k, v, f2_wg, f2_wu, w_co, w_ao, w_o = _inproj(
            n, w_in_l, conv_w[l], gq, gk, ones, tm=tm_p, batch=B, name="inproj_prompt",
            casts=(ffn2_wg[l], ffn2_wu[l], w_conv_out[l], w_attn_o[l], w_out[l]))
        o, f2_wd, w_p, w_pg = _attn_prompt(q, k, v, bias_p, sink_p, batch=B, ta=2 * tm_p, name="attn_prompt",
                                           casts=(ffn2_wd[l], w_ple[l], w_ple_gate[l]))
        w_ao_p = w_ao.reshape(N_KV, GROUP, HEAD_DIM, D).transpose(1, 0, 2, 3).reshape(dq, D)
        h2, = _merge(h1, n, ucb, o, w_co, w_ao_p, w_in_l, w_o, gc_col=gc_col, ga_col=ga_col, tm=tm_p, tn=tn,
                     name="merge_prompt")
        f2 = (vec(ffn2_norm), f2_wg, f2_wu, f2_wd)
        h3, = _ffn(h2, *f2, None, tm=tm_p, tf=tf, name="ffn2_prompt")
        yp = _ple(h3, p_prompt[l].reshape(Mp, -1), vec(ple_norm), w_p, w_pg, tm=tm_p, name="ple_prompt")
        conv_p.append(ulast[:, 6:8])
        k_p.append(k.reshape(B, T, dk)[:, T - WINDOW:].reshape(B, WINDOW, N_KV, HEAD_DIM))
        v_p.append(v.reshape(B, T, dk)[:, T - WINDOW:].reshape(B, WINDOW, N_KV, HEAD_DIM))

        st = state_conv[l]
        zeros = jnp.zeros((Bs, S - 2, d_conv), _F32)
        e2 = jnp.concatenate([st, zeros], axis=1).reshape(Ms, d_conv)
        e1 = jnp.concatenate([st[:, 1:], zeros, zeros[:, :1]], axis=1).reshape(Ms, d_conv)
        h1, n = _ffn(ys, *f1, vec(mix_norm), tm=tm_s, tf=tf, name="ffn1_sample")
        ucb, u, q, k, v = _inproj(n, w_in_l, conv_w[l], gq, gk, ones, tm=tm_s, seg=S, edges=(e1, e2),
                                  name="inproj_sample")
        o = _attn_sample(q, k, v, cache_k[l].reshape(Bs, WINDOW, dk), cache_v[l].reshape(Bs, WINDOW, dk),
                         bias_s, sink_s, nb=4, name="attn_sample")
        h2, = _merge(h1, n, ucb, o, w_co, w_ao_p, w_in_l, w_o, gc_col=gc_col, ga_col=ga_col,
                     tm=tm_s, tn=tn, name="merge_sample")
        h3, = _ffn(h2, *f2, None, tm=tm_s, tf=tf, name="ffn2_sample")
        ys = _ple(h3, p_sample[l].reshape(Ms, -1), vec(ple_norm), w_p, w_pg, tm=tm_s, name="ple_sample")
        conv_s.append(u.reshape(Bs, S, d_conv)[:, S - 2:])
        k_s.append(k.reshape(Bs, S, N_KV, HEAD_DIM))
        v_s.append(v.reshape(Bs, S, N_KV, HEAD_DIM))

    return (yp.reshape(B, T, D), ys.reshape(Bs, S, D), jnp.stack(conv_p), jnp.stack(k_p), jnp.stack(v_p),
            jnp.stack(conv_s), jnp.stack(k_s), jnp.stack(v_s))
```

```python
import functools
import math

import jax
import jax.numpy as jnp
from jax import lax
from jax.experimental import pallas as pl
from jax.experimental.pallas import tpu as pltpu

_BF = jnp.bfloat16
_F32 = jnp.float32

CHUNK = 64
WINDOW = 128
N_HEADS = 16
N_KV = 4
GROUP = N_HEADS // N_KV
HEAD_DIM = 64
N_BUCKETS = 32
MAX_DIST = 128
EPS = 1e-6
LOG2E = math.log2(math.e)
QK_SCALE = HEAD_DIM ** -0.5 * LOG2E

V7X_VMEM_BYTES = 64 * 1024 * 1024
VMEM_INTERNAL_BYTES = 12 * 1024 * 1024


def _vmem_limit(block_bytes, scratch_bytes=0):
    need = 2 * sum(block_bytes) + scratch_bytes + VMEM_INTERNAL_BYTES
    return int(min(need, V7X_VMEM_BYTES - 4 * 1024 * 1024))


def _nbytes(shape, dtype):
    return math.prod(shape) * jnp.dtype(dtype).itemsize


def _dot(a, b):
    return jnp.dot(a, b, preferred_element_type=_F32)


def _rms(x, g):
    ms = jnp.mean(x * x, axis=-1, keepdims=True)
    return x * lax.rsqrt(ms + EPS) * g


BF16_TILE_ROWS = 16


def _cast_specs(arrays, grid):
    steps = math.prod(grid)

    def flat(*ids):
        s = 0
        for k, g in zip(ids, grid):
            s = s * g + k
        return s

    in_specs, out_specs, out_shape, nbytes = [], [], [], 0
    for a in arrays:
        rows, cols = a.shape
        rb = max(BF16_TILE_ROWS, rows // steps)
        assert rows % rb == 0 and steps % (rows // rb) == 0, (a.shape, steps)
        rep = steps // (rows // rb)
        idx = lambda *ids, rep=rep: (flat(*ids) // rep, 0)
        in_specs.append(pl.BlockSpec((rb, cols), idx))
        out_specs.append(pl.BlockSpec((rb, cols), idx))
        out_shape.append(jax.ShapeDtypeStruct((rows, cols), _BF))
        nbytes += _nbytes((rb, cols), _F32) + _nbytes((rb, cols), _BF)
    return in_specs, out_specs, out_shape, nbytes


def _with_casts(body, n_in, n_out, n_cast):
    def kernel(*refs):
        outs_at = n_in + n_cast
        scratch_at = outs_at + n_out + n_cast
        for src, dst in zip(refs[n_in:outs_at], refs[outs_at + n_out:scratch_at]):
            dst[...] = src[...].astype(_BF)
        body(*refs[:n_in], *refs[outs_at:outs_at + n_out], *refs[scratch_at:])
    return kernel


def _ffn_kernel(*refs, emit_norm, keep_weights):
    x_ref, g_ref, wg_ref, wu_ref, wd_ref = refs[:5]
    rest = list(refs[5:])
    g2_ref = rest.pop(0) if emit_norm else None
    out_ref = rest.pop(0)
    n_ref = rest.pop(0) if emit_norm else None
    weight_copies = [rest.pop(0) for _ in range(3)] if keep_weights else []
    xn_ref, = rest
    j = pl.program_id(1)
    last = pl.num_programs(1) - 1

    @pl.when(j == 0)
    def _():
        xn_ref[...] = _rms(x_ref[...], g_ref[...]).astype(_BF)

    wg, wu, wd = wg_ref[...], wu_ref[...], wd_ref[...]
    if keep_weights:
        wg, wu, wd = wg.astype(_BF), wu.astype(_BF), wd.astype(_BF)
        for dst, w in zip(weight_copies, (wg, wu, wd)):
            dst[...] = w
    xn = xn_ref[...]
    gate = _dot(xn, wg)
    up = _dot(xn, wu)
    hid = (jax.nn.silu(gate) * up).astype(_BF)
    out_ref[...] = jnp.where(j == 0, 0.0, out_ref[...]) + _dot(hid, wd)

    @pl.when(j == last)
    def _():
        h = x_ref[...] + 0.5 * out_ref[...]
        out_ref[...] = h
        if emit_norm:
            n_ref[...] = _rms(h, g2_ref[...]).astype(_BF)


def _ffn(x, g, wg, wu, wd, g2, *, tm, tf, name, casts=()):
    M, D = x.shape
    F = wg.shape[1]
    emit_norm = g2 is not None
    keep_weights = wg.dtype == _F32
    grid = (M // tm, F // tf)
    assert not keep_weights or grid[0] == 1
    row = lambda i, j: (i, 0)
    w_specs = [
        pl.BlockSpec((D, tf), lambda i, j: (0, j)),
        pl.BlockSpec((D, tf), lambda i, j: (0, j)),
        pl.BlockSpec((tf, D), lambda i, j: (j, 0)),
    ]
    in_specs = [pl.BlockSpec((tm, D), row), pl.BlockSpec((1, D), lambda i, j: (0, 0))] + w_specs
    args = [x, g, wg, wu, wd]
    out_shape = [jax.ShapeDtypeStruct((M, D), _F32)]
    out_specs = [pl.BlockSpec((tm, D), row)]
    blocks = [_nbytes((tm, D), _F32) * 2, 3 * _nbytes((D, tf), wg.dtype)]
    if emit_norm:
        in_specs.append(pl.BlockSpec((1, D), lambda i, j: (0, 0)))
        args.append(g2)
        out_shape.append(jax.ShapeDtypeStruct((M, D), _BF))
        out_specs.append(pl.BlockSpec((tm, D), row))
        blocks.append(_nbytes((tm, D), _BF))
    if keep_weights:
        out_shape += [jax.ShapeDtypeStruct(w.shape, _BF) for w in (wg, wu, wd)]
        out_specs += w_specs
        blocks.append(3 * _nbytes((D, tf), _BF))
    c_in, c_out, c_shape, c_bytes = _cast_specs(casts, grid)
    blocks.append(c_bytes)
    body = functools.partial(_ffn_kernel, emit_norm=emit_norm, keep_weights=keep_weights)
    res = pl.pallas_call(
        _with_casts(body, len(in_specs), len(out_specs), len(casts)),
        grid=grid,
        in_specs=in_specs + c_in,
        out_specs=out_specs + c_out,
        out_shape=out_shape + c_shape,
        scratch_shapes=[pltpu.VMEM((tm, D), _BF)],
        compiler_params=pltpu.CompilerParams(
            dimension_semantics=("arbitrary", "arbitrary"),
            vmem_limit_bytes=_vmem_limit(blocks, _nbytes((tm, D), _BF)),
        ),
        name=name,
    )(*args, *casts)
    return res


def _conv_kernel(*refs, carry_rows, seg, tiles_per_seg, d_conv, cw):
    if carry_rows:
        n_ref, w_ref, cwt_ref, ucb_ref, ulast_ref, carry_ref = refs
    else:
        n_ref, w_ref, cwt_ref, e1_ref, e2_ref, ucb_ref, u_ref = refs
    tm = n_ref.shape[0]
    n = n_ref[...]
    if carry_rows:
        @pl.when(pl.program_id(0) % tiles_per_seg == 0)
        def _():
            carry_ref[...] = jnp.zeros_like(carry_ref)

    for c in range(d_conv // cw):
        lo = c * cw
        cb = _dot(n, w_ref[:, lo:lo + cw])
        cc = _dot(n, w_ref[:, d_conv + lo:d_conv + lo + cw])
        cv = _dot(n, w_ref[:, 2 * d_conv + lo:2 * d_conv + lo + cw])
        u = cc * cv
        r1 = pltpu.roll(u, 1, 0)
        r2 = pltpu.roll(u, 2, 0)
        if carry_rows:
            prev = carry_ref[:, lo:lo + cw]
            rows8 = lax.broadcasted_iota(jnp.int32, (8, cw), 0)
            h1 = jnp.where(rows8 < 1, pltpu.roll(prev, 1, 0), r1[:8])
            h2 = jnp.where(rows8 < 2, pltpu.roll(prev, 2, 0), r2[:8])
            r1 = jnp.concatenate([h1, r1[8:]], axis=0)
            r2 = jnp.concatenate([h2, r2[8:]], axis=0)
            carry_ref[:, lo:lo + cw] = u[tm - 8:]
            ulast_ref[0, :, lo:lo + cw] = u[tm - 8:]
        else:
            pos = lax.broadcasted_iota(jnp.int32, (tm, cw), 0) % seg
            r1 = jnp.where(pos < 1, e1_ref[:, lo:lo + cw], r1)
            r2 = jnp.where(pos < 2, e2_ref[:, lo:lo + cw], r2)
            u_ref[:, lo:lo + cw] = u
        w0 = cwt_ref[0:1, lo:lo + cw]
        w1 = cwt_ref[1:2, lo:lo + cw]
        w2 = cwt_ref[2:3, lo:lo + cw]
        conv = w0 * r2 + w1 * r1 + w2 * u
        ucb_ref[:, lo:lo + cw] = (cb * conv).astype(_BF)


def _qkv_kernel(n_ref, w_ref, gq_ref, gk_ref, ones_ref, q_ref, k_ref, v_ref, *, cw, col0):
    ones = ones_ref[...]
    dq = q_ref.shape[1]
    dk = k_ref.shape[1]
    p = _dot(n_ref[...], w_ref[:, col0:col0 + dq + 2 * dk])
    x = p[:, :dq + dk]
    sq = x * x
    hi = sq.astype(_BF)
    lo = (sq - hi.astype(_F32)).astype(_BF)
    ss = jnp.concatenate(
        [_dot(hi[:, c * cw:(c + 1) * cw], ones) + _dot(lo[:, c * cw:(c + 1) * cw], ones)
         for c in range((dq + dk) // cw)], axis=1)
    xr = x * lax.rsqrt(ss * (1.0 / HEAD_DIM) + EPS)
    q_ref[...] = (xr[:, :dq] * gq_ref[...] * QK_SCALE).astype(_BF)
    k_ref[...] = xr[:, dq:] * gk_ref[...]
    v_ref[...] = p[:, dq + dk:]


def _inproj_kernel(*refs, carry_rows, seg, tiles_per_seg, d_conv, cw):
    if carry_rows:
        (n_ref, w_ref, cwt_ref, gq_ref, gk_ref, ones_ref,
         ucb_ref, ulast_ref, q_ref, k_ref, v_ref, carry_ref) = refs
        conv_refs = (n_ref, w_ref, cwt_ref, ucb_ref, ulast_ref, carry_ref)
    else:
        (n_ref, w_ref, cwt_ref, gq_ref, gk_ref, ones_ref, e1_ref, e2_ref,
         ucb_ref, u_ref, q_ref, k_ref, v_ref) = refs
        conv_refs = (n_ref, w_ref, cwt_ref, e1_ref, e2_ref, ucb_ref, u_ref)
    _conv_kernel(*conv_refs, carry_rows=carry_rows, seg=seg, tiles_per_seg=tiles_per_seg, d_conv=d_conv, cw=cw)
    _qkv_kernel(n_ref, w_ref, gq_ref, gk_ref, ones_ref, q_ref, k_ref, v_ref, cw=cw, col0=3 * d_conv)


def _inproj(n, w_in, conv_w, gq, gk, ones, *, tm, name, batch=None, seg=None, edges=(), casts=()):
    M, D = n.shape
    d_conv = conv_w.shape[1]
    cw = ones.shape[0]
    dq, dk = N_HEADS * HEAD_DIM, N_KV * HEAD_DIM
    wcols = 3 * d_conv + dq + 2 * dk
    carry_rows = batch is not None
    grid = (M // tm,)
    tiles_per_seg = (M // batch) // tm if carry_rows else None
    c_in, c_out, c_shape, c_bytes = _cast_specs(casts, grid)
    const = lambda i: (0, 0)
    row = lambda i: (i, 0)
    in_specs = [
        pl.BlockSpec((tm, D), row),
        pl.BlockSpec((D, wcols), const, pipeline_mode=pl.Buffered(1)),
        pl.BlockSpec((3, d_conv), const),
        pl.BlockSpec((1, dq), const),
        pl.BlockSpec((1, dk), const),
        pl.BlockSpec((cw, cw), const),
    ] + [pl.BlockSpec((tm, d_conv), row) for _ in edges]
    if carry_rows:
        u_spec = pl.BlockSpec((1, 8, d_conv), lambda i: (i // tiles_per_seg, 0, 0))
        u_shape = jax.ShapeDtypeStruct((batch, 8, d_conv), _F32)
        scratch = [pltpu.VMEM((8, d_conv), _F32)]
    else:
        u_spec = pl.BlockSpec((tm, d_conv), row)
        u_shape = jax.ShapeDtypeStruct((M, d_conv), _F32)
        scratch = []
    out_specs = [pl.BlockSpec((tm, d_conv), row), u_spec,
                 pl.BlockSpec((tm, dq), row), pl.BlockSpec((tm, dk), row), pl.BlockSpec((tm, dk), row)]
    out_shape = [jax.ShapeDtypeStruct((M, d_conv), _BF), u_shape, jax.ShapeDtypeStruct((M, dq), _BF),
                 jax.ShapeDtypeStruct((M, dk), _F32), jax.ShapeDtypeStruct((M, dk), _F32)]
    blocks = [_nbytes((tm, D), _BF), _nbytes((D, wcols), _BF) // 2, (1 + len(edges)) * _nbytes((tm, d_conv), _F32),
              _nbytes((tm, d_conv + dq), _BF), 2 * _nbytes((tm, dk), _F32), c_bytes]
    body = functools.partial(_inproj_kernel, carry_rows=carry_rows, seg=seg, tiles_per_seg=tiles_per_seg,
                             d_conv=d_conv, cw=cw)
    return pl.pallas_call(
        _with_casts(body, len(in_specs), len(out_specs), len(casts)),
        grid=grid,
        in_specs=in_specs + c_in,
        out_specs=out_specs + c_out,
        out_shape=out_shape + c_shape,
        scratch_shapes=scratch,
        compiler_params=pltpu.CompilerParams(
            dimension_semantics=("arbitrary",),
            vmem_limit_bytes=_vmem_limit(blocks),
        ),
        name=name,
    )(n, w_in, conv_w, gq, gk, ones, *edges, *casts)


def _bias_kernel(tab_ref, bkt_ref, out_ref, *, mask_step):
    bkt = bkt_ref[...]
    key = lax.broadcasted_iota(jnp.int32, bkt.shape, 1)
    for h in range(out_ref.shape[1]):
        acc = jnp.zeros(bkt.shape, _F32)
        for b in range(tab_ref.shape[0]):
            acc = jnp.where(bkt == b, tab_ref[b, h] * LOG2E, acc)
        for v in range(out_ref.shape[0]):
            out_ref[v, h] = jnp.where(key < v * mask_step, -jnp.inf, acc)


def _rel_bias(table, bucket, n_variants, mask_step, name):
    nq, nk = bucket.shape
    nh = table.shape[1]
    out = pl.pallas_call(
        functools.partial(_bias_kernel, mask_step=mask_step),
        in_specs=[pl.BlockSpec(memory_space=pltpu.SMEM), pl.BlockSpec((nq, nk), lambda: (0, 0))],
        out_specs=pl.BlockSpec((n_variants, nh, nq, nk), lambda: (0, 0, 0, 0)),
        out_shape=jax.ShapeDtypeStruct((n_variants, nh, nq, nk), _F32),
        name=name,
    )(table, bucket)
    return out.transpose(0, 3, 1, 2).reshape(n_variants, nk, nh * nq)


def _t5_bucket(rel):
    nb = N_BUCKETS // 2
    max_exact = nb // 2
    ret = jnp.where(rel > 0, nb, 0)
    n = jnp.abs(rel)
    nf = jnp.maximum(n, 1).astype(jnp.float32)
    large = max_exact + (jnp.log(nf / max_exact) / math.log(MAX_DIST / max_exact) * (nb - max_exact)).astype(jnp.int32)
    large = jnp.minimum(large, nb - 1)
    return ret + jnp.where(n < max_exact, n, large)


def _bucket_index(n_q, n_k):
    i = jnp.arange(n_q)[:, None]
    j = jnp.arange(n_k)[None, :]
    return _t5_bucket(j - WINDOW - i).astype(jnp.int32)


_NT = (((1,), (1,)), ((), ()))
_TN = (((0,), (0,)), ((), ()))


def _sink_softmax_keys_major(lg, sink_t):
    sink2 = sink_t * LOG2E
    m = jnp.maximum(jnp.max(lg, axis=0, keepdims=True), sink2)
    e = jnp.exp2(lg - m)
    den = jnp.sum(e, axis=0, keepdims=True) + jnp.exp2(sink2 - m)
    return e, 1.0 / den


def _attn_chunk_keys_major(qc, kw, vw, bias_t, sink_t):
    nq = qc.shape[0]
    gq = GROUP * nq
    logits = []
    for kv in range(N_KV):
        qg = jnp.concatenate(
            [qc[:, (kv * GROUP + g) * HEAD_DIM:(kv * GROUP + g + 1) * HEAD_DIM] for g in range(GROUP)], axis=0)
        kk = kw[:, kv * HEAD_DIM:(kv + 1) * HEAD_DIM]
        logits.append(lax.dot_general(kk, qg, _NT, preferred_element_type=_F32))
    e, inv = _sink_softmax_keys_major(jnp.concatenate(logits, axis=1) + bias_t, sink_t)
    e = e.astype(_BF)
    outs = []
    for kv in range(N_KV):
        vv = vw[:, kv * HEAD_DIM:(kv + 1) * HEAD_DIM]
        cols = slice(kv * gq, (kv + 1) * gq)
        outs.append(lax.dot_general(vv, e[:, cols], _TN, preferred_element_type=_F32) * inv[:, cols])
    return jnp.concatenate(outs, axis=0).T


def _attn_prompt_kernel(q_ref, kp_ref, kc_ref, vp_ref, vc_ref, bias_ref, sink_ref, o_ref, kwin, vwin,
                        *, tiles_per_batch):
    ta = q_ref.shape[0]
    nk = WINDOW + CHUNK
    dkv = N_KV * HEAD_DIM
    kwin[0:WINDOW, :] = kp_ref[...].astype(_BF)
    kwin[WINDOW:, :] = kc_ref[...].astype(_BF)
    vwin[0:WINDOW, :] = vp_ref[...].astype(_BF)
    vwin[WINDOW:, :] = vc_ref[...].astype(_BF)
    tile_pos = (pl.program_id(0) % tiles_per_batch) * ta

    def chunk(c, carry):
        r = pl.multiple_of(c * CHUNK, CHUNK)
        n_before = jnp.maximum(0, (WINDOW - (tile_pos + r)) // CHUNK)
        o = _attn_chunk_keys_major(q_ref[pl.ds(r, CHUNK), :], kwin[pl.ds(r, nk), :], vwin[pl.ds(r, nk), :],
                                   bias_ref[n_before], sink_ref[...])
        for g in range(GROUP):
            o_ref[pl.ds(r, CHUNK), g * dkv:(g + 1) * dkv] = o[g * CHUNK:(g + 1) * CHUNK].astype(_BF)
        return carry

    lax.fori_loop(0, ta // CHUNK, chunk, 0, unroll=8)


def _attn_prompt(q, k, v, bias, sink, *, batch, ta, name, casts=()):
    M, dq = q.shape
    dk = k.shape[1]
    tiles_per_batch = (M // batch) // ta
    wb = ta // WINDOW
    nk = WINDOW + CHUNK

    def prev(i):
        return (jnp.where(i % tiles_per_batch == 0, i * wb, i * wb - 1), 0)

    row = lambda i: (i, 0)
    grid = (M // ta,)
    c_in, c_out, c_shape, c_bytes = _cast_specs(casts, grid)
    blocks = [_nbytes((ta, dq), _BF) * 2, 2 * _nbytes((WINDOW + ta, dk), _F32),
              _nbytes(bias.shape, _F32), _nbytes((8, N_HEADS * CHUNK), _F32), c_bytes]
    return pl.pallas_call(
        _with_casts(functools.partial(_attn_prompt_kernel, tiles_per_batch=tiles_per_batch), 7, 1, len(casts)),
        grid=grid,
        in_specs=[
            pl.BlockSpec((ta, dq), row),
            pl.BlockSpec((WINDOW, dk), prev),
            pl.BlockSpec((ta, dk), row),
            pl.BlockSpec((WINDOW, dk), prev),
            pl.BlockSpec((ta, dk), row),
            pl.BlockSpec(bias.shape, lambda i: (0, 0, 0)),
            pl.BlockSpec((1, N_HEADS * CHUNK), lambda i: (0, 0)),
        ] + c_in,
        out_specs=[pl.BlockSpec((ta, dq), row)] + c_out,
        out_shape=[jax.ShapeDtypeStruct((M, dq), _BF)] + c_shape,
        scratch_shapes=[pltpu.VMEM((WINDOW + ta, dk), _BF), pltpu.VMEM((WINDOW + ta, dk), _BF)],
        compiler_params=pltpu.CompilerParams(
            dimension_semantics=("arbitrary",),
            vmem_limit_bytes=_vmem_limit(blocks, 2 * _nbytes((WINDOW + ta, dk), _BF)),
        ),
        name=name,
    )(q, k, k, v, v, bias, sink, *casts)


LANES = 128


def _attn_sample_kernel(q_ref, ck_ref, cv_ref, k_ref, v_ref, bias_ref, sink_ref, o_ref, *, nb, s):
    dkv = N_KV * HEAD_DIM
    gs = GROUP * s
    assert LANES == 2 * HEAD_DIM and dkv == 2 * LANES
    for b in range(nb):
        rows = slice(b * s, (b + 1) * s)
        kw = jnp.concatenate([ck_ref[b].astype(_BF), k_ref[rows, :].astype(_BF)], axis=0)
        vw = jnp.concatenate([cv_ref[b].astype(_BF), v_ref[rows, :].astype(_BF)], axis=0)
        qf = q_ref[rows, :].astype(_F32)
        blocks = []
        for kv in range(N_KV):
            for g in range(GROUP):
                h = kv * GROUP + g
                pair = qf[:, (h // 2) * LANES:(h // 2 + 1) * LANES]
                if h % 2 != kv % 2:
                    pair = pltpu.roll(pair, HEAD_DIM, 1)
                half = lax.broadcasted_iota(jnp.int32, pair.shape, 1) // HEAD_DIM
                piece = jnp.where(half == kv % 2, pair, 0.0)
                zero = jnp.zeros_like(piece)
                blocks.append(jnp.concatenate([piece, zero] if kv < 2 else [zero, piece], axis=1))
        q_bd = jnp.concatenate(blocks, axis=0).astype(_BF)
        lg = lax.dot_general(kw, q_bd, _NT, preferred_element_type=_F32) + bias_ref[0]
        e, inv = _sink_softmax_keys_major(lg, sink_ref[...])
        z = lax.dot_general((e * inv).astype(_BF), vw, _TN, preferred_element_type=_F32)
        row_kv = lax.broadcasted_iota(jnp.int32, z.shape, 0) // gs
        col_kv = lax.broadcasted_iota(jnp.int32, z.shape, 1) // HEAD_DIM
        z = jnp.where(row_kv == col_kv, z, 0.0)
        r = z[0:gs]
        for kv in range(1, N_KV):
            r = r + z[kv * gs:(kv + 1) * gs]
        for g in range(GROUP):
            o_ref[rows, g * dkv:(g + 1) * dkv] = r[g * s:(g + 1) * s].astype(_BF)


def _attn_sample(q, k, v, cache_k, cache_v, bias, sink, *, nb, name):
    B, W, dk = cache_k.shape
    M, dq = q.shape
    S = M // B
    row = lambda b: (b, 0)
    return pl.pallas_call(
        functools.partial(_attn_sample_kernel, nb=nb, s=S),
        grid=(B // nb,),
        in_specs=[
            pl.BlockSpec((nb * S, dq), row),
            pl.BlockSpec((nb, W, dk), lambda b: (b, 0, 0)),
            pl.BlockSpec((nb, W, dk), lambda b: (b, 0, 0)),
            pl.BlockSpec((nb * S, dk), row),
            pl.BlockSpec((nb * S, dk), row),
            pl.BlockSpec(bias.shape, lambda b: (0, 0, 0)),
            pl.BlockSpec(sink.shape, lambda b: (0, 0)),
        ],
        out_specs=pl.BlockSpec((nb * S, dq), row),
        out_shape=jax.ShapeDtypeStruct((M, dq), _BF),
        compiler_params=pltpu.CompilerParams(dimension_semantics=("arbitrary",)),
        name=name,
    )(q, cache_k, cache_v, k, v, bias, sink)


def _merge_kernel(h_ref, n_ref, ucb_ref, o_ref, wco_ref, wao_ref, wgc_ref, wga_ref, wout_ref, out_ref):
    j = pl.program_id(1)
    last = pl.num_programs(1) - 1
    n = n_ref[...]
    y_conv = _dot(ucb_ref[...], wco_ref[...])
    y_attn = _dot(o_ref[...], wao_ref[...])
    g_conv = jax.nn.sigmoid(_dot(n, wgc_ref[...]))
    g_attn = jax.nn.sigmoid(_dot(n, wga_ref[...]))
    mix = (g_conv * y_conv + g_attn * y_attn).astype(_BF)
    out_ref[...] = jnp.where(j == 0, 0.0, out_ref[...]) + _dot(mix, wout_ref[...])

    @pl.when(j == last)
    def _():
        out_ref[...] = h_ref[...] + out_ref[...]


def _merge(h, n, ucb, o, w_conv_out, w_attn_o, w_in, w_out, *, gc_col, ga_col, tm, tn, name, casts=()):
    M, D = h.shape
    grid = (M // tm, D // tn)
    dc = ucb.shape[1]
    da = o.shape[1]
    row = lambda i, j: (i, 0)
    col = lambda i, j: (0, j)
    blocks = [2 * _nbytes((tm, D), _F32), _nbytes((tm, D), _BF), _nbytes((tm, dc), _BF), _nbytes((tm, da), _BF),
              _nbytes((dc + da + 3 * D, tn), _BF)]
    c_in, c_out, c_shape, c_bytes = _cast_specs(casts, grid)
    blocks.append(c_bytes)
    return pl.pallas_call(
        _with_casts(_merge_kernel, 9, 1, len(casts)),
        grid=grid,
        in_specs=[
            pl.BlockSpec((tm, D), row),
            pl.BlockSpec((tm, D), row),
            pl.BlockSpec((tm, dc), row),
            pl.BlockSpec((tm, da), row),
            pl.BlockSpec((dc, tn), col),
            pl.BlockSpec((da, tn), col),
            pl.BlockSpec((D, tn), lambda i, j: (0, gc_col // tn + j)),
            pl.BlockSpec((D, tn), lambda i, j: (0, ga_col // tn + j)),
            pl.BlockSpec((tn, D), lambda i, j: (j, 0)),
        ] + c_in,
        out_specs=[pl.BlockSpec((tm, D), row)] + c_out,
        out_shape=[jax.ShapeDtypeStruct((M, D), _F32)] + c_shape,
        compiler_params=pltpu.CompilerParams(
            dimension_semantics=("arbitrary", "arbitrary"),
            vmem_limit_bytes=_vmem_limit(blocks),
        ),
        name=name,
    )(h, n, ucb, o, w_conv_out, w_attn_o, w_in, w_in, w_out, *casts)


def _ple_kernel(h_ref, pe_ref, g_ref, wp_ref, wg_ref, out_ref):
    h = h_ref[...]
    gate = jax.nn.sigmoid(_dot(_rms(h, g_ref[...]).astype(_BF), wg_ref[...]))
    emb = _dot(pe_ref[...].astype(_BF), wp_ref[...])
    out_ref[...] = h + emb * gate


def _ple(h, pe, g, w_ple, w_gate, *, tm, name):
    M, D = h.shape
    dp = pe.shape[1]
    row = lambda i: (i, 0)
    const = lambda i: (0, 0)
    blocks = [2 * _nbytes((tm, D), _F32), _nbytes((tm, dp), _F32), _nbytes((dp + D, D), _BF)]
    return pl.pallas_call(
        _ple_kernel,
        grid=(M // tm,),
        in_specs=[
            pl.BlockSpec((tm, D), row),
            pl.BlockSpec((tm, dp), row),
            pl.BlockSpec((1, D), const),
            pl.BlockSpec((dp, D), const),
            pl.BlockSpec((D, D), const),
        ],
        out_specs=pl.BlockSpec((tm, D), row),
        out_shape=jax.ShapeDtypeStruct((M, D), _F32),
        compiler_params=pltpu.CompilerParams(
            dimension_semantics=("arbitrary",),
            vmem_limit_bytes=_vmem_limit(blocks),
        ),
        name=name,
    )(h, pe, g, w_ple, w_gate)


def _row_tile(m, want):
    return want if m % want == 0 else m


def kernel(x_prompt, x_sample, p_prompt, p_sample, state_conv, cache_k, cache_v, rel_table, ffn1_norm, ffn1_wg, ffn1_wu, ffn1_wd, mix_norm, w_in, conv_w, q_norm, k_norm, attn_sink, w_conv_out, w_attn_o, w_out, ffn2_norm, ffn2_wg, ffn2_wu, ffn2_wd, ple_norm, w_ple, w_ple_gate):
    B, T, D = x_prompt.shape
    Bs, S, _ = x_sample.shape
    depth = ffn1_wg.shape[0]
    d_conv = conv_w.shape[2]
    dq, dk = N_HEADS * HEAD_DIM, N_KV * HEAD_DIM
    qkv_col = 3 * d_conv
    gc_col = qkv_col + dq + 2 * dk
    ga_col = gc_col + D
    Mp, Ms = B * T, Bs * S
    tm_p = _row_tile(Mp, 512)
    tm_s = _row_tile(Ms, 512)
    tf = 1024
    tn = 512
    cw = 256

    nk_p, nk_s = WINDOW + CHUNK, WINDOW + S
    bias_p = _rel_bias(rel_table, _bucket_index(CHUNK, nk_p), WINDOW // CHUNK + 1, CHUNK, "rel_bias_prompt")
    bias_s = _rel_bias(rel_table, _bucket_index(S, nk_s), 1, 0, "rel_bias_sample")
    lane = jnp.arange(cw) // HEAD_DIM
    ones = (lane[:, None] == lane[None, :]).astype(_BF)

    yp = x_prompt.reshape(Mp, D)
    ys = x_sample.reshape(Ms, D)
    conv_p, k_p, v_p, conv_s, k_s, v_s = [], [], [], [], [], []
    for l in range(depth):
        vec = lambda g: g[l].reshape(1, -1)
        h1_s, n_s, f1_wg, f1_wu, f1_wd = _ffn(ys, vec(ffn1_norm), ffn1_wg[l], ffn1_wu[l], ffn1_wd[l],
                                              vec(mix_norm), tm=tm_s, tf=tf // 2, name="ffn1_sample")
        f1 = (vec(ffn1_norm), f1_wg, f1_wu, f1_wd)
        gq = jnp.tile(q_norm[l], N_HEADS).reshape(1, dq)
        gk = jnp.tile(k_norm[l], N_KV).reshape(1, dk)
        sink_p = jnp.repeat(attn_sink[l], CHUNK).reshape(1, N_HEADS * CHUNK)
        sink_s = jnp.repeat(attn_sink[l], S).reshape(1, N_HEADS * S)

        h1, n, w_in_l = _ffn(yp, *f1, vec(mix_norm), tm=tm_p, tf=tf, name="ffn1_prompt", casts=(w_in[l],))
        ucb, ulast, q, k, v, f2_wg, f2_wu, w_co, w_ao, w_o = _inproj(
            n, w_in_l, conv_w[l], gq, gk, ones, tm=tm_p, batch=B, name="inproj_prompt",
            casts=(ffn2_wg[l], ffn2_wu[l], w_conv_out[l], w_attn_o[l], w_out[l]))
        o, f2_wd, w_p, w_pg = _attn_prompt(q, k, v, bias_p, sink_p, batch=B, ta=2 * tm_p, name="attn_prompt",
                                           casts=(ffn2_wd[l], w_ple[l], w_ple_gate[l]))
        w_ao_p = w_ao.reshape(N_KV, GROUP, HEAD_DIM, D).transpose(1, 0, 2, 3).reshape(dq, D)
        h2, = _merge(h1, n, ucb, o, w_co, w_ao_p, w_in_l, w_o, gc_col=gc_col, ga_col=ga_col, tm=tm_p, tn=tn,
                     name="merge_prompt")
        f2 = (vec(ffn2_norm), f2_wg, f2_wu, f2_wd)
        h3, = _ffn(h2, *f2, None, tm=tm_p, tf=tf, name="ffn2_prompt")
        yp = _ple(h3, p_prompt[l].reshape(Mp, -1), vec(ple_norm), w_p, w_pg, tm=tm_p, name="ple_prompt")
        conv_p.append(ulast[:, 6:8])
        k_p.append(k.reshape(B, T, dk)[:, T - WINDOW:].reshape(B, WINDOW, N_KV, HEAD_DIM))
        v_p.append(v.reshape(B, T, dk)[:, T - WINDOW:].reshape(B, WINDOW, N_KV, HEAD_DIM))

        st = state_conv[l]
        zeros = jnp.zeros((Bs, S - 2, d_conv), _F32)
        e2 = jnp.concatenate([st, zeros], axis=1).reshape(Ms, d_conv)
        e1 = jnp.concatenate([st[:, 1:], zeros, zeros[:, :1]], axis=1).reshape(Ms, d_conv)
        h1, n = h1_s, n_s
        ucb, u, q, k, v = _inproj(n, w_in_l, conv_w[l], gq, gk, ones, tm=tm_s, seg=S, edges=(e1, e2),
                                  name="inproj_sample")
        o = _attn_sample(q, k, v, cache_k[l].reshape(Bs, WINDOW, dk), cache_v[l].reshape(Bs, WINDOW, dk),
                         bias_s, sink_s, nb=4, name="attn_sample")
        h2, = _merge(h1, n, ucb, o, w_co, w_ao_p, w_in_l, w_o, gc_col=gc_col, ga_col=ga_col,
                     tm=tm_s, tn=tn, name="merge_sample")
        h3, = _ffn(h2, *f2, None, tm=tm_s, tf=tf, name="ffn2_sample")
        ys = _ple(h3, p_sample[l].reshape(Ms, -1), vec(ple_norm), w_p, w_pg, tm=tm_s, name="ple_sample")
        conv_s.append(u.reshape(Bs, S, d_conv)[:, S - 2:])
        k_s.append(k.reshape(Bs, S, N_KV, HEAD_DIM))
        v_s.append(v.reshape(Bs, S, N_KV, HEAD_DIM))

    return (yp.reshape(B, T, D), ys.reshape(Bs, S, D), jnp.stack(conv_p), jnp.stack(k_p), jnp.stack(v_p),
            jnp.stack(conv_s), jnp.stack(k_s), jnp.stack(v_s))
```

```python
import functools
import math

import jax
import jax.numpy as jnp
from jax import lax
from jax.experimental import pallas as pl
from jax.experimental.pallas import tpu as pltpu

_BF = jnp.bfloat16
_F32 = jnp.float32

CHUNK = 64
WINDOW = 128
N_HEADS = 16
N_KV = 4
GROUP = N_HEADS // N_KV
HEAD_DIM = 64
N_BUCKETS = 32
MAX_DIST = 128
EPS = 1e-6
LOG2E = math.log2(math.e)
QK_SCALE = HEAD_DIM ** -0.5 * LOG2E

V7X_VMEM_BYTES = 64 * 1024 * 1024
VMEM_INTERNAL_BYTES = 12 * 1024 * 1024


def _vmem_limit(block_bytes, scratch_bytes=0):
    need = 2 * sum(block_bytes) + scratch_bytes + VMEM_INTERNAL_BYTES
    return int(min(need, V7X_VMEM_BYTES - 4 * 1024 * 1024))


def _nbytes(shape, dtype):
    return math.prod(shape) * jnp.dtype(dtype).itemsize


def _dot(a, b):
    return jnp.dot(a, b, preferred_element_type=_F32)


def _rms(x, g):
    ms = jnp.mean(x * x, axis=-1, keepdims=True)
    return x * lax.rsqrt(ms + EPS) * g


BF16_TILE_ROWS = 16


def _cast_specs(arrays, grid):
    steps = math.prod(grid)

    def flat(*ids):
        s = 0
        for k, g in zip(ids, grid):
            s = s * g + k
        return s

    in_specs, out_specs, out_shape, nbytes = [], [], [], 0
    for a in arrays:
        rows, cols = a.shape
        rb = max(BF16_TILE_ROWS, rows // steps)
        assert rows % rb == 0 and steps % (rows // rb) == 0, (a.shape, steps)
        rep = steps // (rows // rb)
        idx = lambda *ids, rep=rep: (flat(*ids) // rep, 0)
        in_specs.append(pl.BlockSpec((rb, cols), idx))
        out_specs.append(pl.BlockSpec((rb, cols), idx))
        out_shape.append(jax.ShapeDtypeStruct((rows, cols), _BF))
        nbytes += _nbytes((rb, cols), _F32) + _nbytes((rb, cols), _BF)
    return in_specs, out_specs, out_shape, nbytes


def _with_casts(body, n_in, n_out, n_cast):
    def kernel(*refs):
        outs_at = n_in + n_cast
        scratch_at = outs_at + n_out + n_cast
        for src, dst in zip(refs[n_in:outs_at], refs[outs_at + n_out:scratch_at]):
            dst[...] = src[...].astype(_BF)
        body(*refs[:n_in], *refs[outs_at:outs_at + n_out], *refs[scratch_at:])
    return kernel


def _ffn_kernel(*refs, emit_norm, keep_weights):
    x_ref, g_ref, wg_ref, wu_ref, wd_ref = refs[:5]
    rest = list(refs[5:])
    g2_ref = rest.pop(0) if emit_norm else None
    out_ref = rest.pop(0)
    n_ref = rest.pop(0) if emit_norm else None
    weight_copies = [rest.pop(0) for _ in range(3)] if keep_weights else []
    xn_ref, = rest
    j = pl.program_id(1)
    last = pl.num_programs(1) - 1

    @pl.when(j == 0)
    def _():
        xn_ref[...] = _rms(x_ref[...], g_ref[...]).astype(_BF)

    wg, wu, wd = wg_ref[...], wu_ref[...], wd_ref[...]
    if keep_weights:
        wg, wu, wd = wg.astype(_BF), wu.astype(_BF), wd.astype(_BF)
        for dst, w in zip(weight_copies, (wg, wu, wd)):
            dst[...] = w
    xn = xn_ref[...]
    gate = _dot(xn, wg)
    up = _dot(xn, wu)
    hid = (jax.nn.silu(gate) * up).astype(_BF)
    out_ref[...] = jnp.where(j == 0, 0.0, out_ref[...]) + _dot(hid, wd)

    @pl.when(j == last)
    def _():
        h = x_ref[...] + 0.5 * out_ref[...]
        out_ref[...] = h
        if emit_norm:
            n_ref[...] = _rms(h, g2_ref[...]).astype(_BF)


def _ffn(x, g, wg, wu, wd, g2, *, tm, tf, name, casts=()):
    M, D = x.shape
    F = wg.shape[1]
    emit_norm = g2 is not None
    keep_weights = wg.dtype == _F32
    grid = (M // tm, F // tf)
    assert not keep_weights or grid[0] == 1
    row = lambda i, j: (i, 0)
    w_specs = [
        pl.BlockSpec((D, tf), lambda i, j: (0, j)),
        pl.BlockSpec((D, tf), lambda i, j: (0, j)),
        pl.BlockSpec((tf, D), lambda i, j: (j, 0)),
    ]
    in_specs = [pl.BlockSpec((tm, D), row), pl.BlockSpec((1, D), lambda i, j: (0, 0))] + w_specs
    args = [x, g, wg, wu, wd]
    out_shape = [jax.ShapeDtypeStruct((M, D), _F32)]
    out_specs = [pl.BlockSpec((tm, D), row)]
    blocks = [_nbytes((tm, D), _F32) * 2, 3 * _nbytes((D, tf), wg.dtype)]
    if emit_norm:
        in_specs.append(pl.BlockSpec((1, D), lambda i, j: (0, 0)))
        args.append(g2)
        out_shape.append(jax.ShapeDtypeStruct((M, D), _BF))
        out_specs.append(pl.BlockSpec((tm, D), row))
        blocks.append(_nbytes((tm, D), _BF))
    if keep_weights:
        out_shape += [jax.ShapeDtypeStruct(w.shape, _BF) for w in (wg, wu, wd)]
        out_specs += w_specs
        blocks.append(3 * _nbytes((D, tf), _BF))
    c_in, c_out, c_shape, c_bytes = _cast_specs(casts, grid)
    blocks.append(c_bytes)
    body = functools.partial(_ffn_kernel, emit_norm=emit_norm, keep_weights=keep_weights)
    res = pl.pallas_call(
        _with_casts(body, len(in_specs), len(out_specs), len(casts)),
        grid=grid,
        in_specs=in_specs + c_in,
        out_specs=out_specs + c_out,
        out_shape=out_shape + c_shape,
        scratch_shapes=[pltpu.VMEM((tm, D), _BF)],
        compiler_params=pltpu.CompilerParams(
            dimension_semantics=("arbitrary", "arbitrary"),
            vmem_limit_bytes=_vmem_limit(blocks, _nbytes((tm, D), _BF)),
        ),
        name=name,
    )(*args, *casts)
    return res


def _conv_kernel(*refs, carry_rows, seg, tiles_per_seg, d_conv, cw):
    if carry_rows:
        n_ref, w_ref, cwt_ref, ucb_ref, ulast_ref, carry_ref = refs
    else:
        n_ref, w_ref, cwt_ref, e1_ref, e2_ref, ucb_ref, u_ref = refs
    tm = n_ref.shape[0]
    n = n_ref[...]
    if carry_rows:
        @pl.when(pl.program_id(0) % tiles_per_seg == 0)
        def _():
            carry_ref[...] = jnp.zeros_like(carry_ref)

    for c in range(d_conv // cw):
        lo = c * cw
        cb = _dot(n, w_ref[:, lo:lo + cw])
        cc = _dot(n, w_ref[:, d_conv + lo:d_conv + lo + cw])
        cv = _dot(n, w_ref[:, 2 * d_conv + lo:2 * d_conv + lo + cw])
        u = cc * cv
        r1 = pltpu.roll(u, 1, 0)
        r2 = pltpu.roll(u, 2, 0)
        if carry_rows:
            prev = carry_ref[:, lo:lo + cw]
            rows8 = lax.broadcasted_iota(jnp.int32, (8, cw), 0)
            h1 = jnp.where(rows8 < 1, pltpu.roll(prev, 1, 0), r1[:8])
            h2 = jnp.where(rows8 < 2, pltpu.roll(prev, 2, 0), r2[:8])
            r1 = jnp.concatenate([h1, r1[8:]], axis=0)
            r2 = jnp.concatenate([h2, r2[8:]], axis=0)
            carry_ref[:, lo:lo + cw] = u[tm - 8:]
            ulast_ref[0, :, lo:lo + cw] = u[tm - 8:]
        else:
            pos = lax.broadcasted_iota(jnp.int32, (tm, cw), 0) % seg
            r1 = jnp.where(pos < 1, e1_ref[:, lo:lo + cw], r1)
            r2 = jnp.where(pos < 2, e2_ref[:, lo:lo + cw], r2)
            u_ref[:, lo:lo + cw] = u
        w0 = cwt_ref[0:1, lo:lo + cw]
        w1 = cwt_ref[1:2, lo:lo + cw]
        w2 = cwt_ref[2:3, lo:lo + cw]
        conv = w0 * r2 + w1 * r1 + w2 * u
        ucb_ref[:, lo:lo + cw] = (cb * conv).astype(_BF)


def _qkv_kernel(n_ref, w_ref, gq_ref, gk_ref, ones_ref, q_ref, k_ref, v_ref, *, cw, col0):
    ones = ones_ref[...]
    dq = q_ref.shape[1]
    dk = k_ref.shape[1]
    p = _dot(n_ref[...], w_ref[:, col0:col0 + dq + 2 * dk])
    x = p[:, :dq + dk]
    sq = x * x
    hi = sq.astype(_BF)
    lo = (sq - hi.astype(_F32)).astype(_BF)
    ss = jnp.concatenate(
        [_dot(hi[:, c * cw:(c + 1) * cw], ones) + _dot(lo[:, c * cw:(c + 1) * cw], ones)
         for c in range((dq + dk) // cw)], axis=1)
    xr = x * lax.rsqrt(ss * (1.0 / HEAD_DIM) + EPS)
    q_ref[...] = (xr[:, :dq] * gq_ref[...] * QK_SCALE).astype(_BF)
    k_ref[...] = xr[:, dq:] * gk_ref[...]
    v_ref[...] = p[:, dq + dk:]


def _inproj_kernel(*refs, carry_rows, seg, tiles_per_seg, d_conv, cw):
    if carry_rows:
        (n_ref, w_ref, cwt_ref, gq_ref, gk_ref, ones_ref,
         ucb_ref, ulast_ref, q_ref, k_ref, v_ref, carry_ref) = refs
        conv_refs = (n_ref, w_ref, cwt_ref, ucb_ref, ulast_ref, carry_ref)
    else:
        (n_ref, w_ref, cwt_ref, gq_ref, gk_ref, ones_ref, e1_ref, e2_ref,
         ucb_ref, u_ref, q_ref, k_ref, v_ref) = refs
        conv_refs = (n_ref, w_ref, cwt_ref, e1_ref, e2_ref, ucb_ref, u_ref)
    _conv_kernel(*conv_refs, carry_rows=carry_rows, seg=seg, tiles_per_seg=tiles_per_seg, d_conv=d_conv, cw=cw)
    _qkv_kernel(n_ref, w_ref, gq_ref, gk_ref, ones_ref, q_ref, k_ref, v_ref, cw=cw, col0=3 * d_conv)


def _inproj(n, w_in, conv_w, gq, gk, ones, *, tm, name, batch=None, seg=None, edges=(), casts=()):
    M, D = n.shape
    d_conv = conv_w.shape[1]
    cw = ones.shape[0]
    dq, dk = N_HEADS * HEAD_DIM, N_KV * HEAD_DIM
    wcols = 3 * d_conv + dq + 2 * dk
    carry_rows = batch is not None
    grid = (M // tm,)
    tiles_per_seg = (M // batch) // tm if carry_rows else None
    c_in, c_out, c_shape, c_bytes = _cast_specs(casts, grid)
    const = lambda i: (0, 0)
    row = lambda i: (i, 0)
    in_specs = [
        pl.BlockSpec((tm, D), row),
        pl.BlockSpec((D, wcols), const, pipeline_mode=pl.Buffered(1)),
        pl.BlockSpec((3, d_conv), const),
        pl.BlockSpec((1, dq), const),
        pl.BlockSpec((1, dk), const),
        pl.BlockSpec((cw, cw), const),
    ] + [pl.BlockSpec((tm, d_conv), row) for _ in edges]
    if carry_rows:
        u_spec = pl.BlockSpec((1, 8, d_conv), lambda i: (i // tiles_per_seg, 0, 0))
        u_shape = jax.ShapeDtypeStruct((batch, 8, d_conv), _F32)
        scratch = [pltpu.VMEM((8, d_conv), _F32)]
    else:
        u_spec = pl.BlockSpec((tm, d_conv), row)
        u_shape = jax.ShapeDtypeStruct((M, d_conv), _F32)
        scratch = []
    out_specs = [pl.BlockSpec((tm, d_conv), row), u_spec,
                 pl.BlockSpec((tm, dq), row), pl.BlockSpec((tm, dk), row), pl.BlockSpec((tm, dk), row)]
    out_shape = [jax.ShapeDtypeStruct((M, d_conv), _BF), u_shape, jax.ShapeDtypeStruct((M, dq), _BF),
                 jax.ShapeDtypeStruct((M, dk), _F32), jax.ShapeDtypeStruct((M, dk), _F32)]
    blocks = [_nbytes((tm, D), _BF), _nbytes((D, wcols), _BF) // 2, (1 + len(edges)) * _nbytes((tm, d_conv), _F32),
              _nbytes((tm, d_conv + dq), _BF), 2 * _nbytes((tm, dk), _F32), c_bytes]
    body = functools.partial(_inproj_kernel, carry_rows=carry_rows, seg=seg, tiles_per_seg=tiles_per_seg,
                             d_conv=d_conv, cw=cw)
    return pl.pallas_call(
        _with_casts(body, len(in_specs), len(out_specs), len(casts)),
        grid=grid,
        in_specs=in_specs + c_in,
        out_specs=out_specs + c_out,
        out_shape=out_shape + c_shape,
        scratch_shapes=scratch,
        compiler_params=pltpu.CompilerParams(
            dimension_semantics=("arbitrary",),
            vmem_limit_bytes=_vmem_limit(blocks),
        ),
        name=name,
    )(n, w_in, conv_w, gq, gk, ones, *edges, *casts)


def _bias_kernel(tab_ref, bkt_ref, out_ref, *, mask_step):
    bkt = bkt_ref[...]
    key = lax.broadcasted_iota(jnp.int32, bkt.shape, 1)
    for h in range(out_ref.shape[1]):
        acc = jnp.zeros(bkt.shape, _F32)
        for b in range(tab_ref.shape[0]):
            acc = jnp.where(bkt == b, tab_ref[b, h] * LOG2E, acc)
        for v in range(out_ref.shape[0]):
            out_ref[v, h] = jnp.where(key < v * mask_step, -jnp.inf, acc)


def _rel_bias(table, bucket, n_variants, mask_step, name):
    nq, nk = bucket.shape
    nh = table.shape[1]
    out = pl.pallas_call(
        functools.partial(_bias_kernel, mask_step=mask_step),
        in_specs=[pl.BlockSpec(memory_space=pltpu.SMEM), pl.BlockSpec((nq, nk), lambda: (0, 0))],
        out_specs=pl.BlockSpec((n_variants, nh, nq, nk), lambda: (0, 0, 0, 0)),
        out_shape=jax.ShapeDtypeStruct((n_variants, nh, nq, nk), _F32),
        name=name,
    )(table, bucket)
    return out.transpose(0, 3, 1, 2).reshape(n_variants, nk, nh * nq)


def _t5_bucket(rel):
    nb = N_BUCKETS // 2
    max_exact = nb // 2
    ret = jnp.where(rel > 0, nb, 0)
    n = jnp.abs(rel)
    nf = jnp.maximum(n, 1).astype(jnp.float32)
    large = max_exact + (jnp.log(nf / max_exact) / math.log(MAX_DIST / max_exact) * (nb - max_exact)).astype(jnp.int32)
    large = jnp.minimum(large, nb - 1)
    return ret + jnp.where(n < max_exact, n, large)


def _bucket_index(n_q, n_k):
    i = jnp.arange(n_q)[:, None]
    j = jnp.arange(n_k)[None, :]
    return _t5_bucket(j - WINDOW - i).astype(jnp.int32)


_NT = (((1,), (1,)), ((), ()))
_TN = (((0,), (0,)), ((), ()))


def _sink_softmax_keys_major(lg, sink_t):
    sink2 = sink_t * LOG2E
    m = jnp.maximum(jnp.max(lg, axis=0, keepdims=True), sink2)
    e = jnp.exp2(lg - m)
    den = jnp.sum(e, axis=0, keepdims=True) + jnp.exp2(sink2 - m)
    return e, 1.0 / den


def _attn_chunk_keys_major(qc, kw, vw, bias_t, sink_t):
    nq = qc.shape[0]
    gq = GROUP * nq
    logits = []
    for kv in range(N_KV):
        qg = jnp.concatenate(
            [qc[:, (kv * GROUP + g) * HEAD_DIM:(kv * GROUP + g + 1) * HEAD_DIM] for g in range(GROUP)], axis=0)
        kk = kw[:, kv * HEAD_DIM:(kv + 1) * HEAD_DIM]
        logits.append(lax.dot_general(kk, qg, _NT, preferred_element_type=_F32))
    e, inv = _sink_softmax_keys_major(jnp.concatenate(logits, axis=1) + bias_t, sink_t)
    e = e.astype(_BF)
    outs = []
    for kv in range(N_KV):
        vv = vw[:, kv * HEAD_DIM:(kv + 1) * HEAD_DIM]
        cols = slice(kv * gq, (kv + 1) * gq)
        outs.append(lax.dot_general(vv, e[:, cols], _TN, preferred_element_type=_F32) * inv[:, cols])
    return jnp.concatenate(outs, axis=0).T


def _attn_prompt_kernel(q_ref, kp_ref, kc_ref, vp_ref, vc_ref, bias_ref, sink_ref, o_ref, kwin, vwin,
                        *, tiles_per_batch):
    ta = q_ref.shape[0]
    nk = WINDOW + CHUNK
    dkv = N_KV * HEAD_DIM
    kwin[0:WINDOW, :] = kp_ref[...].astype(_BF)
    kwin[WINDOW:, :] = kc_ref[...].astype(_BF)
    vwin[0:WINDOW, :] = vp_ref[...].astype(_BF)
    vwin[WINDOW:, :] = vc_ref[...].astype(_BF)
    tile_pos = (pl.program_id(0) % tiles_per_batch) * ta

    def chunk(c, carry):
        r = pl.multiple_of(c * CHUNK, CHUNK)
        n_before = jnp.maximum(0, (WINDOW - (tile_pos + r)) // CHUNK)
        o = _attn_chunk_keys_major(q_ref[pl.ds(r, CHUNK), :], kwin[pl.ds(r, nk), :], vwin[pl.ds(r, nk), :],
                                   bias_ref[n_before], sink_ref[...])
        for g in range(GROUP):
            o_ref[pl.ds(r, CHUNK), g * dkv:(g + 1) * dkv] = o[g * CHUNK:(g + 1) * CHUNK].astype(_BF)
        return carry

    lax.fori_loop(0, ta // CHUNK, chunk, 0, unroll=8)


def _attn_prompt(q, k, v, bias, sink, *, batch, ta, name, casts=()):
    M, dq = q.shape
    dk = k.shape[1]
    tiles_per_batch = (M // batch) // ta
    wb = ta // WINDOW
    nk = WINDOW + CHUNK

    def prev(i):
        return (jnp.where(i % tiles_per_batch == 0, i * wb, i * wb - 1), 0)

    row = lambda i: (i, 0)
    grid = (M // ta,)
    c_in, c_out, c_shape, c_bytes = _cast_specs(casts, grid)
    blocks = [_nbytes((ta, dq), _BF) * 2, 2 * _nbytes((WINDOW + ta, dk), _F32),
              _nbytes(bias.shape, _F32), _nbytes((8, N_HEADS * CHUNK), _F32), c_bytes]
    return pl.pallas_call(
        _with_casts(functools.partial(_attn_prompt_kernel, tiles_per_batch=tiles_per_batch), 7, 1, len(casts)),
        grid=grid,
        in_specs=[
            pl.BlockSpec((ta, dq), row),
            pl.BlockSpec((WINDOW, dk), prev),
            pl.BlockSpec((ta, dk), row),
            pl.BlockSpec((WINDOW, dk), prev),
            pl.BlockSpec((ta, dk), row),
            pl.BlockSpec(bias.shape, lambda i: (0, 0, 0)),
            pl.BlockSpec((1, N_HEADS * CHUNK), lambda i: (0, 0)),
        ] + c_in,
        out_specs=[pl.BlockSpec((ta, dq), row)] + c_out,
        out_shape=[jax.ShapeDtypeStruct((M, dq), _BF)] + c_shape,
        scratch_shapes=[pltpu.VMEM((WINDOW + ta, dk), _BF), pltpu.VMEM((WINDOW + ta, dk), _BF)],
        compiler_params=pltpu.CompilerParams(
            dimension_semantics=("arbitrary",),
            vmem_limit_bytes=_vmem_limit(blocks, 2 * _nbytes((WINDOW + ta, dk), _BF)),
        ),
        name=name,
    )(q, k, k, v, v, bias, sink, *casts)


LANES = 128


def _attn_sample_kernel(q_ref, ck_ref, cv_ref, k_ref, v_ref, bias_ref, sink_ref, o_ref, *, nb, s):
    dkv = N_KV * HEAD_DIM
    gs = GROUP * s
    assert LANES == 2 * HEAD_DIM and dkv == 2 * LANES
    for b in range(nb):
        rows = slice(b * s, (b + 1) * s)
        kw = jnp.concatenate([ck_ref[b].astype(_BF), k_ref[rows, :].astype(_BF)], axis=0)
        vw = jnp.concatenate([cv_ref[b].astype(_BF), v_ref[rows, :].astype(_BF)], axis=0)
        qf = q_ref[rows, :].astype(_F32)
        blocks = []
        for kv in range(N_KV):
            for g in range(GROUP):
                h = kv * GROUP + g
                pair = qf[:, (h // 2) * LANES:(h // 2 + 1) * LANES]
                if h % 2 != kv % 2:
                    pair = pltpu.roll(pair, HEAD_DIM, 1)
                half = lax.broadcasted_iota(jnp.int32, pair.shape, 1) // HEAD_DIM
                piece = jnp.where(half == kv % 2, pair, 0.0)
                zero = jnp.zeros_like(piece)
                blocks.append(jnp.concatenate([piece, zero] if kv < 2 else [zero, piece], axis=1))
        q_bd = jnp.concatenate(blocks, axis=0).astype(_BF)
        lg = lax.dot_general(kw, q_bd, _NT, preferred_element_type=_F32) + bias_ref[0]
        e, inv = _sink_softmax_keys_major(lg, sink_ref[...])
        z = lax.dot_general((e * inv).astype(_BF), vw, _TN, preferred_element_type=_F32)
        row_kv = lax.broadcasted_iota(jnp.int32, z.shape, 0) // gs
        col_kv = lax.broadcasted_iota(jnp.int32, z.shape, 1) // HEAD_DIM
        z = jnp.where(row_kv == col_kv, z, 0.0)
        r = z[0:gs]
        for kv in range(1, N_KV):
            r = r + z[kv * gs:(kv + 1) * gs]
        for g in range(GROUP):
            o_ref[rows, g * dkv:(g + 1) * dkv] = r[g * s:(g + 1) * s].astype(_BF)


def _attn_sample(q, k, v, cache_k, cache_v, bias, sink, *, nb, name):
    B, W, dk = cache_k.shape
    M, dq = q.shape
    S = M // B
    row = lambda b: (b, 0)
    return pl.pallas_call(
        functools.partial(_attn_sample_kernel, nb=nb, s=S),
        grid=(B // nb,),
        in_specs=[
            pl.BlockSpec((nb * S, dq), row),
            pl.BlockSpec((nb, W, dk), lambda b: (b, 0, 0)),
            pl.BlockSpec((nb, W, dk), lambda b: (b, 0, 0)),
            pl.BlockSpec((nb * S, dk), row),
            pl.BlockSpec((nb * S, dk), row),
            pl.BlockSpec(bias.shape, lambda b: (0, 0, 0)),
            pl.BlockSpec(sink.shape, lambda b: (0, 0)),
        ],
        out_specs=pl.BlockSpec((nb * S, dq), row),
        out_shape=jax.ShapeDtypeStruct((M, dq), _BF),
        compiler_params=pltpu.CompilerParams(dimension_semantics=("arbitrary",)),
        name=name,
    )(q, cache_k, cache_v, k, v, bias, sink)


def _mix_kernel(*refs, n_chunks):
    n_ref, ucb_ref, o_ref, wco_ref, wao_ref = refs[:5]
    wgc_refs = refs[5:5 + n_chunks]
    wga_refs = refs[5 + n_chunks:5 + 2 * n_chunks]
    mix_ref = refs[5 + 2 * n_chunks]
    n, ucb, o = n_ref[...], ucb_ref[...], o_ref[...]
    cw = mix_ref.shape[1] // n_chunks
    for c in range(n_chunks):
        cols = slice(c * cw, (c + 1) * cw)
        y_conv = _dot(ucb, wco_ref[:, cols])
        y_attn = _dot(o, wao_ref[:, cols])
        g_conv = jax.nn.sigmoid(_dot(n, wgc_refs[c][...]))
        g_attn = jax.nn.sigmoid(_dot(n, wga_refs[c][...]))
        mix_ref[:, cols] = (g_conv * y_conv + g_attn * y_attn).astype(_BF)


def _mix(n, ucb, o, w_conv_out, w_attn_o, w_in, *, gc_col, ga_col, cw, tm, name):
    M, D = n.shape
    dc, da = ucb.shape[1], o.shape[1]
    n_chunks = D // cw
    row = lambda i: (i, 0)
    once = pl.Buffered(1)
    gate_specs = [pl.BlockSpec((D, cw), lambda i, b=(col // cw + c): (0, b), pipeline_mode=once)
                  for col in (gc_col, ga_col) for c in range(n_chunks)]
    blocks = [_nbytes((tm, D + dc + da), _BF), _nbytes((dc + da + 2 * D, D), _BF) // 2, _nbytes((tm, D), _BF)]
    return pl.pallas_call(
        functools.partial(_mix_kernel, n_chunks=n_chunks),
        grid=(M // tm,),
        in_specs=[
            pl.BlockSpec((tm, D), row),
            pl.BlockSpec((tm, dc), row),
            pl.BlockSpec((tm, da), row),
            pl.BlockSpec((dc, D), lambda i: (0, 0), pipeline_mode=once),
            pl.BlockSpec((da, D), lambda i: (0, 0), pipeline_mode=once),
        ] + gate_specs,
        out_specs=pl.BlockSpec((tm, D), row),
        out_shape=jax.ShapeDtypeStruct((M, D), _BF),
        compiler_params=pltpu.CompilerParams(
            dimension_semantics=("arbitrary",),
            vmem_limit_bytes=_vmem_limit(blocks),
        ),
        name=name,
    )(n, ucb, o, w_conv_out, w_attn_o, *([w_in] * (2 * n_chunks)))


def _outproj_kernel(h_ref, mix_ref, w_ref, out_ref):
    out_ref[...] = h_ref[...] + _dot(mix_ref[...], w_ref[...])


def _outproj(h, mix, w_out, *, tm, name):
    M, D = h.shape
    row = lambda i: (i, 0)
    blocks = [2 * _nbytes((tm, D), _F32), _nbytes((tm, D), _BF), _nbytes((D, D), _BF) // 2]
    return pl.pallas_call(
        _outproj_kernel,
        grid=(M // tm,),
        in_specs=[
            pl.BlockSpec((tm, D), row),
            pl.BlockSpec((tm, D), row),
            pl.BlockSpec((D, D), lambda i: (0, 0), pipeline_mode=pl.Buffered(1)),
        ],
        out_specs=pl.BlockSpec((tm, D), row),
        out_shape=jax.ShapeDtypeStruct((M, D), _F32),
        compiler_params=pltpu.CompilerParams(
            dimension_semantics=("arbitrary",),
            vmem_limit_bytes=_vmem_limit(blocks),
        ),
        name=name,
    )(h, mix, w_out)


def _ple_kernel(h_ref, pe_ref, g_ref, wp_ref, wg_ref, out_ref):
    h = h_ref[...]
    gate = jax.nn.sigmoid(_dot(_rms(h, g_ref[...]).astype(_BF), wg_ref[...]))
    emb = _dot(pe_ref[...].astype(_BF), wp_ref[...])
    out_ref[...] = h + emb * gate


def _ple(h, pe, g, w_ple, w_gate, *, tm, name):
    M, D = h.shape
    dp = pe.shape[1]
    row = lambda i: (i, 0)
    const = lambda i: (0, 0)
    blocks = [2 * _nbytes((tm, D), _F32), _nbytes((tm, dp), _F32), _nbytes((dp + D, D), _BF)]
    return pl.pallas_call(
        _ple_kernel,
        grid=(M // tm,),
        in_specs=[
            pl.BlockSpec((tm, D), row),
            pl.BlockSpec((tm, dp), row),
            pl.BlockSpec((1, D), const),
            pl.BlockSpec((dp, D), const),
            pl.BlockSpec((D, D), const),
        ],
        out_specs=pl.BlockSpec((tm, D), row),
        out_shape=jax.ShapeDtypeStruct((M, D), _F32),
        compiler_params=pltpu.CompilerParams(
            dimension_semantics=("arbitrary",),
            vmem_limit_bytes=_vmem_limit(blocks),
        ),
        name=name,
    )(h, pe, g, w_ple, w_gate)


def _row_tile(m, want):
    return want if m % want == 0 else m


def kernel(x_prompt, x_sample, p_prompt, p_sample, state_conv, cache_k, cache_v, rel_table, ffn1_norm, ffn1_wg, ffn1_wu, ffn1_wd, mix_norm, w_in, conv_w, q_norm, k_norm, attn_sink, w_conv_out, w_attn_o, w_out, ffn2_norm, ffn2_wg, ffn2_wu, ffn2_wd, ple_norm, w_ple, w_ple_gate):
    B, T, D = x_prompt.shape
    Bs, S, _ = x_sample.shape
    depth = ffn1_wg.shape[0]
    d_conv = conv_w.shape[2]
    dq, dk = N_HEADS * HEAD_DIM, N_KV * HEAD_DIM
    qkv_col = 3 * d_conv
    gc_col = qkv_col + dq + 2 * dk
    ga_col = gc_col + D
    Mp, Ms = B * T, Bs * S
    tm_p = _row_tile(Mp, 512)
    tm_s = _row_tile(Ms, 512)
    tf = 1024
    tn = 512
    cw = 256

    nk_p, nk_s = WINDOW + CHUNK, WINDOW + S
    bias_p = _rel_bias(rel_table, _bucket_index(CHUNK, nk_p), WINDOW // CHUNK + 1, CHUNK, "rel_bias_prompt")
    bias_s = _rel_bias(rel_table, _bucket_index(S, nk_s), 1, 0, "rel_bias_sample")
    lane = jnp.arange(cw) // HEAD_DIM
    ones = (lane[:, None] == lane[None, :]).astype(_BF)

    yp = x_prompt.reshape(Mp, D)
    ys = x_sample.reshape(Ms, D)
    conv_p, k_p, v_p, conv_s, k_s, v_s = [], [], [], [], [], []
    for l in range(depth):
        vec = lambda g: g[l].reshape(1, -1)
        h1_s, n_s, f1_wg, f1_wu, f1_wd = _ffn(ys, vec(ffn1_norm), ffn1_wg[l], ffn1_wu[l], ffn1_wd[l],
                                              vec(mix_norm), tm=tm_s, tf=tf // 2, name="ffn1_sample")
        f1 = (vec(ffn1_norm), f1_wg, f1_wu, f1_wd)
        gq = jnp.tile(q_norm[l], N_HEADS).reshape(1, dq)
        gk = jnp.tile(k_norm[l], N_KV).reshape(1, dk)
        sink_p = jnp.repeat(attn_sink[l], CHUNK).reshape(1, N_HEADS * CHUNK)
        sink_s = jnp.repeat(attn_sink[l], S).reshape(1, N_HEADS * S)

        h1, n, w_in_l = _ffn(yp, *f1, vec(mix_norm), tm=tm_p, tf=tf, name="ffn1_prompt", casts=(w_in[l],))
        ucb, ulast, q, k, v, f2_wg, f2_wu, w_co, w_ao, w_o = _inproj(
            n, w_in_l, conv_w[l], gq, gk, ones, tm=tm_p, batch=B, name="inproj_prompt",
            casts=(ffn2_wg[l], ffn2_wu[l], w_conv_out[l], w_attn_o[l], w_out[l]))
        o, f2_wd, w_p, w_pg = _attn_prompt(q, k, v, bias_p, sink_p, batch=B, ta=2 * tm_p, name="attn_prompt",
                                           casts=(ffn2_wd[l], w_ple[l], w_ple_gate[l]))
        w_ao_p = w_ao.reshape(N_KV, GROUP, HEAD_DIM, D).transpose(1, 0, 2, 3).reshape(dq, D)
        mix = _mix(n, ucb, o, w_co, w_ao_p, w_in_l, gc_col=gc_col, ga_col=ga_col, cw=tn, tm=tm_p, name="mix_prompt")
        h2 = _outproj(h1, mix, w_o, tm=tm_p, name="outproj_prompt")
        f2 = (vec(ffn2_norm), f2_wg, f2_wu, f2_wd)
        h3, = _ffn(h2, *f2, None, tm=tm_p, tf=tf, name="ffn2_prompt")
        yp = _ple(h3, p_prompt[l].reshape(Mp, -1), vec(ple_norm), w_p, w_pg, tm=tm_p, name="ple_prompt")
        conv_p.append(ulast[:, 6:8])
        k_p.append(k.reshape(B, T, dk)[:, T - WINDOW:].reshape(B, WINDOW, N_KV, HEAD_DIM))
        v_p.append(v.reshape(B, T, dk)[:, T - WINDOW:].reshape(B, WINDOW, N_KV, HEAD_DIM))

        st = state_conv[l]
        zeros = jnp.zeros((Bs, S - 2, d_conv), _F32)
        e2 = jnp.concatenate([st, zeros], axis=1).reshape(Ms, d_conv)
        e1 = jnp.concatenate([st[:, 1:], zeros, zeros[:, :1]], axis=1).reshape(Ms, d_conv)
        h1, n = h1_s, n_s
        ucb, u, q, k, v = _inproj(n, w_in_l, conv_w[l], gq, gk, ones, tm=tm_s, seg=S, edges=(e1, e2),
                                  name="inproj_sample")
        o = _attn_sample(q, k, v, cache_k[l].reshape(Bs, WINDOW, dk), cache_v[l].reshape(Bs, WINDOW, dk),
                         bias_s, sink_s, nb=4, name="attn_sample")
        mix = _mix(n, ucb, o, w_co, w_ao_p, w_in_l, gc_col=gc_col, ga_col=ga_col, cw=tn, tm=tm_s, name="mix_sample")
        h2 = _outproj(h1, mix, w_o, tm=tm_s, name="outproj_sample")
        h3, = _ffn(h2, *f2, None, tm=tm_s, tf=tf, name="ffn2_sample")
        ys = _ple(h3, p_sample[l].reshape(Ms, -1), vec(ple_norm), w_p, w_pg, tm=tm_s, name="ple_sample")
        conv_s.append(u.reshape(Bs, S, d_conv)[:, S - 2:])
        k_s.append(k.reshape(Bs, S, N_KV, HEAD_DIM))
        v_s.append(v.reshape(Bs, S, N_KV, HEAD_DIM))

    return (yp.reshape(B, T, D), ys.reshape(Bs, S, D), jnp.stack(conv_p), jnp.stack(k_p), jnp.stack(v_p),
            jnp.stack(conv_s), jnp.stack(k_s), jnp.stack(v_s))
```

```python
import functools
import math

import jax
import jax.numpy as jnp
from jax import lax
from jax.experimental import pallas as pl
from jax.experimental.pallas import tpu as pltpu

_BF = jnp.bfloat16
_F32 = jnp.float32

CHUNK = 64
WINDOW = 128
N_HEADS = 16
N_KV = 4
GROUP = N_HEADS // N_KV
HEAD_DIM = 64
N_BUCKETS = 32
MAX_DIST = 128
EPS = 1e-6
LOG2E = math.log2(math.e)
QK_SCALE = HEAD_DIM ** -0.5 * LOG2E

V7X_VMEM_BYTES = 64 * 1024 * 1024
VMEM_INTERNAL_BYTES = 12 * 1024 * 1024


def _vmem_limit(block_bytes, scratch_bytes=0):
    need = 2 * sum(block_bytes) + scratch_bytes + VMEM_INTERNAL_BYTES
    return int(min(need, V7X_VMEM_BYTES - 4 * 1024 * 1024))


def _nbytes(shape, dtype):
    return math.prod(shape) * jnp.dtype(dtype).itemsize


def _dot(a, b):
    return jnp.dot(a, b, preferred_element_type=_F32)


def _rms(x, g):
    ms = jnp.mean(x * x, axis=-1, keepdims=True)
    return x * lax.rsqrt(ms + EPS) * g


BF16_TILE_ROWS = 16


def _cast_specs(arrays, grid):
    steps = math.prod(grid)

    def flat(*ids):
        s = 0
        for k, g in zip(ids, grid):
            s = s * g + k
        return s

    in_specs, out_specs, out_shape, nbytes = [], [], [], 0
    for a in arrays:
        rows, cols = a.shape
        rb = max(BF16_TILE_ROWS, rows // steps)
        assert rows % rb == 0 and steps % (rows // rb) == 0, (a.shape, steps)
        rep = steps // (rows // rb)
        idx = lambda *ids, rep=rep: (flat(*ids) // rep, 0)
        in_specs.append(pl.BlockSpec((rb, cols), idx))
        out_specs.append(pl.BlockSpec((rb, cols), idx))
        out_shape.append(jax.ShapeDtypeStruct((rows, cols), _BF))
        nbytes += _nbytes((rb, cols), _F32) + _nbytes((rb, cols), _BF)
    return in_specs, out_specs, out_shape, nbytes


def _with_casts(body, n_in, n_out, n_cast):
    def kernel(*refs):
        outs_at = n_in + n_cast
        scratch_at = outs_at + n_out + n_cast
        for src, dst in zip(refs[n_in:outs_at], refs[outs_at + n_out:scratch_at]):
            dst[...] = src[...].astype(_BF)
        body(*refs[:n_in], *refs[outs_at:outs_at + n_out], *refs[scratch_at:])
    return kernel


def _ffn_kernel(*refs, emit_norm, keep_weights):
    x_ref, g_ref, wg_ref, wu_ref, wd_ref = refs[:5]
    rest = list(refs[5:])
    g2_ref = rest.pop(0) if emit_norm else None
    out_ref = rest.pop(0)
    n_ref = rest.pop(0) if emit_norm else None
    weight_copies = [rest.pop(0) for _ in range(3)] if keep_weights else []
    xn_ref, = rest
    j = pl.program_id(1)
    last = pl.num_programs(1) - 1

    @pl.when(j == 0)
    def _():
        xn_ref[...] = _rms(x_ref[...], g_ref[...]).astype(_BF)

    wg, wu, wd = wg_ref[...], wu_ref[...], wd_ref[...]
    if keep_weights:
        wg, wu, wd = wg.astype(_BF), wu.astype(_BF), wd.astype(_BF)
        for dst, w in zip(weight_copies, (wg, wu, wd)):
            dst[...] = w
    xn = xn_ref[...]
    gate = _dot(xn, wg)
    up = _dot(xn, wu)
    hid = (jax.nn.silu(gate) * up * 0.5).astype(_BF)
    out_ref[...] = jnp.where(j == 0, x_ref[...], out_ref[...]) + _dot(hid, wd)

    if emit_norm:
        @pl.when(j == last)
        def _():
            n_ref[...] = _rms(out_ref[...], g2_ref[...]).astype(_BF)


def _ffn(x, g, wg, wu, wd, g2, *, tm, tf, name, casts=()):
    M, D = x.shape
    F = wg.shape[1]
    emit_norm = g2 is not None
    keep_weights = wg.dtype == _F32
    grid = (M // tm, F // tf)
    assert not keep_weights or grid[0] == 1
    row = lambda i, j: (i, 0)
    w_specs = [
        pl.BlockSpec((D, tf), lambda i, j: (0, j)),
        pl.BlockSpec((D, tf), lambda i, j: (0, j)),
        pl.BlockSpec((tf, D), lambda i, j: (j, 0)),
    ]
    in_specs = [pl.BlockSpec((tm, D), row), pl.BlockSpec((1, D), lambda i, j: (0, 0))] + w_specs
    args = [x, g, wg, wu, wd]
    out_shape = [jax.ShapeDtypeStruct((M, D), _F32)]
    out_specs = [pl.BlockSpec((tm, D), row)]
    blocks = [_nbytes((tm, D), _F32) * 2, 3 * _nbytes((D, tf), wg.dtype)]
    if emit_norm:
        in_specs.append(pl.BlockSpec((1, D), lambda i, j: (0, 0)))
        args.append(g2)
        out_shape.append(jax.ShapeDtypeStruct((M, D), _BF))
        out_specs.append(pl.BlockSpec((tm, D), row))
        blocks.append(_nbytes((tm, D), _BF))
    if keep_weights:
        out_shape += [jax.ShapeDtypeStruct(w.shape, _BF) for w in (wg, wu, wd)]
        out_specs += w_specs
        blocks.append(3 * _nbytes((D, tf), _BF))
    c_in, c_out, c_shape, c_bytes = _cast_specs(casts, grid)
    blocks.append(c_bytes)
    body = functools.partial(_ffn_kernel, emit_norm=emit_norm, keep_weights=keep_weights)
    res = pl.pallas_call(
        _with_casts(body, len(in_specs), len(out_specs), len(casts)),
        grid=grid,
        in_specs=in_specs + c_in,
        out_specs=out_specs + c_out,
        out_shape=out_shape + c_shape,
        scratch_shapes=[pltpu.VMEM((tm, D), _BF)],
        compiler_params=pltpu.CompilerParams(
            dimension_semantics=("arbitrary", "arbitrary"),
            vmem_limit_bytes=_vmem_limit(blocks, _nbytes((tm, D), _BF)),
        ),
        name=name,
    )(*args, *casts)
    return res


def _conv_kernel(*refs, carry_rows, seg, tiles_per_seg, d_conv, cw):
    if carry_rows:
        n_ref, w_ref, cwt_ref, ucb_ref, ulast_ref, carry_ref = refs
    else:
        n_ref, w_ref, cwt_ref, e1_ref, e2_ref, ucb_ref, u_ref = refs
    tm = n_ref.shape[0]
    n = n_ref[...]
    if carry_rows:
        @pl.when(pl.program_id(0) % tiles_per_seg == 0)
        def _():
            carry_ref[...] = jnp.zeros_like(carry_ref)

    for c in range(d_conv // cw):
        lo = c * cw
        cb = _dot(n, w_ref[:, lo:lo + cw])
        cc = _dot(n, w_ref[:, d_conv + lo:d_conv + lo + cw])
        cv = _dot(n, w_ref[:, 2 * d_conv + lo:2 * d_conv + lo + cw])
        u = cc * cv
        r1 = pltpu.roll(u, 1, 0)
        r2 = pltpu.roll(u, 2, 0)
        if carry_rows:
            prev = carry_ref[:, lo:lo + cw]
            rows8 = lax.broadcasted_iota(jnp.int32, (8, cw), 0)
            h1 = jnp.where(rows8 < 1, pltpu.roll(prev, 1, 0), r1[:8])
            h2 = jnp.where(rows8 < 2, pltpu.roll(prev, 2, 0), r2[:8])
            r1 = jnp.concatenate([h1, r1[8:]], axis=0)
            r2 = jnp.concatenate([h2, r2[8:]], axis=0)
            carry_ref[:, lo:lo + cw] = u[tm - 8:]
            ulast_ref[0, :, lo:lo + cw] = u[tm - 8:]
        else:
            pos = lax.broadcasted_iota(jnp.int32, (tm, cw), 0) % seg
            r1 = jnp.where(pos < 1, e1_ref[:, lo:lo + cw], r1)
            r2 = jnp.where(pos < 2, e2_ref[:, lo:lo + cw], r2)
            u_ref[:, lo:lo + cw] = u
        w0 = cwt_ref[0:1, lo:lo + cw]
        w1 = cwt_ref[1:2, lo:lo + cw]
        w2 = cwt_ref[2:3, lo:lo + cw]
        conv = w0 * r2 + w1 * r1 + w2 * u
        ucb_ref[:, lo:lo + cw] = (cb * conv).astype(_BF)


def _qkv_kernel(n_ref, w_ref, gq_ref, gk_ref, ones_ref, q_ref, k_ref, v_ref, *, cw, col0):
    ones = ones_ref[...]
    dq = q_ref.shape[1]
    dk = k_ref.shape[1]
    p = _dot(n_ref[...], w_ref[:, col0:col0 + dq + 2 * dk])
    x = p[:, :dq + dk]
    sq = x * x
    hi = sq.astype(_BF)
    lo = (sq - hi.astype(_F32)).astype(_BF)
    ss = jnp.concatenate(
        [_dot(hi[:, c * cw:(c + 1) * cw], ones) + _dot(lo[:, c * cw:(c + 1) * cw], ones)
         for c in range((dq + dk) // cw)], axis=1)
    xr = x * lax.rsqrt(ss * (1.0 / HEAD_DIM) + EPS)
    q_ref[...] = (xr[:, :dq] * gq_ref[...] * QK_SCALE).astype(_BF)
    k_ref[...] = xr[:, dq:] * gk_ref[...]
    v_ref[...] = p[:, dq + dk:]


def _inproj_kernel(*refs, carry_rows, seg, tiles_per_seg, d_conv, cw):
    if carry_rows:
        (n_ref, w_ref, cwt_ref, gq_ref, gk_ref, ones_ref,
         ucb_ref, ulast_ref, q_ref, k_ref, v_ref, carry_ref) = refs
        conv_refs = (n_ref, w_ref, cwt_ref, ucb_ref, ulast_ref, carry_ref)
    else:
        (n_ref, w_ref, cwt_ref, gq_ref, gk_ref, ones_ref, e1_ref, e2_ref,
         ucb_ref, u_ref, q_ref, k_ref, v_ref) = refs
        conv_refs = (n_ref, w_ref, cwt_ref, e1_ref, e2_ref, ucb_ref, u_ref)
    _conv_kernel(*conv_refs, carry_rows=carry_rows, seg=seg, tiles_per_seg=tiles_per_seg, d_conv=d_conv, cw=cw)
    _qkv_kernel(n_ref, w_ref, gq_ref, gk_ref, ones_ref, q_ref, k_ref, v_ref, cw=cw, col0=3 * d_conv)


def _inproj(n, w_in, conv_w, gq, gk, ones, *, tm, name, batch=None, seg=None, edges=(), casts=()):
    M, D = n.shape
    d_conv = conv_w.shape[1]
    cw = ones.shape[0]
    dq, dk = N_HEADS * HEAD_DIM, N_KV * HEAD_DIM
    wcols = 3 * d_conv + dq + 2 * dk
    carry_rows = batch is not None
    grid = (M // tm,)
    tiles_per_seg = (M // batch) // tm if carry_rows else None
    c_in, c_out, c_shape, c_bytes = _cast_specs(casts, grid)
    const = lambda i: (0, 0)
    row = lambda i: (i, 0)
    in_specs = [
        pl.BlockSpec((tm, D), row),
        pl.BlockSpec((D, wcols), const, pipeline_mode=pl.Buffered(1)),
        pl.BlockSpec((3, d_conv), const),
        pl.BlockSpec((1, dq), const),
        pl.BlockSpec((1, dk), const),
        pl.BlockSpec((cw, cw), const),
    ] + [pl.BlockSpec((tm, d_conv), row) for _ in edges]
    if carry_rows:
        u_spec = pl.BlockSpec((1, 8, d_conv), lambda i: (i // tiles_per_seg, 0, 0))
        u_shape = jax.ShapeDtypeStruct((batch, 8, d_conv), _F32)
        scratch = [pltpu.VMEM((8, d_conv), _F32)]
    else:
        u_spec = pl.BlockSpec((tm, d_conv), row)
        u_shape = jax.ShapeDtypeStruct((M, d_conv), _F32)
        scratch = []
    out_specs = [pl.BlockSpec((tm, d_conv), row), u_spec,
                 pl.BlockSpec((tm, dq), row), pl.BlockSpec((tm, dk), row), pl.BlockSpec((tm, dk), row)]
    out_shape = [jax.ShapeDtypeStruct((M, d_conv), _BF), u_shape, jax.ShapeDtypeStruct((M, dq), _BF),
                 jax.ShapeDtypeStruct((M, dk), _F32), jax.ShapeDtypeStruct((M, dk), _F32)]
    blocks = [_nbytes((tm, D), _BF), _nbytes((D, wcols), _BF) // 2, (1 + len(edges)) * _nbytes((tm, d_conv), _F32),
              _nbytes((tm, d_conv + dq), _BF), 2 * _nbytes((tm, dk), _F32), c_bytes]
    body = functools.partial(_inproj_kernel, carry_rows=carry_rows, seg=seg, tiles_per_seg=tiles_per_seg,
                             d_conv=d_conv, cw=cw)
    return pl.pallas_call(
        _with_casts(body, len(in_specs), len(out_specs), len(casts)),
        grid=grid,
        in_specs=in_specs + c_in,
        out_specs=out_specs + c_out,
        out_shape=out_shape + c_shape,
        scratch_shapes=scratch,
        compiler_params=pltpu.CompilerParams(
            dimension_semantics=("arbitrary",),
            vmem_limit_bytes=_vmem_limit(blocks),
        ),
        name=name,
    )(n, w_in, conv_w, gq, gk, ones, *edges, *casts)


def _bias_kernel(tab_ref, bkt_ref, out_ref, *, mask_step):
    bkt = bkt_ref[...]
    key = lax.broadcasted_iota(jnp.int32, bkt.shape, 1)
    for h in range(out_ref.shape[1]):
        acc = jnp.zeros(bkt.shape, _F32)
        for b in range(tab_ref.shape[0]):
            acc = jnp.where(bkt == b, tab_ref[b, h] * LOG2E, acc)
        for v in range(out_ref.shape[0]):
            out_ref[v, h] = jnp.where(key < v * mask_step, -jnp.inf, acc)


def _rel_bias(table, bucket, n_variants, mask_step, name):
    nq, nk = bucket.shape
    nh = table.shape[1]
    out = pl.pallas_call(
        functools.partial(_bias_kernel, mask_step=mask_step),
        in_specs=[pl.BlockSpec(memory_space=pltpu.SMEM), pl.BlockSpec((nq, nk), lambda: (0, 0))],
        out_specs=pl.BlockSpec((n_variants, nh, nq, nk), lambda: (0, 0, 0, 0)),
        out_shape=jax.ShapeDtypeStruct((n_variants, nh, nq, nk), _F32),
        name=name,
    )(table, bucket)
    return out.transpose(0, 3, 1, 2).reshape(n_variants, nk, nh * nq)


def _t5_bucket(rel):
    nb = N_BUCKETS // 2
    max_exact = nb // 2
    ret = jnp.where(rel > 0, nb, 0)
    n = jnp.abs(rel)
    nf = jnp.maximum(n, 1).astype(jnp.float32)
    large = max_exact + (jnp.log(nf / max_exact) / math.log(MAX_DIST / max_exact) * (nb - max_exact)).astype(jnp.int32)
    large = jnp.minimum(large, nb - 1)
    return ret + jnp.where(n < max_exact, n, large)


def _bucket_index(n_q, n_k):
    i = jnp.arange(n_q)[:, None]
    j = jnp.arange(n_k)[None, :]
    return _t5_bucket(j - WINDOW - i).astype(jnp.int32)


_NT = (((1,), (1,)), ((), ()))
_TN = (((0,), (0,)), ((), ()))


def _sink_softmax_keys_major(lg, sink_t):
    sink2 = sink_t * LOG2E
    m = jnp.maximum(jnp.max(lg, axis=0, keepdims=True), sink2)
    e = jnp.exp2(lg - m)
    den = jnp.sum(e, axis=0, keepdims=True) + jnp.exp2(sink2 - m)
    return e, 1.0 / den


def _attn_chunk_keys_major(qc, kw, vw, bias_t, sink_t):
    nq = qc.shape[0]
    gq = GROUP * nq
    logits = []
    for kv in range(N_KV):
        qg = jnp.concatenate(
            [qc[:, (kv * GROUP + g) * HEAD_DIM:(kv * GROUP + g + 1) * HEAD_DIM] for g in range(GROUP)], axis=0)
        kk = kw[:, kv * HEAD_DIM:(kv + 1) * HEAD_DIM]
        logits.append(lax.dot_general(kk, qg, _NT, preferred_element_type=_F32))
    e, inv = _sink_softmax_keys_major(jnp.concatenate(logits, axis=1) + bias_t, sink_t)
    e = e.astype(_BF)
    outs = []
    for kv in range(N_KV):
        vv = vw[:, kv * HEAD_DIM:(kv + 1) * HEAD_DIM]
        cols = slice(kv * gq, (kv + 1) * gq)
        outs.append(lax.dot_general(vv, e[:, cols], _TN, preferred_element_type=_F32) * inv[:, cols])
    return jnp.concatenate(outs, axis=0).T


def _attn_prompt_kernel(q_ref, kp_ref, kc_ref, vp_ref, vc_ref, bias_ref, sink_ref, o_ref, kwin, vwin,
                        *, tiles_per_batch):
    ta = q_ref.shape[0]
    nk = WINDOW + CHUNK
    dkv = N_KV * HEAD_DIM
    kwin[0:WINDOW, :] = kp_ref[...].astype(_BF)
    kwin[WINDOW:, :] = kc_ref[...].astype(_BF)
    vwin[0:WINDOW, :] = vp_ref[...].astype(_BF)
    vwin[WINDOW:, :] = vc_ref[...].astype(_BF)
    tile_pos = (pl.program_id(0) % tiles_per_batch) * ta

    def chunk(c, carry):
        r = pl.multiple_of(c * CHUNK, CHUNK)
        n_before = jnp.maximum(0, (WINDOW - (tile_pos + r)) // CHUNK)
        o = _attn_chunk_keys_major(q_ref[pl.ds(r, CHUNK), :], kwin[pl.ds(r, nk), :], vwin[pl.ds(r, nk), :],
                                   bias_ref[n_before], sink_ref[...])
        for g in range(GROUP):
            o_ref[pl.ds(r, CHUNK), g * dkv:(g + 1) * dkv] = o[g * CHUNK:(g + 1) * CHUNK].astype(_BF)
        return carry

    lax.fori_loop(0, ta // CHUNK, chunk, 0, unroll=8)


def _attn_prompt(q, k, v, bias, sink, *, batch, ta, name, casts=()):
    M, dq = q.shape
    dk = k.shape[1]
    tiles_per_batch = (M // batch) // ta
    wb = ta // WINDOW
    nk = WINDOW + CHUNK

    def prev(i):
        return (jnp.where(i % tiles_per_batch == 0, i * wb, i * wb - 1), 0)

    row = lambda i: (i, 0)
    grid = (M // ta,)
    c_in, c_out, c_shape, c_bytes = _cast_specs(casts, grid)
    blocks = [_nbytes((ta, dq), _BF) * 2, 2 * _nbytes((WINDOW + ta, dk), _F32),
              _nbytes(bias.shape, _F32), _nbytes((8, N_HEADS * CHUNK), _F32), c_bytes]
    return pl.pallas_call(
        _with_casts(functools.partial(_attn_prompt_kernel, tiles_per_batch=tiles_per_batch), 7, 1, len(casts)),
        grid=grid,
        in_specs=[
            pl.BlockSpec((ta, dq), row),
            pl.BlockSpec((WINDOW, dk), prev),
            pl.BlockSpec((ta, dk), row),
            pl.BlockSpec((WINDOW, dk), prev),
            pl.BlockSpec((ta, dk), row),
            pl.BlockSpec(bias.shape, lambda i: (0, 0, 0)),
            pl.BlockSpec((1, N_HEADS * CHUNK), lambda i: (0, 0)),
        ] + c_in,
        out_specs=[pl.BlockSpec((ta, dq), row)] + c_out,
        out_shape=[jax.ShapeDtypeStruct((M, dq), _BF)] + c_shape,
        scratch_shapes=[pltpu.VMEM((WINDOW + ta, dk), _BF), pltpu.VMEM((WINDOW + ta, dk), _BF)],
        compiler_params=pltpu.CompilerParams(
            dimension_semantics=("arbitrary",),
            vmem_limit_bytes=_vmem_limit(blocks, 2 * _nbytes((WINDOW + ta, dk), _BF)),
        ),
        name=name,
    )(q, k, k, v, v, bias, sink, *casts)


LANES = 128


def _attn_sample_kernel(q_ref, ck_ref, cv_ref, k_ref, v_ref, bias_ref, sink_ref, o_ref, *, nb, s):
    dkv = N_KV * HEAD_DIM
    gs = GROUP * s
    assert LANES == 2 * HEAD_DIM and dkv == 2 * LANES
    for b in range(nb):
        rows = slice(b * s, (b + 1) * s)
        kw = jnp.concatenate([ck_ref[b].astype(_BF), k_ref[rows, :].astype(_BF)], axis=0)
        vw = jnp.concatenate([cv_ref[b].astype(_BF), v_ref[rows, :].astype(_BF)], axis=0)
        qf = q_ref[rows, :].astype(_F32)
        blocks = []
        for kv in range(N_KV):
            for g in range(GROUP):
                h = kv * GROUP + g
                pair = qf[:, (h // 2) * LANES:(h // 2 + 1) * LANES]
                if h % 2 != kv % 2:
                    pair = pltpu.roll(pair, HEAD_DIM, 1)
                half = lax.broadcasted_iota(jnp.int32, pair.shape, 1) // HEAD_DIM
                piece = jnp.where(half == kv % 2, pair, 0.0)
                zero = jnp.zeros_like(piece)
                blocks.append(jnp.concatenate([piece, zero] if kv < 2 else [zero, piece], axis=1))
        q_bd = jnp.concatenate(blocks, axis=0).astype(_BF)
        lg = lax.dot_general(kw, q_bd, _NT, preferred_element_type=_F32) + bias_ref[0]
        e, inv = _sink_softmax_keys_major(lg, sink_ref[...])
        z = lax.dot_general((e * inv).astype(_BF), vw, _TN, preferred_element_type=_F32)
        row_kv = lax.broadcasted_iota(jnp.int32, z.shape, 0) // gs
        col_kv = lax.broadcasted_iota(jnp.int32, z.shape, 1) // HEAD_DIM
        z = jnp.where(row_kv == col_kv, z, 0.0)
        r = z[0:gs]
        for kv in range(1, N_KV):
            r = r + z[kv * gs:(kv + 1) * gs]
        for g in range(GROUP):
            o_ref[rows, g * dkv:(g + 1) * dkv] = r[g * s:(g + 1) * s].astype(_BF)


def _attn_sample(q, k, v, cache_k, cache_v, bias, sink, *, nb, name):
    B, W, dk = cache_k.shape
    M, dq = q.shape
    S = M // B
    row = lambda b: (b, 0)
    return pl.pallas_call(
        functools.partial(_attn_sample_kernel, nb=nb, s=S),
        grid=(B // nb,),
        in_specs=[
            pl.BlockSpec((nb * S, dq), row),
            pl.BlockSpec((nb, W, dk), lambda b: (b, 0, 0)),
            pl.BlockSpec((nb, W, dk), lambda b: (b, 0, 0)),
            pl.BlockSpec((nb * S, dk), row),
            pl.BlockSpec((nb * S, dk), row),
            pl.BlockSpec(bias.shape, lambda b: (0, 0, 0)),
            pl.BlockSpec(sink.shape, lambda b: (0, 0)),
        ],
        out_specs=pl.BlockSpec((nb * S, dq), row),
        out_shape=jax.ShapeDtypeStruct((M, dq), _BF),
        compiler_params=pltpu.CompilerParams(dimension_semantics=("arbitrary",)),
        name=name,
    )(q, cache_k, cache_v, k, v, bias, sink)


def _mix_kernel(*refs, n_chunks):
    n_ref, ucb_ref, o_ref, wco_ref, wao_ref = refs[:5]
    wgc_refs = refs[5:5 + n_chunks]
    wga_refs = refs[5 + n_chunks:5 + 2 * n_chunks]
    mix_ref = refs[5 + 2 * n_chunks]
    n, ucb, o = n_ref[...], ucb_ref[...], o_ref[...]
    cw = mix_ref.shape[1] // n_chunks
    for c in range(n_chunks):
        cols = slice(c * cw, (c + 1) * cw)
        y_conv = _dot(ucb, wco_ref[:, cols])
        y_attn = _dot(o, wao_ref[:, cols])
        g_conv = jax.nn.sigmoid(_dot(n, wgc_refs[c][...]))
        g_attn = jax.nn.sigmoid(_dot(n, wga_refs[c][...]))
        mix_ref[:, cols] = (g_conv * y_conv + g_attn * y_attn).astype(_BF)


def _mix(n, ucb, o, w_conv_out, w_attn_o, w_in, *, gc_col, ga_col, cw, tm, name):
    M, D = n.shape
    dc, da = ucb.shape[1], o.shape[1]
    n_chunks = D // cw
    row = lambda i: (i, 0)
    once = pl.Buffered(1)
    gate_specs = [pl.BlockSpec((D, cw), lambda i, b=(col // cw + c): (0, b), pipeline_mode=once)
                  for col in (gc_col, ga_col) for c in range(n_chunks)]
    blocks = [_nbytes((tm, D + dc + da), _BF), _nbytes((dc + da + 2 * D, D), _BF) // 2, _nbytes((tm, D), _BF)]
    return pl.pallas_call(
        functools.partial(_mix_kernel, n_chunks=n_chunks),
        grid=(M // tm,),
        in_specs=[
            pl.BlockSpec((tm, D), row),
            pl.BlockSpec((tm, dc), row),
            pl.BlockSpec((tm, da), row),
            pl.BlockSpec((dc, D), lambda i: (0, 0), pipeline_mode=once),
            pl.BlockSpec((da, D), lambda i: (0, 0), pipeline_mode=once),
        ] + gate_specs,
        out_specs=pl.BlockSpec((tm, D), row),
        out_shape=jax.ShapeDtypeStruct((M, D), _BF),
        compiler_params=pltpu.CompilerParams(
            dimension_semantics=("arbitrary",),
            vmem_limit_bytes=_vmem_limit(blocks),
        ),
        name=name,
    )(n, ucb, o, w_conv_out, w_attn_o, *([w_in] * (2 * n_chunks)))


def _outproj_kernel(h_ref, mix_ref, w_ref, out_ref):
    out_ref[...] = h_ref[...] + _dot(mix_ref[...], w_ref[...])


def _outproj(h, mix, w_out, *, tm, name):
    M, D = h.shape
    row = lambda i: (i, 0)
    blocks = [2 * _nbytes((tm, D), _F32), _nbytes((tm, D), _BF), _nbytes((D, D), _BF) // 2]
    return pl.pallas_call(
        _outproj_kernel,
        grid=(M // tm,),
        in_specs=[
            pl.BlockSpec((tm, D), row),
            pl.BlockSpec((tm, D), row),
            pl.BlockSpec((D, D), lambda i: (0, 0), pipeline_mode=pl.Buffered(1)),
        ],
        out_specs=pl.BlockSpec((tm, D), row),
        out_shape=jax.ShapeDtypeStruct((M, D), _F32),
        compiler_params=pltpu.CompilerParams(
            dimension_semantics=("arbitrary",),
            vmem_limit_bytes=_vmem_limit(blocks),
        ),
        name=name,
    )(h, mix, w_out)


def _ple_kernel(h_ref, pe_ref, g_ref, wp_ref, wg_ref, out_ref):
    h = h_ref[...]
    gate = jax.nn.sigmoid(_dot(_rms(h, g_ref[...]).astype(_BF), wg_ref[...]))
    emb = _dot(pe_ref[...].astype(_BF), wp_ref[...])
    out_ref[...] = h + emb * gate


def _ple(h, pe, g, w_ple, w_gate, *, tm, name):
    M, D = h.shape
    dp = pe.shape[1]
    row = lambda i: (i, 0)
    const = lambda i: (0, 0)
    blocks = [2 * _nbytes((tm, D), _F32), _nbytes((tm, dp), _F32), _nbytes((dp + D, D), _BF)]
    return pl.pallas_call(
        _ple_kernel,
        grid=(M // tm,),
        in_specs=[
            pl.BlockSpec((tm, D), row),
            pl.BlockSpec((tm, dp), row),
            pl.BlockSpec((1, D), const),
            pl.BlockSpec((dp, D), const),
            pl.BlockSpec((D, D), const),
        ],
        out_specs=pl.BlockSpec((tm, D), row),
        out_shape=jax.ShapeDtypeStruct((M, D), _F32),
        compiler_params=pltpu.CompilerParams(
            dimension_semantics=("arbitrary",),
            vmem_limit_bytes=_vmem_limit(blocks),
        ),
        name=name,
    )(h, pe, g, w_ple, w_gate)


def _row_tile(m, want):
    return want if m % want == 0 else m


def kernel(x_prompt, x_sample, p_prompt, p_sample, state_conv, cache_k, cache_v, rel_table, ffn1_norm, ffn1_wg, ffn1_wu, ffn1_wd, mix_norm, w_in, conv_w, q_norm, k_norm, attn_sink, w_conv_out, w_attn_o, w_out, ffn2_norm, ffn2_wg, ffn2_wu, ffn2_wd, ple_norm, w_ple, w_ple_gate):
    B, T, D = x_prompt.shape
    Bs, S, _ = x_sample.shape
    depth = ffn1_wg.shape[0]
    d_conv = conv_w.shape[2]
    dq, dk = N_HEADS * HEAD_DIM, N_KV * HEAD_DIM
    qkv_col = 3 * d_conv
    gc_col = qkv_col + dq + 2 * dk
    ga_col = gc_col + D
    Mp, Ms = B * T, Bs * S
    tm_p = _row_tile(Mp, 512)
    tm_s = _row_tile(Ms, 512)
    tf = 1024
    tn = 512
    cw = 256

    nk_p, nk_s = WINDOW + CHUNK, WINDOW + S
    bias_p = _rel_bias(rel_table, _bucket_index(CHUNK, nk_p), WINDOW // CHUNK + 1, CHUNK, "rel_bias_prompt")
    bias_s = _rel_bias(rel_table, _bucket_index(S, nk_s), 1, 0, "rel_bias_sample")
    lane = jnp.arange(cw) // HEAD_DIM
    ones = (lane[:, None] == lane[None, :]).astype(_BF)

    yp = x_prompt.reshape(Mp, D)
    ys = x_sample.reshape(Ms, D)
    conv_p, k_p, v_p, conv_s, k_s, v_s = [], [], [], [], [], []
    for l in range(depth):
        vec = lambda g: g[l].reshape(1, -1)
        h1_s, n_s, f1_wg, f1_wu, f1_wd = _ffn(ys, vec(ffn1_norm), ffn1_wg[l], ffn1_wu[l], ffn1_wd[l],
                                              vec(mix_norm), tm=tm_s, tf=tf // 2, name="ffn1_sample")
        f1 = (vec(ffn1_norm), f1_wg, f1_wu, f1_wd)
        gq = jnp.tile(q_norm[l], N_HEADS).reshape(1, dq)
        gk = jnp.tile(k_norm[l], N_KV).reshape(1, dk)
        sink_p = jnp.repeat(attn_sink[l], CHUNK).reshape(1, N_HEADS * CHUNK)
        sink_s = jnp.repeat(attn_sink[l], S).reshape(1, N_HEADS * S)

        h1, n, w_in_l = _ffn(yp, *f1, vec(mix_norm), tm=tm_p, tf=tf, name="ffn1_prompt", casts=(w_in[l],))
        ucb, ulast, q, k, v, f2_wg, f2_wu, w_co, w_ao, w_o = _inproj(
            n, w_in_l, conv_w[l], gq, gk, ones, tm=tm_p, batch=B, name="inproj_prompt",
            casts=(ffn2_wg[l], ffn2_wu[l], w_conv_out[l], w_attn_o[l], w_out[l]))
        o, f2_wd, w_p, w_pg = _attn_prompt(q, k, v, bias_p, sink_p, batch=B, ta=2 * tm_p, name="attn_prompt",
                                           casts=(ffn2_wd[l], w_ple[l], w_ple_gate[l]))
        w_ao_p = w_ao.reshape(N_KV, GROUP, HEAD_DIM, D).transpose(1, 0, 2, 3).reshape(dq, D)
        mix = _mix(n, ucb, o, w_co, w_ao_p, w_in_l, gc_col=gc_col, ga_col=ga_col, cw=tn, tm=tm_p, name="mix_prompt")
        h2 = _outproj(h1, mix, w_o, tm=tm_p, name="outproj_prompt")
        f2 = (vec(ffn2_norm), f2_wg, f2_wu, f2_wd)
        h3, = _ffn(h2, *f2, None, tm=tm_p, tf=tf, name="ffn2_prompt")
        yp = _ple(h3, p_prompt[l].reshape(Mp, -1), vec(ple_norm), w_p, w_pg, tm=tm_p, name="ple_prompt")
        conv_p.append(ulast[:, 6:8])
        k_p.append(k.reshape(B, T, dk)[:, T - WINDOW:].reshape(B, WINDOW, N_KV, HEAD_DIM))
        v_p.append(v.reshape(B, T, dk)[:, T - WINDOW:].reshape(B, WINDOW, N_KV, HEAD_DIM))

        st = state_conv[l]
        zeros = jnp.zeros((Bs, S - 2, d_conv), _F32)
        e2 = jnp.concatenate([st, zeros], axis=1).reshape(Ms, d_conv)
        e1 = jnp.concatenate([st[:, 1:], zeros, zeros[:, :1]], axis=1).reshape(Ms, d_conv)
        h1, n = h1_s, n_s
        ucb, u, q, k, v = _inproj(n, w_in_l, conv_w[l], gq, gk, ones, tm=tm_s, seg=S, edges=(e1, e2),
                                  name="inproj_sample")
        o = _attn_sample(q, k, v, cache_k[l].reshape(Bs, WINDOW, dk), cache_v[l].reshape(Bs, WINDOW, dk),
                         bias_s, sink_s, nb=4, name="attn_sample")
        mix = _mix(n, ucb, o, w_co, w_ao_p, w_in_l, gc_col=gc_col, ga_col=ga_col, cw=tn, tm=tm_s, name="mix_sample")
        h2 = _outproj(h1, mix, w_o, tm=tm_s, name="outproj_sample")
        h3, = _ffn(h2, *f2, None, tm=tm_s, tf=tf, name="ffn2_sample")
        ys = _ple(h3, p_sample[l].reshape(Ms, -1), vec(ple_norm), w_p, w_pg, tm=tm_s, name="ple_sample")
        conv_s.append(u.reshape(Bs, S, d_conv)[:, S - 2:])
        k_s.append(k.reshape(Bs, S, N_KV, HEAD_DIM))
        v_s.append(v.reshape(Bs, S, N_KV, HEAD_DIM))

    return (yp.reshape(B, T, D), ys.reshape(Bs, S, D), jnp.stack(conv_p), jnp.stack(k_p), jnp.stack(v_p),
            jnp.stack(conv_s), jnp.stack(k_s), jnp.stack(v_s))
```

```python
import functools
import math

import jax
import jax.numpy as jnp
from jax import lax
from jax.experimental import pallas as pl
from jax.experimental.pallas import tpu as pltpu

_BF = jnp.bfloat16
_F32 = jnp.float32

CHUNK = 64
WINDOW = 128
N_HEADS = 16
N_KV = 4
GROUP = N_HEADS // N_KV
HEAD_DIM = 64
N_BUCKETS = 32
MAX_DIST = 128
EPS = 1e-6
LOG2E = math.log2(math.e)
QK_SCALE = HEAD_DIM ** -0.5 * LOG2E

V7X_VMEM_BYTES = 64 * 1024 * 1024
VMEM_INTERNAL_BYTES = 12 * 1024 * 1024


def _vmem_limit(block_bytes, scratch_bytes=0):
    need = 2 * sum(block_bytes) + scratch_bytes + VMEM_INTERNAL_BYTES
    return int(min(need, V7X_VMEM_BYTES - 4 * 1024 * 1024))


def _nbytes(shape, dtype):
    return math.prod(shape) * jnp.dtype(dtype).itemsize


def _dot(a, b):
    return jnp.dot(a, b, preferred_element_type=_F32)


def _rms(x, g):
    ms = jnp.mean(x * x, axis=-1, keepdims=True)
    return x * lax.rsqrt(ms + EPS) * g


BF16_TILE_ROWS = 16


def _cast_specs(arrays, grid):
    steps = math.prod(grid)

    def flat(*ids):
        s = 0
        for k, g in zip(ids, grid):
            s = s * g + k
        return s

    in_specs, out_specs, out_shape, nbytes = [], [], [], 0
    for a in arrays:
        a, rb, out_block = a if isinstance(a, tuple) else (a, None, lambda b: b)
        rows, cols = a.shape
        rb = rb or max(BF16_TILE_ROWS, rows // steps)
        assert rows % rb == 0 and steps % (rows // rb) == 0, (a.shape, steps)
        rep = steps // (rows // rb)
        idx = lambda *ids, rep=rep: (flat(*ids) // rep, 0)
        out_idx = lambda *ids, rep=rep, out_block=out_block: (out_block(flat(*ids) // rep), 0)
        in_specs.append(pl.BlockSpec((rb, cols), idx))
        out_specs.append(pl.BlockSpec((rb, cols), out_idx))
        out_shape.append(jax.ShapeDtypeStruct((rows, cols), _BF))
        nbytes += _nbytes((rb, cols), _F32) + _nbytes((rb, cols), _BF)
    return in_specs, out_specs, out_shape, nbytes


def _cast_args(arrays):
    return [a[0] if isinstance(a, tuple) else a for a in arrays]


def _with_casts(body, n_in, n_out, n_cast):
    def kernel(*refs):
        outs_at = n_in + n_cast
        scratch_at = outs_at + n_out + n_cast
        for src, dst in zip(refs[n_in:outs_at], refs[outs_at + n_out:scratch_at]):
            dst[...] = src[...].astype(_BF)
        body(*refs[:n_in], *refs[outs_at:outs_at + n_out], *refs[scratch_at:])
    return kernel


def _ffn_kernel(*refs, emit_norm, keep_weights, normed_input):
    x_ref, g_ref, wg_ref, wu_ref, wd_ref = refs[:5]
    rest = list(refs[5:])
    g2_ref = rest.pop(0) if emit_norm else None
    out_ref = rest.pop(0)
    n_ref = rest.pop(0) if emit_norm else None
    weight_copies = [rest.pop(0) for _ in range(3)] if keep_weights else []
    j = pl.program_id(1)
    last = pl.num_programs(1) - 1
    if normed_input:
        xn_ref = g_ref
    else:
        xn_ref, = rest

        @pl.when(j == 0)
        def _():
            xn_ref[...] = _rms(x_ref[...], g_ref[...]).astype(_BF)

    wg, wu, wd = wg_ref[...], wu_ref[...], wd_ref[...]
    if keep_weights:
        wg, wu, wd = wg.astype(_BF), wu.astype(_BF), wd.astype(_BF)
        for dst, w in zip(weight_copies, (wg, wu, wd)):
            dst[...] = w
    xn = xn_ref[...]
    gate = _dot(xn, wg)
    up = _dot(xn, wu)
    hid = (jax.nn.silu(gate) * up * 0.5).astype(_BF)
    out_ref[...] = jnp.where(j == 0, x_ref[...], out_ref[...]) + _dot(hid, wd)

    if emit_norm:
        @pl.when(j == last)
        def _():
            n_ref[...] = _rms(out_ref[...], g2_ref[...]).astype(_BF)


def _ffn(x, g, wg, wu, wd, g2, *, tm, tf, name, casts=()):
    M, D = x.shape
    F = wg.shape[1]
    emit_norm = g2 is not None
    keep_weights = wg.dtype == _F32
    normed_input = g.shape[0] == M
    grid = (M // tm, F // tf)
    assert not keep_weights or grid[0] == 1
    row = lambda i, j: (i, 0)
    w_specs = [
        pl.BlockSpec((D, tf), lambda i, j: (0, j)),
        pl.BlockSpec((D, tf), lambda i, j: (0, j)),
        pl.BlockSpec((tf, D), lambda i, j: (j, 0)),
    ]
    g_spec = pl.BlockSpec((tm, D), row) if normed_input else pl.BlockSpec((1, D), lambda i, j: (0, 0))
    in_specs = [pl.BlockSpec((tm, D), row), g_spec] + w_specs
    args = [x, g, wg, wu, wd]
    out_shape = [jax.ShapeDtypeStruct((M, D), _F32)]
    out_specs = [pl.BlockSpec((tm, D), row)]
    blocks = [_nbytes((tm, D), _F32) * 2, 3 * _nbytes((D, tf), wg.dtype)]
    if emit_norm:
        in_specs.append(pl.BlockSpec((1, D), lambda i, j: (0, 0)))
        args.append(g2)
        out_shape.append(jax.ShapeDtypeStruct((M, D), _BF))
        out_specs.append(pl.BlockSpec((tm, D), row))
        blocks.append(_nbytes((tm, D), _BF))
    if keep_weights:
        out_shape += [jax.ShapeDtypeStruct(w.shape, _BF) for w in (wg, wu, wd)]
        out_specs += w_specs
        blocks.append(3 * _nbytes((D, tf), _BF))
    c_in, c_out, c_shape, c_bytes = _cast_specs(casts, grid)
    blocks.append(c_bytes)
    body = functools.partial(_ffn_kernel, emit_norm=emit_norm, keep_weights=keep_weights,
                             normed_input=normed_input)
    res = pl.pallas_call(
        _with_casts(body, len(in_specs), len(out_specs), len(casts)),
        grid=grid,
        in_specs=in_specs + c_in,
        out_specs=out_specs + c_out,
        out_shape=out_shape + c_shape,
        scratch_shapes=[] if normed_input else [pltpu.VMEM((tm, D), _BF)],
        compiler_params=pltpu.CompilerParams(
            dimension_semantics=("arbitrary", "arbitrary"),
            vmem_limit_bytes=_vmem_limit(blocks, _nbytes((tm, D), _BF)),
        ),
        name=name,
    )(*args, *_cast_args(casts))
    return res


def _conv_kernel(*refs, carry_rows, seg, tiles_per_seg, d_conv, cw):
    if carry_rows:
        n_ref, w_ref, cwt_ref, ucb_ref, ulast_ref, carry_ref = refs
    else:
        n_ref, w_ref, cwt_ref, e1_ref, e2_ref, ucb_ref, u_ref = refs
    tm = n_ref.shape[0]
    n = n_ref[...]
    if carry_rows:
        @pl.when(pl.program_id(0) % tiles_per_seg == 0)
        def _():
            carry_ref[...] = jnp.zeros_like(carry_ref)

    for c in range(d_conv // cw):
        lo = c * cw
        cb = _dot(n, w_ref[:, lo:lo + cw])
        cc = _dot(n, w_ref[:, d_conv + lo:d_conv + lo + cw])
        cv = _dot(n, w_ref[:, 2 * d_conv + lo:2 * d_conv + lo + cw])
        u = cc * cv
        r1 = pltpu.roll(u, 1, 0)
        r2 = pltpu.roll(u, 2, 0)
        if carry_rows:
            prev = carry_ref[:, lo:lo + cw]
            rows8 = lax.broadcasted_iota(jnp.int32, (8, cw), 0)
            h1 = jnp.where(rows8 < 1, pltpu.roll(prev, 1, 0), r1[:8])
            h2 = jnp.where(rows8 < 2, pltpu.roll(prev, 2, 0), r2[:8])
            r1 = jnp.concatenate([h1, r1[8:]], axis=0)
            r2 = jnp.concatenate([h2, r2[8:]], axis=0)
            carry_ref[:, lo:lo + cw] = u[tm - 8:]
            ulast_ref[0, :, lo:lo + cw] = u[tm - 8:]
        else:
            pos = lax.broadcasted_iota(jnp.int32, (tm, cw), 0) % seg
            r1 = jnp.where(pos < 1, e1_ref[:, lo:lo + cw], r1)
            r2 = jnp.where(pos < 2, e2_ref[:, lo:lo + cw], r2)
            u_ref[:, lo:lo + cw] = u
        w0 = cwt_ref[0:1, lo:lo + cw]
        w1 = cwt_ref[1:2, lo:lo + cw]
        w2 = cwt_ref[2:3, lo:lo + cw]
        conv = w0 * r2 + w1 * r1 + w2 * u
        ucb_ref[:, lo:lo + cw] = (cb * conv).astype(_BF)


def _qkv_kernel(n_ref, w_ref, gq_ref, gk_ref, ones_ref, q_ref, k_ref, v_ref, *, cw, col0):
    ones = ones_ref[...]
    dq = q_ref.shape[1]
    dk = k_ref.shape[1]
    p = _dot(n_ref[...], w_ref[:, col0:col0 + dq + 2 * dk])
    x = p[:, :dq + dk]
    sq = x * x
    hi = sq.astype(_BF)
    lo = (sq - hi.astype(_F32)).astype(_BF)
    ss = jnp.concatenate(
        [_dot(hi[:, c * cw:(c + 1) * cw], ones) + _dot(lo[:, c * cw:(c + 1) * cw], ones)
         for c in range((dq + dk) // cw)], axis=1)
    xr = x * lax.rsqrt(ss * (1.0 / HEAD_DIM) + EPS)
    q_ref[...] = (xr[:, :dq] * gq_ref[...] * QK_SCALE).astype(_BF)
    k_ref[...] = xr[:, dq:] * gk_ref[...]
    v_ref[...] = p[:, dq + dk:]


def _inproj_kernel(*refs, carry_rows, seg, tiles_per_seg, d_conv, cw):
    if carry_rows:
        (n_ref, w_ref, cwt_ref, gq_ref, gk_ref, ones_ref,
         ucb_ref, ulast_ref, q_ref, k_ref, v_ref, carry_ref) = refs
        conv_refs = (n_ref, w_ref, cwt_ref, ucb_ref, ulast_ref, carry_ref)
    else:
        (n_ref, w_ref, cwt_ref, gq_ref, gk_ref, ones_ref, e1_ref, e2_ref,
         ucb_ref, u_ref, q_ref, k_ref, v_ref) = refs
        conv_refs = (n_ref, w_ref, cwt_ref, e1_ref, e2_ref, ucb_ref, u_ref)
    _conv_kernel(*conv_refs, carry_rows=carry_rows, seg=seg, tiles_per_seg=tiles_per_seg, d_conv=d_conv, cw=cw)
    _qkv_kernel(n_ref, w_ref, gq_ref, gk_ref, ones_ref, q_ref, k_ref, v_ref, cw=cw, col0=3 * d_conv)


def _inproj(n, w_in, conv_w, gq, gk, ones, *, tm, name, batch=None, seg=None, edges=(), casts=()):
    M, D = n.shape
    d_conv = conv_w.shape[1]
    cw = ones.shape[0]
    dq, dk = N_HEADS * HEAD_DIM, N_KV * HEAD_DIM
    wcols = 3 * d_conv + dq + 2 * dk
    carry_rows = batch is not None
    grid = (M // tm,)
    tiles_per_seg = (M // batch) // tm if carry_rows else None
    c_in, c_out, c_shape, c_bytes = _cast_specs(casts, grid)
    const = lambda i: (0, 0)
    row = lambda i: (i, 0)
    in_specs = [
        pl.BlockSpec((tm, D), row),
        pl.BlockSpec((D, wcols), const, pipeline_mode=pl.Buffered(1)),
        pl.BlockSpec((3, d_conv), const),
        pl.BlockSpec((1, dq), const),
        pl.BlockSpec((1, dk), const),
        pl.BlockSpec((cw, cw), const),
    ] + [pl.BlockSpec((tm, d_conv), row) for _ in edges]
    if carry_rows:
        u_spec = pl.BlockSpec((1, 8, d_conv), lambda i: (i // tiles_per_seg, 0, 0))
        u_shape = jax.ShapeDtypeStruct((batch, 8, d_conv), _F32)
        scratch = [pltpu.VMEM((8, d_conv), _F32)]
    else:
        u_spec = pl.BlockSpec((tm, d_conv), row)
        u_shape = jax.ShapeDtypeStruct((M, d_conv), _F32)
        scratch = []
    out_specs = [pl.BlockSpec((tm, d_conv), row), u_spec,
                 pl.BlockSpec((tm, dq), row), pl.BlockSpec((tm, dk), row), pl.BlockSpec((tm, dk), row)]
    out_shape = [jax.ShapeDtypeStruct((M, d_conv), _BF), u_shape, jax.ShapeDtypeStruct((M, dq), _BF),
                 jax.ShapeDtypeStruct((M, dk), _F32), jax.ShapeDtypeStruct((M, dk), _F32)]
    blocks = [_nbytes((tm, D), _BF), _nbytes((D, wcols), _BF) // 2, (1 + len(edges)) * _nbytes((tm, d_conv), _F32),
              _nbytes((tm, d_conv + dq), _BF), 2 * _nbytes((tm, dk), _F32), c_bytes]
    body = functools.partial(_inproj_kernel, carry_rows=carry_rows, seg=seg, tiles_per_seg=tiles_per_seg,
                             d_conv=d_conv, cw=cw)
    return pl.pallas_call(
        _with_casts(body, len(in_specs), len(out_specs), len(casts)),
        grid=grid,
        in_specs=in_specs + c_in,
        out_specs=out_specs + c_out,
        out_shape=out_shape + c_shape,
        scratch_shapes=scratch,
        compiler_params=pltpu.CompilerParams(
            dimension_semantics=("arbitrary",),
            vmem_limit_bytes=_vmem_limit(blocks),
        ),
        name=name,
    )(n, w_in, conv_w, gq, gk, ones, *edges, *_cast_args(casts))


def _bias_kernel(tab_ref, bkt_ref, out_ref, *, mask_step):
    bkt = bkt_ref[...]
    key = lax.broadcasted_iota(jnp.int32, bkt.shape, 1)
    for h in range(out_ref.shape[1]):
        acc = jnp.zeros(bkt.shape, _F32)
        for b in range(tab_ref.shape[0]):
            acc = jnp.where(bkt == b, tab_ref[b, h] * LOG2E, acc)
        for v in range(out_ref.shape[0]):
            out_ref[v, h] = jnp.where(key < v * mask_step, -jnp.inf, acc)


def _rel_bias(table, bucket, n_variants, mask_step, name):
    nq, nk = bucket.shape
    nh = table.shape[1]
    out = pl.pallas_call(
        functools.partial(_bias_kernel, mask_step=mask_step),
        in_specs=[pl.BlockSpec(memory_space=pltpu.SMEM), pl.BlockSpec((nq, nk), lambda: (0, 0))],
        out_specs=pl.BlockSpec((n_variants, nh, nq, nk), lambda: (0, 0, 0, 0)),
        out_shape=jax.ShapeDtypeStruct((n_variants, nh, nq, nk), _F32),
        name=name,
    )(table, bucket)
    return out.transpose(0, 3, 1, 2).reshape(n_variants, nk, nh * nq)


def _t5_bucket(rel):
    nb = N_BUCKETS // 2
    max_exact = nb // 2
    ret = jnp.where(rel > 0, nb, 0)
    n = jnp.abs(rel)
    nf = jnp.maximum(n, 1).astype(jnp.float32)
    large = max_exact + (jnp.log(nf / max_exact) / math.log(MAX_DIST / max_exact) * (nb - max_exact)).astype(jnp.int32)
    large = jnp.minimum(large, nb - 1)
    return ret + jnp.where(n < max_exact, n, large)


def _bucket_index(n_q, n_k):
    i = jnp.arange(n_q)[:, None]
    j = jnp.arange(n_k)[None, :]
    return _t5_bucket(j - WINDOW - i).astype(jnp.int32)


_NT = (((1,), (1,)), ((), ()))
_TN = (((0,), (0,)), ((), ()))


def _sink_softmax_keys_major(lg, sink_t):
    sink2 = sink_t * LOG2E
    m = jnp.maximum(jnp.max(lg, axis=0, keepdims=True), sink2)
    e = jnp.exp2(lg - m)
    den = jnp.sum(e, axis=0, keepdims=True) + jnp.exp2(sink2 - m)
    return e, 1.0 / den


def _attn_chunk_keys_major(qc, kw, vw, bias_t, sink_t):
    nq = qc.shape[0]
    gq = GROUP * nq
    logits = []
    for kv in range(N_KV):
        qg = jnp.concatenate(
            [qc[:, (kv * GROUP + g) * HEAD_DIM:(kv * GROUP + g + 1) * HEAD_DIM] for g in range(GROUP)], axis=0)
        kk = kw[:, kv * HEAD_DIM:(kv + 1) * HEAD_DIM]
        logits.append(lax.dot_general(kk, qg, _NT, preferred_element_type=_F32))
    e, inv = _sink_softmax_keys_major(jnp.concatenate(logits, axis=1) + bias_t, sink_t)
    e = e.astype(_BF)
    outs = []
    for kv in range(N_KV):
        vv = vw[:, kv * HEAD_DIM:(kv + 1) * HEAD_DIM]
        cols = slice(kv * gq, (kv + 1) * gq)
        outs.append(lax.dot_general(vv, e[:, cols], _TN, preferred_element_type=_F32) * inv[:, cols])
    return jnp.concatenate(outs, axis=0).T


def _attn_prompt_kernel(q_ref, kp_ref, kc_ref, vp_ref, vc_ref, bias_ref, sink_ref, o_ref, kwin, vwin,
                        *, tiles_per_batch):
    ta = q_ref.shape[0]
    nk = WINDOW + CHUNK
    dkv = N_KV * HEAD_DIM
    kwin[0:WINDOW, :] = kp_ref[...].astype(_BF)
    kwin[WINDOW:, :] = kc_ref[...].astype(_BF)
    vwin[0:WINDOW, :] = vp_ref[...].astype(_BF)
    vwin[WINDOW:, :] = vc_ref[...].astype(_BF)
    tile_pos = (pl.program_id(0) % tiles_per_batch) * ta

    def chunk(c, carry):
        r = pl.multiple_of(c * CHUNK, CHUNK)
        n_before = jnp.maximum(0, (WINDOW - (tile_pos + r)) // CHUNK)
        o = _attn_chunk_keys_major(q_ref[pl.ds(r, CHUNK), :], kwin[pl.ds(r, nk), :], vwin[pl.ds(r, nk), :],
                                   bias_ref[n_before], sink_ref[...])
        for g in range(GROUP):
            o_ref[pl.ds(r, CHUNK), g * dkv:(g + 1) * dkv] = o[g * CHUNK:(g + 1) * CHUNK].astype(_BF)
        return carry

    lax.fori_loop(0, ta // CHUNK, chunk, 0, unroll=8)


def _attn_prompt(q, k, v, bias, sink, *, batch, ta, name, casts=()):
    M, dq = q.shape
    dk = k.shape[1]
    tiles_per_batch = (M // batch) // ta
    wb = ta // WINDOW
    nk = WINDOW + CHUNK

    def prev(i):
        return (jnp.where(i % tiles_per_batch == 0, i * wb, i * wb - 1), 0)

    row = lambda i: (i, 0)
    grid = (M // ta,)
    c_in, c_out, c_shape, c_bytes = _cast_specs(casts, grid)
    blocks = [_nbytes((ta, dq), _BF) * 2, 2 * _nbytes((WINDOW + ta, dk), _F32),
              _nbytes(bias.shape, _F32), _nbytes((8, N_HEADS * CHUNK), _F32), c_bytes]
    return pl.pallas_call(
        _with_casts(functools.partial(_attn_prompt_kernel, tiles_per_batch=tiles_per_batch), 7, 1, len(casts)),
        grid=grid,
        in_specs=[
            pl.BlockSpec((ta, dq), row),
            pl.BlockSpec((WINDOW, dk), prev),
            pl.BlockSpec((ta, dk), row),
            pl.BlockSpec((WINDOW, dk), prev),
            pl.BlockSpec((ta, dk), row),
            pl.BlockSpec(bias.shape, lambda i: (0, 0, 0)),
            pl.BlockSpec((1, N_HEADS * CHUNK), lambda i: (0, 0)),
        ] + c_in,
        out_specs=[pl.BlockSpec((ta, dq), row)] + c_out,
        out_shape=[jax.ShapeDtypeStruct((M, dq), _BF)] + c_shape,
        scratch_shapes=[pltpu.VMEM((WINDOW + ta, dk), _BF), pltpu.VMEM((WINDOW + ta, dk), _BF)],
        compiler_params=pltpu.CompilerParams(
            dimension_semantics=("arbitrary",),
            vmem_limit_bytes=_vmem_limit(blocks, 2 * _nbytes((WINDOW + ta, dk), _BF)),
        ),
        name=name,
    )(q, k, k, v, v, bias, sink, *_cast_args(casts))


LANES = 128


def _attn_sample_kernel(q_ref, ck_ref, cv_ref, k_ref, v_ref, bias_ref, sink_ref, o_ref, *, nb, s):
    dkv = N_KV * HEAD_DIM
    gs = GROUP * s
    assert LANES == 2 * HEAD_DIM and dkv == 2 * LANES
    for b in range(nb):
        rows = slice(b * s, (b + 1) * s)
        kw = jnp.concatenate([ck_ref[b].astype(_BF), k_ref[rows, :].astype(_BF)], axis=0)
        vw = jnp.concatenate([cv_ref[b].astype(_BF), v_ref[rows, :].astype(_BF)], axis=0)
        qf = q_ref[rows, :].astype(_F32)
        blocks = []
        for kv in range(N_KV):
            for g in range(GROUP):
                h = kv * GROUP + g
                pair = qf[:, (h // 2) * LANES:(h // 2 + 1) * LANES]
                if h % 2 != kv % 2:
                    pair = pltpu.roll(pair, HEAD_DIM, 1)
                half = lax.broadcasted_iota(jnp.int32, pair.shape, 1) // HEAD_DIM
                piece = jnp.where(half == kv % 2, pair, 0.0)
                zero = jnp.zeros_like(piece)
                blocks.append(jnp.concatenate([piece, zero] if kv < 2 else [zero, piece], axis=1))
        q_bd = jnp.concatenate(blocks, axis=0).astype(_BF)
        lg = lax.dot_general(kw, q_bd, _NT, preferred_element_type=_F32) + bias_ref[0]
        e, inv = _sink_softmax_keys_major(lg, sink_ref[...])
        z = lax.dot_general((e * inv).astype(_BF), vw, _TN, preferred_element_type=_F32)
        row_kv = lax.broadcasted_iota(jnp.int32, z.shape, 0) // gs
        col_kv = lax.broadcasted_iota(jnp.int32, z.shape, 1) // HEAD_DIM
        z = jnp.where(row_kv == col_kv, z, 0.0)
        r = z[0:gs]
        for kv in range(1, N_KV):
            r = r + z[kv * gs:(kv + 1) * gs]
        for g in range(GROUP):
            o_ref[rows, g * dkv:(g + 1) * dkv] = r[g * s:(g + 1) * s].astype(_BF)


def _attn_sample(q, k, v, cache_k, cache_v, bias, sink, *, nb, name):
    B, W, dk = cache_k.shape
    M, dq = q.shape
    S = M // B
    row = lambda b: (b, 0)
    return pl.pallas_call(
        functools.partial(_attn_sample_kernel, nb=nb, s=S),
        grid=(B // nb,),
        in_specs=[
            pl.BlockSpec((nb * S, dq), row),
            pl.BlockSpec((nb, W, dk), lambda b: (b, 0, 0)),
            pl.BlockSpec((nb, W, dk), lambda b: (b, 0, 0)),
            pl.BlockSpec((nb * S, dk), row),
            pl.BlockSpec((nb * S, dk), row),
            pl.BlockSpec(bias.shape, lambda b: (0, 0, 0)),
            pl.BlockSpec(sink.shape, lambda b: (0, 0)),
        ],
        out_specs=pl.BlockSpec((nb * S, dq), row),
        out_shape=jax.ShapeDtypeStruct((M, dq), _BF),
        compiler_params=pltpu.CompilerParams(dimension_semantics=("arbitrary",)),
        name=name,
    )(q, cache_k, cache_v, k, v, bias, sink)


def _mix_kernel(*refs, n_chunks):
    n_ref, ucb_ref, o_ref, wco_ref, wao_ref = refs[:5]
    wgc_refs = refs[5:5 + n_chunks]
    wga_refs = refs[5 + n_chunks:5 + 2 * n_chunks]
    mix_ref = refs[5 + 2 * n_chunks]
    n, ucb, o = n_ref[...], ucb_ref[...], o_ref[...]
    cw = mix_ref.shape[1] // n_chunks
    for c in range(n_chunks):
        cols = slice(c * cw, (c + 1) * cw)
        y_conv = _dot(ucb, wco_ref[:, cols])
        y_attn = _dot(o, wao_ref[:, cols])
        g_conv = jax.nn.sigmoid(_dot(n, wgc_refs[c][...]))
        g_attn = jax.nn.sigmoid(_dot(n, wga_refs[c][...]))
        mix_ref[:, cols] = (g_conv * y_conv + g_attn * y_attn).astype(_BF)


def _mix(n, ucb, o, w_conv_out, w_attn_o, w_in, *, gc_col, ga_col, cw, tm, name):
    M, D = n.shape
    dc, da = ucb.shape[1], o.shape[1]
    n_chunks = D // cw
    row = lambda i: (i, 0)
    once = pl.Buffered(1)
    gate_specs = [pl.BlockSpec((D, cw), lambda i, b=(col // cw + c): (0, b), pipeline_mode=once)
                  for col in (gc_col, ga_col) for c in range(n_chunks)]
    blocks = [_nbytes((tm, D + dc + da), _BF), _nbytes((dc + da + 2 * D, D), _BF) // 2, _nbytes((tm, D), _BF)]
    return pl.pallas_call(
        functools.partial(_mix_kernel, n_chunks=n_chunks),
        grid=(M // tm,),
        in_specs=[
            pl.BlockSpec((tm, D), row),
            pl.BlockSpec((tm, dc), row),
            pl.BlockSpec((tm, da), row),
            pl.BlockSpec((dc, D), lambda i: (0, 0), pipeline_mode=once),
            pl.BlockSpec((da, D), lambda i: (0, 0), pipeline_mode=once),
        ] + gate_specs,
        out_specs=pl.BlockSpec((tm, D), row),
        out_shape=jax.ShapeDtypeStruct((M, D), _BF),
        compiler_params=pltpu.CompilerParams(
            dimension_semantics=("arbitrary",),
            vmem_limit_bytes=_vmem_limit(blocks),
        ),
        name=name,
    )(n, ucb, o, w_conv_out, w_attn_o, *([w_in] * (2 * n_chunks)))


OUTPROJ_ROW_CHUNKS = 2


def _outproj_kernel(h_ref, mix_ref, w_ref, g_ref, out_ref, xn_ref):
    rows = h_ref.shape[0] // OUTPROJ_ROW_CHUNKS
    for c in range(OUTPROJ_ROW_CHUNKS):
        sl = slice(c * rows, (c + 1) * rows)
        h = h_ref[sl, :] + _dot(mix_ref[sl, :], w_ref[...])
        out_ref[sl, :] = h
        xn_ref[sl, :] = _rms(h, g_ref[...]).astype(_BF)


def _outproj(h, mix, w_out, g, *, tm, name):
    M, D = h.shape
    row = lambda i: (i, 0)
    blocks = [2 * _nbytes((tm, D), _F32), 2 * _nbytes((tm, D), _BF), _nbytes((D, D), _BF) // 2]
    return pl.pallas_call(
        _outproj_kernel,
        grid=(M // tm,),
        in_specs=[
            pl.BlockSpec((tm, D), row),
            pl.BlockSpec((tm, D), row),
            pl.BlockSpec((D, D), lambda i: (0, 0), pipeline_mode=pl.Buffered(1)),
            pl.BlockSpec((1, D), lambda i: (0, 0)),
        ],
        out_specs=[pl.BlockSpec((tm, D), row), pl.BlockSpec((tm, D), row)],
        out_shape=[jax.ShapeDtypeStruct((M, D), _F32), jax.ShapeDtypeStruct((M, D), _BF)],
        compiler_params=pltpu.CompilerParams(
            dimension_semantics=("arbitrary",),
            vmem_limit_bytes=_vmem_limit(blocks),
        ),
        name=name,
    )(h, mix, w_out, g)


def _ple_kernel(h_ref, pe_ref, g_ref, wp_ref, wg_ref, out_ref):
    h = h_ref[...]
    gate = jax.nn.sigmoid(_dot(_rms(h, g_ref[...]).astype(_BF), wg_ref[...]))
    emb = _dot(pe_ref[...].astype(_BF), wp_ref[...])
    out_ref[...] = h + emb * gate


def _ple(h, pe, g, w_ple, w_gate, *, tm, name):
    M, D = h.shape
    dp = pe.shape[1]
    row = lambda i: (i, 0)
    const = lambda i: (0, 0)
    blocks = [2 * _nbytes((tm, D), _F32), _nbytes((tm, dp), _F32), _nbytes((dp + D, D), _BF)]
    return pl.pallas_call(
        _ple_kernel,
        grid=(M // tm,),
        in_specs=[
            pl.BlockSpec((tm, D), row),
            pl.BlockSpec((tm, dp), row),
            pl.BlockSpec((1, D), const),
            pl.BlockSpec((dp, D), const),
            pl.BlockSpec((D, D), const),
        ],
        out_specs=pl.BlockSpec((tm, D), row),
        out_shape=jax.ShapeDtypeStruct((M, D), _F32),
        compiler_params=pltpu.CompilerParams(
            dimension_semantics=("arbitrary",),
            vmem_limit_bytes=_vmem_limit(blocks),
        ),
        name=name,
    )(h, pe, g, w_ple, w_gate)


def _row_tile(m, want):
    return want if m % want == 0 else m


def kernel(x_prompt, x_sample, p_prompt, p_sample, state_conv, cache_k, cache_v, rel_table, ffn1_norm, ffn1_wg, ffn1_wu, ffn1_wd, mix_norm, w_in, conv_w, q_norm, k_norm, attn_sink, w_conv_out, w_attn_o, w_out, ffn2_norm, ffn2_wg, ffn2_wu, ffn2_wd, ple_norm, w_ple, w_ple_gate):
    B, T, D = x_prompt.shape
    Bs, S, _ = x_sample.shape
    depth = ffn1_wg.shape[0]
    d_conv = conv_w.shape[2]
    dq, dk = N_HEADS * HEAD_DIM, N_KV * HEAD_DIM
    qkv_col = 3 * d_conv
    gc_col = qkv_col + dq + 2 * dk
    ga_col = gc_col + D
    Mp, Ms = B * T, Bs * S
    tm_p = _row_tile(Mp, 512)
    tm_s = _row_tile(Ms, 512)
    tf = 1024
    tn = 512
    cw = 256

    nk_p, nk_s = WINDOW + CHUNK, WINDOW + S
    bias_p = _rel_bias(rel_table, _bucket_index(CHUNK, nk_p), WINDOW // CHUNK + 1, CHUNK, "rel_bias_prompt")
    bias_s = _rel_bias(rel_table, _bucket_index(S, nk_s), 1, 0, "rel_bias_sample")
    lane = jnp.arange(cw) // HEAD_DIM
    ones = (lane[:, None] == lane[None, :]).astype(_BF)

    yp = x_prompt.reshape(Mp, D)
    ys = x_sample.reshape(Ms, D)
    conv_p, k_p, v_p, conv_s, k_s, v_s = [], [], [], [], [], []
    head_to_group_major = lambda h: (h % GROUP) * N_KV + h // GROUP
    for l in range(depth):
        vec = lambda g: g[l].reshape(1, -1)
        h1_s, n_s, f1_wg, f1_wu, f1_wd = _ffn(ys, vec(ffn1_norm), ffn1_wg[l], ffn1_wu[l], ffn1_wd[l],
                                              vec(mix_norm), tm=tm_s, tf=tf // 2, name="ffn1_sample")
        f1 = (vec(ffn1_norm), f1_wg, f1_wu, f1_wd)
        gq = jnp.tile(q_norm[l], N_HEADS).reshape(1, dq)
        gk = jnp.tile(k_norm[l], N_KV).reshape(1, dk)
        sink_p = jnp.repeat(attn_sink[l], CHUNK).reshape(1, N_HEADS * CHUNK)
        sink_s = jnp.repeat(attn_sink[l], S).reshape(1, N_HEADS * S)

        h1, n, w_in_l = _ffn(yp, *f1, vec(mix_norm), tm=tm_p, tf=tf, name="ffn1_prompt", casts=(w_in[l],))
        ucb, ulast, q, k, v, f2_wg, f2_wu, w_co, w_ao_p, w_o = _inproj(
            n, w_in_l, conv_w[l], gq, gk, ones, tm=tm_p, batch=B, name="inproj_prompt",
            casts=(ffn2_wg[l], ffn2_wu[l], w_conv_out[l], (w_attn_o[l], HEAD_DIM, head_to_group_major), w_out[l]))
        o, f2_wd, w_p, w_pg = _attn_prompt(q, k, v, bias_p, sink_p, batch=B, ta=2 * tm_p, name="attn_prompt",
                                           casts=(ffn2_wd[l], w_ple[l], w_ple_gate[l]))
        mix = _mix(n, ucb, o, w_co, w_ao_p, w_in_l, gc_col=gc_col, ga_col=ga_col, cw=tn, tm=tm_p, name="mix_prompt")
        h2, xn2 = _outproj(h1, mix, w_o, vec(ffn2_norm), tm=tm_p, name="outproj_prompt")
        f2 = (f2_wg, f2_wu, f2_wd)
        h3, = _ffn(h2, xn2, *f2, None, tm=tm_p, tf=tf, name="ffn2_prompt")
        yp = _ple(h3, p_prompt[l].reshape(Mp, -1), vec(ple_norm), w_p, w_pg, tm=tm_p, name="ple_prompt")
        conv_p.append(ulast[:, 6:8])
        k_p.append(k.reshape(B, T, dk)[:, T - WINDOW:].reshape(B, WINDOW, N_KV, HEAD_DIM))
        v_p.append(v.reshape(B, T, dk)[:, T - WINDOW:].reshape(B, WINDOW, N_KV, HEAD_DIM))

        st = state_conv[l]
        zeros = jnp.zeros((Bs, S - 2, d_conv), _F32)
        e2 = jnp.concatenate([st, zeros], axis=1).reshape(Ms, d_conv)
        e1 = jnp.concatenate([st[:, 1:], zeros, zeros[:, :1]], axis=1).reshape(Ms, d_conv)
        h1, n = h1_s, n_s
        ucb, u, q, k, v = _inproj(n, w_in_l, conv_w[l], gq, gk, ones, tm=tm_s, seg=S, edges=(e1, e2),
                                  name="inproj_sample")
        o = _attn_sample(q, k, v, cache_k[l].reshape(Bs, WINDOW, dk), cache_v[l].reshape(Bs, WINDOW, dk),
                         bias_s, sink_s, nb=4, name="attn_sample")
        mix = _mix(n, ucb, o, w_co, w_ao_p, w_in_l, gc_col=gc_col, ga_col=ga_col, cw=tn, tm=tm_s, name="mix_sample")
        h2, xn2 = _outproj(h1, mix, w_o, vec(ffn2_norm), tm=tm_s, name="outproj_sample")
        h3, = _ffn(h2, xn2, *f2, None, tm=tm_s, tf=tf, name="ffn2_sample")
        ys = _ple(h3, p_sample[l].reshape(Ms, -1), vec(ple_norm), w_p, w_pg, tm=tm_s, name="ple_sample")
        conv_s.append(u.reshape(Bs, S, d_conv)[:, S - 2:])
        k_s.append(k.reshape(Bs, S, N_KV, HEAD_DIM))
        v_s.append(v.reshape(Bs, S, N_KV, HEAD_DIM))

    return (yp.reshape(B, T, D), ys.reshape(Bs, S, D), jnp.stack(conv_p), jnp.stack(k_p), jnp.stack(v_p),
            jnp.stack(conv_s), jnp.stack(k_s), jnp.stack(v_s))
```

```python
import functools
import math

import jax
import jax.numpy as jnp
from jax import lax
from jax.experimental import pallas as pl
from jax.experimental.pallas import tpu as pltpu

_BF = jnp.bfloat16
_F32 = jnp.float32

CHUNK = 64
WINDOW = 128
N_HEADS = 16
N_KV = 4
GROUP = N_HEADS // N_KV
HEAD_DIM = 64
N_BUCKETS = 32
MAX_DIST = 128
EPS = 1e-6
LOG2E = math.log2(math.e)
QK_SCALE = HEAD_DIM ** -0.5 * LOG2E

V7X_VMEM_BYTES = 64 * 1024 * 1024
VMEM_INTERNAL_BYTES = 12 * 1024 * 1024


def _vmem_limit(block_bytes, scratch_bytes=0):
    need = 2 * sum(block_bytes) + scratch_bytes + VMEM_INTERNAL_BYTES
    return int(min(need, V7X_VMEM_BYTES - 4 * 1024 * 1024))


def _nbytes(shape, dtype):
    return math.prod(shape) * jnp.dtype(dtype).itemsize


def _dot(a, b):
    return jnp.dot(a, b, preferred_element_type=_F32)


def _rms(x, g):
    ms = jnp.mean(x * x, axis=-1, keepdims=True)
    return x * lax.rsqrt(ms + EPS) * g


BF16_TILE_ROWS = 16


def _cast_specs(arrays, grid):
    steps = math.prod(grid)

    def flat(*ids):
        s = 0
        for k, g in zip(ids, grid):
            s = s * g + k
        return s

    in_specs, out_specs, out_shape, nbytes = [], [], [], 0
    for a in arrays:
        a, rb, out_block = a if isinstance(a, tuple) else (a, None, lambda b: b)
        rows, cols = a.shape
        rb = rb or max(BF16_TILE_ROWS, rows // steps)
        assert rows % rb == 0 and steps % (rows // rb) == 0, (a.shape, steps)
        rep = steps // (rows // rb)
        idx = lambda *ids, rep=rep: (flat(*ids) // rep, 0)
        out_idx = lambda *ids, rep=rep, out_block=out_block: (out_block(flat(*ids) // rep), 0)
        in_specs.append(pl.BlockSpec((rb, cols), idx))
        out_specs.append(pl.BlockSpec((rb, cols), out_idx))
        out_shape.append(jax.ShapeDtypeStruct((rows, cols), _BF))
        nbytes += _nbytes((rb, cols), _F32) + _nbytes((rb, cols), _BF)
    return in_specs, out_specs, out_shape, nbytes


def _cast_args(arrays):
    return [a[0] if isinstance(a, tuple) else a for a in arrays]


def _with_casts(body, n_in, n_out, n_cast):
    def kernel(*refs):
        outs_at = n_in + n_cast
        scratch_at = outs_at + n_out + n_cast
        for src, dst in zip(refs[n_in:outs_at], refs[outs_at + n_out:scratch_at]):
            dst[...] = src[...].astype(_BF)
        body(*refs[:n_in], *refs[outs_at:outs_at + n_out], *refs[scratch_at:])
    return kernel


def _ffn_kernel(*refs, emit_norm, keep_weights, normed_input):
    x_ref, g_ref, wg_ref, wu_ref, wd_ref = refs[:5]
    rest = list(refs[5:])
    g2_ref = rest.pop(0) if emit_norm else None
    out_ref = rest.pop(0)
    n_ref = rest.pop(0) if emit_norm else None
    weight_copies = [rest.pop(0) for _ in range(3)] if keep_weights else []
    j = pl.program_id(1)
    last = pl.num_programs(1) - 1
    if normed_input:
        xn_ref = g_ref
    else:
        xn_ref, = rest

        @pl.when(j == 0)
        def _():
            xn_ref[...] = _rms(x_ref[...], g_ref[...]).astype(_BF)

    wg, wu, wd = wg_ref[...], wu_ref[...], wd_ref[...]
    if keep_weights:
        wg, wu, wd = wg.astype(_BF), wu.astype(_BF), wd.astype(_BF)
        for dst, w in zip(weight_copies, (wg, wu, wd)):
            dst[...] = w
    xn = xn_ref[...]
    gate = _dot(xn, wg)
    up = _dot(xn, wu)
    hid = (jax.nn.silu(gate) * up * 0.5).astype(_BF)
    out_ref[...] = jnp.where(j == 0, x_ref[...], out_ref[...]) + _dot(hid, wd)

    if emit_norm:
        @pl.when(j == last)
        def _():
            n_ref[...] = _rms(out_ref[...], g2_ref[...]).astype(_BF)


def _ffn(x, g, wg, wu, wd, g2, *, tm, tf, name, casts=()):
    M, D = x.shape
    F = wg.shape[1]
    emit_norm = g2 is not None
    keep_weights = wg.dtype == _F32
    normed_input = g.shape[0] == M
    grid = (M // tm, F // tf)
    assert not keep_weights or grid[0] == 1
    row = lambda i, j: (i, 0)
    w_specs = [
        pl.BlockSpec((D, tf), lambda i, j: (0, j)),
        pl.BlockSpec((D, tf), lambda i, j: (0, j)),
        pl.BlockSpec((tf, D), lambda i, j: (j, 0)),
    ]
    g_spec = pl.BlockSpec((tm, D), row) if normed_input else pl.BlockSpec((1, D), lambda i, j: (0, 0))
    in_specs = [pl.BlockSpec((tm, D), row), g_spec] + w_specs
    args = [x, g, wg, wu, wd]
    out_shape = [jax.ShapeDtypeStruct((M, D), _F32)]
    out_specs = [pl.BlockSpec((tm, D), row)]
    blocks = [_nbytes((tm, D), _F32) * 2, 3 * _nbytes((D, tf), wg.dtype)]
    if emit_norm:
        in_specs.append(pl.BlockSpec((1, D), lambda i, j: (0, 0)))
        args.append(g2)
        out_shape.append(jax.ShapeDtypeStruct((M, D), _BF))
        out_specs.append(pl.BlockSpec((tm, D), row))
        blocks.append(_nbytes((tm, D), _BF))
    if keep_weights:
        out_shape += [jax.ShapeDtypeStruct(w.shape, _BF) for w in (wg, wu, wd)]
        out_specs += w_specs
        blocks.append(3 * _nbytes((D, tf), _BF))
    c_in, c_out, c_shape, c_bytes = _cast_specs(casts, grid)
    blocks.append(c_bytes)
    body = functools.partial(_ffn_kernel, emit_norm=emit_norm, keep_weights=keep_weights,
                             normed_input=normed_input)
    res = pl.pallas_call(
        _with_casts(body, len(in_specs), len(out_specs), len(casts)),
        grid=grid,
        in_specs=in_specs + c_in,
        out_specs=out_specs + c_out,
        out_shape=out_shape + c_shape,
        scratch_shapes=[] if normed_input else [pltpu.VMEM((tm, D), _BF)],
        compiler_params=pltpu.CompilerParams(
            dimension_semantics=("arbitrary", "arbitrary"),
            vmem_limit_bytes=_vmem_limit(blocks, _nbytes((tm, D), _BF)),
        ),
        name=name,
    )(*args, *_cast_args(casts))
    return res


def _conv_kernel(*refs, carry_rows, seg, tiles_per_seg, d_conv, cw):
    if carry_rows:
        n_ref, w_ref, cwt_ref, ucb_ref, ulast_ref, carry_ref = refs
    else:
        n_ref, w_ref, cwt_ref, e1_ref, e2_ref, ucb_ref, u_ref = refs
    tm = n_ref.shape[0]
    n = n_ref[...]
    if carry_rows:
        @pl.when(pl.program_id(0) % tiles_per_seg == 0)
        def _():
            carry_ref[...] = jnp.zeros_like(carry_ref)

    for c in range(d_conv // cw):
        lo = c * cw
        cb = _dot(n, w_ref[:, lo:lo + cw])
        cc = _dot(n, w_ref[:, d_conv + lo:d_conv + lo + cw])
        cv = _dot(n, w_ref[:, 2 * d_conv + lo:2 * d_conv + lo + cw])
        u = cc * cv
        r1 = pltpu.roll(u, 1, 0)
        r2 = pltpu.roll(u, 2, 0)
        if carry_rows:
            prev = carry_ref[:, lo:lo + cw]
            rows8 = lax.broadcasted_iota(jnp.int32, (8, cw), 0)
            h1 = jnp.where(rows8 < 1, pltpu.roll(prev, 1, 0), r1[:8])
            h2 = jnp.where(rows8 < 2, pltpu.roll(prev, 2, 0), r2[:8])
            r1 = jnp.concatenate([h1, r1[8:]], axis=0)
            r2 = jnp.concatenate([h2, r2[8:]], axis=0)
            carry_ref[:, lo:lo + cw] = u[tm - 8:]
            ulast_ref[0, :, lo:lo + cw] = u[tm - 8:]
        else:
            pos = lax.broadcasted_iota(jnp.int32, (tm, cw), 0) % seg
            r1 = jnp.where(pos < 1, e1_ref[:, lo:lo + cw], r1)
            r2 = jnp.where(pos < 2, e2_ref[:, lo:lo + cw], r2)
            u_ref[:, lo:lo + cw] = u
        w0 = cwt_ref[0:1, lo:lo + cw]
        w1 = cwt_ref[1:2, lo:lo + cw]
        w2 = cwt_ref[2:3, lo:lo + cw]
        conv = w0 * r2 + w1 * r1 + w2 * u
        ucb_ref[:, lo:lo + cw] = (cb * conv).astype(_BF)


def _qkv_kernel(n_ref, w_ref, gq_ref, gk_ref, ones_ref, q_ref, k_ref, v_ref, *, cw, col0):
    ones = ones_ref[...]
    dq = q_ref.shape[1]
    dk = k_ref.shape[1]
    p = _dot(n_ref[...], w_ref[:, col0:col0 + dq + 2 * dk])
    x = p[:, :dq + dk]
    sq = x * x
    hi = sq.astype(_BF)
    lo = (sq - hi.astype(_F32)).astype(_BF)
    ss = jnp.concatenate(
        [_dot(hi[:, c * cw:(c + 1) * cw], ones) + _dot(lo[:, c * cw:(c + 1) * cw], ones)
         for c in range((dq + dk) // cw)], axis=1)
    xr = x * lax.rsqrt(ss * (1.0 / HEAD_DIM) + EPS)
    q_ref[...] = (xr[:, :dq] * gq_ref[...] * QK_SCALE).astype(_BF)
    k_ref[...] = xr[:, dq:] * gk_ref[...]
    v_ref[...] = p[:, dq + dk:]


def _inproj_kernel(*refs, carry_rows, seg, tiles_per_seg, d_conv, cw):
    if carry_rows:
        (n_ref, w_ref, cwt_ref, gq_ref, gk_ref, ones_ref,
         ucb_ref, ulast_ref, q_ref, k_ref, v_ref, carry_ref) = refs
        conv_refs = (n_ref, w_ref, cwt_ref, ucb_ref, ulast_ref, carry_ref)
    else:
        (n_ref, w_ref, cwt_ref, gq_ref, gk_ref, ones_ref, e1_ref, e2_ref,
         ucb_ref, u_ref, q_ref, k_ref, v_ref) = refs
        conv_refs = (n_ref, w_ref, cwt_ref, e1_ref, e2_ref, ucb_ref, u_ref)
    _conv_kernel(*conv_refs, carry_rows=carry_rows, seg=seg, tiles_per_seg=tiles_per_seg, d_conv=d_conv, cw=cw)
    _qkv_kernel(n_ref, w_ref, gq_ref, gk_ref, ones_ref, q_ref, k_ref, v_ref, cw=cw, col0=3 * d_conv)


def _inproj(n, w_in, conv_w, gq, gk, ones, *, tm, name, batch=None, seg=None, edges=(), casts=()):
    M, D = n.shape
    d_conv = conv_w.shape[1]
    cw = ones.shape[0]
    dq, dk = N_HEADS * HEAD_DIM, N_KV * HEAD_DIM
    wcols = 3 * d_conv + dq + 2 * dk
    carry_rows = batch is not None
    grid = (M // tm,)
    tiles_per_seg = (M // batch) // tm if carry_rows else None
    c_in, c_out, c_shape, c_bytes = _cast_specs(casts, grid)
    const = lambda i: (0, 0)
    row = lambda i: (i, 0)
    in_specs = [
        pl.BlockSpec((tm, D), row),
        pl.BlockSpec((D, wcols), const, pipeline_mode=pl.Buffered(1)),
        pl.BlockSpec((3, d_conv), const),
        pl.BlockSpec((1, dq), const),
        pl.BlockSpec((1, dk), const),
        pl.BlockSpec((cw, cw), const),
    ] + [pl.BlockSpec((tm, d_conv), row) for _ in edges]
    if carry_rows:
        u_spec = pl.BlockSpec((1, 8, d_conv), lambda i: (i // tiles_per_seg, 0, 0))
        u_shape = jax.ShapeDtypeStruct((batch, 8, d_conv), _F32)
        scratch = [pltpu.VMEM((8, d_conv), _F32)]
    else:
        u_spec = pl.BlockSpec((tm, d_conv), row)
        u_shape = jax.ShapeDtypeStruct((M, d_conv), _F32)
        scratch = []
    out_specs = [pl.BlockSpec((tm, d_conv), row), u_spec,
                 pl.BlockSpec((tm, dq), row), pl.BlockSpec((tm, dk), row), pl.BlockSpec((tm, dk), row)]
    out_shape = [jax.ShapeDtypeStruct((M, d_conv), _BF), u_shape, jax.ShapeDtypeStruct((M, dq), _BF),
                 jax.ShapeDtypeStruct((M, dk), _F32), jax.ShapeDtypeStruct((M, dk), _F32)]
    blocks = [_nbytes((tm, D), _BF), _nbytes((D, wcols), _BF) // 2, (1 + len(edges)) * _nbytes((tm, d_conv), _F32),
              _nbytes((tm, d_conv + dq), _BF), 2 * _nbytes((tm, dk), _F32), c_bytes]
    body = functools.partial(_inproj_kernel, carry_rows=carry_rows, seg=seg, tiles_per_seg=tiles_per_seg,
                             d_conv=d_conv, cw=cw)
    return pl.pallas_call(
        _with_casts(body, len(in_specs), len(out_specs), len(casts)),
        grid=grid,
        in_specs=in_specs + c_in,
        out_specs=out_specs + c_out,
        out_shape=out_shape + c_shape,
        scratch_shapes=scratch,
        compiler_params=pltpu.CompilerParams(
            dimension_semantics=("arbitrary",),
            vmem_limit_bytes=_vmem_limit(blocks),
        ),
        name=name,
    )(n, w_in, conv_w, gq, gk, ones, *edges, *_cast_args(casts))


def _bias_kernel(tab_ref, bkt_ref, out_ref, *, mask_step):
    bkt = bkt_ref[...]
    key = lax.broadcasted_iota(jnp.int32, bkt.shape, 1)
    for h in range(out_ref.shape[1]):
        acc = jnp.zeros(bkt.shape, _F32)
        for b in range(tab_ref.shape[0]):
            acc = jnp.where(bkt == b, tab_ref[b, h] * LOG2E, acc)
        for v in range(out_ref.shape[0]):
            out_ref[v, h] = jnp.where(key < v * mask_step, -jnp.inf, acc)


def _rel_bias(table, bucket, n_variants, mask_step, name):
    nq, nk = bucket.shape
    nh = table.shape[1]
    out = pl.pallas_call(
        functools.partial(_bias_kernel, mask_step=mask_step),
        in_specs=[pl.BlockSpec(memory_space=pltpu.SMEM), pl.BlockSpec((nq, nk), lambda: (0, 0))],
        out_specs=pl.BlockSpec((n_variants, nh, nq, nk), lambda: (0, 0, 0, 0)),
        out_shape=jax.ShapeDtypeStruct((n_variants, nh, nq, nk), _F32),
        name=name,
    )(table, bucket)
    return out.transpose(0, 3, 1, 2).reshape(n_variants, nk, nh * nq)


def _t5_bucket(rel):
    nb = N_BUCKETS // 2
    max_exact = nb // 2
    ret = jnp.where(rel > 0, nb, 0)
    n = jnp.abs(rel)
    nf = jnp.maximum(n, 1).astype(jnp.float32)
    large = max_exact + (jnp.log(nf / max_exact) / math.log(MAX_DIST / max_exact) * (nb - max_exact)).astype(jnp.int32)
    large = jnp.minimum(large, nb - 1)
    return ret + jnp.where(n < max_exact, n, large)


def _bucket_index(n_q, n_k):
    i = jnp.arange(n_q)[:, None]
    j = jnp.arange(n_k)[None, :]
    return _t5_bucket(j - WINDOW - i).astype(jnp.int32)


_NT = (((1,), (1,)), ((), ()))
_TN = (((0,), (0,)), ((), ()))


def _sink_softmax_keys_major(lg, sink_t):
    sink2 = sink_t * LOG2E
    m = jnp.maximum(jnp.max(lg, axis=0, keepdims=True), sink2)
    e = jnp.exp2(lg - m)
    den = jnp.sum(e, axis=0, keepdims=True) + jnp.exp2(sink2 - m)
    return e, 1.0 / den


def _attn_chunk_logits(qc, kw, bias_t):
    logits = []
    for kv in range(N_KV):
        qg = jnp.concatenate(
            [qc[:, (kv * GROUP + g) * HEAD_DIM:(kv * GROUP + g + 1) * HEAD_DIM] for g in range(GROUP)], axis=0)
        kk = kw[:, kv * HEAD_DIM:(kv + 1) * HEAD_DIM]
        logits.append(lax.dot_general(kk, qg, _NT, preferred_element_type=_F32))
    return jnp.concatenate(logits, axis=1) + bias_t


def _attn_chunk_output(lg, vw, sink_t):
    gq = lg.shape[1] // N_KV
    e, inv = _sink_softmax_keys_major(lg, sink_t)
    e = e.astype(_BF)
    outs = []
    for kv in range(N_KV):
        vv = vw[:, kv * HEAD_DIM:(kv + 1) * HEAD_DIM]
        cols = slice(kv * gq, (kv + 1) * gq)
        outs.append(lax.dot_general(vv, e[:, cols], _TN, preferred_element_type=_F32) * inv[:, cols])
    return jnp.concatenate(outs, axis=0).T


ATTN_CHUNKS_PER_BLOCK = 8


def _attn_prompt_kernel(*refs, tiles_per_batch, n_gate_blocks):
    q_ref, kp_ref, kc_ref, vp_ref, vc_ref, bias_ref, sink_ref, n_ref = refs[:8]
    wgate_refs = refs[8:8 + n_gate_blocks]
    o_ref, gate_ref, kwin, vwin = refs[8 + n_gate_blocks:]
    ta = q_ref.shape[0]
    nk = WINDOW + CHUNK
    dkv = N_KV * HEAD_DIM
    rows = ATTN_CHUNKS_PER_BLOCK * CHUNK
    gate_cols = gate_ref.shape[1] // n_gate_blocks
    kwin[0:WINDOW, :] = kp_ref[...].astype(_BF)
    kwin[WINDOW:, :] = kc_ref[...].astype(_BF)
    vwin[0:WINDOW, :] = vp_ref[...].astype(_BF)
    vwin[WINDOW:, :] = vc_ref[...].astype(_BF)
    tile_pos = (pl.program_id(0) % tiles_per_batch) * ta

    def logits(r):
        n_before = jnp.maximum(0, (WINDOW - (tile_pos + r)) // CHUNK)
        return _attn_chunk_logits(q_ref[pl.ds(r, CHUNK), :], kwin[pl.ds(r, nk), :], bias_ref[n_before])

    def finish(r, lg):
        o = _attn_chunk_output(lg, vwin[pl.ds(r, nk), :], sink_ref[...])
        for g in range(GROUP):
            o_ref[pl.ds(r, CHUNK), g * dkv:(g + 1) * dkv] = o[g * CHUNK:(g + 1) * CHUNK].astype(_BF)

    def block(b, carry):
        r0 = pl.multiple_of(b * rows, rows)
        n = n_ref[pl.ds(r0, rows), :]
        chunk_rows = [pl.multiple_of(r0 + c * CHUNK, CHUNK) for c in range(ATTN_CHUNKS_PER_BLOCK)]
        lg = logits(chunk_rows[0])
        for c in range(max(ATTN_CHUNKS_PER_BLOCK, n_gate_blocks)):
            lg_next = logits(chunk_rows[c + 1]) if c + 1 < ATTN_CHUNKS_PER_BLOCK else None
            if c < n_gate_blocks:
                gate = jax.nn.sigmoid(_dot(n, wgate_refs[c][...]))
                gate_ref[pl.ds(r0, rows), c * gate_cols:(c + 1) * gate_cols] = gate.astype(_BF)
            if c < ATTN_CHUNKS_PER_BLOCK:
                finish(chunk_rows[c], lg)
            lg = lg_next
        return carry

    lax.fori_loop(0, ta // rows, block, 0)


def _attn_prompt(q, k, v, bias, sink, n, w_in, *, gate_col, gate_cols, gate_block, batch, ta, name, casts=()):
    M, dq = q.shape
    dk = k.shape[1]
    D = n.shape[1]
    n_gate_blocks = gate_cols // gate_block
    tiles_per_batch = (M // batch) // ta
    wb = ta // WINDOW
    nk = WINDOW + CHUNK

    def prev(i):
        return (jnp.where(i % tiles_per_batch == 0, i * wb, i * wb - 1), 0)

    row = lambda i: (i, 0)
    grid = (M // ta,)
    c_in, c_out, c_shape, c_bytes = _cast_specs(casts, grid)
    blocks = [_nbytes((ta, dq), _BF) * 2, 2 * _nbytes((WINDOW + ta, dk), _F32),
              _nbytes(bias.shape, _F32), _nbytes((8, N_HEADS * CHUNK), _F32), c_bytes,
              _nbytes((ta, D + gate_cols), _BF), _nbytes((D, gate_cols), _BF) // 2]
    gate_specs = [pl.BlockSpec((D, gate_block), lambda i, b=gate_col // gate_block + c: (0, b),
                               pipeline_mode=pl.Buffered(1)) for c in range(n_gate_blocks)]
    body = functools.partial(_attn_prompt_kernel, tiles_per_batch=tiles_per_batch, n_gate_blocks=n_gate_blocks)
    return pl.pallas_call(
        _with_casts(body, 8 + n_gate_blocks, 2, len(casts)),
        grid=grid,
        in_specs=[
            pl.BlockSpec((ta, dq), row),
            pl.BlockSpec((WINDOW, dk), prev),
            pl.BlockSpec((ta, dk), row),
            pl.BlockSpec((WINDOW, dk), prev),
            pl.BlockSpec((ta, dk), row),
            pl.BlockSpec(bias.shape, lambda i: (0, 0, 0)),
            pl.BlockSpec((1, N_HEADS * CHUNK), lambda i: (0, 0)),
            pl.BlockSpec((ta, D), row),
        ] + gate_specs + c_in,
        out_specs=[pl.BlockSpec((ta, dq), row), pl.BlockSpec((ta, gate_cols), row)] + c_out,
        out_shape=[jax.ShapeDtypeStruct((M, dq), _BF), jax.ShapeDtypeStruct((M, gate_cols), _BF)] + c_shape,
        scratch_shapes=[pltpu.VMEM((WINDOW + ta, dk), _BF), pltpu.VMEM((WINDOW + ta, dk), _BF)],
        compiler_params=pltpu.CompilerParams(
            dimension_semantics=("arbitrary",),
            vmem_limit_bytes=_vmem_limit(blocks, 2 * _nbytes((WINDOW + ta, dk), _BF)),
        ),
        name=name,
    )(q, k, k, v, v, bias, sink, n, *([w_in] * n_gate_blocks), *_cast_args(casts))


LANES = 128


def _attn_sample_kernel(q_ref, ck_ref, cv_ref, k_ref, v_ref, bias_ref, sink_ref, o_ref, *, nb, s):
    dkv = N_KV * HEAD_DIM
    gs = GROUP * s
    assert LANES == 2 * HEAD_DIM and dkv == 2 * LANES
    for b in range(nb):
        rows = slice(b * s, (b + 1) * s)
        kw = jnp.concatenate([ck_ref[b].astype(_BF), k_ref[rows, :].astype(_BF)], axis=0)
        vw = jnp.concatenate([cv_ref[b].astype(_BF), v_ref[rows, :].astype(_BF)], axis=0)
        qf = q_ref[rows, :].astype(_F32)
        blocks = []
        for kv in range(N_KV):
            for g in range(GROUP):
                h = kv * GROUP + g
                pair = qf[:, (h // 2) * LANES:(h // 2 + 1) * LANES]
                if h % 2 != kv % 2:
                    pair = pltpu.roll(pair, HEAD_DIM, 1)
                half = lax.broadcasted_iota(jnp.int32, pair.shape, 1) // HEAD_DIM
                piece = jnp.where(half == kv % 2, pair, 0.0)
                zero = jnp.zeros_like(piece)
                blocks.append(jnp.concatenate([piece, zero] if kv < 2 else [zero, piece], axis=1))
        q_bd = jnp.concatenate(blocks, axis=0).astype(_BF)
        lg = lax.dot_general(kw, q_bd, _NT, preferred_element_type=_F32) + bias_ref[0]
        e, inv = _sink_softmax_keys_major(lg, sink_ref[...])
        z = lax.dot_general((e * inv).astype(_BF), vw, _TN, preferred_element_type=_F32)
        row_kv = lax.broadcasted_iota(jnp.int32, z.shape, 0) // gs
        col_kv = lax.broadcasted_iota(jnp.int32, z.shape, 1) // HEAD_DIM
        z = jnp.where(row_kv == col_kv, z, 0.0)
        r = z[0:gs]
        for kv in range(1, N_KV):
            r = r + z[kv * gs:(kv + 1) * gs]
        for g in range(GROUP):
            o_ref[rows, g * dkv:(g + 1) * dkv] = r[g * s:(g + 1) * s].astype(_BF)


def _attn_sample(q, k, v, cache_k, cache_v, bias, sink, *, nb, name):
    B, W, dk = cache_k.shape
    M, dq = q.shape
    S = M // B
    row = lambda b: (b, 0)
    return pl.pallas_call(
        functools.partial(_attn_sample_kernel, nb=nb, s=S),
        grid=(B // nb,),
        in_specs=[
            pl.BlockSpec((nb * S, dq), row),
            pl.BlockSpec((nb, W, dk), lambda b: (b, 0, 0)),
            pl.BlockSpec((nb, W, dk), lambda b: (b, 0, 0)),
            pl.BlockSpec((nb * S, dk), row),
            pl.BlockSpec((nb * S, dk), row),
            pl.BlockSpec(bias.shape, lambda b: (0, 0, 0)),
            pl.BlockSpec(sink.shape, lambda b: (0, 0)),
        ],
        out_specs=pl.BlockSpec((nb * S, dq), row),
        out_shape=jax.ShapeDtypeStruct((M, dq), _BF),
        compiler_params=pltpu.CompilerParams(dimension_semantics=("arbitrary",)),
        name=name,
    )(q, cache_k, cache_v, k, v, bias, sink)


def _gates_kernel(*refs):
    n_ref, gate_ref = refs[0], refs[-1]
    w_refs = refs[1:-1]
    cw = gate_ref.shape[1] // len(w_refs)
    n = n_ref[...]
    for c, w_ref in enumerate(w_refs):
        gate_ref[:, c * cw:(c + 1) * cw] = jax.nn.sigmoid(_dot(n, w_ref[...])).astype(_BF)


def _gates(n, w_in, *, gate_col, gate_cols, gate_block, name):
    M, D = n.shape
    n_blocks = gate_cols // gate_block
    return pl.pallas_call(
        _gates_kernel,
        grid=(1,),
        in_specs=[pl.BlockSpec((M, D), lambda i: (0, 0))]
        + [pl.BlockSpec((D, gate_block), lambda i, b=gate_col // gate_block + c: (0, b)) for c in range(n_blocks)],
        out_specs=pl.BlockSpec((M, gate_cols), lambda i: (0, 0)),
        out_shape=jax.ShapeDtypeStruct((M, gate_cols), _BF),
        compiler_params=pltpu.CompilerParams(
            dimension_semantics=("arbitrary",),
            vmem_limit_bytes=_vmem_limit([_nbytes((M, D + gate_cols), _BF), _nbytes((D, gate_cols), _BF)]),
        ),
        name=name,
    )(n, *([w_in] * n_blocks))


def _merge_kernel(h_ref, gate_ref, ucb_ref, o_ref, wco_ref, wao_ref, wout_ref, g_ref, out_ref, xn_ref, mix_ref,
                  *, n_chunks):
    ucb, o = ucb_ref[...], o_ref[...]
    D = out_ref.shape[1]
    cw = D // n_chunks
    for c in range(n_chunks):
        cols = slice(c * cw, (c + 1) * cw)
        y_conv = _dot(ucb, wco_ref[:, cols])
        y_attn = _dot(o, wao_ref[:, cols])
        g_conv = gate_ref[:, cols].astype(_F32)
        g_attn = gate_ref[:, D + c * cw:D + (c + 1) * cw].astype(_F32)
        mix_ref[:, cols] = (g_conv * y_conv + g_attn * y_attn).astype(_BF)
    h = h_ref[...] + _dot(mix_ref[...], wout_ref[...])
    out_ref[...] = h
    xn_ref[...] = _rms(h, g_ref[...]).astype(_BF)


def _merge(h, gates, ucb, o, w_conv_out, w_attn_o, w_out, g, *, cw, tm, name):
    M, D = h.shape
    dc, da = ucb.shape[1], o.shape[1]
    row = lambda i: (i, 0)
    once = pl.Buffered(1)
    const = lambda i: (0, 0)
    blocks = [2 * _nbytes((tm, D), _F32), _nbytes((tm, 3 * D + dc + da), _BF), _nbytes((dc + da + D, D), _BF) // 2]
    return pl.pallas_call(
        functools.partial(_merge_kernel, n_chunks=D // cw),
        grid=(M // tm,),
        in_specs=[
            pl.BlockSpec((tm, D), row),
            pl.BlockSpec((tm, 2 * D), row),
            pl.BlockSpec((tm, dc), row),
            pl.BlockSpec((tm, da), row),
            pl.BlockSpec((dc, D), const, pipeline_mode=once),
            pl.BlockSpec((da, D), const, pipeline_mode=once),
            pl.BlockSpec((D, D), const, pipeline_mode=once),
            pl.BlockSpec((1, D), const),
        ],
        out_specs=[pl.BlockSpec((tm, D), row), pl.BlockSpec((tm, D), row)],
        out_shape=[jax.ShapeDtypeStruct((M, D), _F32), jax.ShapeDtypeStruct((M, D), _BF)],
        scratch_shapes=[pltpu.VMEM((tm, D), _BF)],
        compiler_params=pltpu.CompilerParams(
            dimension_semantics=("arbitrary",),
            vmem_limit_bytes=_vmem_limit(blocks, _nbytes((tm, D), _BF)),
        ),
        name=name,
    )(h, gates, ucb, o, w_conv_out, w_attn_o, w_out, g)


def _ple_kernel(h_ref, pe_ref, g_ref, wp_ref, wg_ref, out_ref):
    h = h_ref[...]
    gate = jax.nn.sigmoid(_dot(_rms(h, g_ref[...]).astype(_BF), wg_ref[...]))
    emb = _dot(pe_ref[...].astype(_BF), wp_ref[...])
    out_ref[...] = h + emb * gate


def _ple(h, pe, g, w_ple, w_gate, *, tm, name):
    M, D = h.shape
    dp = pe.shape[1]
    row = lambda i: (i, 0)
    const = lambda i: (0, 0)
    blocks = [2 * _nbytes((tm, D), _F32), _nbytes((tm, dp), _F32), _nbytes((dp + D, D), _BF)]
    return pl.pallas_call(
        _ple_kernel,
        grid=(M // tm,),
        in_specs=[
            pl.BlockSpec((tm, D), row),
            pl.BlockSpec((tm, dp), row),
            pl.BlockSpec((1, D), const),
            pl.BlockSpec((dp, D), const),
            pl.BlockSpec((D, D), const),
        ],
        out_specs=pl.BlockSpec((tm, D), row),
        out_shape=jax.ShapeDtypeStruct((M, D), _F32),
        compiler_params=pltpu.CompilerParams(
            dimension_semantics=("arbitrary",),
            vmem_limit_bytes=_vmem_limit(blocks),
        ),
        name=name,
    )(h, pe, g, w_ple, w_gate)


def _row_tile(m, want):
    return want if m % want == 0 else m


def kernel(x_prompt, x_sample, p_prompt, p_sample, state_conv, cache_k, cache_v, rel_table, ffn1_norm, ffn1_wg, ffn1_wu, ffn1_wd, mix_norm, w_in, conv_w, q_norm, k_norm, attn_sink, w_conv_out, w_attn_o, w_out, ffn2_norm, ffn2_wg, ffn2_wu, ffn2_wd, ple_norm, w_ple, w_ple_gate):
    B, T, D = x_prompt.shape
    Bs, S, _ = x_sample.shape
    depth = ffn1_wg.shape[0]
    d_conv = conv_w.shape[2]
    dq, dk = N_HEADS * HEAD_DIM, N_KV * HEAD_DIM
    qkv_col = 3 * d_conv
    gc_col = qkv_col + dq + 2 * dk
    Mp, Ms = B * T, Bs * S
    tm_p = _row_tile(Mp, 512)
    tm_s = _row_tile(Ms, 512)
    tf = 1024
    tn = 512
    cw = 256

    nk_p, nk_s = WINDOW + CHUNK, WINDOW + S
    bias_p = _rel_bias(rel_table, _bucket_index(CHUNK, nk_p), WINDOW // CHUNK + 1, CHUNK, "rel_bias_prompt")
    bias_s = _rel_bias(rel_table, _bucket_index(S, nk_s), 1, 0, "rel_bias_sample")
    lane = jnp.arange(cw) // HEAD_DIM
    ones = (lane[:, None] == lane[None, :]).astype(_BF)

    yp = x_prompt.reshape(Mp, D)
    ys = x_sample.reshape(Ms, D)
    conv_p, k_p, v_p, conv_s, k_s, v_s = [], [], [], [], [], []
    head_to_group_major = lambda h: (h % GROUP) * N_KV + h // GROUP
    for l in range(depth):
        vec = lambda g: g[l].reshape(1, -1)
        h1_s, n_s, f1_wg, f1_wu, f1_wd = _ffn(ys, vec(ffn1_norm), ffn1_wg[l], ffn1_wu[l], ffn1_wd[l],
                                              vec(mix_norm), tm=tm_s, tf=tf // 2, name="ffn1_sample")
        f1 = (vec(ffn1_norm), f1_wg, f1_wu, f1_wd)
        gq = jnp.tile(q_norm[l], N_HEADS).reshape(1, dq)
        gk = jnp.tile(k_norm[l], N_KV).reshape(1, dk)
        sink_p = jnp.repeat(attn_sink[l], CHUNK).reshape(1, N_HEADS * CHUNK)
        sink_s = jnp.repeat(attn_sink[l], S).reshape(1, N_HEADS * S)

        h1, n, w_in_l = _ffn(yp, *f1, vec(mix_norm), tm=tm_p, tf=tf, name="ffn1_prompt", casts=(w_in[l],))
        ucb, ulast, q, k, v, f2_wg, f2_wu, w_co, w_ao_p, w_o = _inproj(
            n, w_in_l, conv_w[l], gq, gk, ones, tm=tm_p, batch=B, name="inproj_prompt",
            casts=(ffn2_wg[l], ffn2_wu[l], w_conv_out[l], (w_attn_o[l], HEAD_DIM, head_to_group_major), w_out[l]))
        o, gates, f2_wd, w_p, w_pg = _attn_prompt(
            q, k, v, bias_p, sink_p, n, w_in_l, gate_col=gc_col, gate_cols=2 * D, gate_block=tn, batch=B,
            ta=tm_p, name="attn_prompt", casts=(ffn2_wd[l], w_ple[l], w_ple_gate[l]))
        h2, xn2 = _merge(h1, gates, ucb, o, w_co, w_ao_p, w_o, vec(ffn2_norm), cw=tn, tm=tm_p, name="merge_prompt")
        f2 = (f2_wg, f2_wu, f2_wd)
        h3, = _ffn(h2, xn2, *f2, None, tm=tm_p, tf=tf, name="ffn2_prompt")
        yp = _ple(h3, p_prompt[l].reshape(Mp, -1), vec(ple_norm), w_p, w_pg, tm=tm_p, name="ple_prompt")
        conv_p.append(ulast[:, 6:8])
        k_p.append(k.reshape(B, T, dk)[:, T - WINDOW:].reshape(B, WINDOW, N_KV, HEAD_DIM))
        v_p.append(v.reshape(B, T, dk)[:, T - WINDOW:].reshape(B, WINDOW, N_KV, HEAD_DIM))

        st = state_conv[l]
        zeros = jnp.zeros((Bs, S - 2, d_conv), _F32)
        e2 = jnp.concatenate([st, zeros], axis=1).reshape(Ms, d_conv)
        e1 = jnp.concatenate([st[:, 1:], zeros, zeros[:, :1]], axis=1).reshape(Ms, d_conv)
        h1, n = h1_s, n_s
        ucb, u, q, k, v = _inproj(n, w_in_l, conv_w[l], gq, gk, ones, tm=tm_s, seg=S, edges=(e1, e2),
                                  name="inproj_sample")
        o = _attn_sample(q, k, v, cache_k[l].reshape(Bs, WINDOW, dk), cache_v[l].reshape(Bs, WINDOW, dk),
                         bias_s, sink_s, nb=4, name="attn_sample")
        gates = _gates(n, w_in_l, gate_col=gc_col, gate_cols=2 * D, gate_block=tn, name="gates_sample")
        h2, xn2 = _merge(h1, gates, ucb, o, w_co, w_ao_p, w_o, vec(ffn2_norm), cw=tn, tm=tm_s, name="merge_sample")
        h3, = _ffn(h2, xn2, *f2, None, tm=tm_s, tf=tf, name="ffn2_sample")
        ys = _ple(h3, p_sample[l].reshape(Ms, -1), vec(ple_norm), w_p, w_pg, tm=tm_s, name="ple_sample")
        conv_s.append(u.reshape(Bs, S, d_conv)[:, S - 2:])
        k_s.append(k.reshape(Bs, S, N_KV, HEAD_DIM))
        v_s.append(v.reshape(Bs, S, N_KV, HEAD_DIM))

    return (yp.reshape(B, T, D), ys.reshape(Bs, S, D), jnp.stack(conv_p), jnp.stack(k_p), jnp.stack(v_p),
            jnp.stack(conv_s), jnp.stack(k_s), jnp.stack(v_s))
```

```python
import functools
import math

import jax
import jax.numpy as jnp
from jax import lax
from jax.experimental import pallas as pl
from jax.experimental.pallas import tpu as pltpu

_BF = jnp.bfloat16
_F32 = jnp.float32

CHUNK = 64
WINDOW = 128
N_HEADS = 16
N_KV = 4
GROUP = N_HEADS // N_KV
HEAD_DIM = 64
N_BUCKETS = 32
MAX_DIST = 128
EPS = 1e-6
LOG2E = math.log2(math.e)
QK_SCALE = HEAD_DIM ** -0.5 * LOG2E

V7X_VMEM_BYTES = 64 * 1024 * 1024
VMEM_INTERNAL_BYTES = 12 * 1024 * 1024


def _vmem_limit(block_bytes, scratch_bytes=0):
    need = 2 * sum(block_bytes) + scratch_bytes + VMEM_INTERNAL_BYTES
    return int(min(need, V7X_VMEM_BYTES - 4 * 1024 * 1024))


def _nbytes(shape, dtype):
    return math.prod(shape) * jnp.dtype(dtype).itemsize


def _dot(a, b):
    return jnp.dot(a, b, preferred_element_type=_F32)


def _rms(x, g):
    ms = jnp.mean(x * x, axis=-1, keepdims=True)
    return x * lax.rsqrt(ms + EPS) * g


BF16_TILE_ROWS = 16


def _cast_specs(arrays, grid):
    steps = math.prod(grid)

    def flat(*ids):
        s = 0
        for k, g in zip(ids, grid):
            s = s * g + k
        return s

    in_specs, out_specs, out_shape, nbytes = [], [], [], 0
    for a in arrays:
        a, rb, out_block = a if isinstance(a, tuple) else (a, None, lambda b: b)
        rows, cols = a.shape
        rb = rb or max(BF16_TILE_ROWS, rows // steps)
        assert rows % rb == 0 and steps % (rows // rb) == 0, (a.shape, steps)
        rep = steps // (rows // rb)
        idx = lambda *ids, rep=rep: (flat(*ids) // rep, 0)
        out_idx = lambda *ids, rep=rep, out_block=out_block: (out_block(flat(*ids) // rep), 0)
        in_specs.append(pl.BlockSpec((rb, cols), idx))
        out_specs.append(pl.BlockSpec((rb, cols), out_idx))
        out_shape.append(jax.ShapeDtypeStruct((rows, cols), _BF))
        nbytes += _nbytes((rb, cols), _F32) + _nbytes((rb, cols), _BF)
    return in_specs, out_specs, out_shape, nbytes


def _cast_args(arrays):
    return [a[0] if isinstance(a, tuple) else a for a in arrays]


def _with_casts(body, n_in, n_out, n_cast):
    def kernel(*refs):
        outs_at = n_in + n_cast
        scratch_at = outs_at + n_out + n_cast
        for src, dst in zip(refs[n_in:outs_at], refs[outs_at + n_out:scratch_at]):
            dst[...] = src[...].astype(_BF)
        body(*refs[:n_in], *refs[outs_at:outs_at + n_out], *refs[scratch_at:])
    return kernel


def _ffn_kernel(*refs, emit_norm, keep_weights, normed_input):
    x_ref, g_ref, wg_ref, wu_ref, wd_ref = refs[:5]
    rest = list(refs[5:])
    g2_ref = rest.pop(0) if emit_norm else None
    out_ref = rest.pop(0)
    n_ref = rest.pop(0) if emit_norm else None
    weight_copies = [rest.pop(0) for _ in range(3)] if keep_weights else []
    j = pl.program_id(1)
    last = pl.num_programs(1) - 1
    if normed_input:
        xn_ref = g_ref
    else:
        xn_ref, = rest

        @pl.when(j == 0)
        def _():
            xn_ref[...] = _rms(x_ref[...], g_ref[...]).astype(_BF)

    wg, wu, wd = wg_ref[...], wu_ref[...], wd_ref[...]
    if keep_weights:
        wg, wu, wd = wg.astype(_BF), wu.astype(_BF), wd.astype(_BF)
        for dst, w in zip(weight_copies, (wg, wu, wd)):
            dst[...] = w
    xn = xn_ref[...]
    gate = _dot(xn, wg)
    up = _dot(xn, wu)
    hid = (jax.nn.silu(gate) * up * 0.5).astype(_BF)
    out_ref[...] = jnp.where(j == 0, x_ref[...], out_ref[...]) + _dot(hid, wd)

    if emit_norm:
        @pl.when(j == last)
        def _():
            n_ref[...] = _rms(out_ref[...], g2_ref[...]).astype(_BF)


def _ffn(x, g, wg, wu, wd, g2, *, tm, tf, name, casts=()):
    M, D = x.shape
    F = wg.shape[1]
    emit_norm = g2 is not None
    keep_weights = wg.dtype == _F32
    normed_input = g.shape[0] == M
    grid = (M // tm, F // tf)
    assert not keep_weights or grid[0] == 1
    row = lambda i, j: (i, 0)
    w_specs = [
        pl.BlockSpec((D, tf), lambda i, j: (0, j)),
        pl.BlockSpec((D, tf), lambda i, j: (0, j)),
        pl.BlockSpec((tf, D), lambda i, j: (j, 0)),
    ]
    g_spec = pl.BlockSpec((tm, D), row) if normed_input else pl.BlockSpec((1, D), lambda i, j: (0, 0))
    in_specs = [pl.BlockSpec((tm, D), row), g_spec] + w_specs
    args = [x, g, wg, wu, wd]
    out_shape = [jax.ShapeDtypeStruct((M, D), _F32)]
    out_specs = [pl.BlockSpec((tm, D), row)]
    blocks = [_nbytes((tm, D), _F32) * 2, 3 * _nbytes((D, tf), wg.dtype)]
    if emit_norm:
        in_specs.append(pl.BlockSpec((1, D), lambda i, j: (0, 0)))
        args.append(g2)
        out_shape.append(jax.ShapeDtypeStruct((M, D), _BF))
        out_specs.append(pl.BlockSpec((tm, D), row))
        blocks.append(_nbytes((tm, D), _BF))
    if keep_weights:
        out_shape += [jax.ShapeDtypeStruct(w.shape, _BF) for w in (wg, wu, wd)]
        out_specs += w_specs
        blocks.append(3 * _nbytes((D, tf), _BF))
    c_in, c_out, c_shape, c_bytes = _cast_specs(casts, grid)
    blocks.append(c_bytes)
    body = functools.partial(_ffn_kernel, emit_norm=emit_norm, keep_weights=keep_weights,
                             normed_input=normed_input)
    res = pl.pallas_call(
        _with_casts(body, len(in_specs), len(out_specs), len(casts)),
        grid=grid,
        in_specs=in_specs + c_in,
        out_specs=out_specs + c_out,
        out_shape=out_shape + c_shape,
        scratch_shapes=[] if normed_input else [pltpu.VMEM((tm, D), _BF)],
        compiler_params=pltpu.CompilerParams(
            dimension_semantics=("arbitrary", "arbitrary"),
            vmem_limit_bytes=_vmem_limit(blocks, _nbytes((tm, D), _BF)),
        ),
        name=name,
    )(*args, *_cast_args(casts))
    return res


def _conv_kernel(*refs, carry_rows, seg, tiles_per_seg, d_conv, cw):
    if carry_rows:
        n_ref, w_ref, cwt_ref, ucb_ref, ulast_ref, carry_ref = refs
    else:
        n_ref, w_ref, cwt_ref, e1_ref, e2_ref, ucb_ref, u_ref = refs
    tm = n_ref.shape[0]
    n = n_ref[...]
    if carry_rows:
        @pl.when(pl.program_id(0) % tiles_per_seg == 0)
        def _():
            carry_ref[...] = jnp.zeros_like(carry_ref)

    for c in range(d_conv // cw):
        lo = c * cw
        cb = _dot(n, w_ref[:, lo:lo + cw])
        cc = _dot(n, w_ref[:, d_conv + lo:d_conv + lo + cw])
        cv = _dot(n, w_ref[:, 2 * d_conv + lo:2 * d_conv + lo + cw])
        u = cc * cv
        r1 = pltpu.roll(u, 1, 0)
        r2 = pltpu.roll(u, 2, 0)
        if carry_rows:
            prev = carry_ref[:, lo:lo + cw]
            rows8 = lax.broadcasted_iota(jnp.int32, (8, cw), 0)
            h1 = jnp.where(rows8 < 1, pltpu.roll(prev, 1, 0), r1[:8])
            h2 = jnp.where(rows8 < 2, pltpu.roll(prev, 2, 0), r2[:8])
            r1 = jnp.concatenate([h1, r1[8:]], axis=0)
            r2 = jnp.concatenate([h2, r2[8:]], axis=0)
            carry_ref[:, lo:lo + cw] = u[tm - 8:]
            ulast_ref[0, :, lo:lo + cw] = u[tm - 8:]
        else:
            pos = lax.broadcasted_iota(jnp.int32, (tm, cw), 0) % seg
            r1 = jnp.where(pos < 1, e1_ref[:, lo:lo + cw], r1)
            r2 = jnp.where(pos < 2, e2_ref[:, lo:lo + cw], r2)
            u_ref[:, lo:lo + cw] = u
        w0 = cwt_ref[0:1, lo:lo + cw]
        w1 = cwt_ref[1:2, lo:lo + cw]
        w2 = cwt_ref[2:3, lo:lo + cw]
        conv = w0 * r2 + w1 * r1 + w2 * u
        ucb_ref[:, lo:lo + cw] = (cb * conv).astype(_BF)


def _qkv_kernel(n_ref, w_ref, gq_ref, gk_ref, ones_ref, q_ref, k_ref, v_ref, *, cw, col0):
    ones = ones_ref[...]
    dq = q_ref.shape[1]
    dk = k_ref.shape[1]
    p = _dot(n_ref[...], w_ref[:, col0:col0 + dq + 2 * dk])
    x = p[:, :dq + dk]
    sq = x * x
    hi = sq.astype(_BF)
    lo = (sq - hi.astype(_F32)).astype(_BF)
    ss = jnp.concatenate(
        [_dot(hi[:, c * cw:(c + 1) * cw], ones) + _dot(lo[:, c * cw:(c + 1) * cw], ones)
         for c in range((dq + dk) // cw)], axis=1)
    xr = x * lax.rsqrt(ss * (1.0 / HEAD_DIM) + EPS)
    q_ref[...] = (xr[:, :dq] * gq_ref[...] * QK_SCALE).astype(_BF)
    k_ref[...] = xr[:, dq:] * gk_ref[...]
    v_ref[...] = p[:, dq + dk:]


def _inproj_kernel(*refs, carry_rows, seg, tiles_per_seg, d_conv, cw):
    if carry_rows:
        (n_ref, w_ref, cwt_ref, gq_ref, gk_ref, ones_ref,
         ucb_ref, ulast_ref, q_ref, k_ref, v_ref, carry_ref) = refs
        conv_refs = (n_ref, w_ref, cwt_ref, ucb_ref, ulast_ref, carry_ref)
    else:
        (n_ref, w_ref, cwt_ref, gq_ref, gk_ref, ones_ref, e1_ref, e2_ref,
         ucb_ref, u_ref, q_ref, k_ref, v_ref) = refs
        conv_refs = (n_ref, w_ref, cwt_ref, e1_ref, e2_ref, ucb_ref, u_ref)
    _conv_kernel(*conv_refs, carry_rows=carry_rows, seg=seg, tiles_per_seg=tiles_per_seg, d_conv=d_conv, cw=cw)
    _qkv_kernel(n_ref, w_ref, gq_ref, gk_ref, ones_ref, q_ref, k_ref, v_ref, cw=cw, col0=3 * d_conv)


def _inproj(n, w_in, conv_w, gq, gk, ones, *, tm, name, batch=None, seg=None, edges=(), casts=()):
    M, D = n.shape
    d_conv = conv_w.shape[1]
    cw = ones.shape[0]
    dq, dk = N_HEADS * HEAD_DIM, N_KV * HEAD_DIM
    wcols = 3 * d_conv + dq + 2 * dk
    carry_rows = batch is not None
    grid = (M // tm,)
    tiles_per_seg = (M // batch) // tm if carry_rows else None
    c_in, c_out, c_shape, c_bytes = _cast_specs(casts, grid)
    const = lambda i: (0, 0)
    row = lambda i: (i, 0)
    in_specs = [
        pl.BlockSpec((tm, D), row),
        pl.BlockSpec((D, wcols), const, pipeline_mode=pl.Buffered(1)),
        pl.BlockSpec((3, d_conv), const),
        pl.BlockSpec((1, dq), const),
        pl.BlockSpec((1, dk), const),
        pl.BlockSpec((cw, cw), const),
    ] + [pl.BlockSpec((tm, d_conv), row) for _ in edges]
    if carry_rows:
        u_spec = pl.BlockSpec((1, 8, d_conv), lambda i: (i // tiles_per_seg, 0, 0))
        u_shape = jax.ShapeDtypeStruct((batch, 8, d_conv), _F32)
        scratch = [pltpu.VMEM((8, d_conv), _F32)]
    else:
        u_spec = pl.BlockSpec((tm, d_conv), row)
        u_shape = jax.ShapeDtypeStruct((M, d_conv), _F32)
        scratch = []
    out_specs = [pl.BlockSpec((tm, d_conv), row), u_spec,
                 pl.BlockSpec((tm, dq), row), pl.BlockSpec((tm, dk), row), pl.BlockSpec((tm, dk), row)]
    out_shape = [jax.ShapeDtypeStruct((M, d_conv), _BF), u_shape, jax.ShapeDtypeStruct((M, dq), _BF),
                 jax.ShapeDtypeStruct((M, dk), _F32), jax.ShapeDtypeStruct((M, dk), _F32)]
    blocks = [_nbytes((tm, D), _BF), _nbytes((D, wcols), _BF) // 2, (1 + len(edges)) * _nbytes((tm, d_conv), _F32),
              _nbytes((tm, d_conv + dq), _BF), 2 * _nbytes((tm, dk), _F32), c_bytes]
    body = functools.partial(_inproj_kernel, carry_rows=carry_rows, seg=seg, tiles_per_seg=tiles_per_seg,
                             d_conv=d_conv, cw=cw)
    return pl.pallas_call(
        _with_casts(body, len(in_specs), len(out_specs), len(casts)),
        grid=grid,
        in_specs=in_specs + c_in,
        out_specs=out_specs + c_out,
        out_shape=out_shape + c_shape,
        scratch_shapes=scratch,
        compiler_params=pltpu.CompilerParams(
            dimension_semantics=("arbitrary",),
            vmem_limit_bytes=_vmem_limit(blocks),
        ),
        name=name,
    )(n, w_in, conv_w, gq, gk, ones, *edges, *_cast_args(casts))


def _bias_kernel(tab_ref, bkt_ref, out_ref, *, mask_step):
    bkt = bkt_ref[...]
    key = lax.broadcasted_iota(jnp.int32, bkt.shape, 1)
    for h in range(out_ref.shape[1]):
        acc = jnp.zeros(bkt.shape, _F32)
        for b in range(tab_ref.shape[0]):
            acc = jnp.where(bkt == b, tab_ref[b, h] * LOG2E, acc)
        for v in range(out_ref.shape[0]):
            out_ref[v, h] = jnp.where(key < v * mask_step, -jnp.inf, acc)


def _rel_bias(table, bucket, n_variants, mask_step, name):
    nq, nk = bucket.shape
    nh = table.shape[1]
    out = pl.pallas_call(
        functools.partial(_bias_kernel, mask_step=mask_step),
        in_specs=[pl.BlockSpec(memory_space=pltpu.SMEM), pl.BlockSpec((nq, nk), lambda: (0, 0))],
        out_specs=pl.BlockSpec((n_variants, nh, nq, nk), lambda: (0, 0, 0, 0)),
        out_shape=jax.ShapeDtypeStruct((n_variants, nh, nq, nk), _F32),
        name=name,
    )(table, bucket)
    return out.transpose(0, 3, 1, 2).reshape(n_variants, nk, nh * nq)


def _t5_bucket(rel):
    nb = N_BUCKETS // 2
    max_exact = nb // 2
    ret = jnp.where(rel > 0, nb, 0)
    n = jnp.abs(rel)
    nf = jnp.maximum(n, 1).astype(jnp.float32)
    large = max_exact + (jnp.log(nf / max_exact) / math.log(MAX_DIST / max_exact) * (nb - max_exact)).astype(jnp.int32)
    large = jnp.minimum(large, nb - 1)
    return ret + jnp.where(n < max_exact, n, large)


def _bucket_index(n_q, n_k):
    i = jnp.arange(n_q)[:, None]
    j = jnp.arange(n_k)[None, :]
    return _t5_bucket(j - WINDOW - i).astype(jnp.int32)


_NT = (((1,), (1,)), ((), ()))
_TN = (((0,), (0,)), ((), ()))


def _sink_softmax_keys_major(lg, sink_t):
    sink2 = sink_t * LOG2E
    m = jnp.maximum(jnp.max(lg, axis=0, keepdims=True), sink2)
    e = jnp.exp2(lg - m)
    den = jnp.sum(e, axis=0, keepdims=True) + jnp.exp2(sink2 - m)
    return e, 1.0 / den


def _attn_chunk_logits(qc, kw, bias_t):
    logits = []
    for kv in range(N_KV):
        qg = jnp.concatenate(
            [qc[:, (kv * GROUP + g) * HEAD_DIM:(kv * GROUP + g + 1) * HEAD_DIM] for g in range(GROUP)], axis=0)
        kk = kw[:, kv * HEAD_DIM:(kv + 1) * HEAD_DIM]
        logits.append(lax.dot_general(kk, qg, _NT, preferred_element_type=_F32))
    return jnp.concatenate(logits, axis=1) + bias_t


def _attn_chunk_output(lg, vw, sink_t):
    gq = lg.shape[1] // N_KV
    e, inv = _sink_softmax_keys_major(lg, sink_t)
    e = e.astype(_BF)
    outs = []
    for kv in range(N_KV):
        vv = vw[:, kv * HEAD_DIM:(kv + 1) * HEAD_DIM]
        cols = slice(kv * gq, (kv + 1) * gq)
        outs.append(lax.dot_general(vv, e[:, cols], _TN, preferred_element_type=_F32) * inv[:, cols])
    return jnp.concatenate(outs, axis=0).T


ATTN_CHUNKS_PER_BLOCK = 8


def _attn_prompt_kernel(*refs, tiles_per_batch, n_gate_blocks):
    q_ref, kp_ref, kc_ref, vp_ref, vc_ref, bias_ref, sink_ref, n_ref = refs[:8]
    wgate_refs = refs[8:8 + n_gate_blocks]
    o_ref, gate_ref, kwin, vwin = refs[8 + n_gate_blocks:]
    ta = q_ref.shape[0]
    nk = WINDOW + CHUNK
    dkv = N_KV * HEAD_DIM
    rows = ATTN_CHUNKS_PER_BLOCK * CHUNK
    gate_cols = gate_ref.shape[1] // n_gate_blocks
    kwin[0:WINDOW, :] = kp_ref[...].astype(_BF)
    kwin[WINDOW:, :] = kc_ref[...].astype(_BF)
    vwin[0:WINDOW, :] = vp_ref[...].astype(_BF)
    vwin[WINDOW:, :] = vc_ref[...].astype(_BF)
    tile_pos = (pl.program_id(0) % tiles_per_batch) * ta

    def logits(r):
        n_before = jnp.maximum(0, (WINDOW - (tile_pos + r)) // CHUNK)
        return _attn_chunk_logits(q_ref[pl.ds(r, CHUNK), :], kwin[pl.ds(r, nk), :], bias_ref[n_before])

    def finish(r, lg):
        o = _attn_chunk_output(lg, vwin[pl.ds(r, nk), :], sink_ref[...])
        for g in range(GROUP):
            o_ref[pl.ds(r, CHUNK), g * dkv:(g + 1) * dkv] = o[g * CHUNK:(g + 1) * CHUNK].astype(_BF)

    def block(b, carry):
        r0 = pl.multiple_of(b * rows, rows)
        n = n_ref[pl.ds(r0, rows), :]
        chunk_rows = [pl.multiple_of(r0 + c * CHUNK, CHUNK) for c in range(ATTN_CHUNKS_PER_BLOCK)]
        lg = logits(chunk_rows[0])
        for c in range(max(ATTN_CHUNKS_PER_BLOCK, n_gate_blocks)):
            lg_next = logits(chunk_rows[c + 1]) if c + 1 < ATTN_CHUNKS_PER_BLOCK else None
            if c < n_gate_blocks:
                gate = jax.nn.sigmoid(_dot(n, wgate_refs[c][...]))
                gate_ref[pl.ds(r0, rows), c * gate_cols:(c + 1) * gate_cols] = gate.astype(_BF)
            if c < ATTN_CHUNKS_PER_BLOCK:
                finish(chunk_rows[c], lg)
            lg = lg_next
        return carry

    lax.fori_loop(0, ta // rows, block, 0)


def _attn_prompt(q, k, v, bias, sink, n, w_in, *, gate_col, gate_cols, gate_block, batch, ta, name, casts=()):
    M, dq = q.shape
    dk = k.shape[1]
    D = n.shape[1]
    n_gate_blocks = gate_cols // gate_block
    tiles_per_batch = (M // batch) // ta
    wb = ta // WINDOW
    nk = WINDOW + CHUNK

    def prev(i):
        return (jnp.where(i % tiles_per_batch == 0, i * wb, i * wb - 1), 0)

    row = lambda i: (i, 0)
    grid = (M // ta,)
    c_in, c_out, c_shape, c_bytes = _cast_specs(casts, grid)
    blocks = [_nbytes((ta, dq), _BF) * 2, 2 * _nbytes((WINDOW + ta, dk), _F32),
              _nbytes(bias.shape, _F32), _nbytes((8, N_HEADS * CHUNK), _F32), c_bytes,
              _nbytes((ta, D + gate_cols), _BF), _nbytes((D, gate_cols), _BF) // 2]
    gate_specs = [pl.BlockSpec((D, gate_block), lambda i, b=gate_col // gate_block + c: (0, b),
                               pipeline_mode=pl.Buffered(1)) for c in range(n_gate_blocks)]
    body = functools.partial(_attn_prompt_kernel, tiles_per_batch=tiles_per_batch, n_gate_blocks=n_gate_blocks)
    return pl.pallas_call(
        _with_casts(body, 8 + n_gate_blocks, 2, len(casts)),
        grid=grid,
        in_specs=[
            pl.BlockSpec((ta, dq), row),
            pl.BlockSpec((WINDOW, dk), prev),
            pl.BlockSpec((ta, dk), row),
            pl.BlockSpec((WINDOW, dk), prev),
            pl.BlockSpec((ta, dk), row),
            pl.BlockSpec(bias.shape, lambda i: (0, 0, 0)),
            pl.BlockSpec((1, N_HEADS * CHUNK), lambda i: (0, 0)),
            pl.BlockSpec((ta, D), row),
        ] + gate_specs + c_in,
        out_specs=[pl.BlockSpec((ta, dq), row), pl.BlockSpec((ta, gate_cols), row)] + c_out,
        out_shape=[jax.ShapeDtypeStruct((M, dq), _BF), jax.ShapeDtypeStruct((M, gate_cols), _BF)] + c_shape,
        scratch_shapes=[pltpu.VMEM((WINDOW + ta, dk), _BF), pltpu.VMEM((WINDOW + ta, dk), _BF)],
        compiler_params=pltpu.CompilerParams(
            dimension_semantics=("arbitrary",),
            vmem_limit_bytes=_vmem_limit(blocks, 2 * _nbytes((WINDOW + ta, dk), _BF)),
        ),
        name=name,
    )(q, k, k, v, v, bias, sink, n, *([w_in] * n_gate_blocks), *_cast_args(casts))


LANES = 128


def _attn_sample_kernel(q_ref, ck_ref, cv_ref, k_ref, v_ref, bias_ref, sink_ref, o_ref, *, nb, s):
    dkv = N_KV * HEAD_DIM
    gs = GROUP * s
    assert LANES == 2 * HEAD_DIM and dkv == 2 * LANES
    for b in range(nb):
        rows = slice(b * s, (b + 1) * s)
        kw = jnp.concatenate([ck_ref[b].astype(_BF), k_ref[rows, :].astype(_BF)], axis=0)
        vw = jnp.concatenate([cv_ref[b].astype(_BF), v_ref[rows, :].astype(_BF)], axis=0)
        qf = q_ref[rows, :].astype(_F32)
        blocks = []
        for kv in range(N_KV):
            for g in range(GROUP):
                h = kv * GROUP + g
                pair = qf[:, (h // 2) * LANES:(h // 2 + 1) * LANES]
                if h % 2 != kv % 2:
                    pair = pltpu.roll(pair, HEAD_DIM, 1)
                half = lax.broadcasted_iota(jnp.int32, pair.shape, 1) // HEAD_DIM
                piece = jnp.where(half == kv % 2, pair, 0.0)
                zero = jnp.zeros_like(piece)
                blocks.append(jnp.concatenate([piece, zero] if kv < 2 else [zero, piece], axis=1))
        q_bd = jnp.concatenate(blocks, axis=0).astype(_BF)
        lg = lax.dot_general(kw, q_bd, _NT, preferred_element_type=_F32) + bias_ref[0]
        e, inv = _sink_softmax_keys_major(lg, sink_ref[...])
        z = lax.dot_general((e * inv).astype(_BF), vw, _TN, preferred_element_type=_F32)
        row_kv = lax.broadcasted_iota(jnp.int32, z.shape, 0) // gs
        col_kv = lax.broadcasted_iota(jnp.int32, z.shape, 1) // HEAD_DIM
        z = jnp.where(row_kv == col_kv, z, 0.0)
        r = z[0:gs]
        for kv in range(1, N_KV):
            r = r + z[kv * gs:(kv + 1) * gs]
        for g in range(GROUP):
            o_ref[rows, g * dkv:(g + 1) * dkv] = r[g * s:(g + 1) * s].astype(_BF)


def _attn_sample(q, k, v, cache_k, cache_v, bias, sink, *, nb, name):
    B, W, dk = cache_k.shape
    M, dq = q.shape
    S = M // B
    row = lambda b: (b, 0)
    return pl.pallas_call(
        functools.partial(_attn_sample_kernel, nb=nb, s=S),
        grid=(B // nb,),
        in_specs=[
            pl.BlockSpec((nb * S, dq), row),
            pl.BlockSpec((nb, W, dk), lambda b: (b, 0, 0)),
            pl.BlockSpec((nb, W, dk), lambda b: (b, 0, 0)),
            pl.BlockSpec((nb * S, dk), row),
            pl.BlockSpec((nb * S, dk), row),
            pl.BlockSpec(bias.shape, lambda b: (0, 0, 0)),
            pl.BlockSpec(sink.shape, lambda b: (0, 0)),
        ],
        out_specs=pl.BlockSpec((nb * S, dq), row),
        out_shape=jax.ShapeDtypeStruct((M, dq), _BF),
        compiler_params=pltpu.CompilerParams(dimension_semantics=("arbitrary",)),
        name=name,
    )(q, cache_k, cache_v, k, v, bias, sink)


def _gates_kernel(n_ref, w_ref, gate_ref):
    gate_ref[...] = jax.nn.sigmoid(_dot(n_ref[...], w_ref[...])).astype(_BF)


def _gates(n, w_in, *, gate_col, gate_cols, gate_block, name):
    M, D = n.shape
    first = gate_col // gate_block
    return pl.pallas_call(
        _gates_kernel,
        grid=(gate_cols // gate_block,),
        in_specs=[pl.BlockSpec((M, D), lambda c: (0, 0)), pl.BlockSpec((D, gate_block), lambda c: (0, first + c))],
        out_specs=pl.BlockSpec((M, gate_block), lambda c: (0, c)),
        out_shape=jax.ShapeDtypeStruct((M, gate_cols), _BF),
        compiler_params=pltpu.CompilerParams(
            dimension_semantics=("arbitrary",),
            vmem_limit_bytes=_vmem_limit([_nbytes((M, D + gate_block), _BF), _nbytes((D, gate_block), _BF)]),
        ),
        name=name,
    )(n, w_in)


def _merge_kernel(h_ref, gate_ref, ucb_ref, o_ref, wco_ref, wao_ref, wout_ref, g_ref, out_ref, xn_ref, mix_ref,
                  *, n_chunks):
    ucb, o = ucb_ref[...], o_ref[...]
    D = out_ref.shape[1]
    cw = D // n_chunks
    for c in range(n_chunks):
        cols = slice(c * cw, (c + 1) * cw)
        y_conv = _dot(ucb, wco_ref[:, cols])
        y_attn = _dot(o, wao_ref[:, cols])
        g_conv = gate_ref[:, cols].astype(_F32)
        g_attn = gate_ref[:, D + c * cw:D + (c + 1) * cw].astype(_F32)
        mix_ref[:, cols] = (g_conv * y_conv + g_attn * y_attn).astype(_BF)
    h = h_ref[...] + _dot(mix_ref[...], wout_ref[...])
    out_ref[...] = h
    xn_ref[...] = _rms(h, g_ref[...]).astype(_BF)


def _merge(h, gates, ucb, o, w_conv_out, w_attn_o, w_out, g, *, cw, tm, name):
    M, D = h.shape
    dc, da = ucb.shape[1], o.shape[1]
    row = lambda i: (i, 0)
    once = pl.Buffered(1)
    const = lambda i: (0, 0)
    blocks = [2 * _nbytes((tm, D), _F32), _nbytes((tm, 3 * D + dc + da), _BF), _nbytes((dc + da + D, D), _BF) // 2]
    return pl.pallas_call(
        functools.partial(_merge_kernel, n_chunks=D // cw),
        grid=(M // tm,),
        in_specs=[
            pl.BlockSpec((tm, D), row),
            pl.BlockSpec((tm, 2 * D), row),
            pl.BlockSpec((tm, dc), row),
            pl.BlockSpec((tm, da), row),
            pl.BlockSpec((dc, D), const, pipeline_mode=once),
            pl.BlockSpec((da, D), const, pipeline_mode=once),
            pl.BlockSpec((D, D), const, pipeline_mode=once),
            pl.BlockSpec((1, D), const),
        ],
        out_specs=[pl.BlockSpec((tm, D), row), pl.BlockSpec((tm, D), row)],
        out_shape=[jax.ShapeDtypeStruct((M, D), _F32), jax.ShapeDtypeStruct((M, D), _BF)],
        scratch_shapes=[pltpu.VMEM((tm, D), _BF)],
        compiler_params=pltpu.CompilerParams(
            dimension_semantics=("arbitrary",),
            vmem_limit_bytes=_vmem_limit(blocks, _nbytes((tm, D), _BF)),
        ),
        name=name,
    )(h, gates, ucb, o, w_conv_out, w_attn_o, w_out, g)


def _ple_kernel(h_ref, pe_ref, g_ref, wp_ref, wg_ref, out_ref):
    h = h_ref[...]
    gate = jax.nn.sigmoid(_dot(_rms(h, g_ref[...]).astype(_BF), wg_ref[...]))
    emb = _dot(pe_ref[...].astype(_BF), wp_ref[...])
    out_ref[...] = h + emb * gate


def _ple(h, pe, g, w_ple, w_gate, *, tm, name):
    M, D = h.shape
    dp = pe.shape[1]
    row = lambda i: (i, 0)
    const = lambda i: (0, 0)
    blocks = [2 * _nbytes((tm, D), _F32), _nbytes((tm, dp), _F32), _nbytes((dp + D, D), _BF)]
    return pl.pallas_call(
        _ple_kernel,
        grid=(M // tm,),
        in_specs=[
            pl.BlockSpec((tm, D), row),
            pl.BlockSpec((tm, dp), row),
            pl.BlockSpec((1, D), const),
            pl.BlockSpec((dp, D), const),
            pl.BlockSpec((D, D), const),
        ],
        out_specs=pl.BlockSpec((tm, D), row),
        out_shape=jax.ShapeDtypeStruct((M, D), _F32),
        compiler_params=pltpu.CompilerParams(
            dimension_semantics=("arbitrary",),
            vmem_limit_bytes=_vmem_limit(blocks),
        ),
        name=name,
    )(h, pe, g, w_ple, w_gate)


def _row_tile(m, want):
    return want if m % want == 0 else m


def kernel(x_prompt, x_sample, p_prompt, p_sample, state_conv, cache_k, cache_v, rel_table, ffn1_norm, ffn1_wg, ffn1_wu, ffn1_wd, mix_norm, w_in, conv_w, q_norm, k_norm, attn_sink, w_conv_out, w_attn_o, w_out, ffn2_norm, ffn2_wg, ffn2_wu, ffn2_wd, ple_norm, w_ple, w_ple_gate):
    B, T, D = x_prompt.shape
    Bs, S, _ = x_sample.shape
    depth = ffn1_wg.shape[0]
    d_conv = conv_w.shape[2]
    dq, dk = N_HEADS * HEAD_DIM, N_KV * HEAD_DIM
    qkv_col = 3 * d_conv
    gc_col = qkv_col + dq + 2 * dk
    Mp, Ms = B * T, Bs * S
    tm_p = _row_tile(Mp, 512)
    tm_s = _row_tile(Ms, 512)
    tf = 1024
    tn = 512
    cw = 256

    nk_p, nk_s = WINDOW + CHUNK, WINDOW + S
    bias_p = _rel_bias(rel_table, _bucket_index(CHUNK, nk_p), WINDOW // CHUNK + 1, CHUNK, "rel_bias_prompt")
    bias_s = _rel_bias(rel_table, _bucket_index(S, nk_s), 1, 0, "rel_bias_sample")
    lane = jnp.arange(cw) // HEAD_DIM
    ones = (lane[:, None] == lane[None, :]).astype(_BF)

    yp = x_prompt.reshape(Mp, D)
    ys = x_sample.reshape(Ms, D)
    conv_p, k_p, v_p, conv_s, k_s, v_s = [], [], [], [], [], []
    head_to_group_major = lambda h: (h % GROUP) * N_KV + h // GROUP
    for l in range(depth):
        vec = lambda g: g[l].reshape(1, -1)
        h1_s, n_s, f1_wg, f1_wu, f1_wd = _ffn(ys, vec(ffn1_norm), ffn1_wg[l], ffn1_wu[l], ffn1_wd[l],
                                              vec(mix_norm), tm=tm_s, tf=tf // 2, name="ffn1_sample")
        f1 = (vec(ffn1_norm), f1_wg, f1_wu, f1_wd)
        gq = jnp.tile(q_norm[l], N_HEADS).reshape(1, dq)
        gk = jnp.tile(k_norm[l], N_KV).reshape(1, dk)
        sink_p = jnp.repeat(attn_sink[l], CHUNK).reshape(1, N_HEADS * CHUNK)
        sink_s = jnp.repeat(attn_sink[l], S).reshape(1, N_HEADS * S)

        h1, n, w_in_l = _ffn(yp, *f1, vec(mix_norm), tm=tm_p, tf=tf, name="ffn1_prompt", casts=(w_in[l],))
        ucb, ulast, q, k, v, f2_wg, f2_wu, w_co, w_ao_p, w_o = _inproj(
            n, w_in_l, conv_w[l], gq, gk, ones, tm=tm_p, batch=B, name="inproj_prompt",
            casts=(ffn2_wg[l], ffn2_wu[l], w_conv_out[l], (w_attn_o[l], HEAD_DIM, head_to_group_major), w_out[l]))
        o, gates, f2_wd, w_p, w_pg = _attn_prompt(
            q, k, v, bias_p, sink_p, n, w_in_l, gate_col=gc_col, gate_cols=2 * D, gate_block=tn, batch=B,
            ta=tm_p, name="attn_prompt", casts=(ffn2_wd[l], w_ple[l], w_ple_gate[l]))
        h2, xn2 = _merge(h1, gates, ucb, o, w_co, w_ao_p, w_o, vec(ffn2_norm), cw=tn, tm=tm_p, name="merge_prompt")
        f2 = (f2_wg, f2_wu, f2_wd)
        h3, = _ffn(h2, xn2, *f2, None, tm=tm_p, tf=tf, name="ffn2_prompt")
        yp = _ple(h3, p_prompt[l].reshape(Mp, -1), vec(ple_norm), w_p, w_pg, tm=tm_p, name="ple_prompt")
        conv_p.append(ulast[:, 6:8])
        k_p.append(k.reshape(B, T, dk)[:, T - WINDOW:].reshape(B, WINDOW, N_KV, HEAD_DIM))
        v_p.append(v.reshape(B, T, dk)[:, T - WINDOW:].reshape(B, WINDOW, N_KV, HEAD_DIM))

        st = state_conv[l]
        zeros = jnp.zeros((Bs, S - 2, d_conv), _F32)
        e2 = jnp.concatenate([st, zeros], axis=1).reshape(Ms, d_conv)
        e1 = jnp.concatenate([st[:, 1:], zeros, zeros[:, :1]], axis=1).reshape(Ms, d_conv)
        h1, n = h1_s, n_s
        ucb, u, q, k, v = _inproj(n, w_in_l, conv_w[l], gq, gk, ones, tm=tm_s, seg=S, edges=(e1, e2),
                                  name="inproj_sample")
        o = _attn_sample(q, k, v, cache_k[l].reshape(Bs, WINDOW, dk), cache_v[l].reshape(Bs, WINDOW, dk),
                         bias_s, sink_s, nb=4, name="attn_sample")
        gates = _gates(n, w_in_l, gate_col=gc_col, gate_cols=2 * D, gate_block=tn, name="gates_sample")
        h2, xn2 = _merge(h1, gates, ucb, o, w_co, w_ao_p, w_o, vec(ffn2_norm), cw=tn, tm=tm_s, name="merge_sample")
        h3, = _ffn(h2, xn2, *f2, None, tm=tm_s, tf=tf, name="ffn2_sample")
        ys = _ple(h3, p_sample[l].reshape(Ms, -1), vec(ple_norm), w_p, w_pg, tm=tm_s, name="ple_sample")
        conv_s.append(u.reshape(Bs, S, d_conv)[:, S - 2:])
        k_s.append(k.reshape(Bs, S, N_KV, HEAD_DIM))
        v_s.append(v.reshape(Bs, S, N_KV, HEAD_DIM))

    return (yp.reshape(B, T, D), ys.reshape(Bs, S, D), jnp.stack(conv_p), jnp.stack(k_p), jnp.stack(v_p),
            jnp.stack(conv_s), jnp.stack(k_s), jnp.stack(v_s))
```

```python
import functools
import math

import jax
import jax.numpy as jnp
from jax import lax
from jax.experimental import pallas as pl
from jax.experimental.pallas import tpu as pltpu

_BF = jnp.bfloat16
_F32 = jnp.float32

CHUNK = 64
WINDOW = 128
N_HEADS = 16
N_KV = 4
GROUP = N_HEADS // N_KV
HEAD_DIM = 64
N_BUCKETS = 32
MAX_DIST = 128
EPS = 1e-6
LOG2E = math.log2(math.e)
QK_SCALE = HEAD_DIM ** -0.5 * LOG2E

V7X_VMEM_BYTES = 64 * 1024 * 1024
VMEM_INTERNAL_BYTES = 12 * 1024 * 1024


def _vmem_limit(block_bytes, scratch_bytes=0):
    need = 2 * sum(block_bytes) + scratch_bytes + VMEM_INTERNAL_BYTES
    return int(min(need, V7X_VMEM_BYTES - 4 * 1024 * 1024))


def _nbytes(shape, dtype):
    return math.prod(shape) * jnp.dtype(dtype).itemsize


def _dot(a, b):
    return jnp.dot(a, b, preferred_element_type=_F32)


def _rms(x, g):
    ms = jnp.mean(x * x, axis=-1, keepdims=True)
    return x * lax.rsqrt(ms + EPS) * g


BF16_TILE_ROWS = 16


def _cast_specs(arrays, grid):
    steps = math.prod(grid)

    def flat(*ids):
        s = 0
        for k, g in zip(ids, grid):
            s = s * g + k
        return s

    in_specs, out_specs, out_shape, nbytes = [], [], [], 0
    for a in arrays:
        a, rb, out_block = a if isinstance(a, tuple) else (a, None, lambda b: b)
        rows, cols = a.shape
        rb = rb or max(BF16_TILE_ROWS, rows // steps)
        assert rows % rb == 0 and steps % (rows // rb) == 0, (a.shape, steps)
        rep = steps // (rows // rb)
        idx = lambda *ids, rep=rep: (flat(*ids) // rep, 0)
        out_idx = lambda *ids, rep=rep, out_block=out_block: (out_block(flat(*ids) // rep), 0)
        in_specs.append(pl.BlockSpec((rb, cols), idx))
        out_specs.append(pl.BlockSpec((rb, cols), out_idx))
        out_shape.append(jax.ShapeDtypeStruct((rows, cols), _BF))
        nbytes += _nbytes((rb, cols), _F32) + _nbytes((rb, cols), _BF)
    return in_specs, out_specs, out_shape, nbytes


def _cast_args(arrays):
    return [a[0] if isinstance(a, tuple) else a for a in arrays]


def _with_casts(body, n_in, n_out, n_cast):
    def kernel(*refs):
        outs_at = n_in + n_cast
        scratch_at = outs_at + n_out + n_cast
        for src, dst in zip(refs[n_in:outs_at], refs[outs_at + n_out:scratch_at]):
            dst[...] = src[...].astype(_BF)
        body(*refs[:n_in], *refs[outs_at:outs_at + n_out], *refs[scratch_at:])
    return kernel


def _ffn_kernel(*refs, emit_norm, keep_weights, normed_input):
    x_ref, g_ref, wg_ref, wu_ref, wd_ref = refs[:5]
    rest = list(refs[5:])
    g2_ref = rest.pop(0) if emit_norm else None
    out_ref = rest.pop(0)
    n_ref = rest.pop(0) if emit_norm else None
    weight_copies = [rest.pop(0) for _ in range(3)] if keep_weights else []
    j = pl.program_id(1)
    last = pl.num_programs(1) - 1
    if normed_input:
        xn_ref = g_ref
    else:
        xn_ref, = rest

        @pl.when(j == 0)
        def _():
            xn_ref[...] = _rms(x_ref[...], g_ref[...]).astype(_BF)

    wg, wu, wd = wg_ref[...], wu_ref[...], wd_ref[...]
    if keep_weights:
        wg, wu, wd = wg.astype(_BF), wu.astype(_BF), wd.astype(_BF)
        for dst, w in zip(weight_copies, (wg, wu, wd)):
            dst[...] = w
    xn = xn_ref[...]
    gate = _dot(xn, wg)
    up = _dot(xn, wu)
    hid = (jax.nn.silu(gate) * up * 0.5).astype(_BF)
    out_ref[...] = jnp.where(j == 0, x_ref[...], out_ref[...]) + _dot(hid, wd)

    if emit_norm:
        @pl.when(j == last)
        def _():
            n_ref[...] = _rms(out_ref[...], g2_ref[...]).astype(_BF)


def _ffn(x, g, wg, wu, wd, g2, *, tm, tf, name, casts=()):
    M, D = x.shape
    F = wg.shape[1]
    emit_norm = g2 is not None
    keep_weights = wg.dtype == _F32
    normed_input = g.shape[0] == M
    grid = (M // tm, F // tf)
    assert not keep_weights or grid[0] == 1
    row = lambda i, j: (i, 0)
    w_specs = [
        pl.BlockSpec((D, tf), lambda i, j: (0, j)),
        pl.BlockSpec((D, tf), lambda i, j: (0, j)),
        pl.BlockSpec((tf, D), lambda i, j: (j, 0)),
    ]
    g_spec = pl.BlockSpec((tm, D), row) if normed_input else pl.BlockSpec((1, D), lambda i, j: (0, 0))
    in_specs = [pl.BlockSpec((tm, D), row), g_spec] + w_specs
    args = [x, g, wg, wu, wd]
    out_shape = [jax.ShapeDtypeStruct((M, D), _F32)]
    out_specs = [pl.BlockSpec((tm, D), row)]
    blocks = [_nbytes((tm, D), _F32) * 2, 3 * _nbytes((D, tf), wg.dtype)]
    if emit_norm:
        in_specs.append(pl.BlockSpec((1, D), lambda i, j: (0, 0)))
        args.append(g2)
        out_shape.append(jax.ShapeDtypeStruct((M, D), _BF))
        out_specs.append(pl.BlockSpec((tm, D), row))
        blocks.append(_nbytes((tm, D), _BF))
    if keep_weights:
        out_shape += [jax.ShapeDtypeStruct(w.shape, _BF) for w in (wg, wu, wd)]
        out_specs += w_specs
        blocks.append(3 * _nbytes((D, tf), _BF))
    c_in, c_out, c_shape, c_bytes = _cast_specs(casts, grid)
    blocks.append(c_bytes)
    body = functools.partial(_ffn_kernel, emit_norm=emit_norm, keep_weights=keep_weights,
                             normed_input=normed_input)
    res = pl.pallas_call(
        _with_casts(body, len(in_specs), len(out_specs), len(casts)),
        grid=grid,
        in_specs=in_specs + c_in,
        out_specs=out_specs + c_out,
        out_shape=out_shape + c_shape,
        scratch_shapes=[] if normed_input else [pltpu.VMEM((tm, D), _BF)],
        compiler_params=pltpu.CompilerParams(
            dimension_semantics=("arbitrary", "arbitrary"),
            vmem_limit_bytes=_vmem_limit(blocks, _nbytes((tm, D), _BF)),
        ),
        name=name,
    )(*args, *_cast_args(casts))
    return res


def _conv_kernel(*refs, carry_rows, seg, tiles_per_seg, d_conv, cw):
    if carry_rows:
        n_ref, w_ref, cwt_ref, ucb_ref, ulast_ref, carry_ref = refs
    else:
        n_ref, w_ref, cwt_ref, e1_ref, e2_ref, ucb_ref, u_ref = refs
    tm = n_ref.shape[0]
    n = n_ref[...]
    if carry_rows:
        @pl.when(pl.program_id(0) % tiles_per_seg == 0)
        def _():
            carry_ref[...] = jnp.zeros_like(carry_ref)

    for c in range(d_conv // cw):
        lo = c * cw
        cb = _dot(n, w_ref[:, lo:lo + cw])
        cc = _dot(n, w_ref[:, d_conv + lo:d_conv + lo + cw])
        cv = _dot(n, w_ref[:, 2 * d_conv + lo:2 * d_conv + lo + cw])
        u = cc * cv
        r1 = pltpu.roll(u, 1, 0)
        r2 = pltpu.roll(u, 2, 0)
        if carry_rows:
            prev = carry_ref[:, lo:lo + cw]
            rows8 = lax.broadcasted_iota(jnp.int32, (8, cw), 0)
            h1 = jnp.where(rows8 < 1, pltpu.roll(prev, 1, 0), r1[:8])
            h2 = jnp.where(rows8 < 2, pltpu.roll(prev, 2, 0), r2[:8])
            r1 = jnp.concatenate([h1, r1[8:]], axis=0)
            r2 = jnp.concatenate([h2, r2[8:]], axis=0)
            carry_ref[:, lo:lo + cw] = u[tm - 8:]
            ulast_ref[0, :, lo:lo + cw] = u[tm - 8:]
        else:
            pos = lax.broadcasted_iota(jnp.int32, (tm, cw), 0) % seg
            r1 = jnp.where(pos < 1, e1_ref[:, lo:lo + cw], r1)
            r2 = jnp.where(pos < 2, e2_ref[:, lo:lo + cw], r2)
            u_ref[:, lo:lo + cw] = u
        w0 = cwt_ref[0:1, lo:lo + cw]
        w1 = cwt_ref[1:2, lo:lo + cw]
        w2 = cwt_ref[2:3, lo:lo + cw]
        conv = w0 * r2 + w1 * r1 + w2 * u
        ucb_ref[:, lo:lo + cw] = (cb * conv).astype(_BF)


def _qkv_kernel(n_ref, w_ref, gq_ref, gk_ref, ones_ref, q_ref, k_ref, v_ref, *, cw, col0):
    ones = ones_ref[...]
    dq = q_ref.shape[1]
    dk = k_ref.shape[1]
    p = _dot(n_ref[...], w_ref[:, col0:col0 + dq + 2 * dk])
    x = p[:, :dq + dk]
    sq = x * x
    hi = sq.astype(_BF)
    lo = (sq - hi.astype(_F32)).astype(_BF)
    ss = jnp.concatenate(
        [_dot(hi[:, c * cw:(c + 1) * cw], ones) + _dot(lo[:, c * cw:(c + 1) * cw], ones)
         for c in range((dq + dk) // cw)], axis=1)
    xr = x * lax.rsqrt(ss * (1.0 / HEAD_DIM) + EPS)
    q_ref[...] = (xr[:, :dq] * gq_ref[...] * QK_SCALE).astype(_BF)
    k_ref[...] = xr[:, dq:] * gk_ref[...]
    v_ref[...] = p[:, dq + dk:]


def _inproj_kernel(*refs, carry_rows, seg, tiles_per_seg, d_conv, cw):
    if carry_rows:
        (n_ref, w_ref, cwt_ref, gq_ref, gk_ref, ones_ref,
         ucb_ref, ulast_ref, q_ref, k_ref, v_ref, carry_ref) = refs
        conv_refs = (n_ref, w_ref, cwt_ref, ucb_ref, ulast_ref, carry_ref)
    else:
        (n_ref, w_ref, cwt_ref, gq_ref, gk_ref, ones_ref, e1_ref, e2_ref,
         ucb_ref, u_ref, q_ref, k_ref, v_ref) = refs
        conv_refs = (n_ref, w_ref, cwt_ref, e1_ref, e2_ref, ucb_ref, u_ref)
    _conv_kernel(*conv_refs, carry_rows=carry_rows, seg=seg, tiles_per_seg=tiles_per_seg, d_conv=d_conv, cw=cw)
    _qkv_kernel(n_ref, w_ref, gq_ref, gk_ref, ones_ref, q_ref, k_ref, v_ref, cw=cw, col0=3 * d_conv)


def _inproj(n, w_in, conv_w, gq, gk, ones, *, tm, name, batch=None, seg=None, edges=(), casts=()):
    M, D = n.shape
    d_conv = conv_w.shape[1]
    cw = ones.shape[0]
    dq, dk = N_HEADS * HEAD_DIM, N_KV * HEAD_DIM
    wcols = 3 * d_conv + dq + 2 * dk
    carry_rows = batch is not None
    grid = (M // tm,)
    tiles_per_seg = (M // batch) // tm if carry_rows else None
    c_in, c_out, c_shape, c_bytes = _cast_specs(casts, grid)
    const = lambda i: (0, 0)
    row = lambda i: (i, 0)
    in_specs = [
        pl.BlockSpec((tm, D), row),
        pl.BlockSpec((D, wcols), const, pipeline_mode=pl.Buffered(1)),
        pl.BlockSpec((3, d_conv), const),
        pl.BlockSpec((1, dq), const),
        pl.BlockSpec((1, dk), const),
        pl.BlockSpec((cw, cw), const),
    ] + [pl.BlockSpec((tm, d_conv), row) for _ in edges]
    if carry_rows:
        u_spec = pl.BlockSpec((1, 8, d_conv), lambda i: (i // tiles_per_seg, 0, 0))
        u_shape = jax.ShapeDtypeStruct((batch, 8, d_conv), _F32)
        scratch = [pltpu.VMEM((8, d_conv), _F32)]
    else:
        u_spec = pl.BlockSpec((tm, d_conv), row)
        u_shape = jax.ShapeDtypeStruct((M, d_conv), _F32)
        scratch = []
    out_specs = [pl.BlockSpec((tm, d_conv), row), u_spec,
                 pl.BlockSpec((tm, dq), row), pl.BlockSpec((tm, dk), row), pl.BlockSpec((tm, dk), row)]
    out_shape = [jax.ShapeDtypeStruct((M, d_conv), _BF), u_shape, jax.ShapeDtypeStruct((M, dq), _BF),
                 jax.ShapeDtypeStruct((M, dk), _F32), jax.ShapeDtypeStruct((M, dk), _F32)]
    blocks = [_nbytes((tm, D), _BF), _nbytes((D, wcols), _BF) // 2, (1 + len(edges)) * _nbytes((tm, d_conv), _F32),
              _nbytes((tm, d_conv + dq), _BF), 2 * _nbytes((tm, dk), _F32), c_bytes]
    body = functools.partial(_inproj_kernel, carry_rows=carry_rows, seg=seg, tiles_per_seg=tiles_per_seg,
                             d_conv=d_conv, cw=cw)
    return pl.pallas_call(
        _with_casts(body, len(in_specs), len(out_specs), len(casts)),
        grid=grid,
        in_specs=in_specs + c_in,
        out_specs=out_specs + c_out,
        out_shape=out_shape + c_shape,
        scratch_shapes=scratch,
        compiler_params=pltpu.CompilerParams(
            dimension_semantics=("arbitrary",),
            vmem_limit_bytes=_vmem_limit(blocks),
        ),
        name=name,
    )(n, w_in, conv_w, gq, gk, ones, *edges, *_cast_args(casts))


def _bias_kernel(tab_ref, bkt_ref, out_ref, *, mask_step):
    bkt = bkt_ref[...]
    key = lax.broadcasted_iota(jnp.int32, bkt.shape, 1)
    for h in range(out_ref.shape[1]):
        acc = jnp.zeros(bkt.shape, _F32)
        for b in range(tab_ref.shape[0]):
            acc = jnp.where(bkt == b, tab_ref[b, h] * LOG2E, acc)
        for v in range(out_ref.shape[0]):
            out_ref[v, h] = jnp.where(key < v * mask_step, -jnp.inf, acc)


def _rel_bias(table, bucket, n_variants, mask_step, name):
    nq, nk = bucket.shape
    nh = table.shape[1]
    out = pl.pallas_call(
        functools.partial(_bias_kernel, mask_step=mask_step),
        in_specs=[pl.BlockSpec(memory_space=pltpu.SMEM), pl.BlockSpec((nq, nk), lambda: (0, 0))],
        out_specs=pl.BlockSpec((n_variants, nh, nq, nk), lambda: (0, 0, 0, 0)),
        out_shape=jax.ShapeDtypeStruct((n_variants, nh, nq, nk), _F32),
        name=name,
    )(table, bucket)
    return out.transpose(0, 3, 1, 2).reshape(n_variants, nk, nh * nq)


def _t5_bucket(rel):
    nb = N_BUCKETS // 2
    max_exact = nb // 2
    ret = jnp.where(rel > 0, nb, 0)
    n = jnp.abs(rel)
    nf = jnp.maximum(n, 1).astype(jnp.float32)
    large = max_exact + (jnp.log(nf / max_exact) / math.log(MAX_DIST / max_exact) * (nb - max_exact)).astype(jnp.int32)
    large = jnp.minimum(large, nb - 1)
    return ret + jnp.where(n < max_exact, n, large)


def _bucket_index(n_q, n_k):
    i = jnp.arange(n_q)[:, None]
    j = jnp.arange(n_k)[None, :]
    return _t5_bucket(j - WINDOW - i).astype(jnp.int32)


_NT = (((1,), (1,)), ((), ()))
_TN = (((0,), (0,)), ((), ()))


def _sink_softmax_keys_major(lg, sink_t):
    sink2 = sink_t * LOG2E
    m = jnp.maximum(jnp.max(lg, axis=0, keepdims=True), sink2)
    e = jnp.exp2(lg - m)
    den = jnp.sum(e, axis=0, keepdims=True) + jnp.exp2(sink2 - m)
    return e, 1.0 / den


def _attn_chunk_logits(qc, kw, bias_t):
    logits = []
    for kv in range(N_KV):
        qg = jnp.concatenate(
            [qc[:, (kv * GROUP + g) * HEAD_DIM:(kv * GROUP + g + 1) * HEAD_DIM] for g in range(GROUP)], axis=0)
        kk = kw[:, kv * HEAD_DIM:(kv + 1) * HEAD_DIM]
        logits.append(lax.dot_general(kk, qg, _NT, preferred_element_type=_F32))
    return jnp.concatenate(logits, axis=1) + bias_t


def _attn_chunk_output(lg, vw, sink_t):
    gq = lg.shape[1] // N_KV
    e, inv = _sink_softmax_keys_major(lg, sink_t)
    e = e.astype(_BF)
    outs = []
    for kv in range(N_KV):
        vv = vw[:, kv * HEAD_DIM:(kv + 1) * HEAD_DIM]
        cols = slice(kv * gq, (kv + 1) * gq)
        outs.append(lax.dot_general(vv, e[:, cols], _TN, preferred_element_type=_F32) * inv[:, cols])
    return jnp.concatenate(outs, axis=0).T


ATTN_CHUNKS_PER_BLOCK = 8


def _attn_prompt_kernel(*refs, tiles_per_batch, n_gate_blocks):
    q_ref, kp_ref, kc_ref, vp_ref, vc_ref, bias_ref, sink_ref, n_ref = refs[:8]
    wgate_refs = refs[8:8 + n_gate_blocks]
    o_ref, gate_ref, kwin, vwin = refs[8 + n_gate_blocks:]
    ta = q_ref.shape[0]
    nk = WINDOW + CHUNK
    dkv = N_KV * HEAD_DIM
    rows = ATTN_CHUNKS_PER_BLOCK * CHUNK
    gate_cols = gate_ref.shape[1] // n_gate_blocks
    kwin[0:WINDOW, :] = kp_ref[...].astype(_BF)
    kwin[WINDOW:, :] = kc_ref[...].astype(_BF)
    vwin[0:WINDOW, :] = vp_ref[...].astype(_BF)
    vwin[WINDOW:, :] = vc_ref[...].astype(_BF)
    tile_pos = (pl.program_id(0) % tiles_per_batch) * ta

    def logits(r):
        n_before = jnp.maximum(0, (WINDOW - (tile_pos + r)) // CHUNK)
        return _attn_chunk_logits(q_ref[pl.ds(r, CHUNK), :], kwin[pl.ds(r, nk), :], bias_ref[n_before])

    def finish(r, lg):
        o = _attn_chunk_output(lg, vwin[pl.ds(r, nk), :], sink_ref[...])
        for g in range(GROUP):
            o_ref[pl.ds(r, CHUNK), g * dkv:(g + 1) * dkv] = o[g * CHUNK:(g + 1) * CHUNK].astype(_BF)

    def block(b, carry):
        r0 = pl.multiple_of(b * rows, rows)
        n = n_ref[pl.ds(r0, rows), :]
        chunk_rows = [pl.multiple_of(r0 + c * CHUNK, CHUNK) for c in range(ATTN_CHUNKS_PER_BLOCK)]
        lg = logits(chunk_rows[0])
        for c in range(max(ATTN_CHUNKS_PER_BLOCK, n_gate_blocks)):
            lg_next = logits(chunk_rows[c + 1]) if c + 1 < ATTN_CHUNKS_PER_BLOCK else None
            if c < n_gate_blocks:
                gate = jax.nn.sigmoid(_dot(n, wgate_refs[c][...]))
                gate_ref[pl.ds(r0, rows), c * gate_cols:(c + 1) * gate_cols] = gate.astype(_BF)
            if c < ATTN_CHUNKS_PER_BLOCK:
                finish(chunk_rows[c], lg)
            lg = lg_next
        return carry

    lax.fori_loop(0, ta // rows, block, 0)


def _attn_prompt(q, k, v, bias, sink, n, w_in, *, gate_col, gate_cols, gate_block, batch, ta, name, casts=()):
    M, dq = q.shape
    dk = k.shape[1]
    D = n.shape[1]
    n_gate_blocks = gate_cols // gate_block
    tiles_per_batch = (M // batch) // ta
    wb = ta // WINDOW
    nk = WINDOW + CHUNK

    def prev(i):
        return (jnp.where(i % tiles_per_batch == 0, i * wb, i * wb - 1), 0)

    row = lambda i: (i, 0)
    grid = (M // ta,)
    c_in, c_out, c_shape, c_bytes = _cast_specs(casts, grid)
    blocks = [_nbytes((ta, dq), _BF) * 2, 2 * _nbytes((WINDOW + ta, dk), _F32),
              _nbytes(bias.shape, _F32), _nbytes((8, N_HEADS * CHUNK), _F32), c_bytes,
              _nbytes((ta, D + gate_cols), _BF), _nbytes((D, gate_cols), _BF) // 2]
    gate_specs = [pl.BlockSpec((D, gate_block), lambda i, b=gate_col // gate_block + c: (0, b),
                               pipeline_mode=pl.Buffered(1)) for c in range(n_gate_blocks)]
    body = functools.partial(_attn_prompt_kernel, tiles_per_batch=tiles_per_batch, n_gate_blocks=n_gate_blocks)
    return pl.pallas_call(
        _with_casts(body, 8 + n_gate_blocks, 2, len(casts)),
        grid=grid,
        in_specs=[
            pl.BlockSpec((ta, dq), row),
            pl.BlockSpec((WINDOW, dk), prev),
            pl.BlockSpec((ta, dk), row),
            pl.BlockSpec((WINDOW, dk), prev),
            pl.BlockSpec((ta, dk), row),
            pl.BlockSpec(bias.shape, lambda i: (0, 0, 0)),
            pl.BlockSpec((1, N_HEADS * CHUNK), lambda i: (0, 0)),
            pl.BlockSpec((ta, D), row),
        ] + gate_specs + c_in,
        out_specs=[pl.BlockSpec((ta, dq), row), pl.BlockSpec((ta, gate_cols), row)] + c_out,
        out_shape=[jax.ShapeDtypeStruct((M, dq), _BF), jax.ShapeDtypeStruct((M, gate_cols), _BF)] + c_shape,
        scratch_shapes=[pltpu.VMEM((WINDOW + ta, dk), _BF), pltpu.VMEM((WINDOW + ta, dk), _BF)],
        compiler_params=pltpu.CompilerParams(
            dimension_semantics=("arbitrary",),
            vmem_limit_bytes=_vmem_limit(blocks, 2 * _nbytes((WINDOW + ta, dk), _BF)),
        ),
        name=name,
    )(q, k, k, v, v, bias, sink, n, *([w_in] * n_gate_blocks), *_cast_args(casts))


LANES = 128


def _attn_sample_kernel(q_ref, ck_ref, cv_ref, k_ref, v_ref, bias_ref, sink_ref, o_ref, *, nb, s):
    dkv = N_KV * HEAD_DIM
    gs = GROUP * s
    assert LANES == 2 * HEAD_DIM and dkv == 2 * LANES
    for b in range(nb):
        rows = slice(b * s, (b + 1) * s)
        kw = jnp.concatenate([ck_ref[b].astype(_BF), k_ref[rows, :].astype(_BF)], axis=0)
        vw = jnp.concatenate([cv_ref[b].astype(_BF), v_ref[rows, :].astype(_BF)], axis=0)
        qf = q_ref[rows, :].astype(_F32)
        blocks = []
        for kv in range(N_KV):
            for g in range(GROUP):
                h = kv * GROUP + g
                pair = qf[:, (h // 2) * LANES:(h // 2 + 1) * LANES]
                if h % 2 != kv % 2:
                    pair = pltpu.roll(pair, HEAD_DIM, 1)
                half = lax.broadcasted_iota(jnp.int32, pair.shape, 1) // HEAD_DIM
                piece = jnp.where(half == kv % 2, pair, 0.0)
                zero = jnp.zeros_like(piece)
                blocks.append(jnp.concatenate([piece, zero] if kv < 2 else [zero, piece], axis=1))
        q_bd = jnp.concatenate(blocks, axis=0).astype(_BF)
        lg = lax.dot_general(kw, q_bd, _NT, preferred_element_type=_F32) + bias_ref[0]
        e, inv = _sink_softmax_keys_major(lg, sink_ref[...])
        z = lax.dot_general((e * inv).astype(_BF), vw, _TN, preferred_element_type=_F32)
        row_kv = lax.broadcasted_iota(jnp.int32, z.shape, 0) // gs
        col_kv = lax.broadcasted_iota(jnp.int32, z.shape, 1) // HEAD_DIM
        z = jnp.where(row_kv == col_kv, z, 0.0)
        r = z[0:gs]
        for kv in range(1, N_KV):
            r = r + z[kv * gs:(kv + 1) * gs]
        for g in range(GROUP):
            o_ref[rows, g * dkv:(g + 1) * dkv] = r[g * s:(g + 1) * s].astype(_BF)


def _attn_sample(q, k, v, cache_k, cache_v, bias, sink, *, nb, name):
    B, W, dk = cache_k.shape
    M, dq = q.shape
    S = M // B
    row = lambda b: (b, 0)
    return pl.pallas_call(
        functools.partial(_attn_sample_kernel, nb=nb, s=S),
        grid=(B // nb,),
        in_specs=[
            pl.BlockSpec((nb * S, dq), row),
            pl.BlockSpec((nb, W, dk), lambda b: (b, 0, 0)),
            pl.BlockSpec((nb, W, dk), lambda b: (b, 0, 0)),
            pl.BlockSpec((nb * S, dk), row),
            pl.BlockSpec((nb * S, dk), row),
            pl.BlockSpec(bias.shape, lambda b: (0, 0, 0)),
            pl.BlockSpec(sink.shape, lambda b: (0, 0)),
        ],
        out_specs=pl.BlockSpec((nb * S, dq), row),
        out_shape=jax.ShapeDtypeStruct((M, dq), _BF),
        compiler_params=pltpu.CompilerParams(dimension_semantics=("arbitrary",)),
        name=name,
    )(q, cache_k, cache_v, k, v, bias, sink)


def _gates_kernel(n_ref, w_ref, gate_ref):
    gate_ref[...] = jax.nn.sigmoid(_dot(n_ref[...], w_ref[...])).astype(_BF)


def _gates(n, w_in, *, gate_col, gate_cols, gate_block, name):
    M, D = n.shape
    first = gate_col // gate_block
    return pl.pallas_call(
        _gates_kernel,
        grid=(gate_cols // gate_block,),
        in_specs=[pl.BlockSpec((M, D), lambda c: (0, 0)), pl.BlockSpec((D, gate_block), lambda c: (0, first + c))],
        out_specs=pl.BlockSpec((M, gate_block), lambda c: (0, c)),
        out_shape=jax.ShapeDtypeStruct((M, gate_cols), _BF),
        compiler_params=pltpu.CompilerParams(
            dimension_semantics=("arbitrary",),
            vmem_limit_bytes=_vmem_limit([_nbytes((M, D + gate_block), _BF), _nbytes((D, gate_block), _BF)]),
        ),
        name=name,
    )(n, w_in)


def _merge_kernel(h_ref, gate_ref, ucb_ref, o_ref, wco_ref, wao_ref, wout_ref, g_ref, out_ref, xn_ref, mix_ref,
                  *, n_chunks):
    ucb, o = ucb_ref[...], o_ref[...]
    D = out_ref.shape[1]
    cw = D // n_chunks
    for c in range(n_chunks):
        cols = slice(c * cw, (c + 1) * cw)
        y_conv = _dot(ucb, wco_ref[:, cols])
        y_attn = _dot(o, wao_ref[:, cols])
        g_conv = gate_ref[:, cols].astype(_F32)
        g_attn = gate_ref[:, D + c * cw:D + (c + 1) * cw].astype(_F32)
        mix_ref[:, cols] = (g_conv * y_conv + g_attn * y_attn).astype(_BF)
    h = h_ref[...] + _dot(mix_ref[...], wout_ref[...])
    out_ref[...] = h
    xn_ref[...] = _rms(h, g_ref[...]).astype(_BF)


def _merge(h, gates, ucb, o, w_conv_out, w_attn_o, w_out, g, *, cw, tm, name):
    M, D = h.shape
    dc, da = ucb.shape[1], o.shape[1]
    row = lambda i: (i, 0)
    once = pl.Buffered(1)
    const = lambda i: (0, 0)
    blocks = [2 * _nbytes((tm, D), _F32), _nbytes((tm, 3 * D + dc + da), _BF), _nbytes((dc + da + D, D), _BF) // 2]
    return pl.pallas_call(
        functools.partial(_merge_kernel, n_chunks=D // cw),
        grid=(M // tm,),
        in_specs=[
            pl.BlockSpec((tm, D), row),
            pl.BlockSpec((tm, 2 * D), row),
            pl.BlockSpec((tm, dc), row),
            pl.BlockSpec((tm, da), row),
            pl.BlockSpec((dc, D), const, pipeline_mode=once),
            pl.BlockSpec((da, D), const, pipeline_mode=once),
            pl.BlockSpec((D, D), const, pipeline_mode=once),
            pl.BlockSpec((1, D), const),
        ],
        out_specs=[pl.BlockSpec((tm, D), row), pl.BlockSpec((tm, D), row)],
        out_shape=[jax.ShapeDtypeStruct((M, D), _F32), jax.ShapeDtypeStruct((M, D), _BF)],
        scratch_shapes=[pltpu.VMEM((tm, D), _BF)],
        compiler_params=pltpu.CompilerParams(
            dimension_semantics=("arbitrary",),
            vmem_limit_bytes=_vmem_limit(blocks, _nbytes((tm, D), _BF)),
        ),
        name=name,
    )(h, gates, ucb, o, w_conv_out, w_attn_o, w_out, g)


def _ple_kernel(h_ref, pe_ref, g_ref, wp_ref, wg_ref, out_ref):
    h = h_ref[...]
    gate = jax.nn.sigmoid(_dot(_rms(h, g_ref[...]).astype(_BF), wg_ref[...]))
    emb = _dot(pe_ref[...].astype(_BF), wp_ref[...])
    out_ref[...] = h + emb * gate


def _ple(h, pe, g, w_ple, w_gate, *, tm, name):
    M, D = h.shape
    dp = pe.shape[1]
    row = lambda i: (i, 0)
    const = lambda i: (0, 0)
    blocks = [2 * _nbytes((tm, D), _F32), _nbytes((tm, dp), _F32), _nbytes((dp + D, D), _BF)]
    return pl.pallas_call(
        _ple_kernel,
        grid=(M // tm,),
        in_specs=[
            pl.BlockSpec((tm, D), row),
            pl.BlockSpec((tm, dp), row),
            pl.BlockSpec((1, D), const),
            pl.BlockSpec((dp, D), const),
            pl.BlockSpec((D, D), const),
        ],
        out_specs=pl.BlockSpec((tm, D), row),
        out_shape=jax.ShapeDtypeStruct((M, D), _F32),
        compiler_params=pltpu.CompilerParams(
            dimension_semantics=("arbitrary",),
            vmem_limit_bytes=_vmem_limit(blocks),
        ),
        name=name,
    )(h, pe, g, w_ple, w_gate)


def _row_tile(m, want):
    return want if m % want == 0 else m


def kernel(x_prompt, x_sample, p_prompt, p_sample, state_conv, cache_k, cache_v, rel_table, ffn1_norm, ffn1_wg, ffn1_wu, ffn1_wd, mix_norm, w_in, conv_w, q_norm, k_norm, attn_sink, w_conv_out, w_attn_o, w_out, ffn2_norm, ffn2_wg, ffn2_wu, ffn2_wd, ple_norm, w_ple, w_ple_gate):
    B, T, D = x_prompt.shape
    Bs, S, _ = x_sample.shape
    depth = ffn1_wg.shape[0]
    d_conv = conv_w.shape[2]
    dq, dk = N_HEADS * HEAD_DIM, N_KV * HEAD_DIM
    qkv_col = 3 * d_conv
    gc_col = qkv_col + dq + 2 * dk
    Mp, Ms = B * T, Bs * S
    tm_p = _row_tile(Mp, 512)
    tm_s = _row_tile(Ms, 512)
    tf = 1024
    tn = 512
    cw = 256

    nk_p, nk_s = WINDOW + CHUNK, WINDOW + S
    bias_p = _rel_bias(rel_table, _bucket_index(CHUNK, nk_p), WINDOW // CHUNK + 1, CHUNK, "rel_bias_prompt")
    bias_s = _rel_bias(rel_table, _bucket_index(S, nk_s), 1, 0, "rel_bias_sample")
    lane = jnp.arange(cw) // HEAD_DIM
    ones = (lane[:, None] == lane[None, :]).astype(_BF)

    yp = x_prompt.reshape(Mp, D)
    ys = x_sample.reshape(Ms, D)
    conv_p, k_p, v_p, conv_s, k_s, v_s = [], [], [], [], [], []
    head_to_group_major = lambda h: (h % GROUP) * N_KV + h // GROUP
    for l in range(depth):
        vec = lambda g: g[l].reshape(1, -1)
        h1_s, n_s, f1_wg, f1_wu, f1_wd = _ffn(ys, vec(ffn1_norm), ffn1_wg[l], ffn1_wu[l], ffn1_wd[l],
                                              vec(mix_norm), tm=tm_s, tf=tf // 2, name="ffn1_sample")
        f1 = (vec(ffn1_norm), f1_wg, f1_wu, f1_wd)
        gq = jnp.tile(q_norm[l], N_HEADS).reshape(1, dq)
        gk = jnp.tile(k_norm[l], N_KV).reshape(1, dk)
        sink_p = jnp.repeat(attn_sink[l], CHUNK).reshape(1, N_HEADS * CHUNK)
        sink_s = jnp.repeat(attn_sink[l], S).reshape(1, N_HEADS * S)

        h1, n, w_in_l = _ffn(yp, *f1, vec(mix_norm), tm=tm_p, tf=tf, name="ffn1_prompt",
                             casts=((w_in[l], 4 * BF16_TILE_ROWS, lambda b: b),))
        ucb, ulast, q, k, v, f2_wg, f2_wu, w_co, w_ao_p, w_o = _inproj(
            n, w_in_l, conv_w[l], gq, gk, ones, tm=tm_p, batch=B, name="inproj_prompt",
            casts=(ffn2_wg[l], ffn2_wu[l], w_conv_out[l], (w_attn_o[l], HEAD_DIM, head_to_group_major), w_out[l]))
        o, gates, f2_wd, w_p, w_pg = _attn_prompt(
            q, k, v, bias_p, sink_p, n, w_in_l, gate_col=gc_col, gate_cols=2 * D, gate_block=tn, batch=B,
            ta=tm_p, name="attn_prompt", casts=(ffn2_wd[l], w_ple[l], w_ple_gate[l]))
        h2, xn2 = _merge(h1, gates, ucb, o, w_co, w_ao_p, w_o, vec(ffn2_norm), cw=tn, tm=tm_p, name="merge_prompt")
        f2 = (f2_wg, f2_wu, f2_wd)
        h3, = _ffn(h2, xn2, *f2, None, tm=tm_p, tf=tf, name="ffn2_prompt")
        yp = _ple(h3, p_prompt[l].reshape(Mp, -1), vec(ple_norm), w_p, w_pg, tm=tm_p, name="ple_prompt")
        conv_p.append(ulast[:, 6:8])
        k_p.append(k.reshape(B, T, dk)[:, T - WINDOW:].reshape(B, WINDOW, N_KV, HEAD_DIM))
        v_p.append(v.reshape(B, T, dk)[:, T - WINDOW:].reshape(B, WINDOW, N_KV, HEAD_DIM))

        st = state_conv[l]
        zeros = jnp.zeros((Bs, S - 2, d_conv), _F32)
        e2 = jnp.concatenate([st, zeros], axis=1).reshape(Ms, d_conv)
        e1 = jnp.concatenate([st[:, 1:], zeros, zeros[:, :1]], axis=1).reshape(Ms, d_conv)
        h1, n = h1_s, n_s
        ucb, u, q, k, v = _inproj(n, w_in_l, conv_w[l], gq, gk, ones, tm=tm_s, seg=S, edges=(e1, e2),
                                  name="inproj_sample")
        o = _attn_sample(q, k, v, cache_k[l].reshape(Bs, WINDOW, dk), cache_v[l].reshape(Bs, WINDOW, dk),
                         bias_s, sink_s, nb=4, name="attn_sample")
        gates = _gates(n, w_in_l, gate_col=gc_col, gate_cols=2 * D, gate_block=tn, name="gates_sample")
        h2, xn2 = _merge(h1, gates, ucb, o, w_co, w_ao_p, w_o, vec(ffn2_norm), cw=tn, tm=tm_s, name="merge_sample")
        h3, = _ffn(h2, xn2, *f2, None, tm=tm_s, tf=tf, name="ffn2_sample")
        ys = _ple(h3, p_sample[l].reshape(Ms, -1), vec(ple_norm), w_p, w_pg, tm=tm_s, name="ple_sample")
        conv_s.append(u.reshape(Bs, S, d_conv)[:, S - 2:])
        k_s.append(k.reshape(Bs, S, N_KV, HEAD_DIM))
        v_s.append(v.reshape(Bs, S, N_KV, HEAD_DIM))

    return (yp.reshape(B, T, D), ys.reshape(Bs, S, D), jnp.stack(conv_p), jnp.stack(k_p), jnp.stack(v_p),
            jnp.stack(conv_s), jnp.stack(k_s), jnp.stack(v_s))
```

```python
import functools
import math

import jax
import jax.numpy as jnp
from jax import lax
from jax.experimental import pallas as pl
from jax.experimental.pallas import tpu as pltpu

_BF = jnp.bfloat16
_F32 = jnp.float32

CHUNK = 64
WINDOW = 128
N_HEADS = 16
N_KV = 4
GROUP = N_HEADS // N_KV
HEAD_DIM = 64
N_BUCKETS = 32
MAX_DIST = 128
EPS = 1e-6
LOG2E = math.log2(math.e)
QK_SCALE = HEAD_DIM ** -0.5 * LOG2E

V7X_VMEM_BYTES = 64 * 1024 * 1024
VMEM_INTERNAL_BYTES = 12 * 1024 * 1024


def _vmem_limit(block_bytes, scratch_bytes=0):
    need = 2 * sum(block_bytes) + scratch_bytes + VMEM_INTERNAL_BYTES
    return int(min(need, V7X_VMEM_BYTES - 4 * 1024 * 1024))


def _nbytes(shape, dtype):
    return math.prod(shape) * jnp.dtype(dtype).itemsize


def _dot(a, b):
    return jnp.dot(a, b, preferred_element_type=_F32)


def _rms(x, g):
    ms = jnp.mean(x * x, axis=-1, keepdims=True)
    return x * lax.rsqrt(ms + EPS) * g


BF16_TILE_ROWS = 16


def _cast_specs(arrays, grid):
    steps = math.prod(grid)

    def flat(*ids):
        s = 0
        for k, g in zip(ids, grid):
            s = s * g + k
        return s

    in_specs, out_specs, out_shape, nbytes = [], [], [], 0
    for a in arrays:
        a, rb, out_block = a if isinstance(a, tuple) else (a, None, lambda b: b)
        rows, cols = a.shape
        rb = rb or max(BF16_TILE_ROWS, rows // steps)
        assert rows % rb == 0 and steps % (rows // rb) == 0, (a.shape, steps)
        rep = steps // (rows // rb)
        idx = lambda *ids, rep=rep: (flat(*ids) // rep, 0)
        out_idx = lambda *ids, rep=rep, out_block=out_block: (out_block(flat(*ids) // rep), 0)
        in_specs.append(pl.BlockSpec((rb, cols), idx))
        out_specs.append(pl.BlockSpec((rb, cols), out_idx))
        out_shape.append(jax.ShapeDtypeStruct((rows, cols), _BF))
        nbytes += _nbytes((rb, cols), _F32) + _nbytes((rb, cols), _BF)
    return in_specs, out_specs, out_shape, nbytes


def _cast_args(arrays):
    return [a[0] if isinstance(a, tuple) else a for a in arrays]


def _with_casts(body, n_in, n_out, n_cast):
    def kernel(*refs):
        outs_at = n_in + n_cast
        scratch_at = outs_at + n_out + n_cast
        for src, dst in zip(refs[n_in:outs_at], refs[outs_at + n_out:scratch_at]):
            dst[...] = src[...].astype(_BF)
        body(*refs[:n_in], *refs[outs_at:outs_at + n_out], *refs[scratch_at:])
    return kernel


def _ffn_kernel(*refs, emit_norm, keep_weights, normed_input):
    x_ref, g_ref, wg_ref, wu_ref, wd_ref = refs[:5]
    rest = list(refs[5:])
    g2_ref = rest.pop(0) if emit_norm else None
    out_ref = rest.pop(0)
    n_ref = rest.pop(0) if emit_norm else None
    weight_copies = [rest.pop(0) for _ in range(3)] if keep_weights else []
    j = pl.program_id(1)
    last = pl.num_programs(1) - 1
    if normed_input:
        xn_ref = g_ref
    else:
        xn_ref, = rest

        @pl.when(j == 0)
        def _():
            xn_ref[...] = _rms(x_ref[...], g_ref[...]).astype(_BF)

    wg, wu, wd = wg_ref[...], wu_ref[...], wd_ref[...]
    if keep_weights:
        wg, wu, wd = wg.astype(_BF), wu.astype(_BF), wd.astype(_BF)
        for dst, w in zip(weight_copies, (wg, wu, wd)):
            dst[...] = w
    xn = xn_ref[...]
    gate = _dot(xn, wg)
    up = _dot(xn, wu)
    hid = (jax.nn.silu(gate) * up * 0.5).astype(_BF)
    out_ref[...] = jnp.where(j == 0, x_ref[...], out_ref[...]) + _dot(hid, wd)

    if emit_norm:
        @pl.when(j == last)
        def _():
            n_ref[...] = _rms(out_ref[...], g2_ref[...]).astype(_BF)


def _ffn(x, g, wg, wu, wd, g2, *, tm, tf, name, casts=()):
    M, D = x.shape
    F = wg.shape[1]
    emit_norm = g2 is not None
    keep_weights = wg.dtype == _F32
    normed_input = g.shape[0] == M
    grid = (M // tm, F // tf)
    assert not keep_weights or grid[0] == 1
    row = lambda i, j: (i, 0)
    w_specs = [
        pl.BlockSpec((D, tf), lambda i, j: (0, j)),
        pl.BlockSpec((D, tf), lambda i, j: (0, j)),
        pl.BlockSpec((tf, D), lambda i, j: (j, 0)),
    ]
    g_spec = pl.BlockSpec((tm, D), row) if normed_input else pl.BlockSpec((1, D), lambda i, j: (0, 0))
    in_specs = [pl.BlockSpec((tm, D), row), g_spec] + w_specs
    args = [x, g, wg, wu, wd]
    out_shape = [jax.ShapeDtypeStruct((M, D), _F32)]
    out_specs = [pl.BlockSpec((tm, D), row)]
    blocks = [_nbytes((tm, D), _F32) * 2, 3 * _nbytes((D, tf), wg.dtype)]
    if emit_norm:
        in_specs.append(pl.BlockSpec((1, D), lambda i, j: (0, 0)))
        args.append(g2)
        out_shape.append(jax.ShapeDtypeStruct((M, D), _BF))
        out_specs.append(pl.BlockSpec((tm, D), row))
        blocks.append(_nbytes((tm, D), _BF))
    if keep_weights:
        out_shape += [jax.ShapeDtypeStruct(w.shape, _BF) for w in (wg, wu, wd)]
        out_specs += w_specs
        blocks.append(3 * _nbytes((D, tf), _BF))
    c_in, c_out, c_shape, c_bytes = _cast_specs(casts, grid)
    blocks.append(c_bytes)
    body = functools.partial(_ffn_kernel, emit_norm=emit_norm, keep_weights=keep_weights,
                             normed_input=normed_input)
    res = pl.pallas_call(
        _with_casts(body, len(in_specs), len(out_specs), len(casts)),
        grid=grid,
        in_specs=in_specs + c_in,
        out_specs=out_specs + c_out,
        out_shape=out_shape + c_shape,
        scratch_shapes=[] if normed_input else [pltpu.VMEM((tm, D), _BF)],
        compiler_params=pltpu.CompilerParams(
            dimension_semantics=("arbitrary", "arbitrary"),
            vmem_limit_bytes=_vmem_limit(blocks, _nbytes((tm, D), _BF)),
        ),
        name=name,
    )(*args, *_cast_args(casts))
    return res


def _conv_kernel(*refs, carry_rows, seg, tiles_per_seg, d_conv, cw):
    if carry_rows:
        n_ref, w_ref, cwt_ref, ucb_ref, ulast_ref, carry_ref = refs
    else:
        n_ref, w_ref, cwt_ref, e1_ref, e2_ref, ucb_ref, u_ref = refs
    tm = n_ref.shape[0]
    n = n_ref[...]
    if carry_rows:
        @pl.when(pl.program_id(0) % tiles_per_seg == 0)
        def _():
            carry_ref[...] = jnp.zeros_like(carry_ref)

    for c in range(d_conv // cw):
        lo = c * cw
        cb = _dot(n, w_ref[:, lo:lo + cw])
        cc = _dot(n, w_ref[:, d_conv + lo:d_conv + lo + cw])
        cv = _dot(n, w_ref[:, 2 * d_conv + lo:2 * d_conv + lo + cw])
        u = cc * cv
        r1 = pltpu.roll(u, 1, 0)
        r2 = pltpu.roll(u, 2, 0)
        if carry_rows:
            prev = carry_ref[:, lo:lo + cw]
            rows8 = lax.broadcasted_iota(jnp.int32, (8, cw), 0)
            h1 = jnp.where(rows8 < 1, pltpu.roll(prev, 1, 0), r1[:8])
            h2 = jnp.where(rows8 < 2, pltpu.roll(prev, 2, 0), r2[:8])
            r1 = jnp.concatenate([h1, r1[8:]], axis=0)
            r2 = jnp.concatenate([h2, r2[8:]], axis=0)
            carry_ref[:, lo:lo + cw] = u[tm - 8:]
            ulast_ref[0, :, lo:lo + cw] = u[tm - 8:]
        else:
            pos = lax.broadcasted_iota(jnp.int32, (tm, cw), 0) % seg
            r1 = jnp.where(pos < 1, e1_ref[:, lo:lo + cw], r1)
            r2 = jnp.where(pos < 2, e2_ref[:, lo:lo + cw], r2)
            u_ref[:, lo:lo + cw] = u
        w0 = cwt_ref[0:1, lo:lo + cw]
        w1 = cwt_ref[1:2, lo:lo + cw]
        w2 = cwt_ref[2:3, lo:lo + cw]
        conv = w0 * r2 + w1 * r1 + w2 * u
        ucb_ref[:, lo:lo + cw] = (cb * conv).astype(_BF)


def _qkv_kernel(n_ref, w_ref, gq_ref, gk_ref, ones_ref, q_ref, k_ref, v_ref, *, cw, col0):
    ones = ones_ref[...]
    dq = q_ref.shape[1]
    dk = k_ref.shape[1]
    p = _dot(n_ref[...], w_ref[:, col0:col0 + dq + 2 * dk])
    x = p[:, :dq + dk]
    sq = x * x
    hi = sq.astype(_BF)
    lo = (sq - hi.astype(_F32)).astype(_BF)
    ss = jnp.concatenate(
        [_dot(hi[:, c * cw:(c + 1) * cw], ones) + _dot(lo[:, c * cw:(c + 1) * cw], ones)
         for c in range((dq + dk) // cw)], axis=1)
    xr = x * lax.rsqrt(ss * (1.0 / HEAD_DIM) + EPS)
    q_ref[...] = (xr[:, :dq] * gq_ref[...] * QK_SCALE).astype(_BF)
    k_ref[...] = xr[:, dq:] * gk_ref[...]
    v_ref[...] = p[:, dq + dk:]


def _inproj_kernel(*refs, carry_rows, seg, tiles_per_seg, d_conv, cw):
    if carry_rows:
        (n_ref, w_ref, cwt_ref, gq_ref, gk_ref, ones_ref,
         ucb_ref, ulast_ref, q_ref, k_ref, v_ref, carry_ref) = refs
        conv_refs = (n_ref, w_ref, cwt_ref, ucb_ref, ulast_ref, carry_ref)
    else:
        (n_ref, w_ref, cwt_ref, gq_ref, gk_ref, ones_ref, e1_ref, e2_ref,
         ucb_ref, u_ref, q_ref, k_ref, v_ref) = refs
        conv_refs = (n_ref, w_ref, cwt_ref, e1_ref, e2_ref, ucb_ref, u_ref)
    _conv_kernel(*conv_refs, carry_rows=carry_rows, seg=seg, tiles_per_seg=tiles_per_seg, d_conv=d_conv, cw=cw)
    _qkv_kernel(n_ref, w_ref, gq_ref, gk_ref, ones_ref, q_ref, k_ref, v_ref, cw=cw, col0=3 * d_conv)


def _inproj(n, w_in, conv_w, gq, gk, ones, *, tm, name, batch=None, seg=None, edges=(), casts=()):
    M, D = n.shape
    d_conv = conv_w.shape[1]
    cw = ones.shape[0]
    dq, dk = N_HEADS * HEAD_DIM, N_KV * HEAD_DIM
    wcols = 3 * d_conv + dq + 2 * dk
    carry_rows = batch is not None
    grid = (M // tm,)
    tiles_per_seg = (M // batch) // tm if carry_rows else None
    c_in, c_out, c_shape, c_bytes = _cast_specs(casts, grid)
    const = lambda i: (0, 0)
    row = lambda i: (i, 0)
    in_specs = [
        pl.BlockSpec((tm, D), row),
        pl.BlockSpec((D, wcols), const, pipeline_mode=pl.Buffered(1)),
        pl.BlockSpec((3, d_conv), const),
        pl.BlockSpec((1, dq), const),
        pl.BlockSpec((1, dk), const),
        pl.BlockSpec((cw, cw), const),
    ] + [pl.BlockSpec((tm, d_conv), row) for _ in edges]
    if carry_rows:
        u_spec = pl.BlockSpec((1, 8, d_conv), lambda i: (i // tiles_per_seg, 0, 0))
        u_shape = jax.ShapeDtypeStruct((batch, 8, d_conv), _F32)
        scratch = [pltpu.VMEM((8, d_conv), _F32)]
    else:
        u_spec = pl.BlockSpec((tm, d_conv), row)
        u_shape = jax.ShapeDtypeStruct((M, d_conv), _F32)
        scratch = []
    out_specs = [pl.BlockSpec((tm, d_conv), row), u_spec,
                 pl.BlockSpec((tm, dq), row), pl.BlockSpec((tm, dk), row), pl.BlockSpec((tm, dk), row)]
    out_shape = [jax.ShapeDtypeStruct((M, d_conv), _BF), u_shape, jax.ShapeDtypeStruct((M, dq), _BF),
                 jax.ShapeDtypeStruct((M, dk), _F32), jax.ShapeDtypeStruct((M, dk), _F32)]
    blocks = [_nbytes((tm, D), _BF), _nbytes((D, wcols), _BF) // 2, (1 + len(edges)) * _nbytes((tm, d_conv), _F32),
              _nbytes((tm, d_conv + dq), _BF), 2 * _nbytes((tm, dk), _F32), c_bytes]
    body = functools.partial(_inproj_kernel, carry_rows=carry_rows, seg=seg, tiles_per_seg=tiles_per_seg,
                             d_conv=d_conv, cw=cw)
    return pl.pallas_call(
        _with_casts(body, len(in_specs), len(out_specs), len(casts)),
        grid=grid,
        in_specs=in_specs + c_in,
        out_specs=out_specs + c_out,
        out_shape=out_shape + c_shape,
        scratch_shapes=scratch,
        compiler_params=pltpu.CompilerParams(
            dimension_semantics=("arbitrary",),
            vmem_limit_bytes=_vmem_limit(blocks),
        ),
        name=name,
    )(n, w_in, conv_w, gq, gk, ones, *edges, *_cast_args(casts))


def _bias_kernel(tab_ref, bkt_ref, out_ref, *, mask_step):
    bkt = bkt_ref[...]
    key = lax.broadcasted_iota(jnp.int32, bkt.shape, 1)
    for h in range(out_ref.shape[1]):
        acc = jnp.zeros(bkt.shape, _F32)
        for b in range(tab_ref.shape[0]):
            acc = jnp.where(bkt == b, tab_ref[b, h] * LOG2E, acc)
        for v in range(out_ref.shape[0]):
            out_ref[v, h] = jnp.where(key < v * mask_step, -jnp.inf, acc)


def _rel_bias(table, bucket, n_variants, mask_step, name):
    nq, nk = bucket.shape
    nh = table.shape[1]
    out = pl.pallas_call(
        functools.partial(_bias_kernel, mask_step=mask_step),
        in_specs=[pl.BlockSpec(memory_space=pltpu.SMEM), pl.BlockSpec((nq, nk), lambda: (0, 0))],
        out_specs=pl.BlockSpec((n_variants, nh, nq, nk), lambda: (0, 0, 0, 0)),
        out_shape=jax.ShapeDtypeStruct((n_variants, nh, nq, nk), _F32),
        name=name,
    )(table, bucket)
    return out.transpose(0, 3, 1, 2).reshape(n_variants, nk, nh * nq)


def _t5_bucket(rel):
    nb = N_BUCKETS // 2
    max_exact = nb // 2
    ret = jnp.where(rel > 0, nb, 0)
    n = jnp.abs(rel)
    nf = jnp.maximum(n, 1).astype(jnp.float32)
    large = max_exact + (jnp.log(nf / max_exact) / math.log(MAX_DIST / max_exact) * (nb - max_exact)).astype(jnp.int32)
    large = jnp.minimum(large, nb - 1)
    return ret + jnp.where(n < max_exact, n, large)


def _bucket_index(n_q, n_k):
    i = jnp.arange(n_q)[:, None]
    j = jnp.arange(n_k)[None, :]
    return _t5_bucket(j - WINDOW - i).astype(jnp.int32)


_NT = (((1,), (1,)), ((), ()))
_TN = (((0,), (0,)), ((), ()))


def _sink_softmax_keys_major(lg, sink_t):
    sink2 = sink_t * LOG2E
    m = jnp.maximum(jnp.max(lg, axis=0, keepdims=True), sink2)
    e = jnp.exp2(lg - m)
    den = jnp.sum(e, axis=0, keepdims=True) + jnp.exp2(sink2 - m)
    return e, 1.0 / den


def _attn_chunk_logits(qc, kw, bias_t):
    logits = []
    for kv in range(N_KV):
        qg = jnp.concatenate(
            [qc[:, (kv * GROUP + g) * HEAD_DIM:(kv * GROUP + g + 1) * HEAD_DIM] for g in range(GROUP)], axis=0)
        kk = kw[:, kv * HEAD_DIM:(kv + 1) * HEAD_DIM]
        logits.append(lax.dot_general(kk, qg, _NT, preferred_element_type=_F32))
    return jnp.concatenate(logits, axis=1) + bias_t


def _attn_chunk_output(lg, vw, sink_t):
    gq = lg.shape[1] // N_KV
    e, inv = _sink_softmax_keys_major(lg, sink_t)
    e = e.astype(_BF)
    outs = []
    for kv in range(N_KV):
        vv = vw[:, kv * HEAD_DIM:(kv + 1) * HEAD_DIM]
        cols = slice(kv * gq, (kv + 1) * gq)
        outs.append(lax.dot_general(vv, e[:, cols], _TN, preferred_element_type=_F32) * inv[:, cols])
    return jnp.concatenate(outs, axis=0).T


ATTN_CHUNKS_PER_BLOCK = 8


def _attn_prompt_kernel(*refs, tiles_per_batch, n_gate_blocks):
    q_ref, kp_ref, kc_ref, vp_ref, vc_ref, bias_ref, sink_ref, n_ref = refs[:8]
    wgate_refs = refs[8:8 + n_gate_blocks]
    o_ref, gate_ref, kwin, vwin = refs[8 + n_gate_blocks:]
    ta = q_ref.shape[0]
    nk = WINDOW + CHUNK
    dkv = N_KV * HEAD_DIM
    rows = ATTN_CHUNKS_PER_BLOCK * CHUNK
    gate_cols = gate_ref.shape[1] // n_gate_blocks
    kwin[0:WINDOW, :] = kp_ref[...].astype(_BF)
    kwin[WINDOW:, :] = kc_ref[...].astype(_BF)
    vwin[0:WINDOW, :] = vp_ref[...].astype(_BF)
    vwin[WINDOW:, :] = vc_ref[...].astype(_BF)
    tile_pos = (pl.program_id(0) % tiles_per_batch) * ta

    def logits(r):
        n_before = jnp.maximum(0, (WINDOW - (tile_pos + r)) // CHUNK)
        return _attn_chunk_logits(q_ref[pl.ds(r, CHUNK), :], kwin[pl.ds(r, nk), :], bias_ref[n_before])

    def finish(r, lg):
        o = _attn_chunk_output(lg, vwin[pl.ds(r, nk), :], sink_ref[...])
        for g in range(GROUP):
            o_ref[pl.ds(r, CHUNK), g * dkv:(g + 1) * dkv] = o[g * CHUNK:(g + 1) * CHUNK].astype(_BF)

    def block(b, carry):
        r0 = pl.multiple_of(b * rows, rows)
        n = n_ref[pl.ds(r0, rows), :]
        chunk_rows = [pl.multiple_of(r0 + c * CHUNK, CHUNK) for c in range(ATTN_CHUNKS_PER_BLOCK)]
        lg = logits(chunk_rows[0])
        for c in range(max(ATTN_CHUNKS_PER_BLOCK, n_gate_blocks)):
            lg_next = logits(chunk_rows[c + 1]) if c + 1 < ATTN_CHUNKS_PER_BLOCK else None
            if c < n_gate_blocks:
                gate = jax.nn.sigmoid(_dot(n, wgate_refs[c][...]))
                gate_ref[pl.ds(r0, rows), c * gate_cols:(c + 1) * gate_cols] = gate.astype(_BF)
            if c < ATTN_CHUNKS_PER_BLOCK:
                finish(chunk_rows[c], lg)
            lg = lg_next
        return carry

    lax.fori_loop(0, ta // rows, block, 0)


def _attn_prompt(q, k, v, bias, sink, n, w_in, *, gate_col, gate_cols, gate_block, batch, ta, name, casts=()):
    M, dq = q.shape
    dk = k.shape[1]
    D = n.shape[1]
    n_gate_blocks = gate_cols // gate_block
    tiles_per_batch = (M // batch) // ta
    wb = ta // WINDOW
    nk = WINDOW + CHUNK

    def prev(i):
        return (jnp.where(i % tiles_per_batch == 0, i * wb, i * wb - 1), 0)

    row = lambda i: (i, 0)
    grid = (M // ta,)
    c_in, c_out, c_shape, c_bytes = _cast_specs(casts, grid)
    blocks = [_nbytes((ta, dq), _BF) * 2, 2 * _nbytes((WINDOW + ta, dk), _F32),
              _nbytes(bias.shape, _F32), _nbytes((8, N_HEADS * CHUNK), _F32), c_bytes,
              _nbytes((ta, D + gate_cols), _BF), _nbytes((D, gate_cols), _BF) // 2]
    gate_specs = [pl.BlockSpec((D, gate_block), lambda i, b=gate_col // gate_block + c: (0, b),
                               pipeline_mode=pl.Buffered(1)) for c in range(n_gate_blocks)]
    body = functools.partial(_attn_prompt_kernel, tiles_per_batch=tiles_per_batch, n_gate_blocks=n_gate_blocks)
    return pl.pallas_call(
        _with_casts(body, 8 + n_gate_blocks, 2, len(casts)),
        grid=grid,
        in_specs=[
            pl.BlockSpec((ta, dq), row),
            pl.BlockSpec((WINDOW, dk), prev),
            pl.BlockSpec((ta, dk), row),
            pl.BlockSpec((WINDOW, dk), prev),
            pl.BlockSpec((ta, dk), row),
            pl.BlockSpec(bias.shape, lambda i: (0, 0, 0)),
            pl.BlockSpec((1, N_HEADS * CHUNK), lambda i: (0, 0)),
            pl.BlockSpec((ta, D), row),
        ] + gate_specs + c_in,
        out_specs=[pl.BlockSpec((ta, dq), row), pl.BlockSpec((ta, gate_cols), row)] + c_out,
        out_shape=[jax.ShapeDtypeStruct((M, dq), _BF), jax.ShapeDtypeStruct((M, gate_cols), _BF)] + c_shape,
        scratch_shapes=[pltpu.VMEM((WINDOW + ta, dk), _BF), pltpu.VMEM((WINDOW + ta, dk), _BF)],
        compiler_params=pltpu.CompilerParams(
            dimension_semantics=("arbitrary",),
            vmem_limit_bytes=_vmem_limit(blocks, 2 * _nbytes((WINDOW + ta, dk), _BF)),
        ),
        name=name,
    )(q, k, k, v, v, bias, sink, n, *([w_in] * n_gate_blocks), *_cast_args(casts))


LANES = 128


def _attn_sample_kernel(q_ref, ck_ref, cv_ref, k_ref, v_ref, bias_ref, sink_ref, o_ref, *, nb, s):
    dkv = N_KV * HEAD_DIM
    gs = GROUP * s
    assert LANES == 2 * HEAD_DIM and dkv == 2 * LANES
    for b in range(nb):
        rows = slice(b * s, (b + 1) * s)
        kw = jnp.concatenate([ck_ref[b].astype(_BF), k_ref[rows, :].astype(_BF)], axis=0)
        vw = jnp.concatenate([cv_ref[b].astype(_BF), v_ref[rows, :].astype(_BF)], axis=0)
        qf = q_ref[rows, :].astype(_F32)
        blocks = []
        for kv in range(N_KV):
            for g in range(GROUP):
                h = kv * GROUP + g
                pair = qf[:, (h // 2) * LANES:(h // 2 + 1) * LANES]
                if h % 2 != kv % 2:
                    pair = pltpu.roll(pair, HEAD_DIM, 1)
                half = lax.broadcasted_iota(jnp.int32, pair.shape, 1) // HEAD_DIM
                piece = jnp.where(half == kv % 2, pair, 0.0)
                zero = jnp.zeros_like(piece)
                blocks.append(jnp.concatenate([piece, zero] if kv < 2 else [zero, piece], axis=1))
        q_bd = jnp.concatenate(blocks, axis=0).astype(_BF)
        lg = lax.dot_general(kw, q_bd, _NT, preferred_element_type=_F32) + bias_ref[0]
        e, inv = _sink_softmax_keys_major(lg, sink_ref[...])
        z = lax.dot_general((e * inv).astype(_BF), vw, _TN, preferred_element_type=_F32)
        row_kv = lax.broadcasted_iota(jnp.int32, z.shape, 0) // gs
        col_kv = lax.broadcasted_iota(jnp.int32, z.shape, 1) // HEAD_DIM
        z = jnp.where(row_kv == col_kv, z, 0.0)
        r = z[0:gs]
        for kv in range(1, N_KV):
            r = r + z[kv * gs:(kv + 1) * gs]
        for g in range(GROUP):
            o_ref[rows, g * dkv:(g + 1) * dkv] = r[g * s:(g + 1) * s].astype(_BF)


def _attn_sample(q, k, v, cache_k, cache_v, bias, sink, *, nb, name):
    B, W, dk = cache_k.shape
    M, dq = q.shape
    S = M // B
    row = lambda b: (b, 0)
    return pl.pallas_call(
        functools.partial(_attn_sample_kernel, nb=nb, s=S),
        grid=(B // nb,),
        in_specs=[
            pl.BlockSpec((nb * S, dq), row),
            pl.BlockSpec((nb, W, dk), lambda b: (b, 0, 0)),
            pl.BlockSpec((nb, W, dk), lambda b: (b, 0, 0)),
            pl.BlockSpec((nb * S, dk), row),
            pl.BlockSpec((nb * S, dk), row),
            pl.BlockSpec(bias.shape, lambda b: (0, 0, 0)),
            pl.BlockSpec(sink.shape, lambda b: (0, 0)),
        ],
        out_specs=pl.BlockSpec((nb * S, dq), row),
        out_shape=jax.ShapeDtypeStruct((M, dq), _BF),
        compiler_params=pltpu.CompilerParams(dimension_semantics=("arbitrary",)),
        name=name,
    )(q, cache_k, cache_v, k, v, bias, sink)


def _gates_kernel(n_ref, w_ref, gate_ref):
    gate_ref[...] = jax.nn.sigmoid(_dot(n_ref[...], w_ref[...])).astype(_BF)


def _gates(n, w_in, *, gate_col, gate_cols, gate_block, name):
    M, D = n.shape
    first = gate_col // gate_block
    return pl.pallas_call(
        _gates_kernel,
        grid=(gate_cols // gate_block,),
        in_specs=[pl.BlockSpec((M, D), lambda c: (0, 0)), pl.BlockSpec((D, gate_block), lambda c: (0, first + c))],
        out_specs=pl.BlockSpec((M, gate_block), lambda c: (0, c)),
        out_shape=jax.ShapeDtypeStruct((M, gate_cols), _BF),
        compiler_params=pltpu.CompilerParams(
            dimension_semantics=("arbitrary",),
            vmem_limit_bytes=_vmem_limit([_nbytes((M, D + gate_block), _BF), _nbytes((D, gate_block), _BF)]),
        ),
        name=name,
    )(n, w_in)


def _merge_kernel(h_ref, gate_ref, ucb_ref, o_ref, wco_ref, wao_ref, wout_ref, g_ref, out_ref, xn_ref, mix_ref,
                  *, n_chunks):
    ucb, o = ucb_ref[...], o_ref[...]
    D = out_ref.shape[1]
    cw = D // n_chunks
    for c in range(n_chunks):
        cols = slice(c * cw, (c + 1) * cw)
        y_conv = _dot(ucb, wco_ref[:, cols])
        y_attn = _dot(o, wao_ref[:, cols])
        g_conv = gate_ref[:, cols].astype(_F32)
        g_attn = gate_ref[:, D + c * cw:D + (c + 1) * cw].astype(_F32)
        mix_ref[:, cols] = (g_conv * y_conv + g_attn * y_attn).astype(_BF)
    h = h_ref[...] + _dot(mix_ref[...], wout_ref[...])
    out_ref[...] = h
    xn_ref[...] = _rms(h, g_ref[...]).astype(_BF)


def _merge(h, gates, ucb, o, w_conv_out, w_attn_o, w_out, g, *, cw, tm, name):
    M, D = h.shape
    dc, da = ucb.shape[1], o.shape[1]
    row = lambda i: (i, 0)
    once = pl.Buffered(1)
    const = lambda i: (0, 0)
    blocks = [2 * _nbytes((tm, D), _F32), _nbytes((tm, 3 * D + dc + da), _BF), _nbytes((dc + da + D, D), _BF) // 2]
    return pl.pallas_call(
        functools.partial(_merge_kernel, n_chunks=D // cw),
        grid=(M // tm,),
        in_specs=[
            pl.BlockSpec((tm, D), row),
            pl.BlockSpec((tm, 2 * D), row),
            pl.BlockSpec((tm, dc), row),
            pl.BlockSpec((tm, da), row),
            pl.BlockSpec((dc, D), const, pipeline_mode=once),
            pl.BlockSpec((da, D), const, pipeline_mode=once),
            pl.BlockSpec((D, D), const, pipeline_mode=once),
            pl.BlockSpec((1, D), const),
        ],
        out_specs=[pl.BlockSpec((tm, D), row), pl.BlockSpec((tm, D), row)],
        out_shape=[jax.ShapeDtypeStruct((M, D), _F32), jax.ShapeDtypeStruct((M, D), _BF)],
        scratch_shapes=[pltpu.VMEM((tm, D), _BF)],
        compiler_params=pltpu.CompilerParams(
            dimension_semantics=("arbitrary",),
            vmem_limit_bytes=_vmem_limit(blocks, _nbytes((tm, D), _BF)),
        ),
        name=name,
    )(h, gates, ucb, o, w_conv_out, w_attn_o, w_out, g)


def _ple_kernel(h_ref, pe_ref, g_ref, wp_ref, wg_ref, out_ref):
    rows = h_ref.shape[0] // 2
    halves = [slice(0, rows), slice(rows, 2 * rows)]
    normed = [_rms(h_ref[sl, :], g_ref[...]).astype(_BF) for sl in halves]
    embs = [_dot(pe_ref[sl, :].astype(_BF), wp_ref[...]) for sl in halves]
    gates = [_dot(nr, wg_ref[...]) for nr in normed]
    for sl, emb, gate in zip(halves, embs, gates):
        out_ref[sl, :] = h_ref[sl, :] + emb * jax.nn.sigmoid(gate)


def _ple(h, pe, g, w_ple, w_gate, *, tm, name):
    M, D = h.shape
    dp = pe.shape[1]
    row = lambda i: (i, 0)
    const = lambda i: (0, 0)
    blocks = [2 * _nbytes((tm, D), _F32), _nbytes((tm, dp), _F32), _nbytes((dp + D, D), _BF)]
    return pl.pallas_call(
        _ple_kernel,
        grid=(M // tm,),
        in_specs=[
            pl.BlockSpec((tm, D), row),
            pl.BlockSpec((tm, dp), row),
            pl.BlockSpec((1, D), const),
            pl.BlockSpec((dp, D), const),
            pl.BlockSpec((D, D), const),
        ],
        out_specs=pl.BlockSpec((tm, D), row),
        out_shape=jax.ShapeDtypeStruct((M, D), _F32),
        compiler_params=pltpu.CompilerParams(
            dimension_semantics=("arbitrary",),
            vmem_limit_bytes=_vmem_limit(blocks),
        ),
        name=name,
    )(h, pe, g, w_ple, w_gate)


def _row_tile(m, want):
    return want if m % want == 0 else m


def kernel(x_prompt, x_sample, p_prompt, p_sample, state_conv, cache_k, cache_v, rel_table, ffn1_norm, ffn1_wg, ffn1_wu, ffn1_wd, mix_norm, w_in, conv_w, q_norm, k_norm, attn_sink, w_conv_out, w_attn_o, w_out, ffn2_norm, ffn2_wg, ffn2_wu, ffn2_wd, ple_norm, w_ple, w_ple_gate):
    B, T, D = x_prompt.shape
    Bs, S, _ = x_sample.shape
    depth = ffn1_wg.shape[0]
    d_conv = conv_w.shape[2]
    dq, dk = N_HEADS * HEAD_DIM, N_KV * HEAD_DIM
    qkv_col = 3 * d_conv
    gc_col = qkv_col + dq + 2 * dk
    Mp, Ms = B * T, Bs * S
    tm_p = _row_tile(Mp, 512)
    tm_s = _row_tile(Ms, 512)
    tf = 1024
    tn = 512
    cw = 256

    nk_p, nk_s = WINDOW + CHUNK, WINDOW + S
    bias_p = _rel_bias(rel_table, _bucket_index(CHUNK, nk_p), WINDOW // CHUNK + 1, CHUNK, "rel_bias_prompt")
    bias_s = _rel_bias(rel_table, _bucket_index(S, nk_s), 1, 0, "rel_bias_sample")
    lane = jnp.arange(cw) // HEAD_DIM
    ones = (lane[:, None] == lane[None, :]).astype(_BF)

    yp = x_prompt.reshape(Mp, D)
    ys = x_sample.reshape(Ms, D)
    conv_p, k_p, v_p, conv_s, k_s, v_s = [], [], [], [], [], []
    head_to_group_major = lambda h: (h % GROUP) * N_KV + h // GROUP
    for l in range(depth):
        vec = lambda g: g[l].reshape(1, -1)
        h1_s, n_s, f1_wg, f1_wu, f1_wd = _ffn(ys, vec(ffn1_norm), ffn1_wg[l], ffn1_wu[l], ffn1_wd[l],
                                              vec(mix_norm), tm=tm_s, tf=tf // 2, name="ffn1_sample")
        f1 = (vec(ffn1_norm), f1_wg, f1_wu, f1_wd)
        gq = jnp.tile(q_norm[l], N_HEADS).reshape(1, dq)
        gk = jnp.tile(k_norm[l], N_KV).reshape(1, dk)
        sink_p = jnp.repeat(attn_sink[l], CHUNK).reshape(1, N_HEADS * CHUNK)
        sink_s = jnp.repeat(attn_sink[l], S).reshape(1, N_HEADS * S)

        h1, n, w_in_l = _ffn(yp, *f1, vec(mix_norm), tm=tm_p, tf=tf, name="ffn1_prompt", casts=(w_in[l],))
        ucb, ulast, q, k, v, f2_wg, f2_wu, w_co, w_ao_p, w_o = _inproj(
            n, w_in_l, conv_w[l], gq, gk, ones, tm=tm_p, batch=B, name="inproj_prompt",
            casts=(ffn2_wg[l], ffn2_wu[l], w_conv_out[l], (w_attn_o[l], HEAD_DIM, head_to_group_major), w_out[l]))
        o, gates, f2_wd, w_p, w_pg = _attn_prompt(
            q, k, v, bias_p, sink_p, n, w_in_l, gate_col=gc_col, gate_cols=2 * D, gate_block=tn, batch=B,
            ta=tm_p, name="attn_prompt", casts=(ffn2_wd[l], w_ple[l], w_ple_gate[l]))
        h2, xn2 = _merge(h1, gates, ucb, o, w_co, w_ao_p, w_o, vec(ffn2_norm), cw=tn, tm=tm_p, name="merge_prompt")
        f2 = (f2_wg, f2_wu, f2_wd)
        h3, = _ffn(h2, xn2, *f2, None, tm=tm_p, tf=tf, name="ffn2_prompt")
        yp = _ple(h3, p_prompt[l].reshape(Mp, -1), vec(ple_norm), w_p, w_pg, tm=tm_p, name="ple_prompt")
        conv_p.append(ulast[:, 6:8])
        k_p.append(k.reshape(B, T, dk)[:, T - WINDOW:].reshape(B, WINDOW, N_KV, HEAD_DIM))
        v_p.append(v.reshape(B, T, dk)[:, T - WINDOW:].reshape(B, WINDOW, N_KV, HEAD_DIM))

        st = state_conv[l]
        zeros = jnp.zeros((Bs, S - 2, d_conv), _F32)
        e2 = jnp.concatenate([st, zeros], axis=1).reshape(Ms, d_conv)
        e1 = jnp.concatenate([st[:, 1:], zeros, zeros[:, :1]], axis=1).reshape(Ms, d_conv)
        h1, n = h1_s, n_s
        ucb, u, q, k, v = _inproj(n, w_in_l, conv_w[l], gq, gk, ones, tm=tm_s, seg=S, edges=(e1, e2),
                                  name="inproj_sample")
        o = _attn_sample(q, k, v, cache_k[l].reshape(Bs, WINDOW, dk), cache_v[l].reshape(Bs, WINDOW, dk),
                         bias_s, sink_s, nb=4, name="attn_sample")
        gates = _gates(n, w_in_l, gate_col=gc_col, gate_cols=2 * D, gate_block=tn, name="gates_sample")
        h2, xn2 = _merge(h1, gates, ucb, o, w_co, w_ao_p, w_o, vec(ffn2_norm), cw=tn, tm=tm_s, name="merge_sample")
        h3, = _ffn(h2, xn2, *f2, None, tm=tm_s, tf=tf, name="ffn2_sample")
        ys = _ple(h3, p_sample[l].reshape(Ms, -1), vec(ple_norm), w_p, w_pg, tm=tm_s, name="ple_sample")
        conv_s.append(u.reshape(Bs, S, d_conv)[:, S - 2:])
        k_s.append(k.reshape(Bs, S, N_KV, HEAD_DIM))
        v_s.append(v.reshape(Bs, S, N_KV, HEAD_DIM))

    return (yp.reshape(B, T, D), ys.reshape(Bs, S, D), jnp.stack(conv_p), jnp.stack(k_p), jnp.stack(v_p),
            jnp.stack(conv_s), jnp.stack(k_s), jnp.stack(v_s))
```

```python
import functools
import math

import jax
import jax.numpy as jnp
from jax import lax
from jax.experimental import pallas as pl
from jax.experimental.pallas import tpu as pltpu

_BF = jnp.bfloat16
_F32 = jnp.float32

CHUNK = 64
WINDOW = 128
N_HEADS = 16
N_KV = 4
GROUP = N_HEADS // N_KV
HEAD_DIM = 64
N_BUCKETS = 32
MAX_DIST = 128
EPS = 1e-6
LOG2E = math.log2(math.e)
QK_SCALE = HEAD_DIM ** -0.5 * LOG2E

V7X_VMEM_BYTES = 64 * 1024 * 1024
VMEM_INTERNAL_BYTES = 12 * 1024 * 1024


def _vmem_limit(block_bytes, scratch_bytes=0):
    need = 2 * sum(block_bytes) + scratch_bytes + VMEM_INTERNAL_BYTES
    return int(min(need, V7X_VMEM_BYTES - 4 * 1024 * 1024))


def _nbytes(shape, dtype):
    return math.prod(shape) * jnp.dtype(dtype).itemsize


def _dot(a, b):
    return jnp.dot(a, b, preferred_element_type=_F32)


def _rms(x, g):
    ms = jnp.mean(x * x, axis=-1, keepdims=True)
    return x * lax.rsqrt(ms + EPS) * g


BF16_TILE_ROWS = 16


def _cast_specs(arrays, grid):
    steps = math.prod(grid)

    def flat(*ids):
        s = 0
        for k, g in zip(ids, grid):
            s = s * g + k
        return s

    in_specs, out_specs, out_shape, nbytes = [], [], [], 0
    for a in arrays:
        a, rb, out_block = a if isinstance(a, tuple) else (a, None, lambda b: b)
        rows, cols = a.shape
        rb = rb or max(BF16_TILE_ROWS, rows // steps)
        assert rows % rb == 0 and steps % (rows // rb) == 0, (a.shape, steps)
        rep = steps // (rows // rb)
        idx = lambda *ids, rep=rep: (flat(*ids) // rep, 0)
        out_idx = lambda *ids, rep=rep, out_block=out_block: (out_block(flat(*ids) // rep), 0)
        in_specs.append(pl.BlockSpec((rb, cols), idx))
        out_specs.append(pl.BlockSpec((rb, cols), out_idx))
        out_shape.append(jax.ShapeDtypeStruct((rows, cols), _BF))
        nbytes += _nbytes((rb, cols), _F32) + _nbytes((rb, cols), _BF)
    return in_specs, out_specs, out_shape, nbytes


def _cast_args(arrays):
    return [a[0] if isinstance(a, tuple) else a for a in arrays]


def _with_casts(body, n_in, n_out, n_cast):
    def kernel(*refs):
        outs_at = n_in + n_cast
        scratch_at = outs_at + n_out + n_cast
        for src, dst in zip(refs[n_in:outs_at], refs[outs_at + n_out:scratch_at]):
            dst[...] = src[...].astype(_BF)
        body(*refs[:n_in], *refs[outs_at:outs_at + n_out], *refs[scratch_at:])
    return kernel


def _ffn_kernel(*refs, emit_norm, keep_weights, normed_input):
    x_ref, g_ref, wg_ref, wu_ref, wd_ref = refs[:5]
    rest = list(refs[5:])
    g2_ref = rest.pop(0) if emit_norm else None
    out_ref = rest.pop(0)
    n_ref = rest.pop(0) if emit_norm else None
    weight_copies = [rest.pop(0) for _ in range(3)] if keep_weights else []
    j = pl.program_id(1)
    last = pl.num_programs(1) - 1
    if normed_input:
        xn_ref = g_ref
    else:
        xn_ref, = rest

        @pl.when(j == 0)
        def _():
            xn_ref[...] = _rms(x_ref[...], g_ref[...]).astype(_BF)

    wg, wu, wd = wg_ref[...], wu_ref[...], wd_ref[...]
    if keep_weights:
        wg, wu, wd = wg.astype(_BF), wu.astype(_BF), wd.astype(_BF)
        for dst, w in zip(weight_copies, (wg, wu, wd)):
            dst[...] = w
    xn = xn_ref[...]
    gate = _dot(xn, wg)
    up = _dot(xn, wu)
    hid = (jax.nn.silu(gate) * up * 0.5).astype(_BF)
    out_ref[...] = jnp.where(j == 0, x_ref[...], out_ref[...]) + _dot(hid, wd)

    if emit_norm:
        @pl.when(j == last)
        def _():
            n_ref[...] = _rms(out_ref[...], g2_ref[...]).astype(_BF)


def _ffn(x, g, wg, wu, wd, g2, *, tm, tf, name, casts=()):
    M, D = x.shape
    F = wg.shape[1]
    emit_norm = g2 is not None
    keep_weights = wg.dtype == _F32
    normed_input = g.shape[0] == M
    grid = (M // tm, F // tf)
    assert not keep_weights or grid[0] == 1
    row = lambda i, j: (i, 0)
    w_specs = [
        pl.BlockSpec((D, tf), lambda i, j: (0, j)),
        pl.BlockSpec((D, tf), lambda i, j: (0, j)),
        pl.BlockSpec((tf, D), lambda i, j: (j, 0)),
    ]
    g_spec = pl.BlockSpec((tm, D), row) if normed_input else pl.BlockSpec((1, D), lambda i, j: (0, 0))
    in_specs = [pl.BlockSpec((tm, D), row), g_spec] + w_specs
    args = [x, g, wg, wu, wd]
    out_shape = [jax.ShapeDtypeStruct((M, D), _F32)]
    out_specs = [pl.BlockSpec((tm, D), row)]
    blocks = [_nbytes((tm, D), _F32) * 2, 3 * _nbytes((D, tf), wg.dtype)]
    if emit_norm:
        in_specs.append(pl.BlockSpec((1, D), lambda i, j: (0, 0)))
        args.append(g2)
        out_shape.append(jax.ShapeDtypeStruct((M, D), _BF))
        out_specs.append(pl.BlockSpec((tm, D), row))
        blocks.append(_nbytes((tm, D), _BF))
    if keep_weights:
        out_shape += [jax.ShapeDtypeStruct(w.shape, _BF) for w in (wg, wu, wd)]
        out_specs += w_specs
        blocks.append(3 * _nbytes((D, tf), _BF))
    c_in, c_out, c_shape, c_bytes = _cast_specs(casts, grid)
    blocks.append(c_bytes)
    body = functools.partial(_ffn_kernel, emit_norm=emit_norm, keep_weights=keep_weights,
                             normed_input=normed_input)
    res = pl.pallas_call(
        _with_casts(body, len(in_specs), len(out_specs), len(casts)),
        grid=grid,
        in_specs=in_specs + c_in,
        out_specs=out_specs + c_out,
        out_shape=out_shape + c_shape,
        scratch_shapes=[] if normed_input else [pltpu.VMEM((tm, D), _BF)],
        compiler_params=pltpu.CompilerParams(
            dimension_semantics=("arbitrary", "arbitrary"),
            vmem_limit_bytes=_vmem_limit(blocks, _nbytes((tm, D), _BF)),
        ),
        name=name,
    )(*args, *_cast_args(casts))
    return res


def _conv_kernel(*refs, carry_rows, seg, tiles_per_seg, d_conv, cw):
    if carry_rows:
        n_ref, w_ref, cwt_ref, ucb_ref, ulast_ref, carry_ref = refs
    else:
        n_ref, w_ref, cwt_ref, e1_ref, e2_ref, ucb_ref, u_ref = refs
    tm = n_ref.shape[0]
    n = n_ref[...]
    if carry_rows:
        @pl.when(pl.program_id(0) % tiles_per_seg == 0)
        def _():
            carry_ref[...] = jnp.zeros_like(carry_ref)

    for c in range(d_conv // cw):
        lo = c * cw
        cb = _dot(n, w_ref[:, lo:lo + cw])
        cc = _dot(n, w_ref[:, d_conv + lo:d_conv + lo + cw])
        cv = _dot(n, w_ref[:, 2 * d_conv + lo:2 * d_conv + lo + cw])
        u = cc * cv
        r1 = pltpu.roll(u, 1, 0)
        r2 = pltpu.roll(u, 2, 0)
        if carry_rows:
            prev = carry_ref[:, lo:lo + cw]
            rows8 = lax.broadcasted_iota(jnp.int32, (8, cw), 0)
            h1 = jnp.where(rows8 < 1, pltpu.roll(prev, 1, 0), r1[:8])
            h2 = jnp.where(rows8 < 2, pltpu.roll(prev, 2, 0), r2[:8])
            r1 = jnp.concatenate([h1, r1[8:]], axis=0)
            r2 = jnp.concatenate([h2, r2[8:]], axis=0)
            carry_ref[:, lo:lo + cw] = u[tm - 8:]
            ulast_ref[0, :, lo:lo + cw] = u[tm - 8:]
        else:
            pos = lax.broadcasted_iota(jnp.int32, (tm, cw), 0) % seg
            r1 = jnp.where(pos < 1, e1_ref[:, lo:lo + cw], r1)
            r2 = jnp.where(pos < 2, e2_ref[:, lo:lo + cw], r2)
            u_ref[:, lo:lo + cw] = u
        w0 = cwt_ref[0:1, lo:lo + cw]
        w1 = cwt_ref[1:2, lo:lo + cw]
        w2 = cwt_ref[2:3, lo:lo + cw]
        conv = w0 * r2 + w1 * r1 + w2 * u
        ucb_ref[:, lo:lo + cw] = (cb * conv).astype(_BF)


def _qkv_kernel(n_ref, w_ref, gq_ref, gk_ref, ones_ref, q_ref, k_ref, v_ref, *, cw, col0):
    ones = ones_ref[...]
    dq = q_ref.shape[1]
    dk = k_ref.shape[1]
    p = _dot(n_ref[...], w_ref[:, col0:col0 + dq + 2 * dk])
    x = p[:, :dq + dk]
    sq = x * x
    hi = sq.astype(_BF)
    lo = (sq - hi.astype(_F32)).astype(_BF)
    ss = jnp.concatenate(
        [_dot(hi[:, c * cw:(c + 1) * cw], ones) + _dot(lo[:, c * cw:(c + 1) * cw], ones)
         for c in range((dq + dk) // cw)], axis=1)
    xr = x * lax.rsqrt(ss * (1.0 / HEAD_DIM) + EPS)
    q_ref[...] = (xr[:, :dq] * gq_ref[...] * QK_SCALE).astype(_BF)
    k_ref[...] = xr[:, dq:] * gk_ref[...]
    v_ref[...] = p[:, dq + dk:]


def _inproj_kernel(*refs, carry_rows, seg, tiles_per_seg, d_conv, cw):
    if carry_rows:
        (n_ref, w_ref, cwt_ref, gq_ref, gk_ref, ones_ref,
         ucb_ref, ulast_ref, q_ref, k_ref, v_ref, carry_ref) = refs
        conv_refs = (n_ref, w_ref, cwt_ref, ucb_ref, ulast_ref, carry_ref)
    else:
        (n_ref, w_ref, cwt_ref, gq_ref, gk_ref, ones_ref, e1_ref, e2_ref,
         ucb_ref, u_ref, q_ref, k_ref, v_ref) = refs
        conv_refs = (n_ref, w_ref, cwt_ref, e1_ref, e2_ref, ucb_ref, u_ref)
    _conv_kernel(*conv_refs, carry_rows=carry_rows, seg=seg, tiles_per_seg=tiles_per_seg, d_conv=d_conv, cw=cw)
    _qkv_kernel(n_ref, w_ref, gq_ref, gk_ref, ones_ref, q_ref, k_ref, v_ref, cw=cw, col0=3 * d_conv)


def _inproj(n, w_in, conv_w, gq, gk, ones, *, tm, name, batch=None, seg=None, edges=(), casts=()):
    M, D = n.shape
    d_conv = conv_w.shape[1]
    cw = ones.shape[0]
    dq, dk = N_HEADS * HEAD_DIM, N_KV * HEAD_DIM
    wcols = 3 * d_conv + dq + 2 * dk
    carry_rows = batch is not None
    grid = (M // tm,)
    tiles_per_seg = (M // batch) // tm if carry_rows else None
    c_in, c_out, c_shape, c_bytes = _cast_specs(casts, grid)
    const = lambda i: (0, 0)
    row = lambda i: (i, 0)
    in_specs = [
        pl.BlockSpec((tm, D), row),
        pl.BlockSpec((D, wcols), const, pipeline_mode=pl.Buffered(1)),
        pl.BlockSpec((3, d_conv), const),
        pl.BlockSpec((1, dq), const),
        pl.BlockSpec((1, dk), const),
        pl.BlockSpec((cw, cw), const),
    ] + [pl.BlockSpec((tm, d_conv), row) for _ in edges]
    if carry_rows:
        u_spec = pl.BlockSpec((1, 8, d_conv), lambda i: (i // tiles_per_seg, 0, 0))
        u_shape = jax.ShapeDtypeStruct((batch, 8, d_conv), _F32)
        scratch = [pltpu.VMEM((8, d_conv), _F32)]
    else:
        u_spec = pl.BlockSpec((tm, d_conv), row)
        u_shape = jax.ShapeDtypeStruct((M, d_conv), _F32)
        scratch = []
    out_specs = [pl.BlockSpec((tm, d_conv), row), u_spec,
                 pl.BlockSpec((tm, dq), row), pl.BlockSpec((tm, dk), row), pl.BlockSpec((tm, dk), row)]
    out_shape = [jax.ShapeDtypeStruct((M, d_conv), _BF), u_shape, jax.ShapeDtypeStruct((M, dq), _BF),
                 jax.ShapeDtypeStruct((M, dk), _F32), jax.ShapeDtypeStruct((M, dk), _F32)]
    blocks = [_nbytes((tm, D), _BF), _nbytes((D, wcols), _BF) // 2, (1 + len(edges)) * _nbytes((tm, d_conv), _F32),
              _nbytes((tm, d_conv + dq), _BF), 2 * _nbytes((tm, dk), _F32), c_bytes]
    body = functools.partial(_inproj_kernel, carry_rows=carry_rows, seg=seg, tiles_per_seg=tiles_per_seg,
                             d_conv=d_conv, cw=cw)
    return pl.pallas_call(
        _with_casts(body, len(in_specs), len(out_specs), len(casts)),
        grid=grid,
        in_specs=in_specs + c_in,
        out_specs=out_specs + c_out,
        out_shape=out_shape + c_shape,
        scratch_shapes=scratch,
        compiler_params=pltpu.CompilerParams(
            dimension_semantics=("arbitrary",),
            vmem_limit_bytes=_vmem_limit(blocks),
        ),
        name=name,
    )(n, w_in, conv_w, gq, gk, ones, *edges, *_cast_args(casts))


def _bias_kernel(tab_ref, bkt_ref, out_ref, *, mask_step):
    bkt = bkt_ref[...]
    key = lax.broadcasted_iota(jnp.int32, bkt.shape, 1)
    for h in range(out_ref.shape[1]):
        acc = jnp.zeros(bkt.shape, _F32)
        for b in range(tab_ref.shape[0]):
            acc = jnp.where(bkt == b, tab_ref[b, h] * LOG2E, acc)
        for v in range(out_ref.shape[0]):
            out_ref[v, h] = jnp.where(key < v * mask_step, -jnp.inf, acc)


def _rel_bias(table, bucket, n_variants, mask_step, name):
    nq, nk = bucket.shape
    nh = table.shape[1]
    out = pl.pallas_call(
        functools.partial(_bias_kernel, mask_step=mask_step),
        in_specs=[pl.BlockSpec(memory_space=pltpu.SMEM), pl.BlockSpec((nq, nk), lambda: (0, 0))],
        out_specs=pl.BlockSpec((n_variants, nh, nq, nk), lambda: (0, 0, 0, 0)),
        out_shape=jax.ShapeDtypeStruct((n_variants, nh, nq, nk), _F32),
        name=name,
    )(table, bucket)
    return out.transpose(0, 3, 1, 2).reshape(n_variants, nk, nh * nq)


def _t5_bucket(rel):
    nb = N_BUCKETS // 2
    max_exact = nb // 2
    ret = jnp.where(rel > 0, nb, 0)
    n = jnp.abs(rel)
    nf = jnp.maximum(n, 1).astype(jnp.float32)
    large = max_exact + (jnp.log(nf / max_exact) / math.log(MAX_DIST / max_exact) * (nb - max_exact)).astype(jnp.int32)
    large = jnp.minimum(large, nb - 1)
    return ret + jnp.where(n < max_exact, n, large)


def _bucket_index(n_q, n_k):
    i = jnp.arange(n_q)[:, None]
    j = jnp.arange(n_k)[None, :]
    return _t5_bucket(j - WINDOW - i).astype(jnp.int32)


_NT = (((1,), (1,)), ((), ()))
_TN = (((0,), (0,)), ((), ()))


def _sink_softmax_keys_major(lg, sink_t):
    sink2 = sink_t * LOG2E
    m = jnp.maximum(jnp.max(lg, axis=0, keepdims=True), sink2)
    e = jnp.exp2(lg - m)
    den = jnp.sum(e, axis=0, keepdims=True) + jnp.exp2(sink2 - m)
    return e, 1.0 / den


def _attn_chunk_logits(qc, kw, bias_t):
    logits = []
    for kv in range(N_KV):
        qg = jnp.concatenate(
            [qc[:, (kv * GROUP + g) * HEAD_DIM:(kv * GROUP + g + 1) * HEAD_DIM] for g in range(GROUP)], axis=0)
        kk = kw[:, kv * HEAD_DIM:(kv + 1) * HEAD_DIM]
        logits.append(lax.dot_general(kk, qg, _NT, preferred_element_type=_F32))
    return jnp.concatenate(logits, axis=1) + bias_t


def _attn_chunk_output(lg, vw, sink_t):
    gq = lg.shape[1] // N_KV
    e, inv = _sink_softmax_keys_major(lg, sink_t)
    e = e.astype(_BF)
    outs = []
    for kv in range(N_KV):
        vv = vw[:, kv * HEAD_DIM:(kv + 1) * HEAD_DIM]
        cols = slice(kv * gq, (kv + 1) * gq)
        outs.append(lax.dot_general(vv, e[:, cols], _TN, preferred_element_type=_F32) * inv[:, cols])
    return jnp.concatenate(outs, axis=0).T


ATTN_CHUNKS_PER_BLOCK = 8


def _attn_prompt_kernel(*refs, tiles_per_batch, n_gate_blocks):
    q_ref, kp_ref, kc_ref, vp_ref, vc_ref, bias_ref, sink_ref, n_ref = refs[:8]
    wgate_refs = refs[8:8 + n_gate_blocks]
    o_ref, gate_ref, kwin, vwin = refs[8 + n_gate_blocks:]
    ta = q_ref.shape[0]
    nk = WINDOW + CHUNK
    dkv = N_KV * HEAD_DIM
    rows = ATTN_CHUNKS_PER_BLOCK * CHUNK
    gate_cols = gate_ref.shape[1] // n_gate_blocks
    kwin[0:WINDOW, :] = kp_ref[...].astype(_BF)
    kwin[WINDOW:, :] = kc_ref[...].astype(_BF)
    vwin[0:WINDOW, :] = vp_ref[...].astype(_BF)
    vwin[WINDOW:, :] = vc_ref[...].astype(_BF)
    tile_pos = (pl.program_id(0) % tiles_per_batch) * ta

    def logits(r):
        n_before = jnp.maximum(0, (WINDOW - (tile_pos + r)) // CHUNK)
        return _attn_chunk_logits(q_ref[pl.ds(r, CHUNK), :], kwin[pl.ds(r, nk), :], bias_ref[n_before])

    def finish(r, lg):
        o = _attn_chunk_output(lg, vwin[pl.ds(r, nk), :], sink_ref[...])
        for g in range(GROUP):
            o_ref[pl.ds(r, CHUNK), g * dkv:(g + 1) * dkv] = o[g * CHUNK:(g + 1) * CHUNK].astype(_BF)

    def block(b, carry):
        r0 = pl.multiple_of(b * rows, rows)
        n = n_ref[pl.ds(r0, rows), :]
        chunk_rows = [pl.multiple_of(r0 + c * CHUNK, CHUNK) for c in range(ATTN_CHUNKS_PER_BLOCK)]
        def slab(s):
            gate = jax.nn.sigmoid(_dot(n, wgate_refs[s][...]))
            gate_ref[pl.ds(r0, rows), s * gate_cols:(s + 1) * gate_cols] = gate.astype(_BF)

        per = n_gate_blocks // ATTN_CHUNKS_PER_BLOCK
        slab(0)
        lg = logits(chunk_rows[0])
        for c in range(ATTN_CHUNKS_PER_BLOCK):
            lg_next = logits(chunk_rows[c + 1]) if c + 1 < ATTN_CHUNKS_PER_BLOCK else None
            for s in range(c * per + 1, min((c + 1) * per + 1, n_gate_blocks)):
                slab(s)
            finish(chunk_rows[c], lg)
            lg = lg_next
        return carry

    lax.fori_loop(0, ta // rows, block, 0)


def _attn_prompt(q, k, v, bias, sink, n, w_in, *, gate_col, gate_cols, gate_block, batch, ta, name, casts=()):
    M, dq = q.shape
    dk = k.shape[1]
    D = n.shape[1]
    n_gate_blocks = gate_cols // gate_block
    tiles_per_batch = (M // batch) // ta
    wb = ta // WINDOW
    nk = WINDOW + CHUNK

    def prev(i):
        return (jnp.where(i % tiles_per_batch == 0, i * wb, i * wb - 1), 0)

    row = lambda i: (i, 0)
    grid = (M // ta,)
    c_in, c_out, c_shape, c_bytes = _cast_specs(casts, grid)
    blocks = [_nbytes((ta, dq), _BF) * 2, 2 * _nbytes((WINDOW + ta, dk), _F32),
              _nbytes(bias.shape, _F32), _nbytes((8, N_HEADS * CHUNK), _F32), c_bytes,
              _nbytes((ta, D + gate_cols), _BF), _nbytes((D, gate_cols), _BF) // 2]
    gate_specs = [pl.BlockSpec((D, gate_block), lambda i, b=gate_col // gate_block + c: (0, b),
                               pipeline_mode=pl.Buffered(1)) for c in range(n_gate_blocks)]
    body = functools.partial(_attn_prompt_kernel, tiles_per_batch=tiles_per_batch, n_gate_blocks=n_gate_blocks)
    return pl.pallas_call(
        _with_casts(body, 8 + n_gate_blocks, 2, len(casts)),
        grid=grid,
        in_specs=[
            pl.BlockSpec((ta, dq), row),
            pl.BlockSpec((WINDOW, dk), prev),
            pl.BlockSpec((ta, dk), row),
            pl.BlockSpec((WINDOW, dk), prev),
            pl.BlockSpec((ta, dk), row),
            pl.BlockSpec(bias.shape, lambda i: (0, 0, 0)),
            pl.BlockSpec((1, N_HEADS * CHUNK), lambda i: (0, 0)),
            pl.BlockSpec((ta, D), row),
        ] + gate_specs + c_in,
        out_specs=[pl.BlockSpec((ta, dq), row), pl.BlockSpec((ta, gate_cols), row)] + c_out,
        out_shape=[jax.ShapeDtypeStruct((M, dq), _BF), jax.ShapeDtypeStruct((M, gate_cols), _BF)] + c_shape,
        scratch_shapes=[pltpu.VMEM((WINDOW + ta, dk), _BF), pltpu.VMEM((WINDOW + ta, dk), _BF)],
        compiler_params=pltpu.CompilerParams(
            dimension_semantics=("arbitrary",),
            vmem_limit_bytes=_vmem_limit(blocks, 2 * _nbytes((WINDOW + ta, dk), _BF)),
        ),
        name=name,
    )(q, k, k, v, v, bias, sink, n, *([w_in] * n_gate_blocks), *_cast_args(casts))


LANES = 128


def _attn_sample_kernel(q_ref, ck_ref, cv_ref, k_ref, v_ref, bias_ref, sink_ref, o_ref, *, nb, s):
    dkv = N_KV * HEAD_DIM
    gs = GROUP * s
    assert LANES == 2 * HEAD_DIM and dkv == 2 * LANES
    for b in range(nb):
        rows = slice(b * s, (b + 1) * s)
        kw = jnp.concatenate([ck_ref[b].astype(_BF), k_ref[rows, :].astype(_BF)], axis=0)
        vw = jnp.concatenate([cv_ref[b].astype(_BF), v_ref[rows, :].astype(_BF)], axis=0)
        qf = q_ref[rows, :].astype(_F32)
        blocks = []
        for kv in range(N_KV):
            for g in range(GROUP):
                h = kv * GROUP + g
                pair = qf[:, (h // 2) * LANES:(h // 2 + 1) * LANES]
                if h % 2 != kv % 2:
                    pair = pltpu.roll(pair, HEAD_DIM, 1)
                half = lax.broadcasted_iota(jnp.int32, pair.shape, 1) // HEAD_DIM
                piece = jnp.where(half == kv % 2, pair, 0.0)
                zero = jnp.zeros_like(piece)
                blocks.append(jnp.concatenate([piece, zero] if kv < 2 else [zero, piece], axis=1))
        q_bd = jnp.concatenate(blocks, axis=0).astype(_BF)
        lg = lax.dot_general(kw, q_bd, _NT, preferred_element_type=_F32) + bias_ref[0]
        e, inv = _sink_softmax_keys_major(lg, sink_ref[...])
        z = lax.dot_general((e * inv).astype(_BF), vw, _TN, preferred_element_type=_F32)
        row_kv = lax.broadcasted_iota(jnp.int32, z.shape, 0) // gs
        col_kv = lax.broadcasted_iota(jnp.int32, z.shape, 1) // HEAD_DIM
        z = jnp.where(row_kv == col_kv, z, 0.0)
        r = z[0:gs]
        for kv in range(1, N_KV):
            r = r + z[kv * gs:(kv + 1) * gs]
        for g in range(GROUP):
            o_ref[rows, g * dkv:(g + 1) * dkv] = r[g * s:(g + 1) * s].astype(_BF)


def _attn_sample(q, k, v, cache_k, cache_v, bias, sink, *, nb, name):
    B, W, dk = cache_k.shape
    M, dq = q.shape
    S = M // B
    row = lambda b: (b, 0)
    return pl.pallas_call(
        functools.partial(_attn_sample_kernel, nb=nb, s=S),
        grid=(B // nb,),
        in_specs=[
            pl.BlockSpec((nb * S, dq), row),
            pl.BlockSpec((nb, W, dk), lambda b: (b, 0, 0)),
            pl.BlockSpec((nb, W, dk), lambda b: (b, 0, 0)),
            pl.BlockSpec((nb * S, dk), row),
            pl.BlockSpec((nb * S, dk), row),
            pl.BlockSpec(bias.shape, lambda b: (0, 0, 0)),
            pl.BlockSpec(sink.shape, lambda b: (0, 0)),
        ],
        out_specs=pl.BlockSpec((nb * S, dq), row),
        out_shape=jax.ShapeDtypeStruct((M, dq), _BF),
        compiler_params=pltpu.CompilerParams(dimension_semantics=("arbitrary",)),
        name=name,
    )(q, cache_k, cache_v, k, v, bias, sink)


def _gates_kernel(n_ref, w_ref, gate_ref):
    gate_ref[...] = jax.nn.sigmoid(_dot(n_ref[...], w_ref[...])).astype(_BF)


def _gates(n, w_in, *, gate_col, gate_cols, gate_block, name):
    M, D = n.shape
    first = gate_col // gate_block
    return pl.pallas_call(
        _gates_kernel,
        grid=(gate_cols // gate_block,),
        in_specs=[pl.BlockSpec((M, D), lambda c: (0, 0)), pl.BlockSpec((D, gate_block), lambda c: (0, first + c))],
        out_specs=pl.BlockSpec((M, gate_block), lambda c: (0, c)),
        out_shape=jax.ShapeDtypeStruct((M, gate_cols), _BF),
        compiler_params=pltpu.CompilerParams(
            dimension_semantics=("arbitrary",),
            vmem_limit_bytes=_vmem_limit([_nbytes((M, D + gate_block), _BF), _nbytes((D, gate_block), _BF)]),
        ),
        name=name,
    )(n, w_in)


def _merge_kernel(h_ref, gate_ref, ucb_ref, o_ref, wco_ref, wao_ref, wout_ref, g_ref, out_ref, xn_ref, mix_ref,
                  *, n_chunks):
    ucb, o = ucb_ref[...], o_ref[...]
    D = out_ref.shape[1]
    cw = D // n_chunks
    for c in range(n_chunks):
        cols = slice(c * cw, (c + 1) * cw)
        y_conv = _dot(ucb, wco_ref[:, cols])
        y_attn = _dot(o, wao_ref[:, cols])
        g_conv = gate_ref[:, cols].astype(_F32)
        g_attn = gate_ref[:, D + c * cw:D + (c + 1) * cw].astype(_F32)
        mix_ref[:, cols] = (g_conv * y_conv + g_attn * y_attn).astype(_BF)
    h = h_ref[...] + _dot(mix_ref[...], wout_ref[...])
    out_ref[...] = h
    xn_ref[...] = _rms(h, g_ref[...]).astype(_BF)


def _merge(h, gates, ucb, o, w_conv_out, w_attn_o, w_out, g, *, cw, tm, name):
    M, D = h.shape
    dc, da = ucb.shape[1], o.shape[1]
    row = lambda i: (i, 0)
    once = pl.Buffered(1)
    const = lambda i: (0, 0)
    blocks = [2 * _nbytes((tm, D), _F32), _nbytes((tm, 3 * D + dc + da), _BF), _nbytes((dc + da + D, D), _BF) // 2]
    return pl.pallas_call(
        functools.partial(_merge_kernel, n_chunks=D // cw),
        grid=(M // tm,),
        in_specs=[
            pl.BlockSpec((tm, D), row),
            pl.BlockSpec((tm, 2 * D), row),
            pl.BlockSpec((tm, dc), row),
            pl.BlockSpec((tm, da), row),
            pl.BlockSpec((dc, D), const, pipeline_mode=once),
            pl.BlockSpec((da, D), const, pipeline_mode=once),
            pl.BlockSpec((D, D), const, pipeline_mode=once),
            pl.BlockSpec((1, D), const),
        ],
        out_specs=[pl.BlockSpec((tm, D), row), pl.BlockSpec((tm, D), row)],
        out_shape=[jax.ShapeDtypeStruct((M, D), _F32), jax.ShapeDtypeStruct((M, D), _BF)],
        scratch_shapes=[pltpu.VMEM((tm, D), _BF)],
        compiler_params=pltpu.CompilerParams(
            dimension_semantics=("arbitrary",),
            vmem_limit_bytes=_vmem_limit(blocks, _nbytes((tm, D), _BF)),
        ),
        name=name,
    )(h, gates, ucb, o, w_conv_out, w_attn_o, w_out, g)


def _ple_kernel(h_ref, pe_ref, g_ref, wp_ref, wg_ref, out_ref):
    rows = h_ref.shape[0] // 2
    halves = [slice(0, rows), slice(rows, 2 * rows)]
    normed = [_rms(h_ref[sl, :], g_ref[...]).astype(_BF) for sl in halves]
    embs = [_dot(pe_ref[sl, :].astype(_BF), wp_ref[...]) for sl in halves]
    gates = [_dot(nr, wg_ref[...]) for nr in normed]
    for sl, emb, gate in zip(halves, embs, gates):
        out_ref[sl, :] = h_ref[sl, :] + emb * jax.nn.sigmoid(gate)


def _ple(h, pe, g, w_ple, w_gate, *, tm, name):
    M, D = h.shape
    dp = pe.shape[1]
    row = lambda i: (i, 0)
    const = lambda i: (0, 0)
    blocks = [2 * _nbytes((tm, D), _F32), _nbytes((tm, dp), _F32), _nbytes((dp + D, D), _BF)]
    return pl.pallas_call(
        _ple_kernel,
        grid=(M // tm,),
        in_specs=[
            pl.BlockSpec((tm, D), row),
            pl.BlockSpec((tm, dp), row),
            pl.BlockSpec((1, D), const),
            pl.BlockSpec((dp, D), const),
            pl.BlockSpec((D, D), const),
        ],
        out_specs=pl.BlockSpec((tm, D), row),
        out_shape=jax.ShapeDtypeStruct((M, D), _F32),
        compiler_params=pltpu.CompilerParams(
            dimension_semantics=("arbitrary",),
            vmem_limit_bytes=_vmem_limit(blocks),
        ),
        name=name,
    )(h, pe, g, w_ple, w_gate)


def _row_tile(m, want):
    return want if m % want == 0 else m


def kernel(x_prompt, x_sample, p_prompt, p_sample, state_conv, cache_k, cache_v, rel_table, ffn1_norm, ffn1_wg, ffn1_wu, ffn1_wd, mix_norm, w_in, conv_w, q_norm, k_norm, attn_sink, w_conv_out, w_attn_o, w_out, ffn2_norm, ffn2_wg, ffn2_wu, ffn2_wd, ple_norm, w_ple, w_ple_gate):
    B, T, D = x_prompt.shape
    Bs, S, _ = x_sample.shape
    depth = ffn1_wg.shape[0]
    d_conv = conv_w.shape[2]
    dq, dk = N_HEADS * HEAD_DIM, N_KV * HEAD_DIM
    qkv_col = 3 * d_conv
    gc_col = qkv_col + dq + 2 * dk
    Mp, Ms = B * T, Bs * S
    tm_p = _row_tile(Mp, 512)
    tm_s = _row_tile(Ms, 512)
    tf = 1024
    tn = 512
    cw = 256

    nk_p, nk_s = WINDOW + CHUNK, WINDOW + S
    bias_p = _rel_bias(rel_table, _bucket_index(CHUNK, nk_p), WINDOW // CHUNK + 1, CHUNK, "rel_bias_prompt")
    bias_s = _rel_bias(rel_table, _bucket_index(S, nk_s), 1, 0, "rel_bias_sample")
    lane = jnp.arange(cw) // HEAD_DIM
    ones = (lane[:, None] == lane[None, :]).astype(_BF)

    yp = x_prompt.reshape(Mp, D)
    ys = x_sample.reshape(Ms, D)
    conv_p, k_p, v_p, conv_s, k_s, v_s = [], [], [], [], [], []
    head_to_group_major = lambda h: (h % GROUP) * N_KV + h // GROUP
    for l in range(depth):
        vec = lambda g: g[l].reshape(1, -1)
        h1_s, n_s, f1_wg, f1_wu, f1_wd = _ffn(ys, vec(ffn1_norm), ffn1_wg[l], ffn1_wu[l], ffn1_wd[l],
                                              vec(mix_norm), tm=tm_s, tf=tf // 2, name="ffn1_sample")
        f1 = (vec(ffn1_norm), f1_wg, f1_wu, f1_wd)
        gq = jnp.tile(q_norm[l], N_HEADS).reshape(1, dq)
        gk = jnp.tile(k_norm[l], N_KV).reshape(1, dk)
        sink_p = jnp.repeat(attn_sink[l], CHUNK).reshape(1, N_HEADS * CHUNK)
        sink_s = jnp.repeat(attn_sink[l], S).reshape(1, N_HEADS * S)

        h1, n, w_in_l = _ffn(yp, *f1, vec(mix_norm), tm=tm_p, tf=tf, name="ffn1_prompt", casts=(w_in[l],))
        ucb, ulast, q, k, v, f2_wg, f2_wu, w_co, w_ao_p, w_o = _inproj(
            n, w_in_l, conv_w[l], gq, gk, ones, tm=tm_p, batch=B, name="inproj_prompt",
            casts=(ffn2_wg[l], ffn2_wu[l], w_conv_out[l], (w_attn_o[l], HEAD_DIM, head_to_group_major), w_out[l]))
        o, gates, f2_wd, w_p, w_pg = _attn_prompt(
            q, k, v, bias_p, sink_p, n, w_in_l, gate_col=gc_col, gate_cols=2 * D, gate_block=tn // 2, batch=B,
            ta=tm_p, name="attn_prompt", casts=(ffn2_wd[l], w_ple[l], w_ple_gate[l]))
        h2, xn2 = _merge(h1, gates, ucb, o, w_co, w_ao_p, w_o, vec(ffn2_norm), cw=tn, tm=tm_p, name="merge_prompt")
        f2 = (f2_wg, f2_wu, f2_wd)
        h3, = _ffn(h2, xn2, *f2, None, tm=tm_p, tf=tf, name="ffn2_prompt")
        yp = _ple(h3, p_prompt[l].reshape(Mp, -1), vec(ple_norm), w_p, w_pg, tm=tm_p, name="ple_prompt")
        conv_p.append(ulast[:, 6:8])
        k_p.append(k.reshape(B, T, dk)[:, T - WINDOW:].reshape(B, WINDOW, N_KV, HEAD_DIM))
        v_p.append(v.reshape(B, T, dk)[:, T - WINDOW:].reshape(B, WINDOW, N_KV, HEAD_DIM))

        st = state_conv[l]
        zeros = jnp.zeros((Bs, S - 2, d_conv), _F32)
        e2 = jnp.concatenate([st, zeros], axis=1).reshape(Ms, d_conv)
        e1 = jnp.concatenate([st[:, 1:], zeros, zeros[:, :1]], axis=1).reshape(Ms, d_conv)
        h1, n = h1_s, n_s
        ucb, u, q, k, v = _inproj(n, w_in_l, conv_w[l], gq, gk, ones, tm=tm_s, seg=S, edges=(e1, e2),
                                  name="inproj_sample")
        o = _attn_sample(q, k, v, cache_k[l].reshape(Bs, WINDOW, dk), cache_v[l].reshape(Bs, WINDOW, dk),
                         bias_s, sink_s, nb=4, name="attn_sample")
        gates = _gates(n, w_in_l, gate_col=gc_col, gate_cols=2 * D, gate_block=tn, name="gates_sample")
        h2, xn2 = _merge(h1, gates, ucb, o, w_co, w_ao_p, w_o, vec(ffn2_norm), cw=tn, tm=tm_s, name="merge_sample")
        h3, = _ffn(h2, xn2, *f2, None, tm=tm_s, tf=tf, name="ffn2_sample")
        ys = _ple(h3, p_sample[l].reshape(Ms, -1), vec(ple_norm), w_p, w_pg, tm=tm_s, name="ple_sample")
        conv_s.append(u.reshape(Bs, S, d_conv)[:, S - 2:])
        k_s.append(k.reshape(Bs, S, N_KV, HEAD_DIM))
        v_s.append(v.reshape(Bs, S, N_KV, HEAD_DIM))

    return (yp.reshape(B, T, D), ys.reshape(Bs, S, D), jnp.stack(conv_p), jnp.stack(k_p), jnp.stack(v_p),
            jnp.stack(conv_s), jnp.stack(k_s), jnp.stack(v_s))
```

```python
import functools
import math

import jax
import jax.numpy as jnp
from jax import lax
from jax.experimental import pallas as pl
from jax.experimental.pallas import tpu as pltpu

_BF = jnp.bfloat16
_F32 = jnp.float32

CHUNK = 64
WINDOW = 128
N_HEADS = 16
N_KV = 4
GROUP = N_HEADS // N_KV
HEAD_DIM = 64
N_BUCKETS = 32
MAX_DIST = 128
EPS = 1e-6
LOG2E = math.log2(math.e)
QK_SCALE = HEAD_DIM ** -0.5 * LOG2E

V7X_VMEM_BYTES = 64 * 1024 * 1024
VMEM_INTERNAL_BYTES = 12 * 1024 * 1024


def _vmem_limit(block_bytes, scratch_bytes=0):
    need = 2 * sum(block_bytes) + scratch_bytes + VMEM_INTERNAL_BYTES
    return int(min(need, V7X_VMEM_BYTES - 4 * 1024 * 1024))


def _nbytes(shape, dtype):
    return math.prod(shape) * jnp.dtype(dtype).itemsize


def _dot(a, b):
    return jnp.dot(a, b, preferred_element_type=_F32)


def _rms(x, g):
    ms = jnp.mean(x * x, axis=-1, keepdims=True)
    return x * lax.rsqrt(ms + EPS) * g


BF16_TILE_ROWS = 16


def _cast_specs(arrays, grid):
    steps = math.prod(grid)

    def flat(*ids):
        s = 0
        for k, g in zip(ids, grid):
            s = s * g + k
        return s

    in_specs, out_specs, out_shape, nbytes = [], [], [], 0
    for a in arrays:
        a, rb, out_block = a if isinstance(a, tuple) else (a, None, lambda b: b)
        rows, cols = a.shape
        rb = rb or max(BF16_TILE_ROWS, rows // steps)
        assert rows % rb == 0 and steps % (rows // rb) == 0, (a.shape, steps)
        rep = steps // (rows // rb)
        idx = lambda *ids, rep=rep: (flat(*ids) // rep, 0)
        out_idx = lambda *ids, rep=rep, out_block=out_block: (out_block(flat(*ids) // rep), 0)
        in_specs.append(pl.BlockSpec((rb, cols), idx))
        out_specs.append(pl.BlockSpec((rb, cols), out_idx))
        out_shape.append(jax.ShapeDtypeStruct((rows, cols), _BF))
        nbytes += _nbytes((rb, cols), _F32) + _nbytes((rb, cols), _BF)
    return in_specs, out_specs, out_shape, nbytes


def _cast_args(arrays):
    return [a[0] if isinstance(a, tuple) else a for a in arrays]


def _with_casts(body, n_in, n_out, n_cast):
    def kernel(*refs):
        outs_at = n_in + n_cast
        scratch_at = outs_at + n_out + n_cast
        for src, dst in zip(refs[n_in:outs_at], refs[outs_at + n_out:scratch_at]):
            dst[...] = src[...].astype(_BF)
        body(*refs[:n_in], *refs[outs_at:outs_at + n_out], *refs[scratch_at:])
    return kernel


def _ffn_kernel(*refs, emit_norm, keep_weights, normed_input, ring):
    x_ref, g_ref, wg_ref, wu_ref, wd_ref = refs[:5]
    rest = list(refs[5:])
    g2_ref = rest.pop(0) if emit_norm else None
    out_ref = rest.pop(0)
    n_ref = rest.pop(0) if emit_norm else None
    weight_copies = [rest.pop(0) for _ in range(3)] if keep_weights else []
    i = pl.program_id(0)
    j = pl.program_id(1)
    last = pl.num_programs(1) - 1
    if ring:
        sem = rest.pop()
        xbuf = rest.pop()
        x_hbm, tm = x_ref, xbuf.shape[1]

        def tile_copy(t):
            rows = pl.ds(pl.multiple_of(t * tm, tm), tm)
            return pltpu.make_async_copy(x_hbm.at[rows], xbuf.at[t % 2], sem.at[t % 2])

        @pl.when(j == 0)
        def _():
            @pl.when(i == 0)
            def _():
                tile_copy(i).start()

            tile_copy(i).wait()

            @pl.when(i + 1 < pl.num_programs(0))
            def _():
                tile_copy(i + 1).start()

        x_ref = xbuf.at[i % 2]
    if normed_input:
        xn_ref = g_ref
    else:
        xn_ref, = rest

        @pl.when(j == 0)
        def _():
            xn_ref[...] = _rms(x_ref[...], g_ref[...]).astype(_BF)

    wg, wu, wd = wg_ref[...], wu_ref[...], wd_ref[...]
    if keep_weights:
        wg, wu, wd = wg.astype(_BF), wu.astype(_BF), wd.astype(_BF)
        for dst, w in zip(weight_copies, (wg, wu, wd)):
            dst[...] = w
    xn = xn_ref[...]
    gate = _dot(xn, wg)
    up = _dot(xn, wu)
    hid = (jax.nn.silu(gate) * up * 0.5).astype(_BF)
    out_ref[...] = jnp.where(j == 0, x_ref[...], out_ref[...]) + _dot(hid, wd)

    if emit_norm:
        @pl.when(j == last)
        def _():
            n_ref[...] = _rms(out_ref[...], g2_ref[...]).astype(_BF)


def _ffn(x, g, wg, wu, wd, g2, *, tm, tf, name, casts=()):
    M, D = x.shape
    F = wg.shape[1]
    emit_norm = g2 is not None
    keep_weights = wg.dtype == _F32
    normed_input = g.shape[0] == M
    grid = (M // tm, F // tf)
    assert not keep_weights or grid[0] == 1
    row = lambda i, j: (i, 0)
    w_specs = [
        pl.BlockSpec((D, tf), lambda i, j: (0, j)),
        pl.BlockSpec((D, tf), lambda i, j: (0, j)),
        pl.BlockSpec((tf, D), lambda i, j: (j, 0)),
    ]
    g_spec = pl.BlockSpec((tm, D), row) if normed_input else pl.BlockSpec((1, D), lambda i, j: (0, 0))
    ring = grid[0] > 1
    x_spec = pl.BlockSpec(memory_space=pl.ANY) if ring else pl.BlockSpec((tm, D), row)
    in_specs = [x_spec, g_spec] + w_specs
    args = [x, g, wg, wu, wd]
    out_shape = [jax.ShapeDtypeStruct((M, D), _F32)]
    out_specs = [pl.BlockSpec((tm, D), row)]
    blocks = [_nbytes((tm, D), _F32) * (1 if ring else 2), 3 * _nbytes((D, tf), wg.dtype)]
    scratch = [] if normed_input else [pltpu.VMEM((tm, D), _BF)]
    scratch_bytes = _nbytes((tm, D), _BF)
    if ring:
        scratch += [pltpu.VMEM((2, tm, D), _F32), pltpu.SemaphoreType.DMA((2,))]
        scratch_bytes += _nbytes((2, tm, D), _F32)
    if emit_norm:
        in_specs.append(pl.BlockSpec((1, D), lambda i, j: (0, 0)))
        args.append(g2)
        out_shape.append(jax.ShapeDtypeStruct((M, D), _BF))
        out_specs.append(pl.BlockSpec((tm, D), row))
        blocks.append(_nbytes((tm, D), _BF))
    if keep_weights:
        out_shape += [jax.ShapeDtypeStruct(w.shape, _BF) for w in (wg, wu, wd)]
        out_specs += w_specs
        blocks.append(3 * _nbytes((D, tf), _BF))
    c_in, c_out, c_shape, c_bytes = _cast_specs(casts, grid)
    blocks.append(c_bytes)
    body = functools.partial(_ffn_kernel, emit_norm=emit_norm, keep_weights=keep_weights,
                             normed_input=normed_input, ring=ring)
    res = pl.pallas_call(
        _with_casts(body, len(in_specs), len(out_specs), len(casts)),
        grid=grid,
        in_specs=in_specs + c_in,
        out_specs=out_specs + c_out,
        out_shape=out_shape + c_shape,
        scratch_shapes=scratch,
        compiler_params=pltpu.CompilerParams(
            dimension_semantics=("arbitrary", "arbitrary"),
            vmem_limit_bytes=_vmem_limit(blocks, scratch_bytes),
        ),
        name=name,
    )(*args, *_cast_args(casts))
    return res


def _conv_kernel(*refs, carry_rows, seg, tiles_per_seg, d_conv, cw):
    if carry_rows:
        n_ref, w_ref, cwt_ref, ucb_ref, ulast_ref, carry_ref = refs
    else:
        n_ref, w_ref, cwt_ref, e1_ref, e2_ref, ucb_ref, u_ref = refs
    tm = n_ref.shape[0]
    n = n_ref[...]
    if carry_rows:
        @pl.when(pl.program_id(0) % tiles_per_seg == 0)
        def _():
            carry_ref[...] = jnp.zeros_like(carry_ref)

    for c in range(d_conv // cw):
        lo = c * cw
        cb = _dot(n, w_ref[:, lo:lo + cw])
        cc = _dot(n, w_ref[:, d_conv + lo:d_conv + lo + cw])
        cv = _dot(n, w_ref[:, 2 * d_conv + lo:2 * d_conv + lo + cw])
        u = cc * cv
        r1 = pltpu.roll(u, 1, 0)
        r2 = pltpu.roll(u, 2, 0)
        if carry_rows:
            prev = carry_ref[:, lo:lo + cw]
            rows8 = lax.broadcasted_iota(jnp.int32, (8, cw), 0)
            h1 = jnp.where(rows8 < 1, pltpu.roll(prev, 1, 0), r1[:8])
            h2 = jnp.where(rows8 < 2, pltpu.roll(prev, 2, 0), r2[:8])
            r1 = jnp.concatenate([h1, r1[8:]], axis=0)
            r2 = jnp.concatenate([h2, r2[8:]], axis=0)
            carry_ref[:, lo:lo + cw] = u[tm - 8:]
            ulast_ref[0, :, lo:lo + cw] = u[tm - 8:]
        else:
            pos = lax.broadcasted_iota(jnp.int32, (tm, cw), 0) % seg
            r1 = jnp.where(pos < 1, e1_ref[:, lo:lo + cw], r1)
            r2 = jnp.where(pos < 2, e2_ref[:, lo:lo + cw], r2)
            u_ref[:, lo:lo + cw] = u
        w0 = cwt_ref[0:1, lo:lo + cw]
        w1 = cwt_ref[1:2, lo:lo + cw]
        w2 = cwt_ref[2:3, lo:lo + cw]
        conv = w0 * r2 + w1 * r1 + w2 * u
        ucb_ref[:, lo:lo + cw] = (cb * conv).astype(_BF)


def _qkv_kernel(n_ref, w_ref, gq_ref, gk_ref, ones_ref, q_ref, k_ref, v_ref, *, cw, col0):
    ones = ones_ref[...]
    dq = q_ref.shape[1]
    dk = k_ref.shape[1]
    p = _dot(n_ref[...], w_ref[:, col0:col0 + dq + 2 * dk])
    x = p[:, :dq + dk]
    sq = x * x
    hi = sq.astype(_BF)
    lo = (sq - hi.astype(_F32)).astype(_BF)
    ss = jnp.concatenate(
        [_dot(hi[:, c * cw:(c + 1) * cw], ones) + _dot(lo[:, c * cw:(c + 1) * cw], ones)
         for c in range((dq + dk) // cw)], axis=1)
    xr = x * lax.rsqrt(ss * (1.0 / HEAD_DIM) + EPS)
    q_ref[...] = (xr[:, :dq] * gq_ref[...] * QK_SCALE).astype(_BF)
    k_ref[...] = xr[:, dq:] * gk_ref[...]
    v_ref[...] = p[:, dq + dk:]


def _inproj_kernel(*refs, carry_rows, seg, tiles_per_seg, d_conv, cw):
    if carry_rows:
        (n_ref, w_ref, cwt_ref, gq_ref, gk_ref, ones_ref,
         ucb_ref, ulast_ref, q_ref, k_ref, v_ref, carry_ref) = refs
        conv_refs = (n_ref, w_ref, cwt_ref, ucb_ref, ulast_ref, carry_ref)
    else:
        (n_ref, w_ref, cwt_ref, gq_ref, gk_ref, ones_ref, e1_ref, e2_ref,
         ucb_ref, u_ref, q_ref, k_ref, v_ref) = refs
        conv_refs = (n_ref, w_ref, cwt_ref, e1_ref, e2_ref, ucb_ref, u_ref)
    _conv_kernel(*conv_refs, carry_rows=carry_rows, seg=seg, tiles_per_seg=tiles_per_seg, d_conv=d_conv, cw=cw)
    _qkv_kernel(n_ref, w_ref, gq_ref, gk_ref, ones_ref, q_ref, k_ref, v_ref, cw=cw, col0=3 * d_conv)


def _inproj(n, w_in, conv_w, gq, gk, ones, *, tm, name, batch=None, seg=None, edges=(), casts=()):
    M, D = n.shape
    d_conv = conv_w.shape[1]
    cw = ones.shape[0]
    dq, dk = N_HEADS * HEAD_DIM, N_KV * HEAD_DIM
    wcols = 3 * d_conv + dq + 2 * dk
    carry_rows = batch is not None
    grid = (M // tm,)
    tiles_per_seg = (M // batch) // tm if carry_rows else None
    c_in, c_out, c_shape, c_bytes = _cast_specs(casts, grid)
    const = lambda i: (0, 0)
    row = lambda i: (i, 0)
    in_specs = [
        pl.BlockSpec((tm, D), row),
        pl.BlockSpec((D, wcols), const, pipeline_mode=pl.Buffered(1)),
        pl.BlockSpec((3, d_conv), const),
        pl.BlockSpec((1, dq), const),
        pl.BlockSpec((1, dk), const),
        pl.BlockSpec((cw, cw), const),
    ] + [pl.BlockSpec((tm, d_conv), row) for _ in edges]
    if carry_rows:
        u_spec = pl.BlockSpec((1, 8, d_conv), lambda i: (i // tiles_per_seg, 0, 0))
        u_shape = jax.ShapeDtypeStruct((batch, 8, d_conv), _F32)
        scratch = [pltpu.VMEM((8, d_conv), _F32)]
    else:
        u_spec = pl.BlockSpec((tm, d_conv), row)
        u_shape = jax.ShapeDtypeStruct((M, d_conv), _F32)
        scratch = []
    out_specs = [pl.BlockSpec((tm, d_conv), row), u_spec,
                 pl.BlockSpec((tm, dq), row), pl.BlockSpec((tm, dk), row), pl.BlockSpec((tm, dk), row)]
    out_shape = [jax.ShapeDtypeStruct((M, d_conv), _BF), u_shape, jax.ShapeDtypeStruct((M, dq), _BF),
                 jax.ShapeDtypeStruct((M, dk), _F32), jax.ShapeDtypeStruct((M, dk), _F32)]
    blocks = [_nbytes((tm, D), _BF), _nbytes((D, wcols), _BF) // 2, (1 + len(edges)) * _nbytes((tm, d_conv), _F32),
              _nbytes((tm, d_conv + dq), _BF), 2 * _nbytes((tm, dk), _F32), c_bytes]
    body = functools.partial(_inproj_kernel, carry_rows=carry_rows, seg=seg, tiles_per_seg=tiles_per_seg,
                             d_conv=d_conv, cw=cw)
    return pl.pallas_call(
        _with_casts(body, len(in_specs), len(out_specs), len(casts)),
        grid=grid,
        in_specs=in_specs + c_in,
        out_specs=out_specs + c_out,
        out_shape=out_shape + c_shape,
        scratch_shapes=scratch,
        compiler_params=pltpu.CompilerParams(
            dimension_semantics=("arbitrary",),
            vmem_limit_bytes=_vmem_limit(blocks),
        ),
        name=name,
    )(n, w_in, conv_w, gq, gk, ones, *edges, *_cast_args(casts))


def _bias_kernel(tab_ref, bkt_ref, out_ref, *, mask_step):
    bkt = bkt_ref[...]
    key = lax.broadcasted_iota(jnp.int32, bkt.shape, 1)
    for h in range(out_ref.shape[1]):
        acc = jnp.zeros(bkt.shape, _F32)
        for b in range(tab_ref.shape[0]):
            acc = jnp.where(bkt == b, tab_ref[b, h] * LOG2E, acc)
        for v in range(out_ref.shape[0]):
            out_ref[v, h] = jnp.where(key < v * mask_step, -jnp.inf, acc)


def _rel_bias(table, bucket, n_variants, mask_step, name):
    nq, nk = bucket.shape
    nh = table.shape[1]
    out = pl.pallas_call(
        functools.partial(_bias_kernel, mask_step=mask_step),
        in_specs=[pl.BlockSpec(memory_space=pltpu.SMEM), pl.BlockSpec((nq, nk), lambda: (0, 0))],
        out_specs=pl.BlockSpec((n_variants, nh, nq, nk), lambda: (0, 0, 0, 0)),
        out_shape=jax.ShapeDtypeStruct((n_variants, nh, nq, nk), _F32),
        name=name,
    )(table, bucket)
    return out.transpose(0, 3, 1, 2).reshape(n_variants, nk, nh * nq)


def _t5_bucket(rel):
    nb = N_BUCKETS // 2
    max_exact = nb // 2
    ret = jnp.where(rel > 0, nb, 0)
    n = jnp.abs(rel)
    nf = jnp.maximum(n, 1).astype(jnp.float32)
    large = max_exact + (jnp.log(nf / max_exact) / math.log(MAX_DIST / max_exact) * (nb - max_exact)).astype(jnp.int32)
    large = jnp.minimum(large, nb - 1)
    return ret + jnp.where(n < max_exact, n, large)


def _bucket_index(n_q, n_k):
    i = jnp.arange(n_q)[:, None]
    j = jnp.arange(n_k)[None, :]
    return _t5_bucket(j - WINDOW - i).astype(jnp.int32)


_NT = (((1,), (1,)), ((), ()))
_TN = (((0,), (0,)), ((), ()))


def _sink_softmax_keys_major(lg, sink_t):
    sink2 = sink_t * LOG2E
    m = jnp.maximum(jnp.max(lg, axis=0, keepdims=True), sink2)
    e = jnp.exp2(lg - m)
    den = jnp.sum(e, axis=0, keepdims=True) + jnp.exp2(sink2 - m)
    return e, 1.0 / den


def _attn_chunk_logits(qc, kw, bias_t):
    logits = []
    for kv in range(N_KV):
        qg = jnp.concatenate(
            [qc[:, (kv * GROUP + g) * HEAD_DIM:(kv * GROUP + g + 1) * HEAD_DIM] for g in range(GROUP)], axis=0)
        kk = kw[:, kv * HEAD_DIM:(kv + 1) * HEAD_DIM]
        logits.append(lax.dot_general(kk, qg, _NT, preferred_element_type=_F32))
    return jnp.concatenate(logits, axis=1) + bias_t


def _attn_chunk_output(lg, vw, sink_t):
    gq = lg.shape[1] // N_KV
    e, inv = _sink_softmax_keys_major(lg, sink_t)
    e = e.astype(_BF)
    outs = []
    for kv in range(N_KV):
        vv = vw[:, kv * HEAD_DIM:(kv + 1) * HEAD_DIM]
        cols = slice(kv * gq, (kv + 1) * gq)
        outs.append(lax.dot_general(vv, e[:, cols], _TN, preferred_element_type=_F32) * inv[:, cols])
    return jnp.concatenate(outs, axis=0).T


ATTN_CHUNKS_PER_BLOCK = 8


def _attn_prompt_kernel(*refs, tiles_per_batch, n_gate_blocks):
    q_ref, kp_ref, kc_ref, vp_ref, vc_ref, bias_ref, sink_ref, n_ref = refs[:8]
    wgate_refs = refs[8:8 + n_gate_blocks]
    o_ref, gate_ref, kwin, vwin = refs[8 + n_gate_blocks:]
    ta = q_ref.shape[0]
    nk = WINDOW + CHUNK
    dkv = N_KV * HEAD_DIM
    rows = ATTN_CHUNKS_PER_BLOCK * CHUNK
    gate_cols = gate_ref.shape[1] // n_gate_blocks
    kwin[0:WINDOW, :] = kp_ref[...].astype(_BF)
    kwin[WINDOW:, :] = kc_ref[...].astype(_BF)
    vwin[0:WINDOW, :] = vp_ref[...].astype(_BF)
    vwin[WINDOW:, :] = vc_ref[...].astype(_BF)
    tile_pos = (pl.program_id(0) % tiles_per_batch) * ta

    def logits(r):
        n_before = jnp.maximum(0, (WINDOW - (tile_pos + r)) // CHUNK)
        return _attn_chunk_logits(q_ref[pl.ds(r, CHUNK), :], kwin[pl.ds(r, nk), :], bias_ref[n_before])

    def finish(r, lg):
        o = _attn_chunk_output(lg, vwin[pl.ds(r, nk), :], sink_ref[...])
        for g in range(GROUP):
            o_ref[pl.ds(r, CHUNK), g * dkv:(g + 1) * dkv] = o[g * CHUNK:(g + 1) * CHUNK].astype(_BF)

    def block(b, carry):
        r0 = pl.multiple_of(b * rows, rows)
        n = n_ref[pl.ds(r0, rows), :]
        chunk_rows = [pl.multiple_of(r0 + c * CHUNK, CHUNK) for c in range(ATTN_CHUNKS_PER_BLOCK)]
        def slab(s):
            gate = jax.nn.sigmoid(_dot(n, wgate_refs[s][...]))
            gate_ref[pl.ds(r0, rows), s * gate_cols:(s + 1) * gate_cols] = gate.astype(_BF)

        per = n_gate_blocks // ATTN_CHUNKS_PER_BLOCK
        slab(0)
        lg = logits(chunk_rows[0])
        for c in range(ATTN_CHUNKS_PER_BLOCK):
            lg_next = logits(chunk_rows[c + 1]) if c + 1 < ATTN_CHUNKS_PER_BLOCK else None
            for s in range(c * per + 1, min((c + 1) * per + 1, n_gate_blocks)):
                slab(s)
            finish(chunk_rows[c], lg)
            lg = lg_next
        return carry

    lax.fori_loop(0, ta // rows, block, 0)


def _attn_prompt(q, k, v, bias, sink, n, w_in, *, gate_col, gate_cols, gate_block, batch, ta, name, casts=()):
    M, dq = q.shape
    dk = k.shape[1]
    D = n.shape[1]
    n_gate_blocks = gate_cols // gate_block
    tiles_per_batch = (M // batch) // ta
    wb = ta // WINDOW
    nk = WINDOW + CHUNK

    def prev(i):
        return (jnp.where(i % tiles_per_batch == 0, i * wb, i * wb - 1), 0)

    row = lambda i: (i, 0)
    grid = (M // ta,)
    c_in, c_out, c_shape, c_bytes = _cast_specs(casts, grid)
    blocks = [_nbytes((ta, dq), _BF) * 2, 2 * _nbytes((WINDOW + ta, dk), _F32),
              _nbytes(bias.shape, _F32), _nbytes((8, N_HEADS * CHUNK), _F32), c_bytes,
              _nbytes((ta, D + gate_cols), _BF), _nbytes((D, gate_cols), _BF) // 2]
    gate_specs = [pl.BlockSpec((D, gate_block), lambda i, b=gate_col // gate_block + c: (0, b),
                               pipeline_mode=pl.Buffered(1)) for c in range(n_gate_blocks)]
    body = functools.partial(_attn_prompt_kernel, tiles_per_batch=tiles_per_batch, n_gate_blocks=n_gate_blocks)
    return pl.pallas_call(
        _with_casts(body, 8 + n_gate_blocks, 2, len(casts)),
        grid=grid,
        in_specs=[
            pl.BlockSpec((ta, dq), row),
            pl.BlockSpec((WINDOW, dk), prev),
            pl.BlockSpec((ta, dk), row),
            pl.BlockSpec((WINDOW, dk), prev),
            pl.BlockSpec((ta, dk), row),
            pl.BlockSpec(bias.shape, lambda i: (0, 0, 0)),
            pl.BlockSpec((1, N_HEADS * CHUNK), lambda i: (0, 0)),
            pl.BlockSpec((ta, D), row),
        ] + gate_specs + c_in,
        out_specs=[pl.BlockSpec((ta, dq), row), pl.BlockSpec((ta, gate_cols), row)] + c_out,
        out_shape=[jax.ShapeDtypeStruct((M, dq), _BF), jax.ShapeDtypeStruct((M, gate_cols), _BF)] + c_shape,
        scratch_shapes=[pltpu.VMEM((WINDOW + ta, dk), _BF), pltpu.VMEM((WINDOW + ta, dk), _BF)],
        compiler_params=pltpu.CompilerParams(
            dimension_semantics=("arbitrary",),
            vmem_limit_bytes=_vmem_limit(blocks, 2 * _nbytes((WINDOW + ta, dk), _BF)),
        ),
        name=name,
    )(q, k, k, v, v, bias, sink, n, *([w_in] * n_gate_blocks), *_cast_args(casts))


LANES = 128


def _attn_sample_kernel(q_ref, ck_ref, cv_ref, k_ref, v_ref, bias_ref, sink_ref, o_ref, *, nb, s):
    dkv = N_KV * HEAD_DIM
    gs = GROUP * s
    assert LANES == 2 * HEAD_DIM and dkv == 2 * LANES
    for b in range(nb):
        rows = slice(b * s, (b + 1) * s)
        kw = jnp.concatenate([ck_ref[b].astype(_BF), k_ref[rows, :].astype(_BF)], axis=0)
        vw = jnp.concatenate([cv_ref[b].astype(_BF), v_ref[rows, :].astype(_BF)], axis=0)
        qf = q_ref[rows, :].astype(_F32)
        blocks = []
        for kv in range(N_KV):
            for g in range(GROUP):
                h = kv * GROUP + g
                pair = qf[:, (h // 2) * LANES:(h // 2 + 1) * LANES]
                if h % 2 != kv % 2:
                    pair = pltpu.roll(pair, HEAD_DIM, 1)
                half = lax.broadcasted_iota(jnp.int32, pair.shape, 1) // HEAD_DIM
                piece = jnp.where(half == kv % 2, pair, 0.0)
                zero = jnp.zeros_like(piece)
                blocks.append(jnp.concatenate([piece, zero] if kv < 2 else [zero, piece], axis=1))
        q_bd = jnp.concatenate(blocks, axis=0).astype(_BF)
        lg = lax.dot_general(kw, q_bd, _NT, preferred_element_type=_F32) + bias_ref[0]
        e, inv = _sink_softmax_keys_major(lg, sink_ref[...])
        z = lax.dot_general((e * inv).astype(_BF), vw, _TN, preferred_element_type=_F32)
        row_kv = lax.broadcasted_iota(jnp.int32, z.shape, 0) // gs
        col_kv = lax.broadcasted_iota(jnp.int32, z.shape, 1) // HEAD_DIM
        z = jnp.where(row_kv == col_kv, z, 0.0)
        r = z[0:gs]
        for kv in range(1, N_KV):
            r = r + z[kv * gs:(kv + 1) * gs]
        for g in range(GROUP):
            o_ref[rows, g * dkv:(g + 1) * dkv] = r[g * s:(g + 1) * s].astype(_BF)


def _attn_sample(q, k, v, cache_k, cache_v, bias, sink, *, nb, name):
    B, W, dk = cache_k.shape
    M, dq = q.shape
    S = M // B
    row = lambda b: (b, 0)
    return pl.pallas_call(
        functools.partial(_attn_sample_kernel, nb=nb, s=S),
        grid=(B // nb,),
        in_specs=[
            pl.BlockSpec((nb * S, dq), row),
            pl.BlockSpec((nb, W, dk), lambda b: (b, 0, 0)),
            pl.BlockSpec((nb, W, dk), lambda b: (b, 0, 0)),
            pl.BlockSpec((nb * S, dk), row),
            pl.BlockSpec((nb * S, dk), row),
            pl.BlockSpec(bias.shape, lambda b: (0, 0, 0)),
            pl.BlockSpec(sink.shape, lambda b: (0, 0)),
        ],
        out_specs=pl.BlockSpec((nb * S, dq), row),
        out_shape=jax.ShapeDtypeStruct((M, dq), _BF),
        compiler_params=pltpu.CompilerParams(dimension_semantics=("arbitrary",)),
        name=name,
    )(q, cache_k, cache_v, k, v, bias, sink)


def _gates_kernel(n_ref, w_ref, gate_ref):
    gate_ref[...] = jax.nn.sigmoid(_dot(n_ref[...], w_ref[...])).astype(_BF)


def _gates(n, w_in, *, gate_col, gate_cols, gate_block, name):
    M, D = n.shape
    first = gate_col // gate_block
    return pl.pallas_call(
        _gates_kernel,
        grid=(gate_cols // gate_block,),
        in_specs=[pl.BlockSpec((M, D), lambda c: (0, 0)), pl.BlockSpec((D, gate_block), lambda c: (0, first + c))],
        out_specs=pl.BlockSpec((M, gate_block), lambda c: (0, c)),
        out_shape=jax.ShapeDtypeStruct((M, gate_cols), _BF),
        compiler_params=pltpu.CompilerParams(
            dimension_semantics=("arbitrary",),
            vmem_limit_bytes=_vmem_limit([_nbytes((M, D + gate_block), _BF), _nbytes((D, gate_block), _BF)]),
        ),
        name=name,
    )(n, w_in)


def _merge_kernel(h_ref, gate_ref, ucb_ref, o_ref, wco_ref, wao_ref, wout_ref, g_ref, out_ref, xn_ref, mix_ref,
                  *, n_chunks):
    ucb, o = ucb_ref[...], o_ref[...]
    D = out_ref.shape[1]
    cw = D // n_chunks
    for c in range(n_chunks):
        cols = slice(c * cw, (c + 1) * cw)
        y_conv = _dot(ucb, wco_ref[:, cols])
        y_attn = _dot(o, wao_ref[:, cols])
        g_conv = gate_ref[:, cols].astype(_F32)
        g_attn = gate_ref[:, D + c * cw:D + (c + 1) * cw].astype(_F32)
        mix_ref[:, cols] = (g_conv * y_conv + g_attn * y_attn).astype(_BF)
    h = h_ref[...] + _dot(mix_ref[...], wout_ref[...])
    out_ref[...] = h
    xn_ref[...] = _rms(h, g_ref[...]).astype(_BF)


def _merge(h, gates, ucb, o, w_conv_out, w_attn_o, w_out, g, *, cw, tm, name):
    M, D = h.shape
    dc, da = ucb.shape[1], o.shape[1]
    row = lambda i: (i, 0)
    once = pl.Buffered(1)
    const = lambda i: (0, 0)
    blocks = [2 * _nbytes((tm, D), _F32), _nbytes((tm, 3 * D + dc + da), _BF), _nbytes((dc + da + D, D), _BF) // 2]
    return pl.pallas_call(
        functools.partial(_merge_kernel, n_chunks=D // cw),
        grid=(M // tm,),
        in_specs=[
            pl.BlockSpec((tm, D), row),
            pl.BlockSpec((tm, 2 * D), row),
            pl.BlockSpec((tm, dc), row),
            pl.BlockSpec((tm, da), row),
            pl.BlockSpec((dc, D), const, pipeline_mode=once),
            pl.BlockSpec((da, D), const, pipeline_mode=once),
            pl.BlockSpec((D, D), const, pipeline_mode=once),
            pl.BlockSpec((1, D), const),
        ],
        out_specs=[pl.BlockSpec((tm, D), row), pl.BlockSpec((tm, D), row)],
        out_shape=[jax.ShapeDtypeStruct((M, D), _F32), jax.ShapeDtypeStruct((M, D), _BF)],
        scratch_shapes=[pltpu.VMEM((tm, D), _BF)],
        compiler_params=pltpu.CompilerParams(
            dimension_semantics=("arbitrary",),
            vmem_limit_bytes=_vmem_limit(blocks, _nbytes((tm, D), _BF)),
        ),
        name=name,
    )(h, gates, ucb, o, w_conv_out, w_attn_o, w_out, g)


def _ple_kernel(h_ref, pe_ref, g_ref, wp_ref, wg_ref, out_ref):
    rows = h_ref.shape[0] // 2
    halves = [slice(0, rows), slice(rows, 2 * rows)]
    normed = [_rms(h_ref[sl, :], g_ref[...]).astype(_BF) for sl in halves]
    embs = [_dot(pe_ref[sl, :].astype(_BF), wp_ref[...]) for sl in halves]
    gates = [_dot(nr, wg_ref[...]) for nr in normed]
    for sl, emb, gate in zip(halves, embs, gates):
        out_ref[sl, :] = h_ref[sl, :] + emb * jax.nn.sigmoid(gate)


def _ple(h, pe, g, w_ple, w_gate, *, tm, name):
    M, D = h.shape
    dp = pe.shape[1]
    row = lambda i: (i, 0)
    const = lambda i: (0, 0)
    blocks = [2 * _nbytes((tm, D), _F32), _nbytes((tm, dp), _F32), _nbytes((dp + D, D), _BF)]
    return pl.pallas_call(
        _ple_kernel,
        grid=(M // tm,),
        in_specs=[
            pl.BlockSpec((tm, D), row),
            pl.BlockSpec((tm, dp), row),
            pl.BlockSpec((1, D), const),
            pl.BlockSpec((dp, D), const),
            pl.BlockSpec((D, D), const),
        ],
        out_specs=pl.BlockSpec((tm, D), row),
        out_shape=jax.ShapeDtypeStruct((M, D), _F32),
        compiler_params=pltpu.CompilerParams(
            dimension_semantics=("arbitrary",),
            vmem_limit_bytes=_vmem_limit(blocks),
        ),
        name=name,
    )(h, pe, g, w_ple, w_gate)


def _row_tile(m, want):
    return want if m % want == 0 else m


def kernel(x_prompt, x_sample, p_prompt, p_sample, state_conv, cache_k, cache_v, rel_table, ffn1_norm, ffn1_wg, ffn1_wu, ffn1_wd, mix_norm, w_in, conv_w, q_norm, k_norm, attn_sink, w_conv_out, w_attn_o, w_out, ffn2_norm, ffn2_wg, ffn2_wu, ffn2_wd, ple_norm, w_ple, w_ple_gate):
    B, T, D = x_prompt.shape
    Bs, S, _ = x_sample.shape
    depth = ffn1_wg.shape[0]
    d_conv = conv_w.shape[2]
    dq, dk = N_HEADS * HEAD_DIM, N_KV * HEAD_DIM
    qkv_col = 3 * d_conv
    gc_col = qkv_col + dq + 2 * dk
    Mp, Ms = B * T, Bs * S
    tm_p = _row_tile(Mp, 512)
    tm_s = _row_tile(Ms, 512)
    tf = 1024
    tn = 512
    cw = 256

    nk_p, nk_s = WINDOW + CHUNK, WINDOW + S
    bias_p = _rel_bias(rel_table, _bucket_index(CHUNK, nk_p), WINDOW // CHUNK + 1, CHUNK, "rel_bias_prompt")
    bias_s = _rel_bias(rel_table, _bucket_index(S, nk_s), 1, 0, "rel_bias_sample")
    lane = jnp.arange(cw) // HEAD_DIM
    ones = (lane[:, None] == lane[None, :]).astype(_BF)

    yp = x_prompt.reshape(Mp, D)
    ys = x_sample.reshape(Ms, D)
    conv_p, k_p, v_p, conv_s, k_s, v_s = [], [], [], [], [], []
    head_to_group_major = lambda h: (h % GROUP) * N_KV + h // GROUP
    for l in range(depth):
        vec = lambda g: g[l].reshape(1, -1)
        h1_s, n_s, f1_wg, f1_wu, f1_wd = _ffn(ys, vec(ffn1_norm), ffn1_wg[l], ffn1_wu[l], ffn1_wd[l],
                                              vec(mix_norm), tm=tm_s, tf=tf // 2, name="ffn1_sample")
        f1 = (vec(ffn1_norm), f1_wg, f1_wu, f1_wd)
        gq = jnp.tile(q_norm[l], N_HEADS).reshape(1, dq)
        gk = jnp.tile(k_norm[l], N_KV).reshape(1, dk)
        sink_p = jnp.repeat(attn_sink[l], CHUNK).reshape(1, N_HEADS * CHUNK)
        sink_s = jnp.repeat(attn_sink[l], S).reshape(1, N_HEADS * S)

        h1, n, w_in_l = _ffn(yp, *f1, vec(mix_norm), tm=tm_p, tf=tf, name="ffn1_prompt", casts=(w_in[l],))
        ucb, ulast, q, k, v, f2_wg, f2_wu, w_co, w_ao_p, w_o = _inproj(
            n, w_in_l, conv_w[l], gq, gk, ones, tm=tm_p, batch=B, name="inproj_prompt",
            casts=(ffn2_wg[l], ffn2_wu[l], w_conv_out[l], (w_attn_o[l], HEAD_DIM, head_to_group_major), w_out[l]))
        o, gates, f2_wd, w_p, w_pg = _attn_prompt(
            q, k, v, bias_p, sink_p, n, w_in_l, gate_col=gc_col, gate_cols=2 * D, gate_block=tn // 2, batch=B,
            ta=tm_p, name="attn_prompt", casts=(ffn2_wd[l], w_ple[l], w_ple_gate[l]))
        h2, xn2 = _merge(h1, gates, ucb, o, w_co, w_ao_p, w_o, vec(ffn2_norm), cw=tn, tm=tm_p, name="merge_prompt")
        f2 = (f2_wg, f2_wu, f2_wd)
        h3, = _ffn(h2, xn2, *f2, None, tm=tm_p, tf=tf, name="ffn2_prompt")
        yp = _ple(h3, p_prompt[l].reshape(Mp, -1), vec(ple_norm), w_p, w_pg, tm=tm_p, name="ple_prompt")
        conv_p.append(ulast[:, 6:8])
        k_p.append(k.reshape(B, T, dk)[:, T - WINDOW:].reshape(B, WINDOW, N_KV, HEAD_DIM))
        v_p.append(v.reshape(B, T, dk)[:, T - WINDOW:].reshape(B, WINDOW, N_KV, HEAD_DIM))

        st = state_conv[l]
        zeros = jnp.zeros((Bs, S - 2, d_conv), _F32)
        e2 = jnp.concatenate([st, zeros], axis=1).reshape(Ms, d_conv)
        e1 = jnp.concatenate([st[:, 1:], zeros, zeros[:, :1]], axis=1).reshape(Ms, d_conv)
        h1, n = h1_s, n_s
        ucb, u, q, k, v = _inproj(n, w_in_l, conv_w[l], gq, gk, ones, tm=tm_s, seg=S, edges=(e1, e2),
                                  name="inproj_sample")
        o = _attn_sample(q, k, v, cache_k[l].reshape(Bs, WINDOW, dk), cache_v[l].reshape(Bs, WINDOW, dk),
                         bias_s, sink_s, nb=4, name="attn_sample")
        gates = _gates(n, w_in_l, gate_col=gc_col, gate_cols=2 * D, gate_block=tn, name="gates_sample")
        h2, xn2 = _merge(h1, gates, ucb, o, w_co, w_ao_p, w_o, vec(ffn2_norm), cw=tn, tm=tm_s, name="merge_sample")
        h3, = _ffn(h2, xn2, *f2, None, tm=tm_s, tf=tf, name="ffn2_sample")
        ys = _ple(h3, p_sample[l].reshape(Ms, -1), vec(ple_norm), w_p, w_pg, tm=tm_s, name="ple_sample")
        conv_s.append(u.reshape(Bs, S, d_conv)[:, S - 2:])
        k_s.append(k.reshape(Bs, S, N_KV, HEAD_DIM))
        v_s.append(v.reshape(Bs, S, N_KV, HEAD_DIM))

    return (yp.reshape(B, T, D), ys.reshape(Bs, S, D), jnp.stack(conv_p), jnp.stack(k_p), jnp.stack(v_p),
            jnp.stack(conv_s), jnp.stack(k_s), jnp.stack(v_s))
```

```python
import functools
import math

import jax
import jax.numpy as jnp
from jax import lax
from jax.experimental import pallas as pl
from jax.experimental.pallas import tpu as pltpu

_BF = jnp.bfloat16
_F32 = jnp.float32

CHUNK = 64
WINDOW = 128
N_HEADS = 16
N_KV = 4
GROUP = N_HEADS // N_KV
HEAD_DIM = 64
N_BUCKETS = 32
MAX_DIST = 128
EPS = 1e-6
LOG2E = math.log2(math.e)
QK_SCALE = HEAD_DIM ** -0.5 * LOG2E

V7X_VMEM_BYTES = 64 * 1024 * 1024
VMEM_INTERNAL_BYTES = 12 * 1024 * 1024


def _vmem_limit(block_bytes, scratch_bytes=0):
    need = 2 * sum(block_bytes) + scratch_bytes + VMEM_INTERNAL_BYTES
    return int(min(need, V7X_VMEM_BYTES - 4 * 1024 * 1024))


def _nbytes(shape, dtype):
    return math.prod(shape) * jnp.dtype(dtype).itemsize


def _dot(a, b):
    return jnp.dot(a, b, preferred_element_type=_F32)


def _rms(x, g):
    ms = jnp.mean(x * x, axis=-1, keepdims=True)
    return x * lax.rsqrt(ms + EPS) * g


BF16_TILE_ROWS = 16


def _cast_specs(arrays, grid):
    steps = math.prod(grid)

    def flat(*ids):
        s = 0
        for k, g in zip(ids, grid):
            s = s * g + k
        return s

    in_specs, out_specs, out_shape, nbytes = [], [], [], 0
    for a in arrays:
        a, rb, out_block = a if isinstance(a, tuple) else (a, None, lambda b: b)
        rows, cols = a.shape
        rb = rb or max(BF16_TILE_ROWS, rows // steps)
        assert rows % rb == 0 and steps % (rows // rb) == 0, (a.shape, steps)
        rep = steps // (rows // rb)
        idx = lambda *ids, rep=rep: (flat(*ids) // rep, 0)
        out_idx = lambda *ids, rep=rep, out_block=out_block: (out_block(flat(*ids) // rep), 0)
        in_specs.append(pl.BlockSpec((rb, cols), idx))
        out_specs.append(pl.BlockSpec((rb, cols), out_idx))
        out_shape.append(jax.ShapeDtypeStruct((rows, cols), _BF))
        nbytes += _nbytes((rb, cols), _F32) + _nbytes((rb, cols), _BF)
    return in_specs, out_specs, out_shape, nbytes


def _cast_args(arrays):
    return [a[0] if isinstance(a, tuple) else a for a in arrays]


def _with_casts(body, n_in, n_out, n_cast):
    def kernel(*refs):
        outs_at = n_in + n_cast
        scratch_at = outs_at + n_out + n_cast
        for src, dst in zip(refs[n_in:outs_at], refs[outs_at + n_out:scratch_at]):
            dst[...] = src[...].astype(_BF)
        body(*refs[:n_in], *refs[outs_at:outs_at + n_out], *refs[scratch_at:])
    return kernel


def _ffn_kernel(*refs, emit_norm, keep_weights, normed_input):
    x_ref, g_ref, wg_ref, wu_ref, wd_ref = refs[:5]
    rest = list(refs[5:])
    g2_ref = rest.pop(0) if emit_norm else None
    out_ref = rest.pop(0)
    n_ref = rest.pop(0) if emit_norm else None
    weight_copies = [rest.pop(0) for _ in range(3)] if keep_weights else []
    j = pl.program_id(1)
    last = pl.num_programs(1) - 1
    if normed_input:
        xn_ref = g_ref
    else:
        xn_ref, = rest

        @pl.when(j == 0)
        def _():
            xn_ref[...] = _rms(x_ref[...], g_ref[...]).astype(_BF)

    def step(norm_chunks):
        wg, wu, wd = wg_ref[...], wu_ref[...], wd_ref[...]
        if keep_weights:
            wg, wu, wd = wg.astype(_BF), wu.astype(_BF), wd.astype(_BF)
            for dst, w in zip(weight_copies, (wg, wu, wd)):
                dst[...] = w
        xn = xn_ref[...]
        gate = _dot(xn, wg)
        up = _dot(xn, wu)
        hid = (jax.nn.silu(gate) * up * 0.5).astype(_BF)
        if not norm_chunks:
            out_ref[...] = jnp.where(j == 0, x_ref[...], out_ref[...]) + _dot(hid, wd)
            return
        per = out_ref.shape[0] // norm_chunks
        for r in range(norm_chunks):
            rows = slice(r * per, (r + 1) * per)
            h = jnp.where(j == 0, x_ref[rows, :], out_ref[rows, :]) + _dot(hid[rows], wd)
            out_ref[rows, :] = h
            n_ref[rows, :] = _rms(h, g2_ref[...]).astype(_BF)

    if emit_norm:
        pl.when(j != last)(lambda: step(0))
        pl.when(j == last)(lambda: step(4))
    else:
        step(0)


def _ffn(x, g, wg, wu, wd, g2, *, tm, tf, name, casts=()):
    M, D = x.shape
    F = wg.shape[1]
    emit_norm = g2 is not None
    keep_weights = wg.dtype == _F32
    normed_input = g.shape[0] == M
    grid = (M // tm, F // tf)
    assert not keep_weights or grid[0] == 1
    row = lambda i, j: (i, 0)
    w_specs = [
        pl.BlockSpec((D, tf), lambda i, j: (0, j)),
        pl.BlockSpec((D, tf), lambda i, j: (0, j)),
        pl.BlockSpec((tf, D), lambda i, j: (j, 0)),
    ]
    g_spec = pl.BlockSpec((tm, D), row) if normed_input else pl.BlockSpec((1, D), lambda i, j: (0, 0))
    in_specs = [pl.BlockSpec((tm, D), row), g_spec] + w_specs
    args = [x, g, wg, wu, wd]
    out_shape = [jax.ShapeDtypeStruct((M, D), _F32)]
    out_specs = [pl.BlockSpec((tm, D), row)]
    blocks = [_nbytes((tm, D), _F32) * 2, 3 * _nbytes((D, tf), wg.dtype)]
    if emit_norm:
        in_specs.append(pl.BlockSpec((1, D), lambda i, j: (0, 0)))
        args.append(g2)
        out_shape.append(jax.ShapeDtypeStruct((M, D), _BF))
        out_specs.append(pl.BlockSpec((tm, D), row))
        blocks.append(_nbytes((tm, D), _BF))
    if keep_weights:
        out_shape += [jax.ShapeDtypeStruct(w.shape, _BF) for w in (wg, wu, wd)]
        out_specs += w_specs
        blocks.append(3 * _nbytes((D, tf), _BF))
    c_in, c_out, c_shape, c_bytes = _cast_specs(casts, grid)
    blocks.append(c_bytes)
    body = functools.partial(_ffn_kernel, emit_norm=emit_norm, keep_weights=keep_weights,
                             normed_input=normed_input)
    res = pl.pallas_call(
        _with_casts(body, len(in_specs), len(out_specs), len(casts)),
        grid=grid,
        in_specs=in_specs + c_in,
        out_specs=out_specs + c_out,
        out_shape=out_shape + c_shape,
        scratch_shapes=[] if normed_input else [pltpu.VMEM((tm, D), _BF)],
        compiler_params=pltpu.CompilerParams(
            dimension_semantics=("arbitrary", "arbitrary"),
            vmem_limit_bytes=_vmem_limit(blocks, _nbytes((tm, D), _BF)),
        ),
        name=name,
    )(*args, *_cast_args(casts))
    return res


def _conv_kernel(*refs, carry_rows, seg, tiles_per_seg, d_conv, cw):
    if carry_rows:
        n_ref, w_ref, cwt_ref, ucb_ref, ulast_ref, carry_ref = refs
    else:
        n_ref, w_ref, cwt_ref, e1_ref, e2_ref, ucb_ref, u_ref = refs
    tm = n_ref.shape[0]
    n = n_ref[...]
    if carry_rows:
        @pl.when(pl.program_id(0) % tiles_per_seg == 0)
        def _():
            carry_ref[...] = jnp.zeros_like(carry_ref)

    for c in range(d_conv // cw):
        lo = c * cw
        cb = _dot(n, w_ref[:, lo:lo + cw])
        cc = _dot(n, w_ref[:, d_conv + lo:d_conv + lo + cw])
        cv = _dot(n, w_ref[:, 2 * d_conv + lo:2 * d_conv + lo + cw])
        u = cc * cv
        r1 = pltpu.roll(u, 1, 0)
        r2 = pltpu.roll(u, 2, 0)
        if carry_rows:
            prev = carry_ref[:, lo:lo + cw]
            rows8 = lax.broadcasted_iota(jnp.int32, (8, cw), 0)
            h1 = jnp.where(rows8 < 1, pltpu.roll(prev, 1, 0), r1[:8])
            h2 = jnp.where(rows8 < 2, pltpu.roll(prev, 2, 0), r2[:8])
            r1 = jnp.concatenate([h1, r1[8:]], axis=0)
            r2 = jnp.concatenate([h2, r2[8:]], axis=0)
            carry_ref[:, lo:lo + cw] = u[tm - 8:]
            ulast_ref[0, :, lo:lo + cw] = u[tm - 8:]
        else:
            pos = lax.broadcasted_iota(jnp.int32, (tm, cw), 0) % seg
            r1 = jnp.where(pos < 1, e1_ref[:, lo:lo + cw], r1)
            r2 = jnp.where(pos < 2, e2_ref[:, lo:lo + cw], r2)
            u_ref[:, lo:lo + cw] = u
        w0 = cwt_ref[0:1, lo:lo + cw]
        w1 = cwt_ref[1:2, lo:lo + cw]
        w2 = cwt_ref[2:3, lo:lo + cw]
        conv = w0 * r2 + w1 * r1 + w2 * u
        ucb_ref[:, lo:lo + cw] = (cb * conv).astype(_BF)


def _qkv_kernel(n_ref, w_ref, gq_ref, gk_ref, ones_ref, q_ref, k_ref, v_ref, *, cw, col0):
    ones = ones_ref[...]
    dq = q_ref.shape[1]
    dk = k_ref.shape[1]
    p = _dot(n_ref[...], w_ref[:, col0:col0 + dq + 2 * dk])
    x = p[:, :dq + dk]
    sq = x * x
    hi = sq.astype(_BF)
    lo = (sq - hi.astype(_F32)).astype(_BF)
    ss = jnp.concatenate(
        [_dot(hi[:, c * cw:(c + 1) * cw], ones) + _dot(lo[:, c * cw:(c + 1) * cw], ones)
         for c in range((dq + dk) // cw)], axis=1)
    xr = x * lax.rsqrt(ss * (1.0 / HEAD_DIM) + EPS)
    q_ref[...] = (xr[:, :dq] * gq_ref[...] * QK_SCALE).astype(_BF)
    k_ref[...] = xr[:, dq:] * gk_ref[...]
    v_ref[...] = p[:, dq + dk:]


def _inproj_kernel(*refs, carry_rows, seg, tiles_per_seg, d_conv, cw):
    if carry_rows:
        (n_ref, w_ref, cwt_ref, gq_ref, gk_ref, ones_ref,
         ucb_ref, ulast_ref, q_ref, k_ref, v_ref, carry_ref) = refs
        conv_refs = (n_ref, w_ref, cwt_ref, ucb_ref, ulast_ref, carry_ref)
    else:
        (n_ref, w_ref, cwt_ref, gq_ref, gk_ref, ones_ref, e1_ref, e2_ref,
         ucb_ref, u_ref, q_ref, k_ref, v_ref) = refs
        conv_refs = (n_ref, w_ref, cwt_ref, e1_ref, e2_ref, ucb_ref, u_ref)
    _conv_kernel(*conv_refs, carry_rows=carry_rows, seg=seg, tiles_per_seg=tiles_per_seg, d_conv=d_conv, cw=cw)
    _qkv_kernel(n_ref, w_ref, gq_ref, gk_ref, ones_ref, q_ref, k_ref, v_ref, cw=cw, col0=3 * d_conv)


def _inproj(n, w_in, conv_w, gq, gk, ones, *, tm, name, batch=None, seg=None, edges=(), casts=()):
    M, D = n.shape
    d_conv = conv_w.shape[1]
    cw = ones.shape[0]
    dq, dk = N_HEADS * HEAD_DIM, N_KV * HEAD_DIM
    wcols = 3 * d_conv + dq + 2 * dk
    carry_rows = batch is not None
    grid = (M // tm,)
    tiles_per_seg = (M // batch) // tm if carry_rows else None
    c_in, c_out, c_shape, c_bytes = _cast_specs(casts, grid)
    const = lambda i: (0, 0)
    row = lambda i: (i, 0)
    in_specs = [
        pl.BlockSpec((tm, D), row),
        pl.BlockSpec((D, wcols), const, pipeline_mode=pl.Buffered(1)),
        pl.BlockSpec((3, d_conv), const),
        pl.BlockSpec((1, dq), const),
        pl.BlockSpec((1, dk), const),
        pl.BlockSpec((cw, cw), const),
    ] + [pl.BlockSpec((tm, d_conv), row) for _ in edges]
    if carry_rows:
        u_spec = pl.BlockSpec((1, 8, d_conv), lambda i: (i // tiles_per_seg, 0, 0))
        u_shape = jax.ShapeDtypeStruct((batch, 8, d_conv), _F32)
        scratch = [pltpu.VMEM((8, d_conv), _F32)]
    else:
        u_spec = pl.BlockSpec((tm, d_conv), row)
        u_shape = jax.ShapeDtypeStruct((M, d_conv), _F32)
        scratch = []
    out_specs = [pl.BlockSpec((tm, d_conv), row), u_spec,
                 pl.BlockSpec((tm, dq), row), pl.BlockSpec((tm, dk), row), pl.BlockSpec((tm, dk), row)]
    out_shape = [jax.ShapeDtypeStruct((M, d_conv), _BF), u_shape, jax.ShapeDtypeStruct((M, dq), _BF),
                 jax.ShapeDtypeStruct((M, dk), _F32), jax.ShapeDtypeStruct((M, dk), _F32)]
    blocks = [_nbytes((tm, D), _BF), _nbytes((D, wcols), _BF) // 2, (1 + len(edges)) * _nbytes((tm, d_conv), _F32),
              _nbytes((tm, d_conv + dq), _BF), 2 * _nbytes((tm, dk), _F32), c_bytes]
    body = functools.partial(_inproj_kernel, carry_rows=carry_rows, seg=seg, tiles_per_seg=tiles_per_seg,
                             d_conv=d_conv, cw=cw)
    return pl.pallas_call(
        _with_casts(body, len(in_specs), len(out_specs), len(casts)),
        grid=grid,
        in_specs=in_specs + c_in,
        out_specs=out_specs + c_out,
        out_shape=out_shape + c_shape,
        scratch_shapes=scratch,
        compiler_params=pltpu.CompilerParams(
            dimension_semantics=("arbitrary",),
            vmem_limit_bytes=_vmem_limit(blocks),
        ),
        name=name,
    )(n, w_in, conv_w, gq, gk, ones, *edges, *_cast_args(casts))


def _bias_kernel(tab_ref, bkt_ref, out_ref, *, mask_step):
    bkt = bkt_ref[...]
    key = lax.broadcasted_iota(jnp.int32, bkt.shape, 1)
    for h in range(out_ref.shape[1]):
        acc = jnp.zeros(bkt.shape, _F32)
        for b in range(tab_ref.shape[0]):
            acc = jnp.where(bkt == b, tab_ref[b, h] * LOG2E, acc)
        for v in range(out_ref.shape[0]):
            out_ref[v, h] = jnp.where(key < v * mask_step, -jnp.inf, acc)


def _rel_bias(table, bucket, n_variants, mask_step, name):
    nq, nk = bucket.shape
    nh = table.shape[1]
    out = pl.pallas_call(
        functools.partial(_bias_kernel, mask_step=mask_step),
        in_specs=[pl.BlockSpec(memory_space=pltpu.SMEM), pl.BlockSpec((nq, nk), lambda: (0, 0))],
        out_specs=pl.BlockSpec((n_variants, nh, nq, nk), lambda: (0, 0, 0, 0)),
        out_shape=jax.ShapeDtypeStruct((n_variants, nh, nq, nk), _F32),
        name=name,
    )(table, bucket)
    return out.transpose(0, 3, 1, 2).reshape(n_variants, nk, nh * nq)


def _t5_bucket(rel):
    nb = N_BUCKETS // 2
    max_exact = nb // 2
    ret = jnp.where(rel > 0, nb, 0)
    n = jnp.abs(rel)
    nf = jnp.maximum(n, 1).astype(jnp.float32)
    large = max_exact + (jnp.log(nf / max_exact) / math.log(MAX_DIST / max_exact) * (nb - max_exact)).astype(jnp.int32)
    large = jnp.minimum(large, nb - 1)
    return ret + jnp.where(n < max_exact, n, large)


def _bucket_index(n_q, n_k):
    i = jnp.arange(n_q)[:, None]
    j = jnp.arange(n_k)[None, :]
    return _t5_bucket(j - WINDOW - i).astype(jnp.int32)


_NT = (((1,), (1,)), ((), ()))
_TN = (((0,), (0,)), ((), ()))


def _sink_softmax_keys_major(lg, sink_t):
    sink2 = sink_t * LOG2E
    m = jnp.maximum(jnp.max(lg, axis=0, keepdims=True), sink2)
    e = jnp.exp2(lg - m)
    den = jnp.sum(e, axis=0, keepdims=True) + jnp.exp2(sink2 - m)
    return e, 1.0 / den


def _attn_chunk_logits(qc, kw, bias_t):
    logits = []
    for kv in range(N_KV):
        qg = jnp.concatenate(
            [qc[:, (kv * GROUP + g) * HEAD_DIM:(kv * GROUP + g + 1) * HEAD_DIM] for g in range(GROUP)], axis=0)
        kk = kw[:, kv * HEAD_DIM:(kv + 1) * HEAD_DIM]
        logits.append(lax.dot_general(kk, qg, _NT, preferred_element_type=_F32))
    return jnp.concatenate(logits, axis=1) + bias_t


def _attn_chunk_output(lg, vw, sink_t):
    gq = lg.shape[1] // N_KV
    e, inv = _sink_softmax_keys_major(lg, sink_t)
    e = e.astype(_BF)
    outs = []
    for kv in range(N_KV):
        vv = vw[:, kv * HEAD_DIM:(kv + 1) * HEAD_DIM]
        cols = slice(kv * gq, (kv + 1) * gq)
        outs.append(lax.dot_general(vv, e[:, cols], _TN, preferred_element_type=_F32) * inv[:, cols])
    return jnp.concatenate(outs, axis=0).T


ATTN_CHUNKS_PER_BLOCK = 8


def _attn_prompt_kernel(*refs, tiles_per_batch, n_gate_blocks):
    q_ref, kp_ref, kc_ref, vp_ref, vc_ref, bias_ref, sink_ref, n_ref = refs[:8]
    wgate_refs = refs[8:8 + n_gate_blocks]
    o_ref, gate_ref, kwin, vwin = refs[8 + n_gate_blocks:]
    ta = q_ref.shape[0]
    nk = WINDOW + CHUNK
    dkv = N_KV * HEAD_DIM
    rows = ATTN_CHUNKS_PER_BLOCK * CHUNK
    gate_cols = gate_ref.shape[1] // n_gate_blocks
    kwin[0:WINDOW, :] = kp_ref[...].astype(_BF)
    kwin[WINDOW:, :] = kc_ref[...].astype(_BF)
    vwin[0:WINDOW, :] = vp_ref[...].astype(_BF)
    vwin[WINDOW:, :] = vc_ref[...].astype(_BF)
    tile_pos = (pl.program_id(0) % tiles_per_batch) * ta

    def logits(r):
        n_before = jnp.maximum(0, (WINDOW - (tile_pos + r)) // CHUNK)
        return _attn_chunk_logits(q_ref[pl.ds(r, CHUNK), :], kwin[pl.ds(r, nk), :], bias_ref[n_before])

    def finish(r, lg):
        o = _attn_chunk_output(lg, vwin[pl.ds(r, nk), :], sink_ref[...])
        for g in range(GROUP):
            o_ref[pl.ds(r, CHUNK), g * dkv:(g + 1) * dkv] = o[g * CHUNK:(g + 1) * CHUNK].astype(_BF)

    def block(b, carry):
        r0 = pl.multiple_of(b * rows, rows)
        n = n_ref[pl.ds(r0, rows), :]
        chunk_rows = [pl.multiple_of(r0 + c * CHUNK, CHUNK) for c in range(ATTN_CHUNKS_PER_BLOCK)]
        def slab(s):
            gate = jax.nn.sigmoid(_dot(n, wgate_refs[s][...]))
            gate_ref[pl.ds(r0, rows), s * gate_cols:(s + 1) * gate_cols] = gate.astype(_BF)

        per = n_gate_blocks // ATTN_CHUNKS_PER_BLOCK
        slab(0)
        lg = logits(chunk_rows[0])
        for c in range(ATTN_CHUNKS_PER_BLOCK):
            lg_next = logits(chunk_rows[c + 1]) if c + 1 < ATTN_CHUNKS_PER_BLOCK else None
            for s in range(c * per + 1, min((c + 1) * per + 1, n_gate_blocks)):
                slab(s)
            finish(chunk_rows[c], lg)
            lg = lg_next
        return carry

    lax.fori_loop(0, ta // rows, block, 0)


def _attn_prompt(q, k, v, bias, sink, n, w_in, *, gate_col, gate_cols, gate_block, batch, ta, name, casts=()):
    M, dq = q.shape
    dk = k.shape[1]
    D = n.shape[1]
    n_gate_blocks = gate_cols // gate_block
    tiles_per_batch = (M // batch) // ta
    wb = ta // WINDOW
    nk = WINDOW + CHUNK

    def prev(i):
        return (jnp.where(i % tiles_per_batch == 0, i * wb, i * wb - 1), 0)

    row = lambda i: (i, 0)
    grid = (M // ta,)
    c_in, c_out, c_shape, c_bytes = _cast_specs(casts, grid)
    blocks = [_nbytes((ta, dq), _BF) * 2, 2 * _nbytes((WINDOW + ta, dk), _F32),
              _nbytes(bias.shape, _F32), _nbytes((8, N_HEADS * CHUNK), _F32), c_bytes,
              _nbytes((ta, D + gate_cols), _BF), _nbytes((D, gate_cols), _BF) // 2]
    gate_specs = [pl.BlockSpec((D, gate_block), lambda i, b=gate_col // gate_block + c: (0, b),
                               pipeline_mode=pl.Buffered(1)) for c in range(n_gate_blocks)]
    body = functools.partial(_attn_prompt_kernel, tiles_per_batch=tiles_per_batch, n_gate_blocks=n_gate_blocks)
    return pl.pallas_call(
        _with_casts(body, 8 + n_gate_blocks, 2, len(casts)),
        grid=grid,
        in_specs=[
            pl.BlockSpec((ta, dq), row),
            pl.BlockSpec((WINDOW, dk), prev),
            pl.BlockSpec((ta, dk), row),
            pl.BlockSpec((WINDOW, dk), prev),
            pl.BlockSpec((ta, dk), row),
            pl.BlockSpec(bias.shape, lambda i: (0, 0, 0)),
            pl.BlockSpec((1, N_HEADS * CHUNK), lambda i: (0, 0)),
            pl.BlockSpec((ta, D), row),
        ] + gate_specs + c_in,
        out_specs=[pl.BlockSpec((ta, dq), row), pl.BlockSpec((ta, gate_cols), row)] + c_out,
        out_shape=[jax.ShapeDtypeStruct((M, dq), _BF), jax.ShapeDtypeStruct((M, gate_cols), _BF)] + c_shape,
        scratch_shapes=[pltpu.VMEM((WINDOW + ta, dk), _BF), pltpu.VMEM((WINDOW + ta, dk), _BF)],
        compiler_params=pltpu.CompilerParams(
            dimension_semantics=("arbitrary",),
            vmem_limit_bytes=_vmem_limit(blocks, 2 * _nbytes((WINDOW + ta, dk), _BF)),
        ),
        name=name,
    )(q, k, k, v, v, bias, sink, n, *([w_in] * n_gate_blocks), *_cast_args(casts))


LANES = 128


def _attn_sample_kernel(q_ref, ck_ref, cv_ref, k_ref, v_ref, bias_ref, sink_ref, o_ref, *, nb, s):
    dkv = N_KV * HEAD_DIM
    gs = GROUP * s
    assert LANES == 2 * HEAD_DIM and dkv == 2 * LANES
    for b in range(nb):
        rows = slice(b * s, (b + 1) * s)
        kw = jnp.concatenate([ck_ref[b].astype(_BF), k_ref[rows, :].astype(_BF)], axis=0)
        vw = jnp.concatenate([cv_ref[b].astype(_BF), v_ref[rows, :].astype(_BF)], axis=0)
        qf = q_ref[rows, :].astype(_F32)
        blocks = []
        for kv in range(N_KV):
            for g in range(GROUP):
                h = kv * GROUP + g
                pair = qf[:, (h // 2) * LANES:(h // 2 + 1) * LANES]
                if h % 2 != kv % 2:
                    pair = pltpu.roll(pair, HEAD_DIM, 1)
                half = lax.broadcasted_iota(jnp.int32, pair.shape, 1) // HEAD_DIM
                piece = jnp.where(half == kv % 2, pair, 0.0)
                zero = jnp.zeros_like(piece)
                blocks.append(jnp.concatenate([piece, zero] if kv < 2 else [zero, piece], axis=1))
        q_bd = jnp.concatenate(blocks, axis=0).astype(_BF)
        lg = lax.dot_general(kw, q_bd, _NT, preferred_element_type=_F32) + bias_ref[0]
        e, inv = _sink_softmax_keys_major(lg, sink_ref[...])
        z = lax.dot_general((e * inv).astype(_BF), vw, _TN, preferred_element_type=_F32)
        row_kv = lax.broadcasted_iota(jnp.int32, z.shape, 0) // gs
        col_kv = lax.broadcasted_iota(jnp.int32, z.shape, 1) // HEAD_DIM
        z = jnp.where(row_kv == col_kv, z, 0.0)
        r = z[0:gs]
        for kv in range(1, N_KV):
            r = r + z[kv * gs:(kv + 1) * gs]
        for g in range(GROUP):
            o_ref[rows, g * dkv:(g + 1) * dkv] = r[g * s:(g + 1) * s].astype(_BF)


def _attn_sample(q, k, v, cache_k, cache_v, bias, sink, *, nb, name):
    B, W, dk = cache_k.shape
    M, dq = q.shape
    S = M // B
    row = lambda b: (b, 0)
    return pl.pallas_call(
        functools.partial(_attn_sample_kernel, nb=nb, s=S),
        grid=(B // nb,),
        in_specs=[
            pl.BlockSpec((nb * S, dq), row),
            pl.BlockSpec((nb, W, dk), lambda b: (b, 0, 0)),
            pl.BlockSpec((nb, W, dk), lambda b: (b, 0, 0)),
            pl.BlockSpec((nb * S, dk), row),
            pl.BlockSpec((nb * S, dk), row),
            pl.BlockSpec(bias.shape, lambda b: (0, 0, 0)),
            pl.BlockSpec(sink.shape, lambda b: (0, 0)),
        ],
        out_specs=pl.BlockSpec((nb * S, dq), row),
        out_shape=jax.ShapeDtypeStruct((M, dq), _BF),
        compiler_params=pltpu.CompilerParams(dimension_semantics=("arbitrary",)),
        name=name,
    )(q, cache_k, cache_v, k, v, bias, sink)


def _gates_kernel(n_ref, w_ref, gate_ref):
    gate_ref[...] = jax.nn.sigmoid(_dot(n_ref[...], w_ref[...])).astype(_BF)


def _gates(n, w_in, *, gate_col, gate_cols, gate_block, name):
    M, D = n.shape
    first = gate_col // gate_block
    return pl.pallas_call(
        _gates_kernel,
        grid=(gate_cols // gate_block,),
        in_specs=[pl.BlockSpec((M, D), lambda c: (0, 0)), pl.BlockSpec((D, gate_block), lambda c: (0, first + c))],
        out_specs=pl.BlockSpec((M, gate_block), lambda c: (0, c)),
        out_shape=jax.ShapeDtypeStruct((M, gate_cols), _BF),
        compiler_params=pltpu.CompilerParams(
            dimension_semantics=("arbitrary",),
            vmem_limit_bytes=_vmem_limit([_nbytes((M, D + gate_block), _BF), _nbytes((D, gate_block), _BF)]),
        ),
        name=name,
    )(n, w_in)


def _merge_kernel(h_ref, gate_ref, ucb_ref, o_ref, wco_ref, wao_ref, wout_ref, g_ref, out_ref, xn_ref, mix_ref,
                  *, n_chunks):
    ucb, o = ucb_ref[...], o_ref[...]
    D = out_ref.shape[1]
    cw = D // n_chunks
    for c in range(n_chunks):
        cols = slice(c * cw, (c + 1) * cw)
        y_conv = _dot(ucb, wco_ref[:, cols])
        y_attn = _dot(o, wao_ref[:, cols])
        g_conv = gate_ref[:, cols].astype(_F32)
        g_attn = gate_ref[:, D + c * cw:D + (c + 1) * cw].astype(_F32)
        mix_ref[:, cols] = (g_conv * y_conv + g_attn * y_attn).astype(_BF)
    h = h_ref[...] + _dot(mix_ref[...], wout_ref[...])
    out_ref[...] = h
    xn_ref[...] = _rms(h, g_ref[...]).astype(_BF)


def _merge(h, gates, ucb, o, w_conv_out, w_attn_o, w_out, g, *, cw, tm, name):
    M, D = h.shape
    dc, da = ucb.shape[1], o.shape[1]
    row = lambda i: (i, 0)
    once = pl.Buffered(1)
    const = lambda i: (0, 0)
    blocks = [2 * _nbytes((tm, D), _F32), _nbytes((tm, 3 * D + dc + da), _BF), _nbytes((dc + da + D, D), _BF) // 2]
    return pl.pallas_call(
        functools.partial(_merge_kernel, n_chunks=D // cw),
        grid=(M // tm,),
        in_specs=[
            pl.BlockSpec((tm, D), row),
            pl.BlockSpec((tm, 2 * D), row),
            pl.BlockSpec((tm, dc), row),
            pl.BlockSpec((tm, da), row),
            pl.BlockSpec((dc, D), const, pipeline_mode=once),
            pl.BlockSpec((da, D), const, pipeline_mode=once),
            pl.BlockSpec((D, D), const, pipeline_mode=once),
            pl.BlockSpec((1, D), const),
        ],
        out_specs=[pl.BlockSpec((tm, D), row), pl.BlockSpec((tm, D), row)],
        out_shape=[jax.ShapeDtypeStruct((M, D), _F32), jax.ShapeDtypeStruct((M, D), _BF)],
        scratch_shapes=[pltpu.VMEM((tm, D), _BF)],
        compiler_params=pltpu.CompilerParams(
            dimension_semantics=("arbitrary",),
            vmem_limit_bytes=_vmem_limit(blocks, _nbytes((tm, D), _BF)),
        ),
        name=name,
    )(h, gates, ucb, o, w_conv_out, w_attn_o, w_out, g)


def _ple_kernel(h_ref, pe_ref, g_ref, wp_ref, wg_ref, out_ref):
    rows = h_ref.shape[0] // 2
    halves = [slice(0, rows), slice(rows, 2 * rows)]
    normed = [_rms(h_ref[sl, :], g_ref[...]).astype(_BF) for sl in halves]
    embs = [_dot(pe_ref[sl, :].astype(_BF), wp_ref[...]) for sl in halves]
    gates = [_dot(nr, wg_ref[...]) for nr in normed]
    for sl, emb, gate in zip(halves, embs, gates):
        out_ref[sl, :] = h_ref[sl, :] + emb * jax.nn.sigmoid(gate)


def _ple(h, pe, g, w_ple, w_gate, *, tm, name):
    M, D = h.shape
    dp = pe.shape[1]
    row = lambda i: (i, 0)
    const = lambda i: (0, 0)
    blocks = [2 * _nbytes((tm, D), _F32), _nbytes((tm, dp), _F32), _nbytes((dp + D, D), _BF)]
    return pl.pallas_call(
        _ple_kernel,
        grid=(M // tm,),
        in_specs=[
            pl.BlockSpec((tm, D), row),
            pl.BlockSpec((tm, dp), row),
            pl.BlockSpec((1, D), const),
            pl.BlockSpec((dp, D), const),
            pl.BlockSpec((D, D), const),
        ],
        out_specs=pl.BlockSpec((tm, D), row),
        out_shape=jax.ShapeDtypeStruct((M, D), _F32),
        compiler_params=pltpu.CompilerParams(
            dimension_semantics=("arbitrary",),
            vmem_limit_bytes=_vmem_limit(blocks),
        ),
        name=name,
    )(h, pe, g, w_ple, w_gate)


def _row_tile(m, want):
    return want if m % want == 0 else m


def kernel(x_prompt, x_sample, p_prompt, p_sample, state_conv, cache_k, cache_v, rel_table, ffn1_norm, ffn1_wg, ffn1_wu, ffn1_wd, mix_norm, w_in, conv_w, q_norm, k_norm, attn_sink, w_conv_out, w_attn_o, w_out, ffn2_norm, ffn2_wg, ffn2_wu, ffn2_wd, ple_norm, w_ple, w_ple_gate):
    B, T, D = x_prompt.shape
    Bs, S, _ = x_sample.shape
    depth = ffn1_wg.shape[0]
    d_conv = conv_w.shape[2]
    dq, dk = N_HEADS * HEAD_DIM, N_KV * HEAD_DIM
    qkv_col = 3 * d_conv
    gc_col = qkv_col + dq + 2 * dk
    Mp, Ms = B * T, Bs * S
    tm_p = _row_tile(Mp, 512)
    tm_s = _row_tile(Ms, 512)
    tf = 1024
    tn = 512
    cw = 256

    nk_p, nk_s = WINDOW + CHUNK, WINDOW + S
    bias_p = _rel_bias(rel_table, _bucket_index(CHUNK, nk_p), WINDOW // CHUNK + 1, CHUNK, "rel_bias_prompt")
    bias_s = _rel_bias(rel_table, _bucket_index(S, nk_s), 1, 0, "rel_bias_sample")
    lane = jnp.arange(cw) // HEAD_DIM
    ones = (lane[:, None] == lane[None, :]).astype(_BF)

    yp = x_prompt.reshape(Mp, D)
    ys = x_sample.reshape(Ms, D)
    conv_p, k_p, v_p, conv_s, k_s, v_s = [], [], [], [], [], []
    head_to_group_major = lambda h: (h % GROUP) * N_KV + h // GROUP
    for l in range(depth):
        vec = lambda g: g[l].reshape(1, -1)
        h1_s, n_s, f1_wg, f1_wu, f1_wd = _ffn(ys, vec(ffn1_norm), ffn1_wg[l], ffn1_wu[l], ffn1_wd[l],
                                              vec(mix_norm), tm=tm_s, tf=tf // 2, name="ffn1_sample")
        f1 = (vec(ffn1_norm), f1_wg, f1_wu, f1_wd)
        gq = jnp.tile(q_norm[l], N_HEADS).reshape(1, dq)
        gk = jnp.tile(k_norm[l], N_KV).reshape(1, dk)
        sink_p = jnp.repeat(attn_sink[l], CHUNK).reshape(1, N_HEADS * CHUNK)
        sink_s = jnp.repeat(attn_sink[l], S).reshape(1, N_HEADS * S)

        h1, n, w_in_l = _ffn(yp, *f1, vec(mix_norm), tm=tm_p, tf=tf, name="ffn1_prompt", casts=(w_in[l],))
        ucb, ulast, q, k, v, f2_wg, f2_wu, w_co, w_ao_p, w_o = _inproj(
            n, w_in_l, conv_w[l], gq, gk, ones, tm=tm_p, batch=B, name="inproj_prompt",
            casts=(ffn2_wg[l], ffn2_wu[l], w_conv_out[l], (w_attn_o[l], HEAD_DIM, head_to_group_major), w_out[l]))
        o, gates, f2_wd, w_p, w_pg = _attn_prompt(
            q, k, v, bias_p, sink_p, n, w_in_l, gate_col=gc_col, gate_cols=2 * D, gate_block=tn // 2, batch=B,
            ta=tm_p, name="attn_prompt", casts=(ffn2_wd[l], w_ple[l], w_ple_gate[l]))
        h2, xn2 = _merge(h1, gates, ucb, o, w_co, w_ao_p, w_o, vec(ffn2_norm), cw=tn, tm=tm_p, name="merge_prompt")
        f2 = (f2_wg, f2_wu, f2_wd)
        h3, = _ffn(h2, xn2, *f2, None, tm=tm_p, tf=tf, name="ffn2_prompt")
        yp = _ple(h3, p_prompt[l].reshape(Mp, -1), vec(ple_norm), w_p, w_pg, tm=tm_p, name="ple_prompt")
        conv_p.append(ulast[:, 6:8])
        k_p.append(k.reshape(B, T, dk)[:, T - WINDOW:].reshape(B, WINDOW, N_KV, HEAD_DIM))
        v_p.append(v.reshape(B, T, dk)[:, T - WINDOW:].reshape(B, WINDOW, N_KV, HEAD_DIM))

        st = state_conv[l]
        zeros = jnp.zeros((Bs, S - 2, d_conv), _F32)
        e2 = jnp.concatenate([st, zeros], axis=1).reshape(Ms, d_conv)
        e1 = jnp.concatenate([st[:, 1:], zeros, zeros[:, :1]], axis=1).reshape(Ms, d_conv)
        h1, n = h1_s, n_s
        ucb, u, q, k, v = _inproj(n, w_in_l, conv_w[l], gq, gk, ones, tm=tm_s, seg=S, edges=(e1, e2),
                                  name="inproj_sample")
        o = _attn_sample(q, k, v, cache_k[l].reshape(Bs, WINDOW, dk), cache_v[l].reshape(Bs, WINDOW, dk),
                         bias_s, sink_s, nb=4, name="attn_sample")
        gates = _gates(n, w_in_l, gate_col=gc_col, gate_cols=2 * D, gate_block=tn, name="gates_sample")
        h2, xn2 = _merge(h1, gates, ucb, o, w_co, w_ao_p, w_o, vec(ffn2_norm), cw=tn, tm=tm_s, name="merge_sample")
        h3, = _ffn(h2, xn2, *f2, None, tm=tm_s, tf=tf, name="ffn2_sample")
        ys = _ple(h3, p_sample[l].reshape(Ms, -1), vec(ple_norm), w_p, w_pg, tm=tm_s, name="ple_sample")
        conv_s.append(u.reshape(Bs, S, d_conv)[:, S - 2:])
        k_s.append(k.reshape(Bs, S, N_KV, HEAD_DIM))
        v_s.append(v.reshape(Bs, S, N_KV, HEAD_DIM))

    return (yp.reshape(B, T, D), ys.reshape(Bs, S, D), jnp.stack(conv_p), jnp.stack(k_p), jnp.stack(v_p),
            jnp.stack(conv_s), jnp.stack(k_s), jnp.stack(v_s))
```
